```python
import math
import jax
import jax.numpy as jnp
from jax import lax
import numpy as np

D_MODEL = 1024
BATCH = 8
SEQ = 2048
DEPTH = 2

RW_HEADS = 8
RW_HEAD_DIM = 64
RW_WIDTH = RW_HEADS * RW_HEAD_DIM
RW_DECAY_RANK = 64
RW_ICLR_RANK = 64
RW_GATE_RANK = 160
RW_GN_EPS = 64e-5
RW_PROJ = 3 * RW_WIDTH + RW_DECAY_RANK + RW_ICLR_RANK + RW_GATE_RANK
SC_WIDTH = 512
SC_CONV = 3
SC_PROJ = 3 * SC_WIDTH
AB_PROJ = RW_PROJ + SC_PROJ
AB_OUT_IN = RW_WIDTH + SC_WIDTH
DIL_PATTERNS = ((128, 1), (512, 4), (2048, 16))
N_GROUPS = 3
DIL_HEADS = 8
DIL_HEAD_DIM = 64
DIL_WIDTH = DIL_HEADS * DIL_HEAD_DIM
DIL_PROJ = N_GROUPS * 3 * DIL_WIDTH
BLOCK = 128
N_BUCKETS = 32
MAX_DISTANCE = 2048
D_FF = 4 * D_MODEL
N_EVEN = (DEPTH + 1) // 2
N_ODD = DEPTH // 2
DEEPNORM_ALPHA = (2 * DEPTH) ** 0.25
DEEPNORM_BETA = (8 * DEPTH) ** -0.25
LN_EPS = 1e-5

kernel_name = 'hybrid_rwkv7_shortconv_dilated_attn_trunk'


def _layer_norm(x, g, b):
    xf = x.astype(jnp.float32)
    mu = jnp.mean(xf, -1, keepdims=True)
    var = jnp.mean(jnp.square(xf - mu), -1, keepdims=True)
    return ((xf - mu) * lax.rsqrt(var + LN_EPS) * g + b).astype(x.dtype)


def _token_shift(p):
    return jnp.pad(p, ((0, 0), (1, 0), (0, 0)))[:, :-1]


def _rwkv7_step(state, inp):
    r, w, k, v, a, b = inp
    sa = jnp.einsum('bhvk,bhk->bhv', state, a)
    state = state * w[:, :, None, :] + sa[..., None] * b[:, :, None, :] + v[..., None] * k[:, :, None, :]
    return state, jnp.einsum('bhvk,bhk->bhv', state, r)


def _causal_depthwise_conv(z, w):
    return lax.conv_general_dilated(z, w[:, None, :].astype(z.dtype), window_strides=(1,),
                                    padding=((SC_CONV - 1, 0),),
                                    dimension_numbers=('NWC', 'WIO', 'NWC'),
                                    feature_group_count=z.shape[-1])


def rwkv_shortconv_mixer(u, w_in, mu, w0, w_up, a0, a_up, g_up, k_k, k_a, r_k, lnx_g, lnx_b, conv_w, w_out):
    f32 = jnp.float32
    B, T, _ = u.shape
    p = u @ w_in
    pa, pb = p[..., :RW_PROJ], p[..., RW_PROJ:]
    pa = pa + mu * (_token_shift(pa) - pa)
    W = RW_WIDTH
    r, k, v, wd, ad, gd = jnp.split(
        pa, [W, 2 * W, 3 * W, 3 * W + RW_DECAY_RANK, 3 * W + RW_DECAY_RANK + RW_ICLR_RANK], axis=-1)
    logw = -jax.nn.softplus(-(w0 + jnp.tanh(wd) @ w_up).astype(f32)) - 0.5
    decay = jnp.exp(-jnp.exp(logw))
    iclr = jax.nn.sigmoid((a0 + ad @ a_up).astype(f32))
    gate = (jax.nn.sigmoid(gd) @ g_up).astype(f32)

    def heads(t):
        return t.astype(f32).reshape(B, T, RW_HEADS, RW_HEAD_DIM)

    kf = k.astype(f32)
    kk = heads(kf * k_k)
    kk = kk / jnp.maximum(jnp.linalg.norm(kk, axis=-1, keepdims=True), 1e-12)
    k_h = heads(kf * (1.0 + (iclr - 1.0) * k_a))
    r_h, v_h, w_h, a_h = heads(r), heads(v), heads(decay), heads(iclr)
    xs = tuple(jnp.moveaxis(t, 1, 0) for t in (r_h, w_h, k_h, v_h, -kk, kk * a_h))
    s0 = jnp.zeros((B, RW_HEADS, RW_HEAD_DIM, RW_HEAD_DIM), f32)
    _, y = lax.scan(_rwkv7_step, s0, xs)
    y = jnp.moveaxis(y, 0, 1)
    mean = jnp.mean(y, -1, keepdims=True)
    var = jnp.mean(jnp.square(y - mean), -1, keepdims=True)
    y = ((y - mean) * lax.rsqrt(var + RW_GN_EPS)).reshape(B, T, W) * lnx_g + lnx_b
    bonus = jnp.sum(r_h * k_h * r_k, -1, keepdims=True) * v_h
    y_a = ((y + bonus.reshape(B, T, W)) * gate).astype(u.dtype)
    h, b_gate, c_gate = jnp.split(pb, 3, axis=-1)
    y_b = b_gate * _causal_depthwise_conv(c_gate * h, conv_w)
    return jnp.concatenate([y_a, y_b], axis=-1) @ w_out


def _t5_bucket(dist):
    exact = N_BUCKETS // 2
    logd = jnp.log(jnp.maximum(dist, 1).astype(jnp.float32) / exact) / math.log(MAX_DISTANCE / exact)
    large = jnp.minimum(exact + (logd * (N_BUCKETS - exact)).astype(jnp.int32), N_BUCKETS - 1)
    return jnp.where(dist < exact, dist, large)


def _to_blocks(x, dil, nb):
    b, t = x.shape[:2]
    rest = x.shape[2:]
    L = t // dil
    x = jnp.moveaxis(x.reshape((b, L, dil) + rest), 2, 1)
    pad = [(0, 0)] * x.ndim
    pad[2] = (0, nb * BLOCK - L)
    return jnp.pad(x, pad).reshape((b, dil, nb, BLOCK) + rest)


def _from_blocks(x, L):
    b, dil, nb, blk = x.shape[:4]
    rest = x.shape[4:]
    x = x.reshape((b, dil, nb * blk) + rest)[:, :, :L]
    return jnp.moveaxis(x, 1, 2).reshape((b, L * dil) + rest)


def _with_prev_block(xb):
    prev = jnp.pad(xb, ((0, 0), (0, 0), (1, 0)) + ((0, 0),) * (xb.ndim - 3))[:, :, :-1]
    return jnp.concatenate([prev, xb], axis=3)


def _dilated_group_attention(q, k, v, bias, band, dil):
    t = q.shape[1]
    L = t // dil
    nb = -(-L // BLOCK)
    qb = _to_blocks(q, dil, nb)
    kc = _with_prev_block(_to_blocks(k, dil, nb))
    vc = _with_prev_block(_to_blocks(v, dil, nb))
    s = jnp.einsum('brnqhd,brnkhd->brnhqk', qb, kc).astype(jnp.float32) * (DIL_HEAD_DIM ** -0.5) + bias
    kidx = jnp.arange(2 * BLOCK)
    not_pad = (jnp.arange(nb) > 0)[:, None, None] | (kidx >= BLOCK)[None, None, :]
    valid = (band[None] & not_pad)[None, None, :, None]
    s = jnp.where(valid, s, -jnp.inf)
    m = jnp.max(s, -1, keepdims=True)
    e = jnp.exp(s - m)
    den = jnp.sum(e, -1, keepdims=True)
    o = jnp.einsum('brnhqk,brnkhd->brnqhd', (e / den).astype(v.dtype), vc)
    lse = jnp.moveaxis((m + jnp.log(den))[..., 0], 3, 4)
    return _from_blocks(o, L), _from_blocks(lse, L)


def dilated_attention_mixer(u, w_qkv, w_out, rel_bias):
    B, T, _ = u.shape
    p = (u @ w_qkv).reshape(B, T, N_GROUPS, 3, DIL_HEADS, DIL_HEAD_DIM)
    qi = jnp.arange(BLOCK)[:, None]
    ki = jnp.arange(2 * BLOCK)[None, :]
    rel = BLOCK + qi - ki
    outs, lses = [], []
    for g, (window, dil) in enumerate(DIL_PATTERNS):
        span = window // dil
        band = (rel >= 0) & (rel <= span)
        bucket = _t5_bucket(jnp.clip(rel, 0, span) * dil)
        bias = jnp.transpose(rel_bias[bucket][..., g * DIL_HEADS:(g + 1) * DIL_HEADS], (2, 0, 1)).astype(jnp.float32)
        o, lse = _dilated_group_attention(p[:, :, g, 0], p[:, :, g, 1], p[:, :, g, 2], bias, band, dil)
        outs.append(o.astype(jnp.float32))
        lses.append(lse)
    wts = jax.nn.softmax(jnp.stack(lses), axis=0)
    o = jnp.sum(wts[..., None] * jnp.stack(outs), axis=0)
    return o.reshape(B, T, DIL_WIDTH).astype(u.dtype) @ w_out


def sq_relu_mlp(u, w1, w2):
    return jnp.square(jax.nn.relu(u @ w1)) @ w2


def setup_inputs(seed: int = 0) -> dict:
    key = jax.random.key(seed)
    ks = iter(jax.random.split(key, 32))
    D = D_MODEL

    def nrm(shape, s):
        return s * jax.random.normal(next(ks), shape, jnp.float32)

    return {
        'x': nrm((BATCH, SEQ, D), 1.0),
        'c': nrm((BATCH, D), 1.0),
        'ada_w': nrm((DEPTH, D, 6 * D), 0.1 * D ** -0.5),
        'ada_b': nrm((DEPTH, 6 * D), 0.01),
        'ln_g': 1.0 + nrm((DEPTH, 2, D), 0.02),
        'ln_b': nrm((DEPTH, 2, D), 0.02),
        'ab_w_in': nrm((N_EVEN, D, AB_PROJ), D ** -0.5),
        'rw_mu': jax.random.uniform(next(ks), (N_EVEN, RW_PROJ), jnp.float32),
        'rw_w0': -2.0 + nrm((N_EVEN, RW_WIDTH), 1.0),
        'rw_w_up': nrm((N_EVEN, RW_DECAY_RANK, RW_WIDTH), RW_DECAY_RANK ** -0.5),
        'rw_a0': nrm((N_EVEN, RW_WIDTH), 0.5),
        'rw_a_up': nrm((N_EVEN, RW_ICLR_RANK, RW_WIDTH), RW_ICLR_RANK ** -0.5),
        'rw_g_up': nrm((N_EVEN, RW_GATE_RANK, RW_WIDTH), RW_GATE_RANK ** -0.5),
        'rw_k_k': 0.85 + nrm((N_EVEN, RW_WIDTH), 0.05),
        'rw_k_a': 1.0 + nrm((N_EVEN, RW_WIDTH), 0.05),
        'rw_r_k': nrm((N_EVEN, RW_HEADS, RW_HEAD_DIM), 0.1),
        'rw_lnx_g': 1.0 + nrm((N_EVEN, RW_WIDTH), 0.02),
        'rw_lnx_b': nrm((N_EVEN, RW_WIDTH), 0.02),
        'sc_conv_w': nrm((N_EVEN, SC_CONV, SC_WIDTH), SC_CONV ** -0.5),
        'ab_w_out': nrm((N_EVEN, AB_OUT_IN, D), DEEPNORM_BETA * AB_OUT_IN ** -0.5),
        'dil_w_qkv': nrm((N_ODD, D, DIL_PROJ), D ** -0.5),
        'dil_w_out': nrm((N_ODD, DIL_WIDTH, D), DEEPNORM_BETA * DIL_WIDTH ** -0.5),
        'rel_bias': nrm((N_BUCKETS, N_GROUPS * DIL_HEADS), 0.5),
        'mlp_w1': nrm((DEPTH, D, D_FF), D ** -0.5),
        'mlp_w2': nrm((DEPTH, D_FF, D), DEEPNORM_BETA * D_FF ** -0.5),
    }


def reference(x, c, ada_w, ada_b, ln_g, ln_b, ab_w_in, rw_mu, rw_w0, rw_w_up, rw_a0, rw_a_up, rw_g_up,
              rw_k_k, rw_k_a, rw_r_k, rw_lnx_g, rw_lnx_b, sc_conv_w, ab_w_out, dil_w_qkv, dil_w_out,
              rel_bias, mlp_w1, mlp_w2):
    cond = jax.nn.silu(c)
    for i in range(DEPTH):
        mod = (cond @ ada_w[i] + ada_b[i])[:, None, :]
        sh1, sc1, g1, sh2, sc2, g2 = jnp.split(mod, 6, axis=-1)
        j = i // 2
        u = x * (1 + sc1) + sh1
        if i % 2 == 0:
            y = rwkv_shortconv_mixer(u, ab_w_in[j], rw_mu[j], rw_w0[j], rw_w_up[j], rw_a0[j], rw_a_up[j],
                                     rw_g_up[j], rw_k_k[j], rw_k_a[j], rw_r_k[j], rw_lnx_g[j], rw_lnx_b[j],
                                     sc_conv_w[j], ab_w_out[j])
        else:
            y = dilated_attention_mixer(u, dil_w_qkv[j], dil_w_out[j], rel_bias)
        x = _layer_norm(DEEPNORM_ALPHA * x + (1 + g1) * y, ln_g[i, 0], ln_b[i, 0])
        u = x * (1 + sc2) + sh2
        y = sq_relu_mlp(u, mlp_w1[i], mlp_w2[i])
        x = _layer_norm(DEEPNORM_ALPHA * x + (1 + g2) * y, ln_g[i, 1], ln_b[i, 1])
    return x
```

```python
import functools
import math

import jax
import jax.numpy as jnp
import numpy as np
from jax import lax
from jax.experimental import pallas as pl
from jax.experimental.pallas import tpu as pltpu

F32 = jnp.float32
BF16 = jnp.bfloat16

D_MODEL = 1024
DEPTH = 2
RW_HEADS = 8
RW_HEAD_DIM = 64
RW_WIDTH = 512
RW_DECAY_RANK = 64
RW_ICLR_RANK = 64
RW_GATE_RANK = 160
RW_GN_EPS = 64e-5
RW_PROJ = 3 * RW_WIDTH + RW_DECAY_RANK + RW_ICLR_RANK + RW_GATE_RANK
SC_WIDTH = 512
SC_CONV = 3
DIL_PATTERNS = ((128, 1), (512, 4), (2048, 16))
N_GROUPS = 3
DIL_HEADS = 8
DIL_HEAD_DIM = 64
DIL_WIDTH = 512
BLOCK = 128
N_BUCKETS = 32
MAX_DISTANCE = 2048
D_FF = 4 * D_MODEL
DEEPNORM_ALPHA = (2 * DEPTH) ** 0.25
LN_EPS = 1e-5

LANES = 128
CHUNK = 64
LR_PAD = 384
AB_PROJ_PAD = 3 * RW_WIDTH + 3 * SC_WIDTH + LR_PAD
VMEM_LIMIT = 56 * 1024 * 1024


def _cparams(n_axes):
    return pltpu.CompilerParams(dimension_semantics=("arbitrary",) * n_axes,
                                vmem_limit_bytes=VMEM_LIMIT)


def _sigmoid(x):
    return 1.0 / (1.0 + jnp.exp(-x))


def _dot(a, b):
    return jnp.dot(a.astype(BF16), b.astype(BF16), preferred_element_type=F32)


def _dot_nt(a, b):
    return lax.dot_general(a.astype(BF16), b.astype(BF16), (((1,), (1,)), ((), ())),
                           preferred_element_type=F32)


def _layer_norm_rows(z, g, b):
    mu = jnp.mean(z, axis=-1, keepdims=True)
    zc = z - mu
    var = jnp.mean(zc * zc, axis=-1, keepdims=True)
    return zc * lax.rsqrt(var + LN_EPS) * g + b


def _adaln_kernel(c_ref, w_ref, b_ref, o_ref):
    c = c_ref[...]
    cond = c * _sigmoid(c)
    o_ref[0] = _dot(cond, w_ref[0]) + b_ref[0]


def _adaln(c, ada_w, ada_b):
    depth, d, n = ada_w.shape
    bsz = c.shape[0]
    tn = 1536
    return pl.pallas_call(
        _adaln_kernel,
        grid=(depth, n // tn),
        in_specs=[pl.BlockSpec((bsz, d), lambda i, j: (0, 0)),
                  pl.BlockSpec((1, d, tn), lambda i, j: (i, 0, j)),
                  pl.BlockSpec((1, 1, tn), lambda i, j: (i, 0, j))],
        out_specs=pl.BlockSpec((1, bsz, tn), lambda i, j: (i, 0, j)),
        out_shape=jax.ShapeDtypeStruct((depth, bsz, n), F32),
        compiler_params=_cparams(2),
        name="adaln",
    )(c, ada_w, ada_b.reshape(depth, 1, n))


def _proj_kernel(x_ref, mod_ref, w_ref, o_ref, u_ref, *, shift_row, scale_row):
    @pl.when(pl.program_id(2) == 0)
    def _():
        m = mod_ref[0]
        u = x_ref[0] * (1.0 + m[scale_row:scale_row + 1]) + m[shift_row:shift_row + 1]
        u_ref[...] = u.astype(BF16)

    o_ref[0] = jnp.dot(u_ref[...], w_ref[...], preferred_element_type=F32).astype(o_ref.dtype)


def _modulated_proj(x, mod, w, *, shift_row, scale_row, out_dtype, tm=1024, tn=1152):
    bsz, t, d = x.shape
    n = w.shape[1]
    return pl.pallas_call(
        functools.partial(_proj_kernel, shift_row=shift_row, scale_row=scale_row),
        grid=(bsz, t // tm, n // tn),
        in_specs=[pl.BlockSpec((1, tm, d), lambda b, i, j: (b, i, 0)),
                  pl.BlockSpec((1, 6, d), lambda b, i, j: (b, 0, 0)),
                  pl.BlockSpec((d, tn), lambda b, i, j: (0, j))],
        out_specs=pl.BlockSpec((1, tm, tn), lambda b, i, j: (b, i, j)),
        out_shape=jax.ShapeDtypeStruct((bsz, t, n), out_dtype),
        scratch_shapes=[pltpu.VMEM((tm, d), BF16)],
        compiler_params=_cparams(3),
        name="modulated_proj",
    )(x, mod, w)


def _shift_rows(x, carry, k):
    rolled = pltpu.roll(x, k, 0)
    row = lax.broadcasted_iota(jnp.int32, x.shape, 0)
    out = rolled
    for i in range(k):
        out = jnp.where(row == i, carry[8 - k + i:8 - k + i + 1], out)
    return out


def _rwkv_chunk_terms(r, kh, v, ld, kk, ic, tri, consts):
    mask0, mask1, strict, incl, blk16, eye, bdmask, eye_bd = consts

    def bd_rows(q):
        return jnp.concatenate([q * mask0, q * mask1], axis=0)

    def pmm(p, q):
        return _dot(p, bd_rows(q))

    lp = jnp.dot(tri, ld, precision=lax.Precision.HIGHEST, preferred_element_type=F32)
    lp_last = lp[CHUNK - 1:CHUNK]
    e_in = jnp.exp(lp)
    a_t = -kk * jnp.exp(lp - ld)
    r_t = r * e_in
    e_inv = jnp.exp(-lp)
    b = kk * ic
    b_t = b * e_inv
    k_t = kh * e_inv
    e_end = jnp.exp(lp_last - lp)
    b_e = b * e_end
    k_e = kh * e_end
    p_end = jnp.exp(lp_last)

    lhs = jnp.concatenate([a_t, r_t], axis=0)
    g_b = _dot_nt(lhs, bd_rows(b_t))
    g_k = _dot_nt(lhs, bd_rows(k_t))
    a_ab = jnp.where(strict, g_b[:CHUNK], 0.0)
    a_ak = jnp.where(strict, g_k[:CHUNK], 0.0)
    a_rb = jnp.where(incl, g_b[CHUNK:], 0.0)
    a_rk = jnp.where(incl, g_k[CHUNK:], 0.0)

    l_d = jnp.where(blk16, a_ab, 0.0)
    l_e = a_ab - l_d
    l2 = pmm(l_d, l_d)
    l4 = pmm(l2, l2)
    l8 = pmm(l4, l4)
    dinv = eye + l_d
    dinv = dinv + pmm(dinv, l2)
    dinv = dinv + pmm(dinv, l4)
    dinv = dinv + pmm(dinv, l8)
    f1 = pmm(dinv, l_e)
    f2 = pmm(f1, f1)
    g = eye + f1
    g = g + pmm(g, f2)
    tinv = pmm(g, dinv)

    a_hat = pmm(tinv, a_t)
    w_hat = pmm(tinv, pmm(a_ak, v))
    qhat = r_t + pmm(a_rb, a_hat)
    yhat = pmm(a_rb, w_hat) + pmm(a_rk, v)
    mmat = eye_bd * p_end + jnp.where(bdmask, _dot(b_e.T, a_hat), 0.0)
    nmat = jnp.where(bdmask, _dot(jnp.concatenate([b_e, k_e], axis=0).T,
                                  jnp.concatenate([w_hat, v], axis=0)), 0.0)
    return qhat, yhat, mmat, nmat


def _rwkv_kernel(rkv_ref, cv_ref, lr_ref, mu_rkv_ref, mu_lr_ref, w0_ref, a0_ref, wwa_ref, gup_ref,
                 kk_ref, ka_ref, rk_ref, lnxg_ref, lnxb_ref, convw_ref, o_ref,
                 carry_rkv, carry_lr, carry_z, state, q_s, y_s, m_s, n_s, yout_s, *, tb):
    n_chunks = tb // CHUNK
    n_pairs = RW_WIDTH // LANES

    @pl.when(pl.program_id(1) == 0)
    def _():
        carry_rkv[...] = jnp.zeros_like(carry_rkv)
        carry_lr[...] = jnp.zeros_like(carry_lr)
        carry_z[...] = jnp.zeros_like(carry_z)
        state[...] = jnp.zeros_like(state)

    rkv = rkv_ref[0]
    lr = lr_ref[0]
    rkv_m = rkv + mu_rkv_ref[...] * (_shift_rows(rkv, carry_rkv[...], 1) - rkv)
    lr_m = lr + mu_lr_ref[...] * (_shift_rows(lr, carry_lr[...], 1) - lr)
    carry_rkv[...] = rkv[tb - 8:]
    carry_lr[...] = lr[tb - 8:]

    r = rkv_m[:, :RW_WIDTH]
    k = rkv_m[:, RW_WIDTH:2 * RW_WIDTH]
    v = rkv_m[:, 2 * RW_WIDTH:]

    wa = lr_m[:, :LANES]
    lane = lax.broadcasted_iota(jnp.int32, wa.shape, 1)
    wa = jnp.where(lane < RW_DECAY_RANK, jnp.tanh(wa), wa)
    wa_up = _dot(wa, wwa_ref[...])
    ld = -math.exp(-0.5) * _sigmoid(w0_ref[...] + wa_up[:, :RW_WIDTH])
    iclr = _sigmoid(a0_ref[...] + wa_up[:, RW_WIDTH:])
    gate = _dot(_sigmoid(lr_m[:, LANES:]), gup_ref[...])

    hr = lax.broadcasted_iota(jnp.int32, (RW_WIDTH, RW_WIDTH), 0) // RW_HEAD_DIM
    hc = lax.broadcasted_iota(jnp.int32, (RW_WIDTH, RW_WIDTH), 1) // RW_HEAD_DIM
    seg = jnp.where(hr == hc, 1.0 / RW_HEAD_DIM, 0.0).astype(BF16)

    def seg_mean(x):
        hi = x.astype(BF16)
        lo = (x - hi.astype(F32)).astype(BF16)
        return (jnp.dot(hi, seg, preferred_element_type=F32)
                + jnp.dot(lo, seg, preferred_element_type=F32))

    kk = k * kk_ref[...]
    ss = seg_mean(kk * kk) * RW_HEAD_DIM
    kk = kk * lax.rsqrt(jnp.maximum(ss, 1e-24))
    kh = k * (1.0 + (iclr - 1.0) * ka_ref[...])

    row = lax.broadcasted_iota(jnp.int32, (CHUNK, LANES), 0)
    col = lax.broadcasted_iota(jnp.int32, (CHUNK, LANES), 1)
    s_idx = col % CHUNK
    lane1 = lax.broadcasted_iota(jnp.int32, (1, LANES), 1)
    mask0 = jnp.where(lane1 < RW_HEAD_DIM, 1.0, 0.0)
    mask1 = 1.0 - mask0
    strict = s_idx < row
    incl = s_idx <= row
    blk16 = (s_idx // 16) == (row // 16)
    eye = jnp.where(s_idx == row, 1.0, 0.0)
    r2 = lax.broadcasted_iota(jnp.int32, (LANES, LANES), 0)
    c2 = lax.broadcasted_iota(jnp.int32, (LANES, LANES), 1)
    bdmask = (r2 // RW_HEAD_DIM) == (c2 // RW_HEAD_DIM)
    eye_bd = jnp.where(r2 == c2, 1.0, 0.0)
    tr = lax.broadcasted_iota(jnp.int32, (CHUNK, CHUNK), 0)
    tc = lax.broadcasted_iota(jnp.int32, (CHUNK, CHUNK), 1)
    tri = jnp.where(tc <= tr, 1.0, 0.0)
    consts = (mask0, mask1, strict, incl, blk16, eye, bdmask, eye_bd)

    for c in range(n_chunks):
        rs = slice(c * CHUNK, (c + 1) * CHUNK)
        for p in range(n_pairs):
            cs = slice(p * LANES, (p + 1) * LANES)
            qhat, yhat, mmat, nmat = _rwkv_chunk_terms(
                r[rs, cs], kh[rs, cs], v[rs, cs], ld[rs, cs], kk[rs, cs], iclr[rs, cs], tri, consts)
            i = c * n_pairs + p
            q_s[i] = qhat
            y_s[i] = yhat
            m_s[i] = mmat
            n_s[i] = nmat

    for p in range(n_pairs):
        s = state[p]
        for c in range(n_chunks):
            i = c * n_pairs + p
            ym = _dot(jnp.concatenate([q_s[i], m_s[i]], axis=0), s)
            yout_s[c * CHUNK:(c + 1) * CHUNK, p * LANES:(p + 1) * LANES] = ym[:CHUNK] + y_s[i]
            s = ym[CHUNK:] + n_s[i]
        state[p] = s

    y = yout_s[...]
    mean = seg_mean(y)
    yc = y - mean
    var = seg_mean(yc * yc)
    y = yc * lax.rsqrt(var + RW_GN_EPS) * lnxg_ref[...] + lnxb_ref[...]
    bonus = seg_mean(r * kh * rk_ref[...]) * RW_HEAD_DIM * v
    o_ref[0, :, :RW_WIDTH] = ((y + bonus) * gate).astype(o_ref.dtype)

    cv = cv_ref[0]
    h = cv[:, :SC_WIDTH]
    b_gate = cv[:, SC_WIDTH:2 * SC_WIDTH]
    z = cv[:, 2 * SC_WIDTH:] * h
    zc = carry_z[...]
    cw = convw_ref[...]
    conv = cw[2:3] * z + cw[1:2] * _shift_rows(z, zc, 1) + cw[0:1] * _shift_rows(z, zc, 2)
    carry_z[...] = z[tb - 8:]
    o_ref[0, :, RW_WIDTH:] = (b_gate * conv).astype(o_ref.dtype)


def _rwkv_shortconv(p, mu_rkv, mu_lr, w0, a0, wwa, gup, k_k, k_a, r_k, lnx_g, lnx_b, conv_w, *, tb=256):
    bsz, t, _ = p.shape
    n_chunks = tb // CHUNK
    n_pairs = RW_WIDTH // LANES
    w3 = 3 * RW_WIDTH

    def full(shape):
        return pl.BlockSpec(shape, lambda b, i: (0,) * len(shape))

    return pl.pallas_call(
        functools.partial(_rwkv_kernel, tb=tb),
        grid=(bsz, t // tb),
        in_specs=[pl.BlockSpec((1, tb, w3), lambda b, i: (b, i, 0)),
                  pl.BlockSpec((1, tb, w3), lambda b, i: (b, i, 1)),
                  pl.BlockSpec((1, tb, LR_PAD), lambda b, i: (b, i, 2 * w3 // LR_PAD)),
                  full((1, w3)), full((1, LR_PAD)), full((1, RW_WIDTH)), full((1, RW_WIDTH)),
                  full((LANES, 2 * RW_WIDTH)), full((LR_PAD - LANES, RW_WIDTH)),
                  full((1, RW_WIDTH)), full((1, RW_WIDTH)), full((1, RW_WIDTH)),
                  full((1, RW_WIDTH)), full((1, RW_WIDTH)), full((SC_CONV, SC_WIDTH))],
        out_specs=pl.BlockSpec((1, tb, RW_WIDTH + SC_WIDTH), lambda b, i: (b, i, 0)),
        out_shape=jax.ShapeDtypeStruct((bsz, t, RW_WIDTH + SC_WIDTH), BF16),
        scratch_shapes=[pltpu.VMEM((8, w3), F32), pltpu.VMEM((8, LR_PAD), F32), pltpu.VMEM((8, SC_WIDTH), F32),
                        pltpu.VMEM((n_pairs, LANES, LANES), F32),
                        pltpu.VMEM((n_chunks * n_pairs, CHUNK, LANES), F32),
                        pltpu.VMEM((n_chunks * n_pairs, CHUNK, LANES), F32),
                        pltpu.VMEM((n_chunks * n_pairs, LANES, LANES), F32),
                        pltpu.VMEM((n_chunks * n_pairs, LANES, LANES), F32),
                        pltpu.VMEM((tb, RW_WIDTH), F32)],
        compiler_params=_cparams(2),
        name="rwkv_shortconv",
    )(p, p, p, mu_rkv, mu_lr, w0, a0, wwa, gup, k_k, k_a, r_k, lnx_g, lnx_b, conv_w)


def _out_ln_kernel(y_ref, w_ref, x_ref, mod_ref, lng_ref, lnb_ref, o_ref, *, gate_row):
    y = jnp.dot(y_ref[0], w_ref[...], preferred_element_type=F32)
    g = mod_ref[0][gate_row:gate_row + 1]
    z = DEEPNORM_ALPHA * x_ref[0] + (1.0 + g) * y
    o_ref[0] = _layer_norm_rows(z, lng_ref[...], lnb_ref[...])


def _out_ln(y, w, x, mod, ln_g, ln_b, *, gate_row, tm=512):
    bsz, t, d = x.shape
    kdim = y.shape[-1]
    return pl.pallas_call(
        functools.partial(_out_ln_kernel, gate_row=gate_row),
        grid=(bsz, t // tm),
        in_specs=[pl.BlockSpec((1, tm, kdim), lambda b, i: (b, i, 0)),
                  pl.BlockSpec((kdim, d), lambda b, i: (0, 0)),
                  pl.BlockSpec((1, tm, d), lambda b, i: (b, i, 0)),
                  pl.BlockSpec((1, 6, d), lambda b, i: (b, 0, 0)),
                  pl.BlockSpec((1, d), lambda b, i: (0, 0)),
                  pl.BlockSpec((1, d), lambda b, i: (0, 0))],
        out_specs=pl.BlockSpec((1, tm, d), lambda b, i: (b, i, 0)),
        out_shape=jax.ShapeDtypeStruct((bsz, t, d), F32),
        compiler_params=_cparams(2),
        name="out_ln",
    )(y, w, x, mod, ln_g, ln_b)


def _attn_kernel(q_ref, kp_ref, kc_ref, vp_ref, vc_ref, bias_ref, o_ref, lse_ref):
    n = pl.program_id(2)
    qi = lax.broadcasted_iota(jnp.int32, (BLOCK, 2 * BLOCK), 0)
    ki = lax.broadcasted_iota(jnp.int32, (BLOCK, 2 * BLOCK), 1)
    rel = BLOCK + qi - ki
    valid = (rel >= 0) & (rel <= BLOCK) & ((n > 0) | (ki >= BLOCK))
    lane = lax.broadcasted_iota(jnp.int32, (1, LANES), 1)
    scale = DIL_HEAD_DIM ** -0.5
    qmask = (jnp.where(lane < DIL_HEAD_DIM, scale, 0.0).astype(BF16),
             jnp.where(lane >= DIL_HEAD_DIM, scale, 0.0).astype(BF16))
    lane_o = lax.broadcasted_iota(jnp.int32, (BLOCK, LANES), 1)
    lse_all = jnp.zeros((BLOCK, LANES), F32)
    for p in range(DIL_WIDTH // LANES):
        cs = slice(p * LANES, (p + 1) * LANES)
        q = q_ref[0, :, cs]
        kcat = jnp.concatenate([kp_ref[0, :, cs], kc_ref[0, :, cs]], axis=0)
        vcat = jnp.concatenate([vp_ref[0, :, cs], vc_ref[0, :, cs]], axis=0)
        outs = []
        for hh in range(2):
            h = 2 * p + hh
            s = lax.dot_general(q * qmask[hh], kcat, (((1,), (1,)), ((), ())),
                                preferred_element_type=F32) + bias_ref[h]
            s = jnp.where(valid, s, -jnp.inf)
            m = jnp.max(s, axis=-1, keepdims=True)
            e = jnp.exp(s - m)
            den = jnp.sum(e, axis=-1, keepdims=True)
            outs.append(jnp.dot(e.astype(BF16), vcat, preferred_element_type=F32) / den)
            lse_all = jnp.where(lane_o == h, m + jnp.log(den), lse_all)
        o_ref[0, :, cs] = jnp.where(lane_o < DIL_HEAD_DIM, outs[0], outs[1])
    lse_ref[0] = lse_all


def _dilated_attention_group(p, bias, g, dil):
    bsz, t, n = p.shape
    length = t // dil
    nb = length // BLOCK
    pv = p.reshape(bsz, length, dil * n)
    cpb = n // DIL_WIDTH

    def col(j):
        return lambda b, r, i: (b, i, r * cpb + g * 3 + j)

    def col_prev(j):
        return lambda b, r, i: (b, jnp.maximum(i - 1, 0), r * cpb + g * 3 + j)

    blk = (1, BLOCK, DIL_WIDTH)
    o, lse = pl.pallas_call(
        _attn_kernel,
        grid=(bsz, dil, nb),
        in_specs=[pl.BlockSpec(blk, col(0)),
                  pl.BlockSpec(blk, col_prev(1)), pl.BlockSpec(blk, col(1)),
                  pl.BlockSpec(blk, col_prev(2)), pl.BlockSpec(blk, col(2)),
                  pl.BlockSpec((DIL_HEADS, BLOCK, 2 * BLOCK), lambda b, r, i: (0, 0, 0))],
        out_specs=[pl.BlockSpec(blk, lambda b, r, i: (b, i, r)),
                   pl.BlockSpec((1, BLOCK, LANES), lambda b, r, i: (b, i, r))],
        out_shape=[jax.ShapeDtypeStruct((bsz, length, dil * DIL_WIDTH), F32),
                   jax.ShapeDtypeStruct((bsz, length, dil * LANES), F32)],
        compiler_params=_cparams(3),
        name=f"dilated_attn_g{g}",
    )(pv, pv, pv, pv, pv, bias)
    return o.reshape(bsz, t, DIL_WIDTH), lse.reshape(bsz, t, LANES)


def _merge_out_ln_kernel(o0_ref, o1_ref, o2_ref, l0_ref, l1_ref, l2_ref, w_ref, x_ref, mod_ref,
                         lng_ref, lnb_ref, o_ref, *, gate_row):
    l0, l1, l2 = l0_ref[0], l1_ref[0], l2_ref[0]
    m = jnp.maximum(jnp.maximum(l0, l1), l2)
    e0, e1, e2 = jnp.exp(l0 - m), jnp.exp(l1 - m), jnp.exp(l2 - m)
    inv = 1.0 / (e0 + e1 + e2)
    hr = lax.broadcasted_iota(jnp.int32, (LANES, DIL_WIDTH), 0)
    hc = lax.broadcasted_iota(jnp.int32, (LANES, DIL_WIDTH), 1) // DIL_HEAD_DIM
    expand = jnp.where(hr == hc, 1.0, 0.0).astype(BF16)

    def widen(wt):
        hi = wt.astype(BF16)
        lo = (wt - hi.astype(F32)).astype(BF16)
        return (jnp.dot(hi, expand, preferred_element_type=F32)
                + jnp.dot(lo, expand, preferred_element_type=F32))

    merged = widen(e0 * inv) * o0_ref[0] + widen(e1 * inv) * o1_ref[0] + widen(e2 * inv) * o2_ref[0]
    y = jnp.dot(merged.astype(BF16), w_ref[...], preferred_element_type=F32)
    g = mod_ref[0][gate_row:gate_row + 1]
    z = DEEPNORM_ALPHA * x_ref[0] + (1.0 + g) * y
    o_ref[0] = _layer_norm_rows(z, lng_ref[...], lnb_ref[...])


def _merge_out_ln(outs, lses, w, x, mod, ln_g, ln_b, *, gate_row, tm=512):
    bsz, t, d = x.shape
    ospec = pl.BlockSpec((1, tm, DIL_WIDTH), lambda b, i: (b, i, 0))
    lspec = pl.BlockSpec((1, tm, LANES), lambda b, i: (b, i, 0))
    return pl.pallas_call(
        functools.partial(_merge_out_ln_kernel, gate_row=gate_row),
        grid=(bsz, t // tm),
        in_specs=[ospec, ospec, ospec, lspec, lspec, lspec,
                  pl.BlockSpec((DIL_WIDTH, d), lambda b, i: (0, 0)),
                  pl.BlockSpec((1, tm, d), lambda b, i: (b, i, 0)),
                  pl.BlockSpec((1, 6, d), lambda b, i: (b, 0, 0)),
                  pl.BlockSpec((1, d), lambda b, i: (0, 0)),
                  pl.BlockSpec((1, d), lambda b, i: (0, 0))],
        out_specs=pl.BlockSpec((1, tm, d), lambda b, i: (b, i, 0)),
        out_shape=jax.ShapeDtypeStruct((bsz, t, d), F32),
        compiler_params=_cparams(2),
        name="merge_out_ln",
    )(*outs, *lses, w, x, mod, ln_g, ln_b)


def _t5_bucket(dist):
    exact = N_BUCKETS // 2
    logd = jnp.log(jnp.maximum(dist, 1).astype(F32) / exact) / math.log(MAX_DISTANCE / exact)
    large = jnp.minimum(exact + (logd * (N_BUCKETS - exact)).astype(jnp.int32), N_BUCKETS - 1)
    return jnp.where(dist < exact, dist, large)


def _bias_tables(rel_bias):
    qi = jnp.arange(BLOCK)[:, None]
    ki = jnp.arange(2 * BLOCK)[None, :]
    rel = BLOCK + qi - ki
    tables = []
    for g, (window, dil) in enumerate(DIL_PATTERNS):
        span = window // dil
        bucket = _t5_bucket(jnp.clip(rel, 0, span) * dil)
        tables.append(jnp.transpose(rel_bias[bucket][..., g * DIL_HEADS:(g + 1) * DIL_HEADS], (2, 0, 1)))
    return jnp.stack(tables).astype(F32)


def _mlp_kernel(x_ref, mod_ref, w1_ref, w2_ref, lng_ref, lnb_ref, o_ref, *, shift_row, scale_row, gate_row, ff_chunk):
    m = mod_ref[0]
    x = x_ref[0]
    u = (x * (1.0 + m[scale_row:scale_row + 1]) + m[shift_row:shift_row + 1]).astype(BF16)
    acc = None
    for c in range(D_FF // ff_chunk):
        h = jnp.dot(u, w1_ref[:, c * ff_chunk:(c + 1) * ff_chunk], preferred_element_type=F32)
        h = jnp.maximum(h, 0.0)
        h = (h * h).astype(BF16)
        part = jnp.dot(h, w2_ref[c * ff_chunk:(c + 1) * ff_chunk, :], preferred_element_type=F32)
        acc = part if acc is None else acc + part
    z = DEEPNORM_ALPHA * x + (1.0 + m[gate_row:gate_row + 1]) * acc
    o_ref[0] = _layer_norm_rows(z, lng_ref[...], lnb_ref[...])


def _mlp_ln(x, mod, w1, w2, ln_g, ln_b, *, tm=512, ff_chunk=1024):
    bsz, t, d = x.shape
    return pl.pallas_call(
        functools.partial(_mlp_kernel, shift_row=3, scale_row=4, gate_row=5, ff_chunk=ff_chunk),
        grid=(bsz, t // tm),
        in_specs=[pl.BlockSpec((1, tm, d), lambda b, i: (b, i, 0)),
                  pl.BlockSpec((1, 6, d), lambda b, i: (b, 0, 0)),
                  pl.BlockSpec((d, D_FF), lambda b, i: (0, 0)),
                  pl.BlockSpec((D_FF, d), lambda b, i: (0, 0)),
                  pl.BlockSpec((1, d), lambda b, i: (0, 0)),
                  pl.BlockSpec((1, d), lambda b, i: (0, 0))],
        out_specs=pl.BlockSpec((1, tm, d), lambda b, i: (b, i, 0)),
        out_shape=jax.ShapeDtypeStruct((bsz, t, d), F32),
        compiler_params=_cparams(2),
        name="mlp_ln",
    )(x, mod, w1, w2, ln_g, ln_b)


def _prep_even_weights(w_in, mu, w_up, a_up, g_up):
    w3 = 3 * RW_WIDTH
    lr_end = RW_PROJ
    pad = LR_PAD - (RW_PROJ - w3)
    d = w_in.shape[0]
    w_perm = jnp.concatenate([w_in[:, :w3], w_in[:, lr_end:], w_in[:, w3:lr_end],
                              jnp.zeros((d, pad), w_in.dtype)], axis=1).astype(BF16)
    mu_rkv = mu[:w3].reshape(1, w3)
    mu_lr = jnp.concatenate([mu[w3:], jnp.zeros((pad,), mu.dtype)]).reshape(1, LR_PAD)
    zeros = jnp.zeros((RW_DECAY_RANK, RW_WIDTH), w_up.dtype)
    wwa = jnp.concatenate([jnp.concatenate([w_up, zeros], axis=1),
                           jnp.concatenate([zeros, a_up], axis=1)], axis=0).astype(BF16)
    gup = jnp.concatenate([g_up, jnp.zeros((pad, RW_WIDTH), g_up.dtype)], axis=0).astype(BF16)
    return w_perm, mu_rkv, mu_lr, wwa, gup


def kernel(x, c, ada_w, ada_b, ln_g, ln_b, ab_w_in, rw_mu, rw_w0, rw_w_up, rw_a0, rw_a_up, rw_g_up,
           rw_k_k, rw_k_a, rw_r_k, rw_lnx_g, rw_lnx_b, sc_conv_w, ab_w_out, dil_w_qkv, dil_w_out,
           rel_bias, mlp_w1, mlp_w2):
    bsz, t, d = x.shape
    mods = _adaln(c, ada_w, ada_b).reshape(DEPTH, bsz, 6, d)
    bias = None
    row = lambda a: a.reshape(1, -1)
    for i in range(DEPTH):
        mod = mods[i]
        j = i // 2
        if i % 2 == 0:
            w_perm, mu_rkv, mu_lr, wwa, gup = _prep_even_weights(
                ab_w_in[j], rw_mu[j], rw_w_up[j], rw_a_up[j], rw_g_up[j])
            p = _modulated_proj(x, mod, w_perm, shift_row=0, scale_row=1, out_dtype=F32)
            y = _rwkv_shortconv(p, mu_rkv, mu_lr, row(rw_w0[j]), row(rw_a0[j]), wwa, gup,
                                row(rw_k_k[j]), row(rw_k_a[j]), row(rw_r_k[j]),
                                row(rw_lnx_g[j]), row(rw_lnx_b[j]), sc_conv_w[j])
            x = _out_ln(y, ab_w_out[j].astype(BF16), x, mod, row(ln_g[i, 0]), row(ln_b[i, 0]), gate_row=2)
        else:
            if bias is None:
                bias = _bias_tables(rel_bias)
            p = _modulated_proj(x, mod, dil_w_qkv[j].astype(BF16), shift_row=0, scale_row=1, out_dtype=BF16)
            outs, lses = [], []
            for g, (_, dil) in enumerate(DIL_PATTERNS):
                o, lse = _dilated_attention_group(p, bias[g], g, dil)
                outs.append(o)
                lses.append(lse)
            x = _merge_out_ln(outs, lses, dil_w_out[j].astype(BF16), x, mod,
                              row(ln_g[i, 0]), row(ln_b[i, 0]), gate_row=2)
        x = _mlp_ln(x, mod, mlp_w1[i].astype(BF16), mlp_w2[i].astype(BF16),
                    row(ln_g[i, 1]), row(ln_b[i, 1]))
    return x
```

```python
import functools
import math

import jax
import jax.numpy as jnp
import numpy as np
from jax import lax
from jax.experimental import pallas as pl
from jax.experimental.pallas import tpu as pltpu

F32 = jnp.float32
BF16 = jnp.bfloat16

D_MODEL = 1024
DEPTH = 2
RW_HEADS = 8
RW_HEAD_DIM = 64
RW_WIDTH = 512
RW_DECAY_RANK = 64
RW_ICLR_RANK = 64
RW_GATE_RANK = 160
RW_GN_EPS = 64e-5
RW_PROJ = 3 * RW_WIDTH + RW_DECAY_RANK + RW_ICLR_RANK + RW_GATE_RANK
SC_WIDTH = 512
SC_CONV = 3
DIL_PATTERNS = ((128, 1), (512, 4), (2048, 16))
N_GROUPS = 3
DIL_HEADS = 8
DIL_HEAD_DIM = 64
DIL_WIDTH = 512
BLOCK = 128
N_BUCKETS = 32
MAX_DISTANCE = 2048
D_FF = 4 * D_MODEL
DEEPNORM_ALPHA = (2 * DEPTH) ** 0.25
LN_EPS = 1e-5

LANES = 128
CHUNK = 64
LR_PAD = 384
AB_PROJ_PAD = 3 * RW_WIDTH + 3 * SC_WIDTH + LR_PAD
VMEM_LIMIT = 56 * 1024 * 1024


def _cparams(n_axes):
    return pltpu.CompilerParams(dimension_semantics=("arbitrary",) * n_axes,
                                vmem_limit_bytes=VMEM_LIMIT)


def _sigmoid(x):
    return 1.0 / (1.0 + jnp.exp(-x))


def _dot(a, b):
    return jnp.dot(a.astype(BF16), b.astype(BF16), preferred_element_type=F32)


def _layer_norm_rows(z, g, b):
    mu = jnp.mean(z, axis=-1, keepdims=True)
    zc = z - mu
    var = jnp.mean(zc * zc, axis=-1, keepdims=True)
    return zc * lax.rsqrt(var + LN_EPS) * g + b


def _adaln_kernel(c_ref, w_ref, b_ref, o_ref):
    c = c_ref[...]
    cond = c * _sigmoid(c)
    o_ref[0] = _dot(cond, w_ref[0]) + b_ref[0]


def _adaln(c, ada_w, ada_b):
    depth, d, n = ada_w.shape
    bsz = c.shape[0]
    tn = 1536
    return pl.pallas_call(
        _adaln_kernel,
        grid=(depth, n // tn),
        in_specs=[pl.BlockSpec((bsz, d), lambda i, j: (0, 0)),
                  pl.BlockSpec((1, d, tn), lambda i, j: (i, 0, j)),
                  pl.BlockSpec((1, 1, tn), lambda i, j: (i, 0, j))],
        out_specs=pl.BlockSpec((1, bsz, tn), lambda i, j: (i, 0, j)),
        out_shape=jax.ShapeDtypeStruct((depth, bsz, n), F32),
        compiler_params=_cparams(2),
        name="adaln",
    )(c, ada_w, ada_b.reshape(depth, 1, n))


def _proj_kernel(x_ref, mod_ref, w_ref, o_ref, u_ref, *, shift_row, scale_row):
    @pl.when(pl.program_id(2) == 0)
    def _():
        m = mod_ref[0]
        u = x_ref[0] * (1.0 + m[scale_row:scale_row + 1]) + m[shift_row:shift_row + 1]
        u_ref[...] = u.astype(BF16)

    o_ref[0] = jnp.dot(u_ref[...], w_ref[...], preferred_element_type=F32).astype(o_ref.dtype)


def _modulated_proj(x, mod, w, *, shift_row, scale_row, out_dtype, tm=1024, tn=1152):
    bsz, t, d = x.shape
    n = w.shape[1]
    return pl.pallas_call(
        functools.partial(_proj_kernel, shift_row=shift_row, scale_row=scale_row),
        grid=(bsz, t // tm, n // tn),
        in_specs=[pl.BlockSpec((1, tm, d), lambda b, i, j: (b, i, 0)),
                  pl.BlockSpec((1, 6, d), lambda b, i, j: (b, 0, 0)),
                  pl.BlockSpec((d, tn), lambda b, i, j: (0, j))],
        out_specs=pl.BlockSpec((1, tm, tn), lambda b, i, j: (b, i, j)),
        out_shape=jax.ShapeDtypeStruct((bsz, t, n), out_dtype),
        scratch_shapes=[pltpu.VMEM((tm, d), BF16)],
        compiler_params=_cparams(3),
        name="modulated_proj",
    )(x, mod, w)


def _shift_rows(x, carry, k):
    rolled = pltpu.roll(x, k, 0)
    row = lax.broadcasted_iota(jnp.int32, x.shape, 0)
    out = rolled
    for i in range(k):
        out = jnp.where(row == i, carry[8 - k + i:8 - k + i + 1], out)
    return out


def _rwkv_chunk_terms(a_t, r_t, b_t, k_t, b_e, k_e, v, p_end):
    n = len(a_t)
    idx = range(n)
    row = lax.broadcasted_iota(jnp.int32, (CHUNK, LANES), 0)
    col = lax.broadcasted_iota(jnp.int32, (CHUNK, LANES), 1)
    s_idx = col % CHUNK
    head0 = col < RW_HEAD_DIM
    strict = s_idx < row
    incl = s_idx <= row
    near = strict & ((s_idx // 16) == (row // 16))
    far = strict & ((s_idx // 16) != (row // 16))
    eye = jnp.where(s_idx == row, 1.0, 0.0)
    r2 = lax.broadcasted_iota(jnp.int32, (LANES, 2 * LANES), 0)
    c2 = lax.broadcasted_iota(jnp.int32, (LANES, 2 * LANES), 1)
    bdmask2 = (r2 // RW_HEAD_DIM) == ((c2 % LANES) // RW_HEAD_DIM)
    eye_bd = jnp.where(lax.broadcasted_iota(jnp.int32, (LANES, LANES), 0)
                       == lax.broadcasted_iota(jnp.int32, (LANES, LANES), 1), 1.0, 0.0)
    zero16 = jnp.zeros((CHUNK, LANES), BF16)
    lane1 = lax.broadcasted_iota(jnp.int32, (1, LANES), 1)
    m0_16 = jnp.where(lane1 < RW_HEAD_DIM, 1.0, 0.0).astype(BF16)
    m1_16 = jnp.where(lane1 < RW_HEAD_DIM, 0.0, 1.0).astype(BF16)

    def cast(xs):
        return [x.astype(BF16) for x in xs]

    def bd(q16):
        return jnp.concatenate([q16 * m0_16, q16 * m1_16], axis=0)

    def mm(p16, q16):
        return jnp.dot(p16, q16, preferred_element_type=F32)

    def mm_nt(p16, q16):
        return lax.dot_general(p16, q16, (((1,), (1,)), ((), ())), preferred_element_type=F32)

    a16, r16, v16 = cast(a_t), cast(r_t), cast(v)
    lhs = [jnp.concatenate([a16[i], r16[i]], axis=0) for i in idx]
    bdb = [bd(x) for x in cast(b_t)]
    bdk = [bd(x) for x in cast(k_t)]
    bdv = [bd(x) for x in v16]
    g_b = [mm_nt(lhs[i], bdb[i]) for i in idx]
    g_k = [mm_nt(lhs[i], bdk[i]) for i in idx]
    l_d = [jnp.where(near, g_b[i][:CHUNK], 0.0) for i in idx]
    l_d16 = cast(l_d)
    l_e16 = [jnp.where(far, g_b[i][:CHUNK], 0.0).astype(BF16) for i in idx]
    a_ak16 = [jnp.where(strict, g_k[i][:CHUNK], 0.0).astype(BF16) for i in idx]
    a_rb16 = [jnp.where(incl, g_b[i][CHUNK:], 0.0).astype(BF16) for i in idx]
    a_rk16 = [jnp.where(incl, g_k[i][CHUNK:], 0.0).astype(BF16) for i in idx]

    bd_ld = [bd(x) for x in l_d16]
    l2_16 = cast([mm(l_d16[i], bd_ld[i]) for i in idx])
    bd_l2 = [bd(x) for x in l2_16]
    l4_16 = cast([mm(l2_16[i], bd_l2[i]) for i in idx])
    bd_l4 = [bd(x) for x in l4_16]
    l8_16 = cast([mm(l4_16[i], bd_l4[i]) for i in idx])
    bd_l8 = [bd(x) for x in l8_16]
    dinv = [eye + l_d[i] for i in idx]
    dinv = [dinv[i] + mm(dinv[i].astype(BF16), bd_l2[i]) for i in idx]
    dinv = [dinv[i] + mm(dinv[i].astype(BF16), bd_l4[i]) for i in idx]
    dinv = [dinv[i] + mm(dinv[i].astype(BF16), bd_l8[i]) for i in idx]
    dinv16 = cast(dinv)
    f1 = [mm(dinv16[i], bd(l_e16[i])) for i in idx]
    f1_16 = cast(f1)
    bd_f1 = [bd(x) for x in f1_16]
    f2_16 = cast([mm(f1_16[i], bd_f1[i]) for i in idx])
    g = [eye + f1[i] for i in idx]
    g = [g[i] + mm(g[i].astype(BF16), bd(f2_16[i])) for i in idx]
    tinv16 = cast([mm(g[i].astype(BF16), bd(dinv16[i])) for i in idx])

    wv16 = cast([mm(a_ak16[i], bdv[i]) for i in idx])
    aw = [mm(tinv16[i], jnp.concatenate([bd(a16[i]), bd(wv16[i])], axis=1)) for i in idx]
    aw16 = cast(aw)
    qy = [mm(a_rb16[i], jnp.concatenate([bd(aw16[i][:, :LANES]), bd(aw16[i][:, LANES:])], axis=1))
          + jnp.concatenate([r_t[i], mm(a_rk16[i], bdv[i])], axis=1) for i in idx]
    be_ke_t = [jnp.concatenate([b_e[i], k_e[i]], axis=0).T.astype(BF16) for i in idx]
    rhs = [jnp.concatenate([aw16[i], jnp.concatenate([zero16, v16[i]], axis=1)], axis=0) for i in idx]
    mn = [jnp.where(bdmask2, mm(be_ke_t[i], rhs[i]), 0.0) for i in idx]
    qhat = [x[:, :LANES] for x in qy]
    yhat = [x[:, LANES:] for x in qy]
    mmat = [eye_bd * p_end[i] + mn[i][:, :LANES] for i in idx]
    nmat = [x[:, LANES:] for x in mn]
    return qhat, yhat, mmat, nmat


def _rwkv_kernel(rkv_ref, cv_ref, lr_ref, mu_rkv_ref, mu_lr_ref, w0_ref, a0_ref, wwa_ref, gup_ref,
                 kk_ref, ka_ref, rk_ref, lnxg_ref, lnxb_ref, convw_ref, o_ref,
                 carry_rkv, carry_lr, carry_z, state, yout_s, *, tb):
    n_chunks = tb // CHUNK
    n_pairs = RW_WIDTH // LANES

    @pl.when(pl.program_id(1) == 0)
    def _():
        carry_rkv[...] = jnp.zeros_like(carry_rkv)
        carry_lr[...] = jnp.zeros_like(carry_lr)
        carry_z[...] = jnp.zeros_like(carry_z)
        state[...] = jnp.zeros_like(state)

    rkv = rkv_ref[0]
    lr = lr_ref[0]
    rkv_m = rkv + mu_rkv_ref[...] * (_shift_rows(rkv, carry_rkv[...], 1) - rkv)
    lr_m = lr + mu_lr_ref[...] * (_shift_rows(lr, carry_lr[...], 1) - lr)
    carry_rkv[...] = rkv[tb - 8:]
    carry_lr[...] = lr[tb - 8:]

    r = rkv_m[:, :RW_WIDTH]
    k = rkv_m[:, RW_WIDTH:2 * RW_WIDTH]
    v = rkv_m[:, 2 * RW_WIDTH:]

    wa = lr_m[:, :LANES]
    lane = lax.broadcasted_iota(jnp.int32, wa.shape, 1)
    wa = jnp.where(lane < RW_DECAY_RANK, jnp.tanh(wa), wa)
    wa_up = _dot(wa, wwa_ref[...])
    ld = -math.exp(-0.5) * _sigmoid(w0_ref[...] + wa_up[:, :RW_WIDTH])
    iclr = _sigmoid(a0_ref[...] + wa_up[:, RW_WIDTH:])
    gate = _dot(_sigmoid(lr_m[:, LANES:]), gup_ref[...])

    hr = lax.broadcasted_iota(jnp.int32, (RW_WIDTH, RW_WIDTH), 0) // RW_HEAD_DIM
    hc = lax.broadcasted_iota(jnp.int32, (RW_WIDTH, RW_WIDTH), 1) // RW_HEAD_DIM
    seg = jnp.where(hr == hc, 1.0 / RW_HEAD_DIM, 0.0).astype(BF16)

    def split3(x):
        hi = x.astype(BF16)
        rem = x - hi.astype(F32)
        mid = rem.astype(BF16)
        return hi, mid, (rem - mid.astype(F32)).astype(BF16)

    def seg_mean(x):
        hi, lo, _ = split3(x)
        return (jnp.dot(hi, seg, preferred_element_type=F32)
                + jnp.dot(lo, seg, preferred_element_type=F32))

    kk = k * kk_ref[...]
    ss = seg_mean(kk * kk) * RW_HEAD_DIM
    kk = kk * lax.rsqrt(jnp.maximum(ss, 1e-24))
    kh = k * (1.0 + (iclr - 1.0) * ka_ref[...])

    tr = lax.broadcasted_iota(jnp.int32, (tb, tb), 0)
    tc = lax.broadcasted_iota(jnp.int32, (tb, tb), 1)
    tri = jnp.where((tc <= tr) & (tc // CHUNK == tr // CHUNK), 1.0, 0.0).astype(BF16)
    lp = sum(jnp.dot(tri, part, preferred_element_type=F32) for part in split3(ld))

    a_t, r_t, b_t, k_t, b_e, k_e, vs, p_end = [], [], [], [], [], [], [], []
    b = kk * iclr
    for c in range(n_chunks):
        rs = slice(c * CHUNK, (c + 1) * CHUNK)
        lp_c = lp[rs]
        lp_last = lp_c[CHUNK - 1:CHUNK]
        e_in = jnp.exp(lp_c)
        e_inv = jnp.exp(-lp_c)
        e_end = jnp.exp(lp_last - lp_c)
        a_c = -kk[rs] * jnp.exp(lp_c - ld[rs])
        r_c = r[rs] * e_in
        b_c = b[rs] * e_inv
        k_c = kh[rs] * e_inv
        be_c = b[rs] * e_end
        ke_c = kh[rs] * e_end
        pe_c = jnp.exp(lp_last)
        for p in range(n_pairs):
            cs = slice(p * LANES, (p + 1) * LANES)
            a_t.append(a_c[:, cs])
            r_t.append(r_c[:, cs])
            b_t.append(b_c[:, cs])
            k_t.append(k_c[:, cs])
            b_e.append(be_c[:, cs])
            k_e.append(ke_c[:, cs])
            vs.append(v[rs, cs])
            p_end.append(pe_c[:, cs])

    qhat, yhat, mmat, nmat = _rwkv_chunk_terms(a_t, r_t, b_t, k_t, b_e, k_e, vs, p_end)

    s = [state[p] for p in range(n_pairs)]
    for c in range(n_chunks):
        ym = [_dot(jnp.concatenate([qhat[c * n_pairs + p], mmat[c * n_pairs + p]], axis=0), s[p])
              for p in range(n_pairs)]
        for p in range(n_pairs):
            i = c * n_pairs + p
            yout_s[c * CHUNK:(c + 1) * CHUNK, p * LANES:(p + 1) * LANES] = ym[p][:CHUNK] + yhat[i]
            s[p] = ym[p][CHUNK:] + nmat[i]
    for p in range(n_pairs):
        state[p] = s[p]

    y = yout_s[...]
    mean = seg_mean(y)
    yc = y - mean
    var = seg_mean(yc * yc)
    y = yc * lax.rsqrt(var + RW_GN_EPS) * lnxg_ref[...] + lnxb_ref[...]
    bonus = seg_mean(r * kh * rk_ref[...]) * RW_HEAD_DIM * v
    o_ref[0, :, :RW_WIDTH] = ((y + bonus) * gate).astype(o_ref.dtype)

    cv = cv_ref[0]
    h = cv[:, :SC_WIDTH]
    b_gate = cv[:, SC_WIDTH:2 * SC_WIDTH]
    z = cv[:, 2 * SC_WIDTH:] * h
    zc = carry_z[...]
    cw = convw_ref[...]
    conv = cw[2:3] * z + cw[1:2] * _shift_rows(z, zc, 1) + cw[0:1] * _shift_rows(z, zc, 2)
    carry_z[...] = z[tb - 8:]
    o_ref[0, :, RW_WIDTH:] = (b_gate * conv).astype(o_ref.dtype)


def _rwkv_shortconv(p, mu_rkv, mu_lr, w0, a0, wwa, gup, k_k, k_a, r_k, lnx_g, lnx_b, conv_w, *, tb=256):
    bsz, t, _ = p.shape
    n_pairs = RW_WIDTH // LANES
    w3 = 3 * RW_WIDTH

    def full(shape):
        return pl.BlockSpec(shape, lambda b, i: (0,) * len(shape))

    return pl.pallas_call(
        functools.partial(_rwkv_kernel, tb=tb),
        grid=(bsz, t // tb),
        in_specs=[pl.BlockSpec((1, tb, w3), lambda b, i: (b, i, 0)),
                  pl.BlockSpec((1, tb, w3), lambda b, i: (b, i, 1)),
                  pl.BlockSpec((1, tb, LR_PAD), lambda b, i: (b, i, 2 * w3 // LR_PAD)),
                  full((1, w3)), full((1, LR_PAD)), full((1, RW_WIDTH)), full((1, RW_WIDTH)),
                  full((LANES, 2 * RW_WIDTH)), full((LR_PAD - LANES, RW_WIDTH)),
                  full((1, RW_WIDTH)), full((1, RW_WIDTH)), full((1, RW_WIDTH)),
                  full((1, RW_WIDTH)), full((1, RW_WIDTH)), full((SC_CONV, SC_WIDTH))],
        out_specs=pl.BlockSpec((1, tb, RW_WIDTH + SC_WIDTH), lambda b, i: (b, i, 0)),
        out_shape=jax.ShapeDtypeStruct((bsz, t, RW_WIDTH + SC_WIDTH), BF16),
        scratch_shapes=[pltpu.VMEM((8, w3), F32), pltpu.VMEM((8, LR_PAD), F32), pltpu.VMEM((8, SC_WIDTH), F32),
                        pltpu.VMEM((n_pairs, LANES, LANES), F32),
                        pltpu.VMEM((tb, RW_WIDTH), F32)],
        compiler_params=_cparams(2),
        name="rwkv_shortconv",
    )(p, p, p, mu_rkv, mu_lr, w0, a0, wwa, gup, k_k, k_a, r_k, lnx_g, lnx_b, conv_w)


def _out_ln_kernel(y_ref, w_ref, x_ref, mod_ref, lng_ref, lnb_ref, o_ref, *, gate_row):
    y = jnp.dot(y_ref[0], w_ref[...], preferred_element_type=F32)
    g = mod_ref[0][gate_row:gate_row + 1]
    z = DEEPNORM_ALPHA * x_ref[0] + (1.0 + g) * y
    o_ref[0] = _layer_norm_rows(z, lng_ref[...], lnb_ref[...])


def _out_ln(y, w, x, mod, ln_g, ln_b, *, gate_row, tm=512):
    bsz, t, d = x.shape
    kdim = y.shape[-1]
    return pl.pallas_call(
        functools.partial(_out_ln_kernel, gate_row=gate_row),
        grid=(bsz, t // tm),
        in_specs=[pl.BlockSpec((1, tm, kdim), lambda b, i: (b, i, 0)),
                  pl.BlockSpec((kdim, d), lambda b, i: (0, 0)),
                  pl.BlockSpec((1, tm, d), lambda b, i: (b, i, 0)),
                  pl.BlockSpec((1, 6, d), lambda b, i: (b, 0, 0)),
                  pl.BlockSpec((1, d), lambda b, i: (0, 0)),
                  pl.BlockSpec((1, d), lambda b, i: (0, 0))],
        out_specs=pl.BlockSpec((1, tm, d), lambda b, i: (b, i, 0)),
        out_shape=jax.ShapeDtypeStruct((bsz, t, d), F32),
        compiler_params=_cparams(2),
        name="out_ln",
    )(y, w, x, mod, ln_g, ln_b)


def _attn_kernel(q_ref, kp_ref, kc_ref, vp_ref, vc_ref, bias_ref, o_ref, lse_ref):
    n = pl.program_id(2)
    qi = lax.broadcasted_iota(jnp.int32, (BLOCK, 2 * BLOCK), 0)
    ki = lax.broadcasted_iota(jnp.int32, (BLOCK, 2 * BLOCK), 1)
    rel = BLOCK + qi - ki
    valid = (rel >= 0) & (rel <= BLOCK) & ((n > 0) | (ki >= BLOCK))
    lane = lax.broadcasted_iota(jnp.int32, (1, LANES), 1)
    scale = DIL_HEAD_DIM ** -0.5
    qmask = (jnp.where(lane < DIL_HEAD_DIM, scale, 0.0).astype(BF16),
             jnp.where(lane >= DIL_HEAD_DIM, scale, 0.0).astype(BF16))
    lane_o = lax.broadcasted_iota(jnp.int32, (BLOCK, LANES), 1)
    lse_all = jnp.zeros((BLOCK, LANES), F32)
    for p in range(DIL_WIDTH // LANES):
        cs = slice(p * LANES, (p + 1) * LANES)
        q = q_ref[0, :, cs]
        kcat = jnp.concatenate([kp_ref[0, :, cs], kc_ref[0, :, cs]], axis=0)
        vcat = jnp.concatenate([vp_ref[0, :, cs], vc_ref[0, :, cs]], axis=0)
        outs = []
        for hh in range(2):
            h = 2 * p + hh
            s = lax.dot_general(q * qmask[hh], kcat, (((1,), (1,)), ((), ())),
                                preferred_element_type=F32) + bias_ref[h]
            s = jnp.where(valid, s, -jnp.inf)
            m = jnp.max(s, axis=-1, keepdims=True)
            e = jnp.exp(s - m)
            den = jnp.sum(e, axis=-1, keepdims=True)
            outs.append(jnp.dot(e.astype(BF16), vcat, preferred_element_type=F32) / den)
            lse_all = jnp.where(lane_o == h, m + jnp.log(den), lse_all)
        o_ref[0, :, cs] = jnp.where(lane_o < DIL_HEAD_DIM, outs[0], outs[1])
    lse_ref[0] = lse_all


def _dilated_attention_group(p, bias, g, dil):
    bsz, t, n = p.shape
    length = t // dil
    nb = length // BLOCK
    pv = p.reshape(bsz, length, dil * n)
    cpb = n // DIL_WIDTH

    def col(j):
        return lambda b, r, i: (b, i, r * cpb + g * 3 + j)

    def col_prev(j):
        return lambda b, r, i: (b, jnp.maximum(i - 1, 0), r * cpb + g * 3 + j)

    blk = (1, BLOCK, DIL_WIDTH)
    o, lse = pl.pallas_call(
        _attn_kernel,
        grid=(bsz, dil, nb),
        in_specs=[pl.BlockSpec(blk, col(0)),
                  pl.BlockSpec(blk, col_prev(1)), pl.BlockSpec(blk, col(1)),
                  pl.BlockSpec(blk, col_prev(2)), pl.BlockSpec(blk, col(2)),
                  pl.BlockSpec((DIL_HEADS, BLOCK, 2 * BLOCK), lambda b, r, i: (0, 0, 0))],
        out_specs=[pl.BlockSpec(blk, lambda b, r, i: (b, i, r)),
                   pl.BlockSpec((1, BLOCK, LANES), lambda b, r, i: (b, i, r))],
        out_shape=[jax.ShapeDtypeStruct((bsz, length, dil * DIL_WIDTH), F32),
                   jax.ShapeDtypeStruct((bsz, length, dil * LANES), F32)],
        compiler_params=_cparams(3),
        name=f"dilated_attn_g{g}",
    )(pv, pv, pv, pv, pv, bias)
    return o.reshape(bsz, t, DIL_WIDTH), lse.reshape(bsz, t, LANES)


def _merge_out_ln_kernel(o0_ref, o1_ref, o2_ref, l0_ref, l1_ref, l2_ref, w_ref, x_ref, mod_ref,
                         lng_ref, lnb_ref, o_ref, *, gate_row):
    l0, l1, l2 = l0_ref[0], l1_ref[0], l2_ref[0]
    m = jnp.maximum(jnp.maximum(l0, l1), l2)
    e0, e1, e2 = jnp.exp(l0 - m), jnp.exp(l1 - m), jnp.exp(l2 - m)
    inv = 1.0 / (e0 + e1 + e2)
    hr = lax.broadcasted_iota(jnp.int32, (LANES, DIL_WIDTH), 0)
    hc = lax.broadcasted_iota(jnp.int32, (LANES, DIL_WIDTH), 1) // DIL_HEAD_DIM
    expand = jnp.where(hr == hc, 1.0, 0.0).astype(BF16)

    def widen(wt):
        hi = wt.astype(BF16)
        lo = (wt - hi.astype(F32)).astype(BF16)
        return (jnp.dot(hi, expand, preferred_element_type=F32)
                + jnp.dot(lo, expand, preferred_element_type=F32))

    merged = widen(e0 * inv) * o0_ref[0] + widen(e1 * inv) * o1_ref[0] + widen(e2 * inv) * o2_ref[0]
    y = jnp.dot(merged.astype(BF16), w_ref[...], preferred_element_type=F32)
    g = mod_ref[0][gate_row:gate_row + 1]
    z = DEEPNORM_ALPHA * x_ref[0] + (1.0 + g) * y
    o_ref[0] = _layer_norm_rows(z, lng_ref[...], lnb_ref[...])


def _merge_out_ln(outs, lses, w, x, mod, ln_g, ln_b, *, gate_row, tm=512):
    bsz, t, d = x.shape
    ospec = pl.BlockSpec((1, tm, DIL_WIDTH), lambda b, i: (b, i, 0))
    lspec = pl.BlockSpec((1, tm, LANES), lambda b, i: (b, i, 0))
    return pl.pallas_call(
        functools.partial(_merge_out_ln_kernel, gate_row=gate_row),
        grid=(bsz, t // tm),
        in_specs=[ospec, ospec, ospec, lspec, lspec, lspec,
                  pl.BlockSpec((DIL_WIDTH, d), lambda b, i: (0, 0)),
                  pl.BlockSpec((1, tm, d), lambda b, i: (b, i, 0)),
                  pl.BlockSpec((1, 6, d), lambda b, i: (b, 0, 0)),
                  pl.BlockSpec((1, d), lambda b, i: (0, 0)),
                  pl.BlockSpec((1, d), lambda b, i: (0, 0))],
        out_specs=pl.BlockSpec((1, tm, d), lambda b, i: (b, i, 0)),
        out_shape=jax.ShapeDtypeStruct((bsz, t, d), F32),
        compiler_params=_cparams(2),
        name="merge_out_ln",
    )(*outs, *lses, w, x, mod, ln_g, ln_b)


def _t5_bucket(dist):
    exact = N_BUCKETS // 2
    logd = jnp.log(jnp.maximum(dist, 1).astype(F32) / exact) / math.log(MAX_DISTANCE / exact)
    large = jnp.minimum(exact + (logd * (N_BUCKETS - exact)).astype(jnp.int32), N_BUCKETS - 1)
    return jnp.where(dist < exact, dist, large)


def _bias_tables(rel_bias):
    qi = jnp.arange(BLOCK)[:, None]
    ki = jnp.arange(2 * BLOCK)[None, :]
    rel = BLOCK + qi - ki
    tables = []
    for g, (window, dil) in enumerate(DIL_PATTERNS):
        span = window // dil
        bucket = _t5_bucket(jnp.clip(rel, 0, span) * dil)
        tables.append(jnp.transpose(rel_bias[bucket][..., g * DIL_HEADS:(g + 1) * DIL_HEADS], (2, 0, 1)))
    return jnp.stack(tables).astype(F32)


def _mlp_kernel(x_ref, mod_ref, w1_ref, w2_ref, lng_ref, lnb_ref, o_ref, *, shift_row, scale_row, gate_row, ff_chunk):
    m = mod_ref[0]
    x = x_ref[0]
    u = (x * (1.0 + m[scale_row:scale_row + 1]) + m[shift_row:shift_row + 1]).astype(BF16)
    acc = None
    for c in range(D_FF // ff_chunk):
        h = jnp.dot(u, w1_ref[:, c * ff_chunk:(c + 1) * ff_chunk], preferred_element_type=F32)
        h = jnp.maximum(h, 0.0)
        h = (h * h).astype(BF16)
        part = jnp.dot(h, w2_ref[c * ff_chunk:(c + 1) * ff_chunk, :], preferred_element_type=F32)
        acc = part if acc is None else acc + part
    z = DEEPNORM_ALPHA * x + (1.0 + m[gate_row:gate_row + 1]) * acc
    o_ref[0] = _layer_norm_rows(z, lng_ref[...], lnb_ref[...])


def _mlp_ln(x, mod, w1, w2, ln_g, ln_b, *, tm=512, ff_chunk=1024):
    bsz, t, d = x.shape
    return pl.pallas_call(
        functools.partial(_mlp_kernel, shift_row=3, scale_row=4, gate_row=5, ff_chunk=ff_chunk),
        grid=(bsz, t // tm),
        in_specs=[pl.BlockSpec((1, tm, d), lambda b, i: (b, i, 0)),
                  pl.BlockSpec((1, 6, d), lambda b, i: (b, 0, 0)),
                  pl.BlockSpec((d, D_FF), lambda b, i: (0, 0)),
                  pl.BlockSpec((D_FF, d), lambda b, i: (0, 0)),
                  pl.BlockSpec((1, d), lambda b, i: (0, 0)),
                  pl.BlockSpec((1, d), lambda b, i: (0, 0))],
        out_specs=pl.BlockSpec((1, tm, d), lambda b, i: (b, i, 0)),
        out_shape=jax.ShapeDtypeStruct((bsz, t, d), F32),
        compiler_params=_cparams(2),
        name="mlp_ln",
    )(x, mod, w1, w2, ln_g, ln_b)


def _prep_even_weights(w_in, mu, w_up, a_up, g_up):
    w3 = 3 * RW_WIDTH
    lr_end = RW_PROJ
    pad = LR_PAD - (RW_PROJ - w3)
    d = w_in.shape[0]
    w_perm = jnp.concatenate([w_in[:, :w3], w_in[:, lr_end:], w_in[:, w3:lr_end],
                              jnp.zeros((d, pad), w_in.dtype)], axis=1).astype(BF16)
    mu_rkv = mu[:w3].reshape(1, w3)
    mu_lr = jnp.concatenate([mu[w3:], jnp.zeros((pad,), mu.dtype)]).reshape(1, LR_PAD)
    zeros = jnp.zeros((RW_DECAY_RANK, RW_WIDTH), w_up.dtype)
    wwa = jnp.concatenate([jnp.concatenate([w_up, zeros], axis=1),
                           jnp.concatenate([zeros, a_up], axis=1)], axis=0).astype(BF16)
    gup = jnp.concatenate([g_up, jnp.zeros((pad, RW_WIDTH), g_up.dtype)], axis=0).astype(BF16)
    return w_perm, mu_rkv, mu_lr, wwa, gup


def kernel(x, c, ada_w, ada_b, ln_g, ln_b, ab_w_in, rw_mu, rw_w0, rw_w_up, rw_a0, rw_a_up, rw_g_up,
           rw_k_k, rw_k_a, rw_r_k, rw_lnx_g, rw_lnx_b, sc_conv_w, ab_w_out, dil_w_qkv, dil_w_out,
           rel_bias, mlp_w1, mlp_w2):
    bsz, t, d = x.shape
    mods = _adaln(c, ada_w, ada_b).reshape(DEPTH, bsz, 6, d)
    bias = None
    row = lambda a: a.reshape(1, -1)
    for i in range(DEPTH):
        mod = mods[i]
        j = i // 2
        if i % 2 == 0:
            w_perm, mu_rkv, mu_lr, wwa, gup = _prep_even_weights(
                ab_w_in[j], rw_mu[j], rw_w_up[j], rw_a_up[j], rw_g_up[j])
            p = _modulated_proj(x, mod, w_perm, shift_row=0, scale_row=1, out_dtype=F32)
            y = _rwkv_shortconv(p, mu_rkv, mu_lr, row(rw_w0[j]), row(rw_a0[j]), wwa, gup,
                                row(rw_k_k[j]), row(rw_k_a[j]), row(rw_r_k[j]),
                                row(rw_lnx_g[j]), row(rw_lnx_b[j]), sc_conv_w[j])
            x = _out_ln(y, ab_w_out[j].astype(BF16), x, mod, row(ln_g[i, 0]), row(ln_b[i, 0]), gate_row=2)
        else:
            if bias is None:
                bias = _bias_tables(rel_bias)
            p = _modulated_proj(x, mod, dil_w_qkv[j].astype(BF16), shift_row=0, scale_row=1, out_dtype=BF16)
            outs, lses = [], []
            for g, (_, dil) in enumerate(DIL_PATTERNS):
                o, lse = _dilated_attention_group(p, bias[g], g, dil)
                outs.append(o)
                lses.append(lse)
            x = _merge_out_ln(outs, lses, dil_w_out[j].astype(BF16), x, mod,
                              row(ln_g[i, 0]), row(ln_b[i, 0]), gate_row=2)
        x = _mlp_ln(x, mod, mlp_w1[i].astype(BF16), mlp_w2[i].astype(BF16),
                    row(ln_g[i, 1]), row(ln_b[i, 1]))
    return x
```

```python
import functools
import math

import jax
import jax.numpy as jnp
import numpy as np
from jax import lax
from jax.experimental import pallas as pl
from jax.experimental.pallas import tpu as pltpu

F32 = jnp.float32
BF16 = jnp.bfloat16

D_MODEL = 1024
DEPTH = 2
RW_HEADS = 8
RW_HEAD_DIM = 64
RW_WIDTH = 512
RW_DECAY_RANK = 64
RW_ICLR_RANK = 64
RW_GATE_RANK = 160
RW_GN_EPS = 64e-5
RW_PROJ = 3 * RW_WIDTH + RW_DECAY_RANK + RW_ICLR_RANK + RW_GATE_RANK
SC_WIDTH = 512
SC_CONV = 3
DIL_PATTERNS = ((128, 1), (512, 4), (2048, 16))
N_GROUPS = 3
DIL_HEADS = 8
DIL_HEAD_DIM = 64
DIL_WIDTH = 512
BLOCK = 128
N_BUCKETS = 32
MAX_DISTANCE = 2048
D_FF = 4 * D_MODEL
DEEPNORM_ALPHA = (2 * DEPTH) ** 0.25
LN_EPS = 1e-5

LANES = 128
CHUNK = 64
LR_PAD = 384
AB_PROJ_PAD = 3 * RW_WIDTH + 3 * SC_WIDTH + LR_PAD
VMEM_LIMIT = 56 * 1024 * 1024


def _cparams(n_axes):
    return pltpu.CompilerParams(dimension_semantics=("arbitrary",) * n_axes,
                                vmem_limit_bytes=VMEM_LIMIT)


def _sigmoid(x):
    return 1.0 / (1.0 + jnp.exp(-x))


def _dot(a, b):
    return jnp.dot(a.astype(BF16), b.astype(BF16), preferred_element_type=F32)


def _layer_norm_rows(z, g, b):
    mu = jnp.mean(z, axis=-1, keepdims=True)
    zc = z - mu
    var = jnp.mean(zc * zc, axis=-1, keepdims=True)
    return zc * lax.rsqrt(var + LN_EPS) * g + b


def _adaln_kernel(c_ref, w_ref, b_ref, o_ref):
    c = c_ref[...]
    cond = c * _sigmoid(c)
    o_ref[0] = _dot(cond, w_ref[0]) + b_ref[0]


def _adaln(c, ada_w, ada_b):
    depth, d, n = ada_w.shape
    bsz = c.shape[0]
    tn = 1536
    return pl.pallas_call(
        _adaln_kernel,
        grid=(depth, n // tn),
        in_specs=[pl.BlockSpec((bsz, d), lambda i, j: (0, 0)),
                  pl.BlockSpec((1, d, tn), lambda i, j: (i, 0, j)),
                  pl.BlockSpec((1, 1, tn), lambda i, j: (i, 0, j))],
        out_specs=pl.BlockSpec((1, bsz, tn), lambda i, j: (i, 0, j)),
        out_shape=jax.ShapeDtypeStruct((depth, bsz, n), F32),
        compiler_params=_cparams(2),
        name="adaln",
    )(c, ada_w, ada_b.reshape(depth, 1, n))


def _proj_kernel(x_ref, mod_ref, w_ref, o_ref, u_ref, *, shift_row, scale_row):
    @pl.when(pl.program_id(2) == 0)
    def _():
        m = mod_ref[0]
        u = x_ref[0] * (1.0 + m[scale_row:scale_row + 1]) + m[shift_row:shift_row + 1]
        u_ref[...] = u.astype(BF16)

    o_ref[0] = jnp.dot(u_ref[...], w_ref[...], preferred_element_type=F32).astype(o_ref.dtype)


def _modulated_proj(x, mod, w, *, shift_row, scale_row, out_dtype, tm=1024, tn=1152):
    bsz, t, d = x.shape
    n = w.shape[1]
    return pl.pallas_call(
        functools.partial(_proj_kernel, shift_row=shift_row, scale_row=scale_row),
        grid=(bsz, t // tm, n // tn),
        in_specs=[pl.BlockSpec((1, tm, d), lambda b, i, j: (b, i, 0)),
                  pl.BlockSpec((1, 6, d), lambda b, i, j: (b, 0, 0)),
                  pl.BlockSpec((d, tn), lambda b, i, j: (0, j))],
        out_specs=pl.BlockSpec((1, tm, tn), lambda b, i, j: (b, i, j)),
        out_shape=jax.ShapeDtypeStruct((bsz, t, n), out_dtype),
        scratch_shapes=[pltpu.VMEM((tm, d), BF16)],
        compiler_params=_cparams(3),
        name="modulated_proj",
    )(x, mod, w)


def _shift_rows(x, carry, k):
    rolled = pltpu.roll(x, k, 0)
    row = lax.broadcasted_iota(jnp.int32, x.shape, 0)
    out = rolled
    for i in range(k):
        out = jnp.where(row == i, carry[8 - k + i:8 - k + i + 1], out)
    return out


def _rwkv_chunk_terms(a_t, r_t, b_t, k_t, b_e, k_e, v, p_end):
    n = len(a_t)
    idx = range(n)
    row = lax.broadcasted_iota(jnp.int32, (CHUNK, LANES), 0)
    col = lax.broadcasted_iota(jnp.int32, (CHUNK, LANES), 1)
    s_idx = col % CHUNK
    head0 = col < RW_HEAD_DIM
    strict = s_idx < row
    incl = s_idx <= row
    near = strict & ((s_idx // 16) == (row // 16))
    far = strict & ((s_idx // 16) != (row // 16))
    eye = jnp.where(s_idx == row, 1.0, 0.0)
    r2 = lax.broadcasted_iota(jnp.int32, (LANES, 2 * LANES), 0)
    c2 = lax.broadcasted_iota(jnp.int32, (LANES, 2 * LANES), 1)
    bdmask2 = (r2 // RW_HEAD_DIM) == ((c2 % LANES) // RW_HEAD_DIM)
    eye_bd = jnp.where(lax.broadcasted_iota(jnp.int32, (LANES, LANES), 0)
                       == lax.broadcasted_iota(jnp.int32, (LANES, LANES), 1), 1.0, 0.0)
    zero16 = jnp.zeros((CHUNK, LANES), BF16)
    lane1 = lax.broadcasted_iota(jnp.int32, (1, LANES), 1)
    m0_16 = jnp.where(lane1 < RW_HEAD_DIM, 1.0, 0.0).astype(BF16)
    m1_16 = jnp.where(lane1 < RW_HEAD_DIM, 0.0, 1.0).astype(BF16)

    def cast(xs):
        return [x.astype(BF16) for x in xs]

    def bd(q16):
        return jnp.concatenate([q16 * m0_16, q16 * m1_16], axis=0)

    def mm(p16, q16):
        return jnp.dot(p16, q16, preferred_element_type=F32)

    def mm_nt(p16, q16):
        return lax.dot_general(p16, q16, (((1,), (1,)), ((), ())), preferred_element_type=F32)

    a16, r16, v16 = cast(a_t), cast(r_t), cast(v)
    lhs = [jnp.concatenate([a16[i], r16[i]], axis=0) for i in idx]
    bdb = [bd(x) for x in cast(b_t)]
    bdk = [bd(x) for x in cast(k_t)]
    bdv = [bd(x) for x in v16]
    g_b = [mm_nt(lhs[i], bdb[i]) for i in idx]
    g_k = [mm_nt(lhs[i], bdk[i]) for i in idx]
    l_d = [jnp.where(near, g_b[i][:CHUNK], 0.0) for i in idx]
    l_d16 = cast(l_d)
    l_e16 = [jnp.where(far, g_b[i][:CHUNK], 0.0).astype(BF16) for i in idx]
    a_ak16 = [jnp.where(strict, g_k[i][:CHUNK], 0.0).astype(BF16) for i in idx]
    a_rb16 = [jnp.where(incl, g_b[i][CHUNK:], 0.0).astype(BF16) for i in idx]
    a_rk16 = [jnp.where(incl, g_k[i][CHUNK:], 0.0).astype(BF16) for i in idx]

    bd_ld = [bd(x) for x in l_d16]
    l2_16 = cast([mm(l_d16[i], bd_ld[i]) for i in idx])
    bd_l2 = [bd(x) for x in l2_16]
    l4_16 = cast([mm(l2_16[i], bd_l2[i]) for i in idx])
    bd_l4 = [bd(x) for x in l4_16]
    l8_16 = cast([mm(l4_16[i], bd_l4[i]) for i in idx])
    bd_l8 = [bd(x) for x in l8_16]
    dinv = [eye + l_d[i] for i in idx]
    dinv = [dinv[i] + mm(dinv[i].astype(BF16), bd_l2[i]) for i in idx]
    dinv = [dinv[i] + mm(dinv[i].astype(BF16), bd_l4[i]) for i in idx]
    dinv = [dinv[i] + mm(dinv[i].astype(BF16), bd_l8[i]) for i in idx]
    dinv16 = cast(dinv)
    f1 = [mm(dinv16[i], bd(l_e16[i])) for i in idx]
    f1_16 = cast(f1)
    bd_f1 = [bd(x) for x in f1_16]
    f2_16 = cast([mm(f1_16[i], bd_f1[i]) for i in idx])
    g = [eye + f1[i] for i in idx]
    g = [g[i] + mm(g[i].astype(BF16), bd(f2_16[i])) for i in idx]
    tinv16 = cast([mm(g[i].astype(BF16), bd(dinv16[i])) for i in idx])

    wv16 = cast([mm(a_ak16[i], bdv[i]) for i in idx])
    aw = [mm(tinv16[i], jnp.concatenate([bd(a16[i]), bd(wv16[i])], axis=1)) for i in idx]
    aw16 = cast(aw)
    qy = [mm(a_rb16[i], jnp.concatenate([bd(aw16[i][:, :LANES]), bd(aw16[i][:, LANES:])], axis=1))
          + jnp.concatenate([r_t[i], mm(a_rk16[i], bdv[i])], axis=1) for i in idx]
    be_ke_t = [jnp.concatenate([b_e[i], k_e[i]], axis=0).T.astype(BF16) for i in idx]
    rhs = [jnp.concatenate([aw16[i], jnp.concatenate([zero16, v16[i]], axis=1)], axis=0) for i in idx]
    mn = [jnp.where(bdmask2, mm(be_ke_t[i], rhs[i]), 0.0) for i in idx]
    qhat = [x[:, :LANES] for x in qy]
    yhat = [x[:, LANES:] for x in qy]
    mmat = [eye_bd * p_end[i] + mn[i][:, :LANES] for i in idx]
    nmat = [x[:, LANES:] for x in mn]
    return qhat, yhat, mmat, nmat


def _rwkv_kernel(rkv_ref, cv_ref, lr_ref, mu_rkv_ref, mu_lr_ref, w0_ref, a0_ref, wwa_ref, gup_ref,
                 kk_ref, ka_ref, rk_ref, lnxg_ref, lnxb_ref, convw_ref, o_ref,
                 carry_rkv, carry_lr, carry_z, state, yout_s, *, tb):
    n_chunks = tb // CHUNK
    n_pairs = RW_WIDTH // LANES

    @pl.when(pl.program_id(1) == 0)
    def _():
        carry_rkv[...] = jnp.zeros_like(carry_rkv)
        carry_lr[...] = jnp.zeros_like(carry_lr)
        carry_z[...] = jnp.zeros_like(carry_z)
        state[...] = jnp.zeros_like(state)

    rkv = rkv_ref[0]
    lr = lr_ref[0]
    rkv_m = rkv + mu_rkv_ref[...] * (_shift_rows(rkv, carry_rkv[...], 1) - rkv)
    lr_m = lr + mu_lr_ref[...] * (_shift_rows(lr, carry_lr[...], 1) - lr)
    carry_rkv[...] = rkv[tb - 8:]
    carry_lr[...] = lr[tb - 8:]

    r = rkv_m[:, :RW_WIDTH]
    k = rkv_m[:, RW_WIDTH:2 * RW_WIDTH]
    v = rkv_m[:, 2 * RW_WIDTH:]

    wa = lr_m[:, :LANES]
    lane = lax.broadcasted_iota(jnp.int32, wa.shape, 1)
    wa = jnp.where(lane < RW_DECAY_RANK, jnp.tanh(wa), wa)
    wa_up = _dot(wa, wwa_ref[...])
    ld = -math.exp(-0.5) * _sigmoid(w0_ref[...] + wa_up[:, :RW_WIDTH])
    iclr = _sigmoid(a0_ref[...] + wa_up[:, RW_WIDTH:])
    gate = _dot(_sigmoid(lr_m[:, LANES:]), gup_ref[...])

    hr = lax.broadcasted_iota(jnp.int32, (RW_WIDTH, RW_WIDTH), 0) // RW_HEAD_DIM
    hc = lax.broadcasted_iota(jnp.int32, (RW_WIDTH, RW_WIDTH), 1) // RW_HEAD_DIM
    seg = jnp.where(hr == hc, 1.0 / RW_HEAD_DIM, 0.0).astype(BF16)

    def split3(x):
        hi = x.astype(BF16)
        rem = x - hi.astype(F32)
        mid = rem.astype(BF16)
        return hi, mid, (rem - mid.astype(F32)).astype(BF16)

    def seg_mean(x):
        hi, lo, _ = split3(x)
        return (jnp.dot(hi, seg, preferred_element_type=F32)
                + jnp.dot(lo, seg, preferred_element_type=F32))

    kk = k * kk_ref[...]
    ss = seg_mean(kk * kk) * RW_HEAD_DIM
    kk = kk * lax.rsqrt(jnp.maximum(ss, 1e-24))
    kh = k * (1.0 + (iclr - 1.0) * ka_ref[...])

    tr = lax.broadcasted_iota(jnp.int32, (tb, tb), 0)
    tc = lax.broadcasted_iota(jnp.int32, (tb, tb), 1)
    tri = jnp.where((tc <= tr) & (tc // CHUNK == tr // CHUNK), 1.0, 0.0).astype(BF16)
    lp = sum(jnp.dot(tri, part, preferred_element_type=F32) for part in split3(ld))

    a_t, r_t, b_t, k_t, b_e, k_e, vs, p_end = [], [], [], [], [], [], [], []
    b = kk * iclr
    for c in range(n_chunks):
        rs = slice(c * CHUNK, (c + 1) * CHUNK)
        lp_c = lp[rs]
        lp_last = lp_c[CHUNK - 1:CHUNK]
        e_in = jnp.exp(lp_c)
        e_inv = jnp.exp(-lp_c)
        e_end = jnp.exp(lp_last - lp_c)
        a_c = -kk[rs] * jnp.exp(lp_c - ld[rs])
        r_c = r[rs] * e_in
        b_c = b[rs] * e_inv
        k_c = kh[rs] * e_inv
        be_c = b[rs] * e_end
        ke_c = kh[rs] * e_end
        pe_c = jnp.exp(lp_last)
        for p in range(n_pairs):
            cs = slice(p * LANES, (p + 1) * LANES)
            a_t.append(a_c[:, cs])
            r_t.append(r_c[:, cs])
            b_t.append(b_c[:, cs])
            k_t.append(k_c[:, cs])
            b_e.append(be_c[:, cs])
            k_e.append(ke_c[:, cs])
            vs.append(v[rs, cs])
            p_end.append(pe_c[:, cs])

    qhat, yhat, mmat, nmat = _rwkv_chunk_terms(a_t, r_t, b_t, k_t, b_e, k_e, vs, p_end)

    s = [state[p] for p in range(n_pairs)]
    for c in range(n_chunks):
        ym = [_dot(jnp.concatenate([qhat[c * n_pairs + p], mmat[c * n_pairs + p]], axis=0), s[p])
              for p in range(n_pairs)]
        for p in range(n_pairs):
            i = c * n_pairs + p
            yout_s[c * CHUNK:(c + 1) * CHUNK, p * LANES:(p + 1) * LANES] = ym[p][:CHUNK] + yhat[i]
            s[p] = ym[p][CHUNK:] + nmat[i]
    for p in range(n_pairs):
        state[p] = s[p]

    y = yout_s[...]
    mean = seg_mean(y)
    yc = y - mean
    var = seg_mean(yc * yc)
    y = yc * lax.rsqrt(var + RW_GN_EPS) * lnxg_ref[...] + lnxb_ref[...]
    bonus = seg_mean(r * kh * rk_ref[...]) * RW_HEAD_DIM * v
    o_ref[0, :, :RW_WIDTH] = ((y + bonus) * gate).astype(o_ref.dtype)

    cv = cv_ref[0]
    h = cv[:, :SC_WIDTH]
    b_gate = cv[:, SC_WIDTH:2 * SC_WIDTH]
    z = cv[:, 2 * SC_WIDTH:] * h
    zc = carry_z[...]
    cw = convw_ref[...]
    conv = cw[2:3] * z + cw[1:2] * _shift_rows(z, zc, 1) + cw[0:1] * _shift_rows(z, zc, 2)
    carry_z[...] = z[tb - 8:]
    o_ref[0, :, RW_WIDTH:] = (b_gate * conv).astype(o_ref.dtype)


def _rwkv_shortconv(p, mu_rkv, mu_lr, w0, a0, wwa, gup, k_k, k_a, r_k, lnx_g, lnx_b, conv_w, *, tb=256):
    bsz, t, _ = p.shape
    n_pairs = RW_WIDTH // LANES
    w3 = 3 * RW_WIDTH

    def full(shape):
        return pl.BlockSpec(shape, lambda b, i: (0,) * len(shape))

    return pl.pallas_call(
        functools.partial(_rwkv_kernel, tb=tb),
        grid=(bsz, t // tb),
        in_specs=[pl.BlockSpec((1, tb, w3), lambda b, i: (b, i, 0)),
                  pl.BlockSpec((1, tb, w3), lambda b, i: (b, i, 1)),
                  pl.BlockSpec((1, tb, LR_PAD), lambda b, i: (b, i, 2 * w3 // LR_PAD)),
                  full((1, w3)), full((1, LR_PAD)), full((1, RW_WIDTH)), full((1, RW_WIDTH)),
                  full((LANES, 2 * RW_WIDTH)), full((LR_PAD - LANES, RW_WIDTH)),
                  full((1, RW_WIDTH)), full((1, RW_WIDTH)), full((1, RW_WIDTH)),
                  full((1, RW_WIDTH)), full((1, RW_WIDTH)), full((SC_CONV, SC_WIDTH))],
        out_specs=pl.BlockSpec((1, tb, RW_WIDTH + SC_WIDTH), lambda b, i: (b, i, 0)),
        out_shape=jax.ShapeDtypeStruct((bsz, t, RW_WIDTH + SC_WIDTH), BF16),
        scratch_shapes=[pltpu.VMEM((8, w3), F32), pltpu.VMEM((8, LR_PAD), F32), pltpu.VMEM((8, SC_WIDTH), F32),
                        pltpu.VMEM((n_pairs, LANES, LANES), F32),
                        pltpu.VMEM((tb, RW_WIDTH), F32)],
        compiler_params=_cparams(2),
        name="rwkv_shortconv",
    )(p, p, p, mu_rkv, mu_lr, w0, a0, wwa, gup, k_k, k_a, r_k, lnx_g, lnx_b, conv_w)


def _out_ln_kernel(y_ref, w_ref, x_ref, mod_ref, lng_ref, lnb_ref, o_ref, *, gate_row):
    y = jnp.dot(y_ref[0], w_ref[...], preferred_element_type=F32)
    g = mod_ref[0][gate_row:gate_row + 1]
    z = DEEPNORM_ALPHA * x_ref[0] + (1.0 + g) * y
    o_ref[0] = _layer_norm_rows(z, lng_ref[...], lnb_ref[...])


def _out_ln(y, w, x, mod, ln_g, ln_b, *, gate_row, tm=512):
    bsz, t, d = x.shape
    kdim = y.shape[-1]
    return pl.pallas_call(
        functools.partial(_out_ln_kernel, gate_row=gate_row),
        grid=(bsz, t // tm),
        in_specs=[pl.BlockSpec((1, tm, kdim), lambda b, i: (b, i, 0)),
                  pl.BlockSpec((kdim, d), lambda b, i: (0, 0)),
                  pl.BlockSpec((1, tm, d), lambda b, i: (b, i, 0)),
                  pl.BlockSpec((1, 6, d), lambda b, i: (b, 0, 0)),
                  pl.BlockSpec((1, d), lambda b, i: (0, 0)),
                  pl.BlockSpec((1, d), lambda b, i: (0, 0))],
        out_specs=pl.BlockSpec((1, tm, d), lambda b, i: (b, i, 0)),
        out_shape=jax.ShapeDtypeStruct((bsz, t, d), F32),
        compiler_params=_cparams(2),
        name="out_ln",
    )(y, w, x, mod, ln_g, ln_b)


def _attn_kernel(q_ref, kp_ref, kc_ref, vp_ref, vc_ref, bias_ref, o_ref, lse_ref):
    n = pl.program_id(2)
    qi = lax.broadcasted_iota(jnp.int32, (BLOCK, 2 * BLOCK), 0)
    ki = lax.broadcasted_iota(jnp.int32, (BLOCK, 2 * BLOCK), 1)
    rel = BLOCK + qi - ki
    valid = (rel >= 0) & (rel <= BLOCK) & ((n > 0) | (ki >= BLOCK))
    lane = lax.broadcasted_iota(jnp.int32, (1, LANES), 1)
    scale = DIL_HEAD_DIM ** -0.5
    qmask = (jnp.where(lane < DIL_HEAD_DIM, scale, 0.0).astype(BF16),
             jnp.where(lane >= DIL_HEAD_DIM, scale, 0.0).astype(BF16))
    lane_o = lax.broadcasted_iota(jnp.int32, (BLOCK, LANES), 1)
    lse_all = jnp.zeros((BLOCK, LANES), F32)
    for p in range(DIL_WIDTH // LANES):
        cs = slice(p * LANES, (p + 1) * LANES)
        q = q_ref[:, cs]
        kcat = jnp.concatenate([kp_ref[:, cs], kc_ref[:, cs]], axis=0)
        vcat = jnp.concatenate([vp_ref[:, cs], vc_ref[:, cs]], axis=0)
        outs = []
        for hh in range(2):
            h = 2 * p + hh
            s = lax.dot_general(q * qmask[hh], kcat, (((1,), (1,)), ((), ())),
                                preferred_element_type=F32) + bias_ref[h]
            s = jnp.where(valid, s, -jnp.inf)
            m = jnp.max(s, axis=-1, keepdims=True)
            e = jnp.exp(s - m)
            den = jnp.sum(e, axis=-1, keepdims=True)
            outs.append(jnp.dot(e.astype(BF16), vcat, preferred_element_type=F32) / den)
            lse_all = jnp.where(lane_o == h, m + jnp.log(den), lse_all)
        o_ref[:, cs] = jnp.where(lane_o < DIL_HEAD_DIM, outs[0], outs[1])
    lse_ref[...] = lse_all


def _qkv_proj_kernel(x_ref, mod_ref, w_ref, o0_ref, o1_ref, o2_ref, xs_ref, *, tm):
    m = mod_ref[0]
    scale = 1.0 + m[1:2]
    shift = m[0:1]
    j = pl.program_id(2)
    n_col = x_ref.shape[-1] // LANES

    @pl.when(j == 0)
    def _():
        for c in range(n_col):
            xs_ref[c] = x_ref[0, :, c * LANES:(c + 1) * LANES]

    def group(o_ref, dil):
        rows = tm // dil
        if dil == 1:
            xp = x_ref[0]
        else:
            xp = jnp.concatenate(
                [jnp.concatenate([xs_ref[c, pl.ds(r, rows, stride=dil), :] for r in range(dil)], axis=0)
                 for c in range(n_col)], axis=1)
        u = (xp * scale + shift).astype(BF16)
        res = jnp.dot(u, w_ref[...], preferred_element_type=F32).astype(BF16)
        for r in range(dil):
            o_ref[0, r] = res[r * rows:(r + 1) * rows]

    for g, o_ref in enumerate((o0_ref, o1_ref, o2_ref)):
        pl.when(j == g)(functools.partial(group, o_ref, DIL_PATTERNS[g][1]))


def _qkv_proj(x, mod, w, *, tm=1024):
    bsz, t, d = x.shape
    gw = 3 * DIL_WIDTH
    out_specs, out_shape = [], []
    for _, dil in DIL_PATTERNS:
        out_specs.append(pl.BlockSpec((1, dil, tm // dil, gw), lambda b, i, j: (b, 0, i, 0)))
        out_shape.append(jax.ShapeDtypeStruct((bsz, dil, t // dil, gw), BF16))
    return pl.pallas_call(
        functools.partial(_qkv_proj_kernel, tm=tm),
        grid=(bsz, t // tm, N_GROUPS),
        in_specs=[pl.BlockSpec((1, tm, d), lambda b, i, j: (b, i, 0)),
                  pl.BlockSpec((1, 6, d), lambda b, i, j: (b, 0, 0)),
                  pl.BlockSpec((d, gw), lambda b, i, j: (0, j))],
        out_specs=out_specs,
        out_shape=out_shape,
        scratch_shapes=[pltpu.VMEM((d // LANES, tm, LANES), F32)],
        compiler_params=_cparams(3),
        name="qkv_proj",
    )(x, mod, w)


def _dilated_attention_group(qkv, bias, g):
    bsz, dil, length, _ = qkv.shape
    nb = length // BLOCK

    def cur(j):
        return lambda b, r, i: (b, r, i, j)

    def prev(j):
        return lambda b, r, i: (b, r, jnp.maximum(i - 1, 0), j)

    blk = (None, None, BLOCK, DIL_WIDTH)
    return pl.pallas_call(
        _attn_kernel,
        grid=(bsz, dil, nb),
        in_specs=[pl.BlockSpec(blk, cur(0)),
                  pl.BlockSpec(blk, prev(1)), pl.BlockSpec(blk, cur(1)),
                  pl.BlockSpec(blk, prev(2)), pl.BlockSpec(blk, cur(2)),
                  pl.BlockSpec((DIL_HEADS, BLOCK, 2 * BLOCK), lambda b, r, i: (0, 0, 0))],
        out_specs=[pl.BlockSpec(blk, lambda b, r, i: (b, r, i, 0)),
                   pl.BlockSpec((None, None, BLOCK, LANES), lambda b, r, i: (b, r, i, 0))],
        out_shape=[jax.ShapeDtypeStruct((bsz, dil, length, DIL_WIDTH), F32),
                   jax.ShapeDtypeStruct((bsz, dil, length, LANES), F32)],
        compiler_params=_cparams(3),
        name=f"dilated_attn_g{g}",
    )(qkv, qkv, qkv, qkv, qkv, bias)


def _merge_out_ln_kernel(o0_ref, o1_ref, o2_ref, l0_ref, l1_ref, l2_ref, w_ref, x_ref, mod_ref,
                         lng_ref, lnb_ref, o_ref, o_nat, l_nat, *, gate_row, tm):
    def natural(src_ref, scr, dil):
        if dil == 1:
            return src_ref[0, 0]
        n_col = src_ref.shape[-1] // LANES
        for c in range(n_col):
            for r in range(dil):
                scr[c, pl.ds(r, tm // dil, stride=dil), :] = src_ref[0, r, :, c * LANES:(c + 1) * LANES]
        return jnp.concatenate([scr[c] for c in range(n_col)], axis=1)

    dils = [dil for _, dil in DIL_PATTERNS]
    lses = [natural(ref, l_nat, dil) for ref, dil in zip((l0_ref, l1_ref, l2_ref), dils)]
    m = jnp.maximum(jnp.maximum(lses[0], lses[1]), lses[2])
    es = [jnp.exp(l - m) for l in lses]
    inv = 1.0 / (es[0] + es[1] + es[2])
    hr = lax.broadcasted_iota(jnp.int32, (LANES, DIL_WIDTH), 0)
    hc = lax.broadcasted_iota(jnp.int32, (LANES, DIL_WIDTH), 1) // DIL_HEAD_DIM
    expand = jnp.where(hr == hc, 1.0, 0.0).astype(BF16)

    def widen(wt):
        hi = wt.astype(BF16)
        lo = (wt - hi.astype(F32)).astype(BF16)
        return (jnp.dot(hi, expand, preferred_element_type=F32)
                + jnp.dot(lo, expand, preferred_element_type=F32))

    merged = None
    for e, ref, dil in zip(es, (o0_ref, o1_ref, o2_ref), dils):
        term = widen(e * inv) * natural(ref, o_nat, dil)
        merged = term if merged is None else merged + term
    y = jnp.dot(merged.astype(BF16), w_ref[...], preferred_element_type=F32)
    g = mod_ref[0][gate_row:gate_row + 1]
    z = DEEPNORM_ALPHA * x_ref[0] + (1.0 + g) * y
    o_ref[0] = _layer_norm_rows(z, lng_ref[...], lnb_ref[...])


def _merge_out_ln(outs, lses, w, x, mod, ln_g, ln_b, *, gate_row, tm=512):
    bsz, t, d = x.shape
    ospecs = [pl.BlockSpec((1, dil, tm // dil, DIL_WIDTH), lambda b, i: (b, 0, i, 0)) for _, dil in DIL_PATTERNS]
    lspecs = [pl.BlockSpec((1, dil, tm // dil, LANES), lambda b, i: (b, 0, i, 0)) for _, dil in DIL_PATTERNS]
    return pl.pallas_call(
        functools.partial(_merge_out_ln_kernel, gate_row=gate_row, tm=tm),
        grid=(bsz, t // tm),
        in_specs=ospecs + lspecs + [
            pl.BlockSpec((DIL_WIDTH, d), lambda b, i: (0, 0)),
            pl.BlockSpec((1, tm, d), lambda b, i: (b, i, 0)),
            pl.BlockSpec((1, 6, d), lambda b, i: (b, 0, 0)),
            pl.BlockSpec((1, d), lambda b, i: (0, 0)),
            pl.BlockSpec((1, d), lambda b, i: (0, 0))],
        out_specs=pl.BlockSpec((1, tm, d), lambda b, i: (b, i, 0)),
        out_shape=jax.ShapeDtypeStruct((bsz, t, d), F32),
        scratch_shapes=[pltpu.VMEM((DIL_WIDTH // LANES, tm, LANES), F32), pltpu.VMEM((1, tm, LANES), F32)],
        compiler_params=_cparams(2),
        name="merge_out_ln",
    )(*outs, *lses, w, x, mod, ln_g, ln_b)


def _t5_bucket(dist):
    exact = N_BUCKETS // 2
    logd = jnp.log(jnp.maximum(dist, 1).astype(F32) / exact) / math.log(MAX_DISTANCE / exact)
    large = jnp.minimum(exact + (logd * (N_BUCKETS - exact)).astype(jnp.int32), N_BUCKETS - 1)
    return jnp.where(dist < exact, dist, large)


def _bias_tables(rel_bias):
    rel = jnp.arange(BLOCK + 1)
    tables = []
    for g, (window, dil) in enumerate(DIL_PATTERNS):
        span = window // dil
        bucket = _t5_bucket(jnp.clip(rel, 0, span) * dil)
        vec = rel_bias[bucket][:, g * DIL_HEADS:(g + 1) * DIL_HEADS].T.astype(F32)
        ext = jnp.concatenate([vec[:, ::-1], jnp.zeros((DIL_HEADS, BLOCK), F32)], axis=1)
        flat = jnp.tile(ext, (1, BLOCK))[:, :BLOCK * 2 * BLOCK]
        tables.append(flat.reshape(DIL_HEADS, BLOCK, 2 * BLOCK))
    return jnp.stack(tables)


def _mlp_kernel(x_ref, mod_ref, w1_ref, w2_ref, lng_ref, lnb_ref, o_ref, *, shift_row, scale_row, gate_row, ff_chunk):
    m = mod_ref[0]
    x = x_ref[0]
    u = (x * (1.0 + m[scale_row:scale_row + 1]) + m[shift_row:shift_row + 1]).astype(BF16)
    acc = None
    for c in range(D_FF // ff_chunk):
        h = jnp.dot(u, w1_ref[:, c * ff_chunk:(c + 1) * ff_chunk], preferred_element_type=F32)
        h = jnp.maximum(h, 0.0)
        h = (h * h).astype(BF16)
        part = jnp.dot(h, w2_ref[c * ff_chunk:(c + 1) * ff_chunk, :], preferred_element_type=F32)
        acc = part if acc is None else acc + part
    z = DEEPNORM_ALPHA * x + (1.0 + m[gate_row:gate_row + 1]) * acc
    o_ref[0] = _layer_norm_rows(z, lng_ref[...], lnb_ref[...])


def _mlp_ln(x, mod, w1, w2, ln_g, ln_b, *, tm=512, ff_chunk=1024):
    bsz, t, d = x.shape
    return pl.pallas_call(
        functools.partial(_mlp_kernel, shift_row=3, scale_row=4, gate_row=5, ff_chunk=ff_chunk),
        grid=(bsz, t // tm),
        in_specs=[pl.BlockSpec((1, tm, d), lambda b, i: (b, i, 0)),
                  pl.BlockSpec((1, 6, d), lambda b, i: (b, 0, 0)),
                  pl.BlockSpec((d, D_FF), lambda b, i: (0, 0)),
                  pl.BlockSpec((D_FF, d), lambda b, i: (0, 0)),
                  pl.BlockSpec((1, d), lambda b, i: (0, 0)),
                  pl.BlockSpec((1, d), lambda b, i: (0, 0))],
        out_specs=pl.BlockSpec((1, tm, d), lambda b, i: (b, i, 0)),
        out_shape=jax.ShapeDtypeStruct((bsz, t, d), F32),
        compiler_params=_cparams(2),
        name="mlp_ln",
    )(x, mod, w1, w2, ln_g, ln_b)


def _prep_even_weights(w_in, mu, w_up, a_up, g_up):
    w3 = 3 * RW_WIDTH
    lr_end = RW_PROJ
    pad = LR_PAD - (RW_PROJ - w3)
    d = w_in.shape[0]
    w_perm = jnp.concatenate([w_in[:, :w3], w_in[:, lr_end:], w_in[:, w3:lr_end],
                              jnp.zeros((d, pad), w_in.dtype)], axis=1).astype(BF16)
    mu_rkv = mu[:w3].reshape(1, w3)
    mu_lr = jnp.concatenate([mu[w3:], jnp.zeros((pad,), mu.dtype)]).reshape(1, LR_PAD)
    zeros = jnp.zeros((RW_DECAY_RANK, RW_WIDTH), w_up.dtype)
    wwa = jnp.concatenate([jnp.concatenate([w_up, zeros], axis=1),
                           jnp.concatenate([zeros, a_up], axis=1)], axis=0).astype(BF16)
    gup = jnp.concatenate([g_up, jnp.zeros((pad, RW_WIDTH), g_up.dtype)], axis=0).astype(BF16)
    return w_perm, mu_rkv, mu_lr, wwa, gup


def kernel(x, c, ada_w, ada_b, ln_g, ln_b, ab_w_in, rw_mu, rw_w0, rw_w_up, rw_a0, rw_a_up, rw_g_up,
           rw_k_k, rw_k_a, rw_r_k, rw_lnx_g, rw_lnx_b, sc_conv_w, ab_w_out, dil_w_qkv, dil_w_out,
           rel_bias, mlp_w1, mlp_w2):
    bsz, t, d = x.shape
    mods = _adaln(c, ada_w, ada_b).reshape(DEPTH, bsz, 6, d)
    bias = None
    row = lambda a: a.reshape(1, -1)
    for i in range(DEPTH):
        mod = mods[i]
        j = i // 2
        if i % 2 == 0:
            w_perm, mu_rkv, mu_lr, wwa, gup = _prep_even_weights(
                ab_w_in[j], rw_mu[j], rw_w_up[j], rw_a_up[j], rw_g_up[j])
            p = _modulated_proj(x, mod, w_perm, shift_row=0, scale_row=1, out_dtype=F32)
            y = _rwkv_shortconv(p, mu_rkv, mu_lr, row(rw_w0[j]), row(rw_a0[j]), wwa, gup,
                                row(rw_k_k[j]), row(rw_k_a[j]), row(rw_r_k[j]),
                                row(rw_lnx_g[j]), row(rw_lnx_b[j]), sc_conv_w[j])
            x = _out_ln(y, ab_w_out[j].astype(BF16), x, mod, row(ln_g[i, 0]), row(ln_b[i, 0]), gate_row=2)
        else:
            if bias is None:
                bias = _bias_tables(rel_bias)
            qkvs = _qkv_proj(x, mod, dil_w_qkv[j].astype(BF16))
            outs, lses = [], []
            for g in range(N_GROUPS):
                o, lse = _dilated_attention_group(qkvs[g], bias[g], g)
                outs.append(o)
                lses.append(lse)
            x = _merge_out_ln(outs, lses, dil_w_out[j].astype(BF16), x, mod,
                              row(ln_g[i, 0]), row(ln_b[i, 0]), gate_row=2)
        x = _mlp_ln(x, mod, mlp_w1[i].astype(BF16), mlp_w2[i].astype(BF16),
                    row(ln_g[i, 1]), row(ln_b[i, 1]))
    return x
```

```python
import functools
import math

import jax
import jax.numpy as jnp
import numpy as np
from jax import lax
from jax.experimental import pallas as pl
from jax.experimental.pallas import tpu as pltpu

F32 = jnp.float32
BF16 = jnp.bfloat16

D_MODEL = 1024
DEPTH = 2
RW_HEADS = 8
RW_HEAD_DIM = 64
RW_WIDTH = 512
RW_DECAY_RANK = 64
RW_ICLR_RANK = 64
RW_GATE_RANK = 160
RW_GN_EPS = 64e-5
RW_PROJ = 3 * RW_WIDTH + RW_DECAY_RANK + RW_ICLR_RANK + RW_GATE_RANK
SC_WIDTH = 512
SC_CONV = 3
DIL_PATTERNS = ((128, 1), (512, 4), (2048, 16))
N_GROUPS = 3
DIL_HEADS = 8
DIL_HEAD_DIM = 64
DIL_WIDTH = 512
BLOCK = 128
N_BUCKETS = 32
MAX_DISTANCE = 2048
D_FF = 4 * D_MODEL
DEEPNORM_ALPHA = (2 * DEPTH) ** 0.25
LN_EPS = 1e-5

LANES = 128
MXU_DIM = 256
CHUNK = 64
LR_PAD = 384
AB_PROJ_PAD = 3 * RW_WIDTH + 3 * SC_WIDTH + LR_PAD
VMEM_LIMIT = 56 * 1024 * 1024


def _cparams(n_axes):
    return pltpu.CompilerParams(dimension_semantics=("arbitrary",) * n_axes,
                                vmem_limit_bytes=VMEM_LIMIT)


def _sigmoid(x):
    return 1.0 / (1.0 + jnp.exp(-x))


def _dot(a, b):
    return jnp.dot(a.astype(BF16), b.astype(BF16), preferred_element_type=F32)


def _layer_norm_rows(z, g, b):
    mu = jnp.mean(z, axis=-1, keepdims=True)
    zc = z - mu
    var = jnp.mean(zc * zc, axis=-1, keepdims=True)
    return zc * lax.rsqrt(var + LN_EPS) * g + b


def _adaln_kernel(c_ref, w_ref, b_ref, o_ref):
    c = c_ref[...]
    cond = c * _sigmoid(c)
    o_ref[0] = _dot(cond, w_ref[0]) + b_ref[0]


def _adaln(c, ada_w, ada_b):
    depth, d, n = ada_w.shape
    bsz = c.shape[0]
    tn = 1536
    return pl.pallas_call(
        _adaln_kernel,
        grid=(depth, n // tn),
        in_specs=[pl.BlockSpec((bsz, d), lambda i, j: (0, 0)),
                  pl.BlockSpec((1, d, tn), lambda i, j: (i, 0, j)),
                  pl.BlockSpec((1, 1, tn), lambda i, j: (i, 0, j))],
        out_specs=pl.BlockSpec((1, bsz, tn), lambda i, j: (i, 0, j)),
        out_shape=jax.ShapeDtypeStruct((depth, bsz, n), F32),
        compiler_params=_cparams(2),
        name="adaln",
    )(c, ada_w, ada_b.reshape(depth, 1, n))


def _proj_kernel(x_ref, mod_ref, w_ref, o_ref, u_ref, *, shift_row, scale_row):
    @pl.when(pl.program_id(2) == 0)
    def _():
        m = mod_ref[0]
        u = x_ref[0] * (1.0 + m[scale_row:scale_row + 1]) + m[shift_row:shift_row + 1]
        u_ref[...] = u.astype(BF16)

    o_ref[0] = jnp.dot(u_ref[...], w_ref[...], preferred_element_type=F32).astype(o_ref.dtype)


def _modulated_proj(x, mod, w, *, shift_row, scale_row, out_dtype, tm=1024, tn=1152):
    bsz, t, d = x.shape
    n = w.shape[1]
    return pl.pallas_call(
        functools.partial(_proj_kernel, shift_row=shift_row, scale_row=scale_row),
        grid=(bsz, t // tm, n // tn),
        in_specs=[pl.BlockSpec((1, tm, d), lambda b, i, j: (b, i, 0)),
                  pl.BlockSpec((1, 6, d), lambda b, i, j: (b, 0, 0)),
                  pl.BlockSpec((d, tn), lambda b, i, j: (0, j))],
        out_specs=pl.BlockSpec((1, tm, tn), lambda b, i, j: (b, i, j)),
        out_shape=jax.ShapeDtypeStruct((bsz, t, n), out_dtype),
        scratch_shapes=[pltpu.VMEM((tm, d), BF16)],
        compiler_params=_cparams(3),
        name="modulated_proj",
    )(x, mod, w)


def _shift_rows(x, carry, k):
    rolled = pltpu.roll(x, k, 0)
    head = rolled[:8]
    row = lax.broadcasted_iota(jnp.int32, head.shape, 0)
    for i in range(k):
        head = jnp.where(row == i, carry[8 - k + i:8 - k + i + 1], head)
    return jnp.concatenate([head, rolled[8:]], axis=0)


def _rwkv_chunk_terms(a_t, r_t, b_t, k_t, b_e, k_e, v, p_end):
    n = len(a_t)
    idx = range(n)
    row = lax.broadcasted_iota(jnp.int32, (CHUNK, LANES), 0)
    col = lax.broadcasted_iota(jnp.int32, (CHUNK, LANES), 1)
    s_idx = col % CHUNK
    head0 = col < RW_HEAD_DIM
    strict = s_idx < row
    incl = s_idx <= row
    near = strict & ((s_idx // 16) == (row // 16))
    far = strict & ((s_idx // 16) != (row // 16))
    eye = jnp.where(s_idx == row, 1.0, 0.0)
    r2 = lax.broadcasted_iota(jnp.int32, (LANES, 2 * LANES), 0)
    c2 = lax.broadcasted_iota(jnp.int32, (LANES, 2 * LANES), 1)
    bdmask2 = (r2 // RW_HEAD_DIM) == ((c2 % LANES) // RW_HEAD_DIM)
    eye_bd = jnp.where(lax.broadcasted_iota(jnp.int32, (LANES, LANES), 0)
                       == lax.broadcasted_iota(jnp.int32, (LANES, LANES), 1), 1.0, 0.0)
    zero16 = jnp.zeros((CHUNK, LANES), BF16)
    lane1 = lax.broadcasted_iota(jnp.int32, (1, LANES), 1)
    m0_16 = jnp.where(lane1 < RW_HEAD_DIM, 1.0, 0.0).astype(BF16)
    m1_16 = jnp.where(lane1 < RW_HEAD_DIM, 0.0, 1.0).astype(BF16)

    def cast(xs):
        return [x.astype(BF16) for x in xs]

    def bd(q16):
        return jnp.concatenate([q16 * m0_16, q16 * m1_16], axis=0)

    def mm(p16, q16):
        return jnp.dot(p16, q16, preferred_element_type=F32)

    def mm_nt(p16, q16):
        return lax.dot_general(p16, q16, (((1,), (1,)), ((), ())), preferred_element_type=F32)

    a16, r16, v16 = cast(a_t), cast(r_t), cast(v)
    lhs = [jnp.concatenate([a16[i], r16[i]], axis=0) for i in idx]
    b16, k16 = cast(b_t), cast(k_t)
    bdv = [bd(x) for x in v16]
    g = [mm_nt(lhs[i], jnp.concatenate([bd(b16[i]), bd(k16[i])], axis=0)) for i in idx]
    x1 = [jnp.where(near, g[i][:CHUNK, :LANES], 0.0) for i in idx]
    x1_16 = cast(x1)
    l_e16 = [jnp.where(far, g[i][:CHUNK, :LANES], 0.0).astype(BF16) for i in idx]
    a_k16 = [jnp.concatenate([jnp.where(strict, g[i][:CHUNK, LANES:], 0.0),
                              jnp.where(incl, g[i][CHUNK:, LANES:], 0.0)], axis=0).astype(BF16) for i in idx]
    a_rb16 = [jnp.where(incl, g[i][CHUNK:, :LANES], 0.0).astype(BF16) for i in idx]

    def both(p16, q16):
        return jnp.concatenate([bd(p16), bd(q16)], axis=1)

    x2_16 = cast([mm(x1_16[i], bd(x1_16[i])) for i in idx])
    acc = [eye + x1[i] for i in idx]
    sq = [mm(x2_16[i], both(x2_16[i], acc[i].astype(BF16))) for i in idx]
    x4_16 = [sq[i][:, :LANES].astype(BF16) for i in idx]
    acc = [acc[i] + sq[i][:, LANES:] for i in idx]
    sq = [mm(x4_16[i], both(x4_16[i], acc[i].astype(BF16))) for i in idx]
    x8_16 = [sq[i][:, :LANES].astype(BF16) for i in idx]
    acc = [acc[i] + sq[i][:, LANES:] for i in idx]
    dinv = [acc[i] + mm(x8_16[i], bd(acc[i].astype(BF16))) for i in idx]
    dinv16 = cast(dinv)
    f16 = cast([mm(dinv16[i], bd(l_e16[i])) for i in idx])
    sq = [mm(f16[i], both(f16[i], dinv16[i])) for i in idx]
    f2_16 = [sq[i][:, :LANES].astype(BF16) for i in idx]
    hmat = [dinv[i] + sq[i][:, LANES:] for i in idx]
    tinv16 = cast([hmat[i] + mm(f2_16[i], bd(hmat[i].astype(BF16))) for i in idx])

    kv = [mm(a_k16[i], bdv[i]) for i in idx]
    wv16 = [kv[i][:CHUNK].astype(BF16) for i in idx]
    aw = [mm(tinv16[i], both(a16[i], wv16[i])) for i in idx]
    aw16 = cast(aw)
    qy = [mm(a_rb16[i], both(aw16[i][:, :LANES], aw16[i][:, LANES:]))
          + jnp.concatenate([r_t[i], kv[i][CHUNK:]], axis=1) for i in idx]
    be_ke_t = [jnp.concatenate([b_e[i], k_e[i]], axis=0).T.astype(BF16) for i in idx]
    rhs = [jnp.concatenate([aw16[i], jnp.concatenate([zero16, v16[i]], axis=1)], axis=0) for i in idx]
    mn = [jnp.where(bdmask2, mm(be_ke_t[i], rhs[i]), 0.0) for i in idx]
    qhat = [x[:, :LANES] for x in qy]
    yhat = [x[:, LANES:] for x in qy]
    mmat = [eye_bd * p_end[i] + mn[i][:, :LANES] for i in idx]
    nmat = [x[:, LANES:] for x in mn]
    return qhat, yhat, mmat, nmat


def _rwkv_kernel(rkv_ref, cv_ref, lr_ref, mu_rkv_ref, mu_lr_ref, w0_ref, a0_ref, wwa_ref, gup_ref,
                 kk_ref, ka_ref, rk_ref, lnxg_ref, lnxb_ref, convw_ref, o_ref,
                 carry_rkv, carry_lr, carry_z, state, yout_s, *, tb):
    n_chunks = tb // CHUNK
    n_pairs = RW_WIDTH // LANES

    @pl.when(pl.program_id(1) == 0)
    def _():
        carry_rkv[...] = jnp.zeros_like(carry_rkv)
        carry_lr[...] = jnp.zeros_like(carry_lr)
        carry_z[...] = jnp.zeros_like(carry_z)
        state[...] = jnp.zeros_like(state)

    rkv = rkv_ref[0]
    lr = lr_ref[0]
    rkv_m = rkv + mu_rkv_ref[...] * (_shift_rows(rkv, carry_rkv[...], 1) - rkv)
    lr_m = lr + mu_lr_ref[...] * (_shift_rows(lr, carry_lr[...], 1) - lr)
    carry_rkv[...] = rkv[tb - 8:]
    carry_lr[...] = lr[tb - 8:]

    r = rkv_m[:, :RW_WIDTH]
    k = rkv_m[:, RW_WIDTH:2 * RW_WIDTH]
    v = rkv_m[:, 2 * RW_WIDTH:]

    wa = lr_m[:, :LANES]
    lane = lax.broadcasted_iota(jnp.int32, wa.shape, 1)
    wa = jnp.where(lane < RW_DECAY_RANK, jnp.tanh(wa), wa)
    wa_up = _dot(wa, wwa_ref[...])
    ld = -math.exp(-0.5) * _sigmoid(w0_ref[...] + wa_up[:, :RW_WIDTH])
    iclr = _sigmoid(a0_ref[...] + wa_up[:, RW_WIDTH:])
    gate = _dot(_sigmoid(lr_m[:, LANES:]), gup_ref[...])

    hr = lax.broadcasted_iota(jnp.int32, (MXU_DIM, MXU_DIM), 0) // RW_HEAD_DIM
    hc = lax.broadcasted_iota(jnp.int32, (MXU_DIM, MXU_DIM), 1) // RW_HEAD_DIM
    seg = jnp.where(hr == hc, 1.0 / RW_HEAD_DIM, 0.0).astype(BF16)

    def split3(x):
        hi = x.astype(BF16)
        rem = x - hi.astype(F32)
        mid = rem.astype(BF16)
        return hi, mid, (rem - mid.astype(F32)).astype(BF16)

    def seg_mean(x, terms=1):
        parts = split3(x)[:terms]
        cols = []
        for c in range(0, RW_WIDTH, MXU_DIM):
            cols.append(sum(jnp.dot(part[:, c:c + MXU_DIM], seg, preferred_element_type=F32) for part in parts))
        return jnp.concatenate(cols, axis=1)

    kk = k * kk_ref[...]
    ss = seg_mean(kk * kk) * RW_HEAD_DIM
    kk = kk * lax.rsqrt(jnp.maximum(ss, 1e-24))
    kh = k * (1.0 + (iclr - 1.0) * ka_ref[...])

    tr = lax.broadcasted_iota(jnp.int32, (MXU_DIM, MXU_DIM), 0)
    tc = lax.broadcasted_iota(jnp.int32, (MXU_DIM, MXU_DIM), 1)
    tri = jnp.where((tc <= tr) & (tc // CHUNK == tr // CHUNK), 1.0, 0.0).astype(BF16)
    ld_parts = split3(ld)
    lp = jnp.concatenate(
        [sum(jnp.dot(tri, part[t0:t0 + MXU_DIM], preferred_element_type=F32) for part in ld_parts)
         for t0 in range(0, tb, MXU_DIM)], axis=0)

    a_t, r_t, b_t, k_t, b_e, k_e, vs, p_end = [], [], [], [], [], [], [], []
    b = kk * iclr
    for c in range(n_chunks):
        rs = slice(c * CHUNK, (c + 1) * CHUNK)
        lp_c = lp[rs]
        lp_last = lp_c[CHUNK - 1:CHUNK]
        e_in = jnp.exp(lp_c)
        e_inv = jnp.exp(-lp_c)
        e_end = jnp.exp(lp_last - lp_c)
        a_c = -kk[rs] * jnp.exp(lp_c - ld[rs])
        r_c = r[rs] * e_in
        b_c = b[rs] * e_inv
        k_c = kh[rs] * e_inv
        be_c = b[rs] * e_end
        ke_c = kh[rs] * e_end
        pe_c = jnp.exp(lp_last)
        for p in range(n_pairs):
            cs = slice(p * LANES, (p + 1) * LANES)
            a_t.append(a_c[:, cs])
            r_t.append(r_c[:, cs])
            b_t.append(b_c[:, cs])
            k_t.append(k_c[:, cs])
            b_e.append(be_c[:, cs])
            k_e.append(ke_c[:, cs])
            vs.append(v[rs, cs])
            p_end.append(pe_c[:, cs])

    qhat, yhat, mmat, nmat = _rwkv_chunk_terms(a_t, r_t, b_t, k_t, b_e, k_e, vs, p_end)

    s = [state[p] for p in range(n_pairs)]
    for c in range(n_chunks):
        ym = [_dot(jnp.concatenate([qhat[c * n_pairs + p], mmat[c * n_pairs + p]], axis=0), s[p])
              for p in range(n_pairs)]
        for p in range(n_pairs):
            i = c * n_pairs + p
            yout_s[c * CHUNK:(c + 1) * CHUNK, p * LANES:(p + 1) * LANES] = ym[p][:CHUNK] + yhat[i]
            s[p] = ym[p][CHUNK:] + nmat[i]
    for p in range(n_pairs):
        state[p] = s[p]

    y = yout_s[...]
    mean = seg_mean(y, terms=2)
    yc = y - mean
    var = seg_mean(yc * yc)
    y = yc * lax.rsqrt(var + RW_GN_EPS) * lnxg_ref[...] + lnxb_ref[...]
    bonus = seg_mean(r * kh * rk_ref[...]) * RW_HEAD_DIM * v
    o_ref[0, :, :RW_WIDTH] = ((y + bonus) * gate).astype(o_ref.dtype)

    cv = cv_ref[0]
    h = cv[:, :SC_WIDTH]
    b_gate = cv[:, SC_WIDTH:2 * SC_WIDTH]
    z = cv[:, 2 * SC_WIDTH:] * h
    zc = carry_z[...]
    cw = convw_ref[...]
    conv = cw[2:3] * z + cw[1:2] * _shift_rows(z, zc, 1) + cw[0:1] * _shift_rows(z, zc, 2)
    carry_z[...] = z[tb - 8:]
    o_ref[0, :, RW_WIDTH:] = (b_gate * conv).astype(o_ref.dtype)


def _rwkv_shortconv(p, mu_rkv, mu_lr, w0, a0, wwa, gup, k_k, k_a, r_k, lnx_g, lnx_b, conv_w, *, tb=512):
    bsz, t, _ = p.shape
    n_pairs = RW_WIDTH // LANES
    w3 = 3 * RW_WIDTH

    def full(shape):
        return pl.BlockSpec(shape, lambda b, i: (0,) * len(shape))

    return pl.pallas_call(
        functools.partial(_rwkv_kernel, tb=tb),
        grid=(bsz, t // tb),
        in_specs=[pl.BlockSpec((1, tb, w3), lambda b, i: (b, i, 0)),
                  pl.BlockSpec((1, tb, w3), lambda b, i: (b, i, 1)),
                  pl.BlockSpec((1, tb, LR_PAD), lambda b, i: (b, i, 2 * w3 // LR_PAD)),
                  full((1, w3)), full((1, LR_PAD)), full((1, RW_WIDTH)), full((1, RW_WIDTH)),
                  full((LANES, 2 * RW_WIDTH)), full((LR_PAD - LANES, RW_WIDTH)),
                  full((1, RW_WIDTH)), full((1, RW_WIDTH)), full((1, RW_WIDTH)),
                  full((1, RW_WIDTH)), full((1, RW_WIDTH)), full((SC_CONV, SC_WIDTH))],
        out_specs=pl.BlockSpec((1, tb, RW_WIDTH + SC_WIDTH), lambda b, i: (b, i, 0)),
        out_shape=jax.ShapeDtypeStruct((bsz, t, RW_WIDTH + SC_WIDTH), BF16),
        scratch_shapes=[pltpu.VMEM((8, w3), F32), pltpu.VMEM((8, LR_PAD), F32), pltpu.VMEM((8, SC_WIDTH), F32),
                        pltpu.VMEM((n_pairs, LANES, LANES), F32),
                        pltpu.VMEM((tb, RW_WIDTH), F32)],
        compiler_params=_cparams(2),
        name="rwkv_shortconv",
    )(p, p, p, mu_rkv, mu_lr, w0, a0, wwa, gup, k_k, k_a, r_k, lnx_g, lnx_b, conv_w)


def _out_ln_kernel(y_ref, w_ref, x_ref, mod_ref, lng_ref, lnb_ref, o_ref, *, gate_row):
    y = jnp.dot(y_ref[0], w_ref[...], preferred_element_type=F32)
    g = mod_ref[0][gate_row:gate_row + 1]
    z = DEEPNORM_ALPHA * x_ref[0] + (1.0 + g) * y
    o_ref[0] = _layer_norm_rows(z, lng_ref[...], lnb_ref[...])


def _out_ln(y, w, x, mod, ln_g, ln_b, *, gate_row, tm=512):
    bsz, t, d = x.shape
    kdim = y.shape[-1]
    return pl.pallas_call(
        functools.partial(_out_ln_kernel, gate_row=gate_row),
        grid=(bsz, t // tm),
        in_specs=[pl.BlockSpec((1, tm, kdim), lambda b, i: (b, i, 0)),
                  pl.BlockSpec((kdim, d), lambda b, i: (0, 0)),
                  pl.BlockSpec((1, tm, d), lambda b, i: (b, i, 0)),
                  pl.BlockSpec((1, 6, d), lambda b, i: (b, 0, 0)),
                  pl.BlockSpec((1, d), lambda b, i: (0, 0)),
                  pl.BlockSpec((1, d), lambda b, i: (0, 0))],
        out_specs=pl.BlockSpec((1, tm, d), lambda b, i: (b, i, 0)),
        out_shape=jax.ShapeDtypeStruct((bsz, t, d), F32),
        compiler_params=_cparams(2),
        name="out_ln",
    )(y, w, x, mod, ln_g, ln_b)


def _attn_kernel(q_ref, kp_ref, kc_ref, vp_ref, vc_ref, bias_ref, o_ref, lse_ref):
    n = pl.program_id(2)
    qi = lax.broadcasted_iota(jnp.int32, (BLOCK, 2 * BLOCK), 0)
    ki = lax.broadcasted_iota(jnp.int32, (BLOCK, 2 * BLOCK), 1)
    rel = BLOCK + qi - ki
    valid = (rel >= 0) & (rel <= BLOCK) & ((n > 0) | (ki >= BLOCK))
    lane = lax.broadcasted_iota(jnp.int32, (1, LANES), 1)
    scale = DIL_HEAD_DIM ** -0.5
    qmask = (jnp.where(lane < DIL_HEAD_DIM, scale, 0.0).astype(BF16),
             jnp.where(lane >= DIL_HEAD_DIM, scale, 0.0).astype(BF16))
    lane_o = lax.broadcasted_iota(jnp.int32, (BLOCK, LANES), 1)
    lse_all = jnp.zeros((BLOCK, LANES), F32)
    for p in range(DIL_WIDTH // LANES):
        cs = slice(p * LANES, (p + 1) * LANES)
        q = q_ref[:, cs]
        kcat = jnp.concatenate([kp_ref[:, cs], kc_ref[:, cs]], axis=0)
        vcat = jnp.concatenate([vp_ref[:, cs], vc_ref[:, cs]], axis=0)
        outs = []
        for hh in range(2):
            h = 2 * p + hh
            s = lax.dot_general(q * qmask[hh], kcat, (((1,), (1,)), ((), ())),
                                preferred_element_type=F32) + bias_ref[h]
            s = jnp.where(valid, s, -jnp.inf)
            m = jnp.max(s, axis=-1, keepdims=True)
            e = jnp.exp(s - m)
            den = jnp.sum(e, axis=-1, keepdims=True)
            outs.append(jnp.dot(e.astype(BF16), vcat, preferred_element_type=F32) / den)
            lse_all = jnp.where(lane_o == h, m + jnp.log(den), lse_all)
        o_ref[:, cs] = jnp.where(lane_o < DIL_HEAD_DIM, outs[0], outs[1])
    lse_ref[...] = lse_all


def _qkv_proj_kernel(x_ref, mod_ref, w_ref, o0_ref, o1_ref, o2_ref, xs_ref, *, tm):
    m = mod_ref[0]
    scale = 1.0 + m[1:2]
    shift = m[0:1]
    j = pl.program_id(2)
    n_col = x_ref.shape[-1] // LANES

    @pl.when(j == 0)
    def _():
        for c in range(n_col):
            xs_ref[c] = x_ref[0, :, c * LANES:(c + 1) * LANES]

    def group(o_ref, dil):
        rows = tm // dil
        if dil == 1:
            xp = x_ref[0]
        else:
            xp = jnp.concatenate(
                [jnp.concatenate([xs_ref[c, pl.ds(r, rows, stride=dil), :] for r in range(dil)], axis=0)
                 for c in range(n_col)], axis=1)
        u = (xp * scale + shift).astype(BF16)
        res = jnp.dot(u, w_ref[...], preferred_element_type=F32).astype(BF16)
        for r in range(dil):
            o_ref[0, r] = res[r * rows:(r + 1) * rows]

    for g, o_ref in enumerate((o0_ref, o1_ref, o2_ref)):
        pl.when(j == g)(functools.partial(group, o_ref, DIL_PATTERNS[g][1]))


def _qkv_proj(x, mod, w, *, tm=1024):
    bsz, t, d = x.shape
    gw = 3 * DIL_WIDTH
    out_specs, out_shape = [], []
    for _, dil in DIL_PATTERNS:
        out_specs.append(pl.BlockSpec((1, dil, tm // dil, gw), lambda b, i, j: (b, 0, i, 0)))
        out_shape.append(jax.ShapeDtypeStruct((bsz, dil, t // dil, gw), BF16))
    return pl.pallas_call(
        functools.partial(_qkv_proj_kernel, tm=tm),
        grid=(bsz, t // tm, N_GROUPS),
        in_specs=[pl.BlockSpec((1, tm, d), lambda b, i, j: (b, i, 0)),
                  pl.BlockSpec((1, 6, d), lambda b, i, j: (b, 0, 0)),
                  pl.BlockSpec((d, gw), lambda b, i, j: (0, j))],
        out_specs=out_specs,
        out_shape=out_shape,
        scratch_shapes=[pltpu.VMEM((d // LANES, tm, LANES), F32)],
        compiler_params=_cparams(3),
        name="qkv_proj",
    )(x, mod, w)


def _dilated_attention_group(qkv, bias, g):
    bsz, dil, length, _ = qkv.shape
    nb = length // BLOCK

    def cur(j):
        return lambda b, r, i: (b, r, i, j)

    def prev(j):
        return lambda b, r, i: (b, r, jnp.maximum(i - 1, 0), j)

    blk = (None, None, BLOCK, DIL_WIDTH)
    return pl.pallas_call(
        _attn_kernel,
        grid=(bsz, dil, nb),
        in_specs=[pl.BlockSpec(blk, cur(0)),
                  pl.BlockSpec(blk, prev(1)), pl.BlockSpec(blk, cur(1)),
                  pl.BlockSpec(blk, prev(2)), pl.BlockSpec(blk, cur(2)),
                  pl.BlockSpec((DIL_HEADS, BLOCK, 2 * BLOCK), lambda b, r, i: (0, 0, 0))],
        out_specs=[pl.BlockSpec(blk, lambda b, r, i: (b, r, i, 0)),
                   pl.BlockSpec((None, None, BLOCK, LANES), lambda b, r, i: (b, r, i, 0))],
        out_shape=[jax.ShapeDtypeStruct((bsz, dil, length, DIL_WIDTH), F32),
                   jax.ShapeDtypeStruct((bsz, dil, length, LANES), F32)],
        compiler_params=_cparams(3),
        name=f"dilated_attn_g{g}",
    )(qkv, qkv, qkv, qkv, qkv, bias)


def _merge_out_ln_kernel(o0_ref, o1_ref, o2_ref, l0_ref, l1_ref, l2_ref, w_ref, x_ref, mod_ref,
                         lng_ref, lnb_ref, o_ref, o_nat, l_nat, *, gate_row, tm):
    def natural(src_ref, scr, dil):
        if dil == 1:
            return src_ref[0, 0]
        n_col = src_ref.shape[-1] // LANES
        for c in range(n_col):
            for r in range(dil):
                scr[c, pl.ds(r, tm // dil, stride=dil), :] = src_ref[0, r, :, c * LANES:(c + 1) * LANES]
        return jnp.concatenate([scr[c] for c in range(n_col)], axis=1)

    dils = [dil for _, dil in DIL_PATTERNS]
    lses = [natural(ref, l_nat, dil) for ref, dil in zip((l0_ref, l1_ref, l2_ref), dils)]
    m = jnp.maximum(jnp.maximum(lses[0], lses[1]), lses[2])
    es = [jnp.exp(l - m) for l in lses]
    inv = 1.0 / (es[0] + es[1] + es[2])
    hr = lax.broadcasted_iota(jnp.int32, (LANES, DIL_WIDTH), 0)
    hc = lax.broadcasted_iota(jnp.int32, (LANES, DIL_WIDTH), 1) // DIL_HEAD_DIM
    expand = jnp.where(hr == hc, 1.0, 0.0).astype(BF16)

    def widen(wt):
        hi = wt.astype(BF16)
        lo = (wt - hi.astype(F32)).astype(BF16)
        return (jnp.dot(hi, expand, preferred_element_type=F32)
                + jnp.dot(lo, expand, preferred_element_type=F32))

    merged = None
    for e, ref, dil in zip(es, (o0_ref, o1_ref, o2_ref), dils):
        term = widen(e * inv) * natural(ref, o_nat, dil)
        merged = term if merged is None else merged + term
    y = jnp.dot(merged.astype(BF16), w_ref[...], preferred_element_type=F32)
    g = mod_ref[0][gate_row:gate_row + 1]
    z = DEEPNORM_ALPHA * x_ref[0] + (1.0 + g) * y
    o_ref[0] = _layer_norm_rows(z, lng_ref[...], lnb_ref[...])


def _merge_out_ln(outs, lses, w, x, mod, ln_g, ln_b, *, gate_row, tm=512):
    bsz, t, d = x.shape
    ospecs = [pl.BlockSpec((1, dil, tm // dil, DIL_WIDTH), lambda b, i: (b, 0, i, 0)) for _, dil in DIL_PATTERNS]
    lspecs = [pl.BlockSpec((1, dil, tm // dil, LANES), lambda b, i: (b, 0, i, 0)) for _, dil in DIL_PATTERNS]
    return pl.pallas_call(
        functools.partial(_merge_out_ln_kernel, gate_row=gate_row, tm=tm),
        grid=(bsz, t // tm),
        in_specs=ospecs + lspecs + [
            pl.BlockSpec((DIL_WIDTH, d), lambda b, i: (0, 0)),
            pl.BlockSpec((1, tm, d), lambda b, i: (b, i, 0)),
            pl.BlockSpec((1, 6, d), lambda b, i: (b, 0, 0)),
            pl.BlockSpec((1, d), lambda b, i: (0, 0)),
            pl.BlockSpec((1, d), lambda b, i: (0, 0))],
        out_specs=pl.BlockSpec((1, tm, d), lambda b, i: (b, i, 0)),
        out_shape=jax.ShapeDtypeStruct((bsz, t, d), F32),
        scratch_shapes=[pltpu.VMEM((DIL_WIDTH // LANES, tm, LANES), F32), pltpu.VMEM((1, tm, LANES), F32)],
        compiler_params=_cparams(2),
        name="merge_out_ln",
    )(*outs, *lses, w, x, mod, ln_g, ln_b)


def _t5_bucket(dist):
    exact = N_BUCKETS // 2
    logd = jnp.log(jnp.maximum(dist, 1).astype(F32) / exact) / math.log(MAX_DISTANCE / exact)
    large = jnp.minimum(exact + (logd * (N_BUCKETS - exact)).astype(jnp.int32), N_BUCKETS - 1)
    return jnp.where(dist < exact, dist, large)


def _bias_tables(rel_bias):
    rel = jnp.arange(BLOCK + 1)
    tables = []
    for g, (window, dil) in enumerate(DIL_PATTERNS):
        span = window // dil
        bucket = _t5_bucket(jnp.clip(rel, 0, span) * dil)
        vec = rel_bias[bucket][:, g * DIL_HEADS:(g + 1) * DIL_HEADS].T.astype(F32)
        ext = jnp.concatenate([vec[:, ::-1], jnp.zeros((DIL_HEADS, BLOCK), F32)], axis=1)
        flat = jnp.tile(ext, (1, BLOCK))[:, :BLOCK * 2 * BLOCK]
        tables.append(flat.reshape(DIL_HEADS, BLOCK, 2 * BLOCK))
    return jnp.stack(tables)


def _mlp_kernel(x_ref, mod_ref, w1_ref, w2_ref, lng_ref, lnb_ref, o_ref, *, shift_row, scale_row, gate_row, ff_chunk):
    m = mod_ref[0]
    x = x_ref[0]
    u = (x * (1.0 + m[scale_row:scale_row + 1]) + m[shift_row:shift_row + 1]).astype(BF16)
    acc = None
    for c in range(D_FF // ff_chunk):
        h = jnp.dot(u, w1_ref[:, c * ff_chunk:(c + 1) * ff_chunk], preferred_element_type=F32)
        h = jnp.maximum(h, 0.0)
        h = (h * h).astype(BF16)
        part = jnp.dot(h, w2_ref[c * ff_chunk:(c + 1) * ff_chunk, :], preferred_element_type=F32)
        acc = part if acc is None else acc + part
    z = DEEPNORM_ALPHA * x + (1.0 + m[gate_row:gate_row + 1]) * acc
    o_ref[0] = _layer_norm_rows(z, lng_ref[...], lnb_ref[...])


def _mlp_ln(x, mod, w1, w2, ln_g, ln_b, *, tm=512, ff_chunk=1024):
    bsz, t, d = x.shape
    return pl.pallas_call(
        functools.partial(_mlp_kernel, shift_row=3, scale_row=4, gate_row=5, ff_chunk=ff_chunk),
        grid=(bsz, t // tm),
        in_specs=[pl.BlockSpec((1, tm, d), lambda b, i: (b, i, 0)),
                  pl.BlockSpec((1, 6, d), lambda b, i: (b, 0, 0)),
                  pl.BlockSpec((d, D_FF), lambda b, i: (0, 0)),
                  pl.BlockSpec((D_FF, d), lambda b, i: (0, 0)),
                  pl.BlockSpec((1, d), lambda b, i: (0, 0)),
                  pl.BlockSpec((1, d), lambda b, i: (0, 0))],
        out_specs=pl.BlockSpec((1, tm, d), lambda b, i: (b, i, 0)),
        out_shape=jax.ShapeDtypeStruct((bsz, t, d), F32),
        compiler_params=_cparams(2),
        name="mlp_ln",
    )(x, mod, w1, w2, ln_g, ln_b)


def _prep_even_weights(w_in, mu, w_up, a_up, g_up):
    w3 = 3 * RW_WIDTH
    lr_end = RW_PROJ
    pad = LR_PAD - (RW_PROJ - w3)
    d = w_in.shape[0]
    w_perm = jnp.concatenate([w_in[:, :w3], w_in[:, lr_end:], w_in[:, w3:lr_end],
                              jnp.zeros((d, pad), w_in.dtype)], axis=1).astype(BF16)
    mu_rkv = mu[:w3].reshape(1, w3)
    mu_lr = jnp.concatenate([mu[w3:], jnp.zeros((pad,), mu.dtype)]).reshape(1, LR_PAD)
    zeros = jnp.zeros((RW_DECAY_RANK, RW_WIDTH), w_up.dtype)
    wwa = jnp.concatenate([jnp.concatenate([w_up, zeros], axis=1),
                           jnp.concatenate([zeros, a_up], axis=1)], axis=0).astype(BF16)
    gup = jnp.concatenate([g_up, jnp.zeros((pad, RW_WIDTH), g_up.dtype)], axis=0).astype(BF16)
    return w_perm, mu_rkv, mu_lr, wwa, gup


def kernel(x, c, ada_w, ada_b, ln_g, ln_b, ab_w_in, rw_mu, rw_w0, rw_w_up, rw_a0, rw_a_up, rw_g_up,
           rw_k_k, rw_k_a, rw_r_k, rw_lnx_g, rw_lnx_b, sc_conv_w, ab_w_out, dil_w_qkv, dil_w_out,
           rel_bias, mlp_w1, mlp_w2):
    bsz, t, d = x.shape
    mods = _adaln(c, ada_w, ada_b).reshape(DEPTH, bsz, 6, d)
    bias = None
    row = lambda a: a.reshape(1, -1)
    for i in range(DEPTH):
        mod = mods[i]
        j = i // 2
        if i % 2 == 0:
            w_perm, mu_rkv, mu_lr, wwa, gup = _prep_even_weights(
                ab_w_in[j], rw_mu[j], rw_w_up[j], rw_a_up[j], rw_g_up[j])
            p = _modulated_proj(x, mod, w_perm, shift_row=0, scale_row=1, out_dtype=F32)
            y = _rwkv_shortconv(p, mu_rkv, mu_lr, row(rw_w0[j]), row(rw_a0[j]), wwa, gup,
                                row(rw_k_k[j]), row(rw_k_a[j]), row(rw_r_k[j]),
                                row(rw_lnx_g[j]), row(rw_lnx_b[j]), sc_conv_w[j])
            x = _out_ln(y, ab_w_out[j].astype(BF16), x, mod, row(ln_g[i, 0]), row(ln_b[i, 0]), gate_row=2)
        else:
            if bias is None:
                bias = _bias_tables(rel_bias)
            qkvs = _qkv_proj(x, mod, dil_w_qkv[j].astype(BF16))
            outs, lses = [], []
            for g in range(N_GROUPS):
                o, lse = _dilated_attention_group(qkvs[g], bias[g], g)
                outs.append(o)
                lses.append(lse)
            x = _merge_out_ln(outs, lses, dil_w_out[j].astype(BF16), x, mod,
                              row(ln_g[i, 0]), row(ln_b[i, 0]), gate_row=2)
        x = _mlp_ln(x, mod, mlp_w1[i].astype(BF16), mlp_w2[i].astype(BF16),
                    row(ln_g[i, 1]), row(ln_b[i, 1]))
    return x
```

```python
import functools
import math

import jax
import jax.numpy as jnp
import numpy as np
from jax import lax
from jax.experimental import pallas as pl
from jax.experimental.pallas import tpu as pltpu

F32 = jnp.float32
BF16 = jnp.bfloat16

D_MODEL = 1024
DEPTH = 2
RW_HEADS = 8
RW_HEAD_DIM = 64
RW_WIDTH = 512
RW_DECAY_RANK = 64
RW_ICLR_RANK = 64
RW_GATE_RANK = 160
RW_GN_EPS = 64e-5
RW_PROJ = 3 * RW_WIDTH + RW_DECAY_RANK + RW_ICLR_RANK + RW_GATE_RANK
SC_WIDTH = 512
SC_CONV = 3
DIL_PATTERNS = ((128, 1), (512, 4), (2048, 16))
N_GROUPS = 3
DIL_HEADS = 8
DIL_HEAD_DIM = 64
DIL_WIDTH = 512
BLOCK = 128
N_BUCKETS = 32
MAX_DISTANCE = 2048
D_FF = 4 * D_MODEL
DEEPNORM_ALPHA = (2 * DEPTH) ** 0.25
LN_EPS = 1e-5

LANES = 128
MXU_DIM = 256
CHUNK = 64
LR_PAD = 384
AB_PROJ_PAD = 3 * RW_WIDTH + 3 * SC_WIDTH + LR_PAD
VMEM_LIMIT = 56 * 1024 * 1024
MASKED = -1e30


def _cparams(n_axes):
    return pltpu.CompilerParams(dimension_semantics=("arbitrary",) * n_axes,
                                vmem_limit_bytes=VMEM_LIMIT)


def _sigmoid(x):
    return 1.0 / (1.0 + jnp.exp(-x))


def _dot(a, b):
    return jnp.dot(a.astype(BF16), b.astype(BF16), preferred_element_type=F32)


def _layer_norm_rows(z, g, b):
    mu = jnp.mean(z, axis=-1, keepdims=True)
    zc = z - mu
    var = jnp.mean(zc * zc, axis=-1, keepdims=True)
    return zc * lax.rsqrt(var + LN_EPS) * g + b


def _adaln_kernel(c_ref, w_ref, b_ref, o_ref):
    c = c_ref[...]
    cond = c * _sigmoid(c)
    o_ref[0] = _dot(cond, w_ref[0]) + b_ref[0]


def _adaln(c, ada_w, ada_b):
    depth, d, n = ada_w.shape
    bsz = c.shape[0]
    tn = 1536
    return pl.pallas_call(
        _adaln_kernel,
        grid=(depth, n // tn),
        in_specs=[pl.BlockSpec((bsz, d), lambda i, j: (0, 0)),
                  pl.BlockSpec((1, d, tn), lambda i, j: (i, 0, j)),
                  pl.BlockSpec((1, 1, tn), lambda i, j: (i, 0, j))],
        out_specs=pl.BlockSpec((1, bsz, tn), lambda i, j: (i, 0, j)),
        out_shape=jax.ShapeDtypeStruct((depth, bsz, n), F32),
        compiler_params=_cparams(2),
        name="adaln",
    )(c, ada_w, ada_b.reshape(depth, 1, n))


def _proj_kernel(x_ref, mod_ref, w_ref, o_ref, u_ref, *, shift_row, scale_row):
    @pl.when(pl.program_id(2) == 0)
    def _():
        m = mod_ref[0]
        u = x_ref[0] * (1.0 + m[scale_row:scale_row + 1]) + m[shift_row:shift_row + 1]
        u_ref[...] = u.astype(BF16)

    o_ref[0] = jnp.dot(u_ref[...], w_ref[...], preferred_element_type=F32).astype(o_ref.dtype)


def _modulated_proj(x, mod, w, *, shift_row, scale_row, out_dtype, tm=1024, tn=1152):
    bsz, t, d = x.shape
    n = w.shape[1]
    return pl.pallas_call(
        functools.partial(_proj_kernel, shift_row=shift_row, scale_row=scale_row),
        grid=(bsz, t // tm, n // tn),
        in_specs=[pl.BlockSpec((1, tm, d), lambda b, i, j: (b, i, 0)),
                  pl.BlockSpec((1, 6, d), lambda b, i, j: (b, 0, 0)),
                  pl.BlockSpec((d, tn), lambda b, i, j: (0, j))],
        out_specs=pl.BlockSpec((1, tm, tn), lambda b, i, j: (b, i, j)),
        out_shape=jax.ShapeDtypeStruct((bsz, t, n), out_dtype),
        scratch_shapes=[pltpu.VMEM((tm, d), BF16)],
        compiler_params=_cparams(3),
        name="modulated_proj",
    )(x, mod, w)


def _shift_rows(x, carry, k):
    rolled = pltpu.roll(x, k, 0)
    head = rolled[:8]
    row = lax.broadcasted_iota(jnp.int32, head.shape, 0)
    for i in range(k):
        head = jnp.where(row == i, carry[8 - k + i:8 - k + i + 1], head)
    return jnp.concatenate([head, rolled[8:]], axis=0)


def _rwkv_chunk_terms(a_t, r_t, b_t, k_t, b_e, k_e, v, p_end):
    n = len(a_t)
    idx = range(n)
    row = lax.broadcasted_iota(jnp.int32, (CHUNK, LANES), 0)
    col = lax.broadcasted_iota(jnp.int32, (CHUNK, LANES), 1)
    s_idx = col % CHUNK
    head0 = col < RW_HEAD_DIM
    strict = s_idx < row
    incl = s_idx <= row
    near = strict & ((s_idx // 16) == (row // 16))
    far = strict & ((s_idx // 16) != (row // 16))
    eye = jnp.where(s_idx == row, 1.0, 0.0)
    r2 = lax.broadcasted_iota(jnp.int32, (LANES, 2 * LANES), 0)
    c2 = lax.broadcasted_iota(jnp.int32, (LANES, 2 * LANES), 1)
    bdmask2 = (r2 // RW_HEAD_DIM) == ((c2 % LANES) // RW_HEAD_DIM)
    eye_bd = jnp.where(lax.broadcasted_iota(jnp.int32, (LANES, LANES), 0)
                       == lax.broadcasted_iota(jnp.int32, (LANES, LANES), 1), 1.0, 0.0)
    zero16 = jnp.zeros((CHUNK, LANES), BF16)
    lane1 = lax.broadcasted_iota(jnp.int32, (1, LANES), 1)
    m0_16 = jnp.where(lane1 < RW_HEAD_DIM, 1.0, 0.0).astype(BF16)
    m1_16 = jnp.where(lane1 < RW_HEAD_DIM, 0.0, 1.0).astype(BF16)

    def cast(xs):
        return [x.astype(BF16) for x in xs]

    def bd(q16):
        return jnp.concatenate([q16 * m0_16, q16 * m1_16], axis=0)

    def mm(p16, q16):
        return jnp.dot(p16, q16, preferred_element_type=F32)

    def mm_nt(p16, q16):
        return lax.dot_general(p16, q16, (((1,), (1,)), ((), ())), preferred_element_type=F32)

    a16, r16, v16 = cast(a_t), cast(r_t), cast(v)
    lhs = [jnp.concatenate([a16[i], r16[i]], axis=0) for i in idx]
    b16, k16 = cast(b_t), cast(k_t)
    bdv = [bd(x) for x in v16]
    g = [mm_nt(lhs[i], jnp.concatenate([bd(b16[i]), bd(k16[i])], axis=0)) for i in idx]
    x1 = [jnp.where(near, g[i][:CHUNK, :LANES], 0.0) for i in idx]
    x1_16 = cast(x1)
    l_e16 = [jnp.where(far, g[i][:CHUNK, :LANES], 0.0).astype(BF16) for i in idx]
    a_k16 = [jnp.concatenate([jnp.where(strict, g[i][:CHUNK, LANES:], 0.0),
                              jnp.where(incl, g[i][CHUNK:, LANES:], 0.0)], axis=0).astype(BF16) for i in idx]
    a_rb16 = [jnp.where(incl, g[i][CHUNK:, :LANES], 0.0).astype(BF16) for i in idx]

    def both(p16, q16):
        return jnp.concatenate([bd(p16), bd(q16)], axis=1)

    x2_16 = cast([mm(x1_16[i], bd(x1_16[i])) for i in idx])
    acc = [eye + x1[i] for i in idx]
    sq = [mm(x2_16[i], both(x2_16[i], acc[i].astype(BF16))) for i in idx]
    x4_16 = [sq[i][:, :LANES].astype(BF16) for i in idx]
    acc = [acc[i] + sq[i][:, LANES:] for i in idx]
    sq = [mm(x4_16[i], both(x4_16[i], acc[i].astype(BF16))) for i in idx]
    x8_16 = [sq[i][:, :LANES].astype(BF16) for i in idx]
    acc = [acc[i] + sq[i][:, LANES:] for i in idx]
    dinv = [acc[i] + mm(x8_16[i], bd(acc[i].astype(BF16))) for i in idx]
    dinv16 = cast(dinv)
    f16 = cast([mm(dinv16[i], bd(l_e16[i])) for i in idx])
    sq = [mm(f16[i], both(f16[i], dinv16[i])) for i in idx]
    f2_16 = [sq[i][:, :LANES].astype(BF16) for i in idx]
    hmat = [dinv[i] + sq[i][:, LANES:] for i in idx]
    tinv16 = cast([hmat[i] + mm(f2_16[i], bd(hmat[i].astype(BF16))) for i in idx])

    kv = [mm(a_k16[i], bdv[i]) for i in idx]
    wv16 = [kv[i][:CHUNK].astype(BF16) for i in idx]
    aw = [mm(tinv16[i], both(a16[i], wv16[i])) for i in idx]
    aw16 = cast(aw)
    qy = [mm(a_rb16[i], both(aw16[i][:, :LANES], aw16[i][:, LANES:]))
          + jnp.concatenate([r_t[i], kv[i][CHUNK:]], axis=1) for i in idx]
    be_ke_t = [jnp.concatenate([b_e[i], k_e[i]], axis=0).T.astype(BF16) for i in idx]
    rhs = [jnp.concatenate([aw16[i], jnp.concatenate([zero16, v16[i]], axis=1)], axis=0) for i in idx]
    mn = [jnp.where(bdmask2, mm(be_ke_t[i], rhs[i]), 0.0) for i in idx]
    qhat = [x[:, :LANES] for x in qy]
    yhat = [x[:, LANES:] for x in qy]
    mmat = [eye_bd * p_end[i] + mn[i][:, :LANES] for i in idx]
    nmat = [x[:, LANES:] for x in mn]
    return qhat, yhat, mmat, nmat


def _rwkv_kernel(rkv_ref, cv_ref, lr_ref, mu_rkv_ref, mu_lr_ref, w0_ref, a0_ref, wwa_ref, gup_ref,
                 kk_ref, ka_ref, rk_ref, lnxg_ref, lnxb_ref, convw_ref, o_ref,
                 carry_rkv, carry_lr, carry_z, state, yout_s, *, tb):
    n_chunks = tb // CHUNK
    n_pairs = RW_WIDTH // LANES

    @pl.when(pl.program_id(1) == 0)
    def _():
        carry_rkv[...] = jnp.zeros_like(carry_rkv)
        carry_lr[...] = jnp.zeros_like(carry_lr)
        carry_z[...] = jnp.zeros_like(carry_z)
        state[...] = jnp.zeros_like(state)

    rkv = rkv_ref[0]
    lr = lr_ref[0]
    rkv_m = rkv + mu_rkv_ref[...] * (_shift_rows(rkv, carry_rkv[...], 1) - rkv)
    lr_m = lr + mu_lr_ref[...] * (_shift_rows(lr, carry_lr[...], 1) - lr)
    carry_rkv[...] = rkv[tb - 8:]
    carry_lr[...] = lr[tb - 8:]

    r = rkv_m[:, :RW_WIDTH]
    k = rkv_m[:, RW_WIDTH:2 * RW_WIDTH]
    v = rkv_m[:, 2 * RW_WIDTH:]

    wa = lr_m[:, :LANES]
    lane = lax.broadcasted_iota(jnp.int32, wa.shape, 1)
    wa = jnp.where(lane < RW_DECAY_RANK, jnp.tanh(wa), wa)
    wa_up = _dot(wa, wwa_ref[...])
    ld = -math.exp(-0.5) * _sigmoid(w0_ref[...] + wa_up[:, :RW_WIDTH])
    iclr = _sigmoid(a0_ref[...] + wa_up[:, RW_WIDTH:])
    gate = _dot(_sigmoid(lr_m[:, LANES:]), gup_ref[...])

    hr = lax.broadcasted_iota(jnp.int32, (MXU_DIM, MXU_DIM), 0) // RW_HEAD_DIM
    hc = lax.broadcasted_iota(jnp.int32, (MXU_DIM, MXU_DIM), 1) // RW_HEAD_DIM
    seg = jnp.where(hr == hc, 1.0 / RW_HEAD_DIM, 0.0).astype(BF16)

    def split3(x):
        hi = x.astype(BF16)
        rem = x - hi.astype(F32)
        mid = rem.astype(BF16)
        return hi, mid, (rem - mid.astype(F32)).astype(BF16)

    def seg_mean(x, terms=1):
        parts = split3(x)[:terms]
        cols = []
        for c in range(0, RW_WIDTH, MXU_DIM):
            cols.append(sum(jnp.dot(part[:, c:c + MXU_DIM], seg, preferred_element_type=F32) for part in parts))
        return jnp.concatenate(cols, axis=1)

    kk = k * kk_ref[...]
    ss = seg_mean(kk * kk) * RW_HEAD_DIM
    kk = kk * lax.rsqrt(jnp.maximum(ss, 1e-24))
    kh = k * (1.0 + (iclr - 1.0) * ka_ref[...])

    tr = lax.broadcasted_iota(jnp.int32, (MXU_DIM, MXU_DIM), 0)
    tc = lax.broadcasted_iota(jnp.int32, (MXU_DIM, MXU_DIM), 1)
    tri = jnp.where((tc <= tr) & (tc // CHUNK == tr // CHUNK), 1.0, 0.0).astype(BF16)
    ld_parts = split3(ld)
    lp = jnp.concatenate(
        [sum(jnp.dot(tri, part[t0:t0 + MXU_DIM], preferred_element_type=F32) for part in ld_parts)
         for t0 in range(0, tb, MXU_DIM)], axis=0)

    a_t, r_t, b_t, k_t, b_e, k_e, vs, p_end = [], [], [], [], [], [], [], []
    b = kk * iclr
    for c in range(n_chunks):
        rs = slice(c * CHUNK, (c + 1) * CHUNK)
        lp_c = lp[rs]
        lp_last = lp_c[CHUNK - 1:CHUNK]
        e_in = jnp.exp(lp_c)
        e_inv = jnp.exp(-lp_c)
        e_end = jnp.exp(lp_last - lp_c)
        a_c = -kk[rs] * jnp.exp(lp_c - ld[rs])
        r_c = r[rs] * e_in
        b_c = b[rs] * e_inv
        k_c = kh[rs] * e_inv
        be_c = b[rs] * e_end
        ke_c = kh[rs] * e_end
        pe_c = jnp.exp(lp_last)
        for p in range(n_pairs):
            cs = slice(p * LANES, (p + 1) * LANES)
            a_t.append(a_c[:, cs])
            r_t.append(r_c[:, cs])
            b_t.append(b_c[:, cs])
            k_t.append(k_c[:, cs])
            b_e.append(be_c[:, cs])
            k_e.append(ke_c[:, cs])
            vs.append(v[rs, cs])
            p_end.append(pe_c[:, cs])

    qhat, yhat, mmat, nmat = _rwkv_chunk_terms(a_t, r_t, b_t, k_t, b_e, k_e, vs, p_end)

    s = [state[p] for p in range(n_pairs)]
    for c in range(n_chunks):
        ym = [_dot(jnp.concatenate([qhat[c * n_pairs + p], mmat[c * n_pairs + p]], axis=0), s[p])
              for p in range(n_pairs)]
        for p in range(n_pairs):
            i = c * n_pairs + p
            yout_s[c * CHUNK:(c + 1) * CHUNK, p * LANES:(p + 1) * LANES] = ym[p][:CHUNK] + yhat[i]
            s[p] = ym[p][CHUNK:] + nmat[i]
    for p in range(n_pairs):
        state[p] = s[p]

    y = yout_s[...]
    mean = seg_mean(y, terms=2)
    yc = y - mean
    var = seg_mean(yc * yc)
    y = yc * lax.rsqrt(var + RW_GN_EPS) * lnxg_ref[...] + lnxb_ref[...]
    bonus = seg_mean(r * kh * rk_ref[...]) * RW_HEAD_DIM * v
    o_ref[0, :, :RW_WIDTH] = ((y + bonus) * gate).astype(o_ref.dtype)

    cv = cv_ref[0]
    h = cv[:, :SC_WIDTH]
    b_gate = cv[:, SC_WIDTH:2 * SC_WIDTH]
    z = cv[:, 2 * SC_WIDTH:] * h
    zc = carry_z[...]
    cw = convw_ref[...]
    conv = cw[2:3] * z + cw[1:2] * _shift_rows(z, zc, 1) + cw[0:1] * _shift_rows(z, zc, 2)
    carry_z[...] = z[tb - 8:]
    o_ref[0, :, RW_WIDTH:] = (b_gate * conv).astype(o_ref.dtype)


def _rwkv_shortconv(p, mu_rkv, mu_lr, w0, a0, wwa, gup, k_k, k_a, r_k, lnx_g, lnx_b, conv_w, *, tb=512):
    bsz, t, _ = p.shape
    n_pairs = RW_WIDTH // LANES
    w3 = 3 * RW_WIDTH

    def full(shape):
        return pl.BlockSpec(shape, lambda b, i: (0,) * len(shape))

    return pl.pallas_call(
        functools.partial(_rwkv_kernel, tb=tb),
        grid=(bsz, t // tb),
        in_specs=[pl.BlockSpec((1, tb, w3), lambda b, i: (b, i, 0)),
                  pl.BlockSpec((1, tb, w3), lambda b, i: (b, i, 1)),
                  pl.BlockSpec((1, tb, LR_PAD), lambda b, i: (b, i, 2 * w3 // LR_PAD)),
                  full((1, w3)), full((1, LR_PAD)), full((1, RW_WIDTH)), full((1, RW_WIDTH)),
                  full((LANES, 2 * RW_WIDTH)), full((LR_PAD - LANES, RW_WIDTH)),
                  full((1, RW_WIDTH)), full((1, RW_WIDTH)), full((1, RW_WIDTH)),
                  full((1, RW_WIDTH)), full((1, RW_WIDTH)), full((SC_CONV, SC_WIDTH))],
        out_specs=pl.BlockSpec((1, tb, RW_WIDTH + SC_WIDTH), lambda b, i: (b, i, 0)),
        out_shape=jax.ShapeDtypeStruct((bsz, t, RW_WIDTH + SC_WIDTH), BF16),
        scratch_shapes=[pltpu.VMEM((8, w3), F32), pltpu.VMEM((8, LR_PAD), F32), pltpu.VMEM((8, SC_WIDTH), F32),
                        pltpu.VMEM((n_pairs, LANES, LANES), F32),
                        pltpu.VMEM((tb, RW_WIDTH), F32)],
        compiler_params=_cparams(2),
        name="rwkv_shortconv",
    )(p, p, p, mu_rkv, mu_lr, w0, a0, wwa, gup, k_k, k_a, r_k, lnx_g, lnx_b, conv_w)


def _out_ln_kernel(y_ref, w_ref, x_ref, mod_ref, lng_ref, lnb_ref, o_ref, *, gate_row):
    y = jnp.dot(y_ref[0], w_ref[...], preferred_element_type=F32)
    g = mod_ref[0][gate_row:gate_row + 1]
    z = DEEPNORM_ALPHA * x_ref[0] + (1.0 + g) * y
    o_ref[0] = _layer_norm_rows(z, lng_ref[...], lnb_ref[...])


def _out_ln(y, w, x, mod, ln_g, ln_b, *, gate_row, tm=512):
    bsz, t, d = x.shape
    kdim = y.shape[-1]
    return pl.pallas_call(
        functools.partial(_out_ln_kernel, gate_row=gate_row),
        grid=(bsz, t // tm),
        in_specs=[pl.BlockSpec((1, tm, kdim), lambda b, i: (b, i, 0)),
                  pl.BlockSpec((kdim, d), lambda b, i: (0, 0)),
                  pl.BlockSpec((1, tm, d), lambda b, i: (b, i, 0)),
                  pl.BlockSpec((1, 6, d), lambda b, i: (b, 0, 0)),
                  pl.BlockSpec((1, d), lambda b, i: (0, 0)),
                  pl.BlockSpec((1, d), lambda b, i: (0, 0))],
        out_specs=pl.BlockSpec((1, tm, d), lambda b, i: (b, i, 0)),
        out_shape=jax.ShapeDtypeStruct((bsz, t, d), F32),
        compiler_params=_cparams(2),
        name="out_ln",
    )(y, w, x, mod, ln_g, ln_b)


def _attn_kernel(q_ref, kp_ref, kc_ref, vp_ref, vc_ref, bias_ref, o_ref, lse_ref, *, nb, rb):
    step = pl.program_id(1)
    n_blk = rb // BLOCK
    lane = lax.broadcasted_iota(jnp.int32, (1, LANES), 1)
    scale = DIL_HEAD_DIM ** -0.5
    qmask = (jnp.where(lane < DIL_HEAD_DIM, scale, 0.0).astype(BF16),
             jnp.where(lane >= DIL_HEAD_DIM, scale, 0.0).astype(BF16))
    lane_o = lax.broadcasted_iota(jnp.int32, (BLOCK, LANES), 1)
    head0 = lane_o < DIL_HEAD_DIM

    ones = jnp.ones((BLOCK, LANES), BF16)

    def attend(rows, q, parts):
        heads = range(DIL_HEADS)
        lanes = [slice((h // 2) * LANES, (h // 2 + 1) * LANES) for h in heads]
        qh = [q[:, lanes[h]] * qmask[h % 2] for h in heads]
        scores = []
        for h in heads:
            row = []
            for k_blk, _, col0, pen in parts:
                s = lax.dot_general(qh[h], k_blk[:, lanes[h]], (((1,), (1,)), ((), ())),
                                    preferred_element_type=F32) + bias_ref[h, :, col0:col0 + BLOCK]
                row.append(s if pen is None else s + pen)
            scores.append(row)
        ms = []
        for h in heads:
            m = jnp.max(scores[h][0], axis=-1, keepdims=True)
            for s in scores[h][1:]:
                m = jnp.maximum(m, jnp.max(s, axis=-1, keepdims=True))
            ms.append(m)
        es = [[jnp.exp((s - ms[h]).astype(BF16)) for s in scores[h]] for h in heads]
        vaug = [[jnp.where(head0 if h % 2 == 0 else jnp.logical_not(head0), part[1][:, lanes[h]], ones)
                 for part in parts] for h in heads]
        res = [sum(jnp.dot(e, v, preferred_element_type=F32) for e, v in zip(es[h], vaug[h])) for h in heads]
        m_all = jnp.zeros((BLOCK, LANES), F32)
        den_all = jnp.ones((BLOCK, LANES), F32)
        for p in range(DIL_WIDTH // LANES):
            r0, r1 = res[2 * p], res[2 * p + 1]
            swapped = jnp.where(head0, r1, r0)
            den = pltpu.roll(swapped, DIL_HEAD_DIM, 1)
            o_ref[rows, p * LANES:(p + 1) * LANES] = jnp.where(head0, r0, r1) / den
            m_all = jnp.where(lane_o == 2 * p, ms[2 * p], jnp.where(lane_o == 2 * p + 1, ms[2 * p + 1], m_all))
            den_all = jnp.where(lane_o == 2 * p, den, jnp.where(lane_o == 2 * p + 1, swapped, den_all))
        lse_ref[rows, :] = m_all + jnp.log(den_all)

    for j in range(n_blk):
        rows = slice(j * BLOCK, (j + 1) * BLOCK)
        q = q_ref[rows, :]
        cur = (kc_ref[rows, :], vc_ref[rows, :], BLOCK, None)
        if n_blk % nb == 0:
            prev = "none" if j % nb == 0 else "inside"
        else:
            prev = "inside" if j > 0 else ("none" if nb == n_blk else "before")
        if prev == "none":
            attend(rows, q, [cur])
        elif prev == "inside":
            before = slice((j - 1) * BLOCK, j * BLOCK)
            attend(rows, q, [(kc_ref[before, :], vc_ref[before, :], 0, None), cur])
        else:
            has_prev = (step % (nb // n_blk)) != 0
            pen = jnp.where(has_prev, 0.0, MASKED)
            attend(rows, q, [(kp_ref[...], vp_ref[...], 0, pen), cur])


def _qkv_proj_kernel(x_ref, mod_ref, w_ref, o0_ref, o1_ref, o2_ref, xs_ref, *, tm):
    m = mod_ref[0]
    scale = 1.0 + m[1:2]
    shift = m[0:1]
    j = pl.program_id(2)
    n_col = x_ref.shape[-1] // LANES

    @pl.when(j == 0)
    def _():
        for c in range(n_col):
            xs_ref[c] = x_ref[0, :, c * LANES:(c + 1) * LANES]

    def group(o_ref, dil):
        rows = tm // dil
        if dil == 1:
            xp = x_ref[0]
        else:
            xp = jnp.concatenate(
                [jnp.concatenate([xs_ref[c, pl.ds(r, rows, stride=dil), :] for r in range(dil)], axis=0)
                 for c in range(n_col)], axis=1)
        u = (xp * scale + shift).astype(BF16)
        res = jnp.dot(u, w_ref[...], preferred_element_type=F32).astype(BF16)
        for r in range(dil):
            o_ref[0, r] = res[r * rows:(r + 1) * rows]

    for g, o_ref in enumerate((o0_ref, o1_ref, o2_ref)):
        pl.when(j == g)(functools.partial(group, o_ref, DIL_PATTERNS[g][1]))


def _qkv_proj(x, mod, w, *, tm=1024):
    bsz, t, d = x.shape
    gw = 3 * DIL_WIDTH
    out_specs, out_shape = [], []
    for _, dil in DIL_PATTERNS:
        out_specs.append(pl.BlockSpec((1, dil, tm // dil, gw), lambda b, i, j: (b, 0, i, 0)))
        out_shape.append(jax.ShapeDtypeStruct((bsz, dil, t // dil, gw), BF16))
    return pl.pallas_call(
        functools.partial(_qkv_proj_kernel, tm=tm),
        grid=(bsz, t // tm, N_GROUPS),
        in_specs=[pl.BlockSpec((1, tm, d), lambda b, i, j: (b, i, 0)),
                  pl.BlockSpec((1, 6, d), lambda b, i, j: (b, 0, 0)),
                  pl.BlockSpec((d, gw), lambda b, i, j: (0, j))],
        out_specs=out_specs,
        out_shape=out_shape,
        scratch_shapes=[pltpu.VMEM((d // LANES, tm, LANES), F32)],
        compiler_params=_cparams(3),
        name="qkv_proj",
    )(x, mod, w)


def _dilated_attention_group(qkv, bias, g, *, rb=512):
    bsz, dil, length, width = qkv.shape
    t = dil * length
    nb = length // BLOCK
    n_blk = rb // BLOCK
    assert nb % n_blk == 0 or n_blk % nb == 0
    flat = qkv.reshape(bsz, t, width)

    def cur(j):
        return lambda b, s: (b, s, j)

    def prev(j):
        return lambda b, s: (b, jnp.maximum(s * n_blk - 1, 0), j)

    blk = (None, rb, DIL_WIDTH)
    pblk = (None, BLOCK, DIL_WIDTH)
    o, lse = pl.pallas_call(
        functools.partial(_attn_kernel, nb=nb, rb=rb),
        grid=(bsz, t // rb),
        in_specs=[pl.BlockSpec(blk, cur(0)),
                  pl.BlockSpec(pblk, prev(1)), pl.BlockSpec(blk, cur(1)),
                  pl.BlockSpec(pblk, prev(2)), pl.BlockSpec(blk, cur(2)),
                  pl.BlockSpec((DIL_HEADS, BLOCK, 2 * BLOCK), lambda b, s: (0, 0, 0))],
        out_specs=[pl.BlockSpec(blk, lambda b, s: (b, s, 0)),
                   pl.BlockSpec((None, rb, LANES), lambda b, s: (b, s, 0))],
        out_shape=[jax.ShapeDtypeStruct((bsz, t, DIL_WIDTH), F32),
                   jax.ShapeDtypeStruct((bsz, t, LANES), F32)],
        compiler_params=_cparams(2),
        name=f"dilated_attn_g{g}",
    )(flat, flat, flat, flat, flat, bias)
    return o.reshape(bsz, dil, length, DIL_WIDTH), lse.reshape(bsz, dil, length, LANES)


def _merge_out_ln_kernel(o0_ref, o1_ref, o2_ref, l0_ref, l1_ref, l2_ref, w_ref, x_ref, mod_ref,
                         lng_ref, lnb_ref, o_ref, o_nat, l_nat, *, gate_row, tm):
    def natural(src_ref, scr, dil):
        if dil == 1:
            return src_ref[0, 0]
        n_col = src_ref.shape[-1] // LANES
        for c in range(n_col):
            for r in range(dil):
                scr[c, pl.ds(r, tm // dil, stride=dil), :] = src_ref[0, r, :, c * LANES:(c + 1) * LANES]
        return jnp.concatenate([scr[c] for c in range(n_col)], axis=1)

    dils = [dil for _, dil in DIL_PATTERNS]
    lses = [natural(ref, l_nat, dil) for ref, dil in zip((l0_ref, l1_ref, l2_ref), dils)]
    m = jnp.maximum(jnp.maximum(lses[0], lses[1]), lses[2])
    es = [jnp.exp(l - m) for l in lses]
    inv = 1.0 / (es[0] + es[1] + es[2])
    hr = lax.broadcasted_iota(jnp.int32, (LANES, DIL_WIDTH), 0)
    hc = lax.broadcasted_iota(jnp.int32, (LANES, DIL_WIDTH), 1) // DIL_HEAD_DIM
    expand = jnp.where(hr == hc, 1.0, 0.0).astype(BF16)

    def widen(wt):
        hi = wt.astype(BF16)
        lo = (wt - hi.astype(F32)).astype(BF16)
        return (jnp.dot(hi, expand, preferred_element_type=F32)
                + jnp.dot(lo, expand, preferred_element_type=F32))

    merged = None
    for e, ref, dil in zip(es, (o0_ref, o1_ref, o2_ref), dils):
        term = widen(e * inv) * natural(ref, o_nat, dil)
        merged = term if merged is None else merged + term
    y = jnp.dot(merged.astype(BF16), w_ref[...], preferred_element_type=F32)
    g = mod_ref[0][gate_row:gate_row + 1]
    z = DEEPNORM_ALPHA * x_ref[0] + (1.0 + g) * y
    o_ref[0] = _layer_norm_rows(z, lng_ref[...], lnb_ref[...])


def _merge_out_ln(outs, lses, w, x, mod, ln_g, ln_b, *, gate_row, tm=512):
    bsz, t, d = x.shape
    ospecs = [pl.BlockSpec((1, dil, tm // dil, DIL_WIDTH), lambda b, i: (b, 0, i, 0)) for _, dil in DIL_PATTERNS]
    lspecs = [pl.BlockSpec((1, dil, tm // dil, LANES), lambda b, i: (b, 0, i, 0)) for _, dil in DIL_PATTERNS]
    return pl.pallas_call(
        functools.partial(_merge_out_ln_kernel, gate_row=gate_row, tm=tm),
        grid=(bsz, t // tm),
        in_specs=ospecs + lspecs + [
            pl.BlockSpec((DIL_WIDTH, d), lambda b, i: (0, 0)),
            pl.BlockSpec((1, tm, d), lambda b, i: (b, i, 0)),
            pl.BlockSpec((1, 6, d), lambda b, i: (b, 0, 0)),
            pl.BlockSpec((1, d), lambda b, i: (0, 0)),
            pl.BlockSpec((1, d), lambda b, i: (0, 0))],
        out_specs=pl.BlockSpec((1, tm, d), lambda b, i: (b, i, 0)),
        out_shape=jax.ShapeDtypeStruct((bsz, t, d), F32),
        scratch_shapes=[pltpu.VMEM((DIL_WIDTH // LANES, tm, LANES), F32), pltpu.VMEM((1, tm, LANES), F32)],
        compiler_params=_cparams(2),
        name="merge_out_ln",
    )(*outs, *lses, w, x, mod, ln_g, ln_b)


def _t5_bucket(dist):
    exact = N_BUCKETS // 2
    logd = jnp.log(jnp.maximum(dist, 1).astype(F32) / exact) / math.log(MAX_DISTANCE / exact)
    large = jnp.minimum(exact + (logd * (N_BUCKETS - exact)).astype(jnp.int32), N_BUCKETS - 1)
    return jnp.where(dist < exact, dist, large)


def _bias_tables(rel_bias):
    rel = jnp.arange(BLOCK + 1)
    tables = []
    for g, (window, dil) in enumerate(DIL_PATTERNS):
        span = window // dil
        bucket = _t5_bucket(jnp.clip(rel, 0, span) * dil)
        vec = rel_bias[bucket][:, g * DIL_HEADS:(g + 1) * DIL_HEADS].T.astype(F32)
        ext = jnp.concatenate([vec[:, ::-1], jnp.zeros((DIL_HEADS, BLOCK), F32)], axis=1)
        flat = jnp.tile(ext, (1, BLOCK))[:, :BLOCK * 2 * BLOCK]
        tables.append(flat.reshape(DIL_HEADS, BLOCK, 2 * BLOCK))
    qi = np.arange(BLOCK)[:, None]
    ki = np.arange(2 * BLOCK)[None, :]
    band = (ki >= qi) & (ki <= qi + BLOCK)
    return jnp.where(band, jnp.stack(tables), MASKED)


def _mlp_kernel(x_ref, mod_ref, w1_ref, w2_ref, lng_ref, lnb_ref, o_ref, *, shift_row, scale_row, gate_row, ff_chunk):
    m = mod_ref[0]
    x = x_ref[0]
    u = (x * (1.0 + m[scale_row:scale_row + 1]) + m[shift_row:shift_row + 1]).astype(BF16)
    acc = None
    for c in range(D_FF // ff_chunk):
        h = jnp.dot(u, w1_ref[:, c * ff_chunk:(c + 1) * ff_chunk], preferred_element_type=F32)
        h = jnp.maximum(h, 0.0)
        h = (h * h).astype(BF16)
        part = jnp.dot(h, w2_ref[c * ff_chunk:(c + 1) * ff_chunk, :], preferred_element_type=F32)
        acc = part if acc is None else acc + part
    z = DEEPNORM_ALPHA * x + (1.0 + m[gate_row:gate_row + 1]) * acc
    o_ref[0] = _layer_norm_rows(z, lng_ref[...], lnb_ref[...])


def _mlp_ln(x, mod, w1, w2, ln_g, ln_b, *, tm=512, ff_chunk=1024):
    bsz, t, d = x.shape
    return pl.pallas_call(
        functools.partial(_mlp_kernel, shift_row=3, scale_row=4, gate_row=5, ff_chunk=ff_chunk),
        grid=(bsz, t // tm),
        in_specs=[pl.BlockSpec((1, tm, d), lambda b, i: (b, i, 0)),
                  pl.BlockSpec((1, 6, d), lambda b, i: (b, 0, 0)),
                  pl.BlockSpec((d, D_FF), lambda b, i: (0, 0)),
                  pl.BlockSpec((D_FF, d), lambda b, i: (0, 0)),
                  pl.BlockSpec((1, d), lambda b, i: (0, 0)),
                  pl.BlockSpec((1, d), lambda b, i: (0, 0))],
        out_specs=pl.BlockSpec((1, tm, d), lambda b, i: (b, i, 0)),
        out_shape=jax.ShapeDtypeStruct((bsz, t, d), F32),
        compiler_params=_cparams(2),
        name="mlp_ln",
    )(x, mod, w1, w2, ln_g, ln_b)


def _prep_even_weights(w_in, mu, w_up, a_up, g_up):
    w3 = 3 * RW_WIDTH
    lr_end = RW_PROJ
    pad = LR_PAD - (RW_PROJ - w3)
    d = w_in.shape[0]
    w_perm = jnp.concatenate([w_in[:, :w3], w_in[:, lr_end:], w_in[:, w3:lr_end],
                              jnp.zeros((d, pad), w_in.dtype)], axis=1).astype(BF16)
    mu_rkv = mu[:w3].reshape(1, w3)
    mu_lr = jnp.concatenate([mu[w3:], jnp.zeros((pad,), mu.dtype)]).reshape(1, LR_PAD)
    zeros = jnp.zeros((RW_DECAY_RANK, RW_WIDTH), w_up.dtype)
    wwa = jnp.concatenate([jnp.concatenate([w_up, zeros], axis=1),
                           jnp.concatenate([zeros, a_up], axis=1)], axis=0).astype(BF16)
    gup = jnp.concatenate([g_up, jnp.zeros((pad, RW_WIDTH), g_up.dtype)], axis=0).astype(BF16)
    return w_perm, mu_rkv, mu_lr, wwa, gup


def kernel(x, c, ada_w, ada_b, ln_g, ln_b, ab_w_in, rw_mu, rw_w0, rw_w_up, rw_a0, rw_a_up, rw_g_up,
           rw_k_k, rw_k_a, rw_r_k, rw_lnx_g, rw_lnx_b, sc_conv_w, ab_w_out, dil_w_qkv, dil_w_out,
           rel_bias, mlp_w1, mlp_w2):
    bsz, t, d = x.shape
    mods = _adaln(c, ada_w, ada_b).reshape(DEPTH, bsz, 6, d)
    bias = None
    row = lambda a: a.reshape(1, -1)
    for i in range(DEPTH):
        mod = mods[i]
        j = i // 2
        if i % 2 == 0:
            w_perm, mu_rkv, mu_lr, wwa, gup = _prep_even_weights(
                ab_w_in[j], rw_mu[j], rw_w_up[j], rw_a_up[j], rw_g_up[j])
            p = _modulated_proj(x, mod, w_perm, shift_row=0, scale_row=1, out_dtype=F32)
            y = _rwkv_shortconv(p, mu_rkv, mu_lr, row(rw_w0[j]), row(rw_a0[j]), wwa, gup,
                                row(rw_k_k[j]), row(rw_k_a[j]), row(rw_r_k[j]),
                                row(rw_lnx_g[j]), row(rw_lnx_b[j]), sc_conv_w[j])
            x = _out_ln(y, ab_w_out[j].astype(BF16), x, mod, row(ln_g[i, 0]), row(ln_b[i, 0]), gate_row=2)
        else:
            if bias is None:
                bias = _bias_tables(rel_bias)
            qkvs = _qkv_proj(x, mod, dil_w_qkv[j].astype(BF16))
            outs, lses = [], []
            for g in range(N_GROUPS):
                o, lse = _dilated_attention_group(qkvs[g], bias[g], g)
                outs.append(o)
                lses.append(lse)
            x = _merge_out_ln(outs, lses, dil_w_out[j].astype(BF16), x, mod,
                              row(ln_g[i, 0]), row(ln_b[i, 0]), gate_row=2)
        x = _mlp_ln(x, mod, mlp_w1[i].astype(BF16), mlp_w2[i].astype(BF16),
                    row(ln_g[i, 1]), row(ln_b[i, 1]))
    return x
```

```python
import functools
import math

import jax
import jax.numpy as jnp
import numpy as np
from jax import lax
from jax.experimental import pallas as pl
from jax.experimental.pallas import tpu as pltpu

F32 = jnp.float32
BF16 = jnp.bfloat16

D_MODEL = 1024
DEPTH = 2
RW_HEADS = 8
RW_HEAD_DIM = 64
RW_WIDTH = 512
RW_DECAY_RANK = 64
RW_ICLR_RANK = 64
RW_GATE_RANK = 160
RW_GN_EPS = 64e-5
RW_PROJ = 3 * RW_WIDTH + RW_DECAY_RANK + RW_ICLR_RANK + RW_GATE_RANK
SC_WIDTH = 512
SC_CONV = 3
DIL_PATTERNS = ((128, 1), (512, 4), (2048, 16))
N_GROUPS = 3
DIL_HEADS = 8
DIL_HEAD_DIM = 64
DIL_WIDTH = 512
BLOCK = 128
N_BUCKETS = 32
MAX_DISTANCE = 2048
D_FF = 4 * D_MODEL
DEEPNORM_ALPHA = (2 * DEPTH) ** 0.25
LN_EPS = 1e-5

LANES = 128
MXU_DIM = 256
CHUNK = 64
LR_PAD = 384
VMEM_LIMIT = 56 * 1024 * 1024
MASKED = -1e30


def _cparams(n_axes):
    return pltpu.CompilerParams(dimension_semantics=("arbitrary",) * n_axes,
                                vmem_limit_bytes=VMEM_LIMIT)


def _sigmoid(x):
    return 1.0 / (1.0 + jnp.exp(-x))


def _dot(a, b):
    return jnp.dot(a.astype(BF16), b.astype(BF16), preferred_element_type=F32)


def _layer_norm_rows(z, g, b):
    mu = jnp.mean(z, axis=-1, keepdims=True)
    zc = z - mu
    var = jnp.mean(zc * zc, axis=-1, keepdims=True)
    return zc * lax.rsqrt(var + LN_EPS) * g + b


def _adaln_kernel(c_ref, w_ref, b_ref, o_ref):
    c = c_ref[...]
    cond = c * _sigmoid(c)
    o_ref[0] = _dot(cond, w_ref[0]) + b_ref[0]


def _adaln(c, ada_w, ada_b):
    depth, d, n = ada_w.shape
    bsz = c.shape[0]
    tn = 1536
    return pl.pallas_call(
        _adaln_kernel,
        grid=(depth, n // tn),
        in_specs=[pl.BlockSpec((bsz, d), lambda i, j: (0, 0)),
                  pl.BlockSpec((1, d, tn), lambda i, j: (i, 0, j)),
                  pl.BlockSpec((1, 1, tn), lambda i, j: (i, 0, j))],
        out_specs=pl.BlockSpec((1, bsz, tn), lambda i, j: (i, 0, j)),
        out_shape=jax.ShapeDtypeStruct((depth, bsz, n), F32),
        compiler_params=_cparams(2),
        name="adaln",
    )(c, ada_w, ada_b.reshape(depth, 1, n))


def _shift_rows(x, carry, k):
    rolled = pltpu.roll(x, k, 0)
    head = rolled[:8]
    row = lax.broadcasted_iota(jnp.int32, head.shape, 0)
    for i in range(k):
        head = jnp.where(row == i, carry[8 - k + i:8 - k + i + 1], head)
    return jnp.concatenate([head, rolled[8:]], axis=0)


def _ab_proj_kernel(x_ref, mod_ref, w_ref, mu_ref, convw_ref, rkv_ref, lr_ref, yb_ref, carry_p, carry_z, *, tm):
    @pl.when(pl.program_id(1) == 0)
    def _():
        carry_p[...] = jnp.zeros_like(carry_p)
        carry_z[...] = jnp.zeros_like(carry_z)

    m = mod_ref[0]
    u = (x_ref[0] * (1.0 + m[1:2]) + m[0:1]).astype(BF16)

    def proj(c0, width):
        return jnp.dot(u, w_ref[:, c0:c0 + width], preferred_element_type=F32)

    def mixed(c0, m0, width):
        p = proj(c0, width)
        prev = _shift_rows(p, carry_p[:, m0:m0 + width], 1)
        carry_p[:, m0:m0 + width] = p[tm - 8:]
        return (p + mu_ref[:, m0:m0 + width] * (prev - p)).astype(BF16)

    w3 = 3 * RW_WIDTH
    for c in range(3):
        rkv_ref[0, :, c * RW_WIDTH:(c + 1) * RW_WIDTH] = mixed(c * RW_WIDTH, c * RW_WIDTH, RW_WIDTH)
    lr_ref[0] = mixed(2 * w3, w3, LR_PAD)

    h = proj(w3, SC_WIDTH)
    z = proj(w3 + 2 * SC_WIDTH, SC_WIDTH) * h
    zc = carry_z[...]
    cw = convw_ref[...]
    conv = cw[2:3] * z + cw[1:2] * _shift_rows(z, zc, 1) + cw[0:1] * _shift_rows(z, zc, 2)
    carry_z[...] = z[tm - 8:]
    yb_ref[0] = (proj(w3 + SC_WIDTH, SC_WIDTH) * conv).astype(BF16)


def _ab_proj(x, mod, w, mu, conv_w, *, tm=1024):
    bsz, t, d = x.shape
    w3 = 3 * RW_WIDTH

    def const(shape):
        return pl.BlockSpec(shape, lambda b, i: (0,) * len(shape), pipeline_mode=pl.Buffered(1))

    def rows(width):
        return pl.BlockSpec((1, tm, width), lambda b, i: (b, i, 0))

    return pl.pallas_call(
        functools.partial(_ab_proj_kernel, tm=tm),
        grid=(bsz, t // tm),
        in_specs=[rows(d), pl.BlockSpec((1, 6, d), lambda b, i: (b, 0, 0)),
                  const(w.shape), const((1, w3 + LR_PAD)), const((SC_CONV, SC_WIDTH))],
        out_specs=[rows(w3), rows(LR_PAD), rows(SC_WIDTH)],
        out_shape=[jax.ShapeDtypeStruct((bsz, t, w3), BF16), jax.ShapeDtypeStruct((bsz, t, LR_PAD), BF16),
                   jax.ShapeDtypeStruct((bsz, t, SC_WIDTH), BF16)],
        scratch_shapes=[pltpu.VMEM((8, w3 + LR_PAD), F32), pltpu.VMEM((8, SC_WIDTH), F32)],
        compiler_params=_cparams(2),
        name="ab_proj",
    )(x, mod, w, mu, conv_w)


def _rwkv_chunk_terms(a_t, r_t, b_t, k_t, b_e, k_e, v, p_end):
    n = len(a_t)
    idx = range(n)
    row = lax.broadcasted_iota(jnp.int32, (CHUNK, LANES), 0)
    col = lax.broadcasted_iota(jnp.int32, (CHUNK, LANES), 1)
    s_idx = col % CHUNK
    head0 = col < RW_HEAD_DIM
    strict = s_idx < row
    incl = s_idx <= row
    near = strict & ((s_idx // 16) == (row // 16))
    far = strict & ((s_idx // 16) != (row // 16))
    eye = jnp.where(s_idx == row, 1.0, 0.0)
    r2 = lax.broadcasted_iota(jnp.int32, (LANES, 2 * LANES), 0)
    c2 = lax.broadcasted_iota(jnp.int32, (LANES, 2 * LANES), 1)
    bdmask2 = (r2 // RW_HEAD_DIM) == ((c2 % LANES) // RW_HEAD_DIM)
    eye_bd = jnp.where(lax.broadcasted_iota(jnp.int32, (LANES, LANES), 0)
                       == lax.broadcasted_iota(jnp.int32, (LANES, LANES), 1), 1.0, 0.0)
    zero16 = jnp.zeros((CHUNK, LANES), BF16)
    lane1 = lax.broadcasted_iota(jnp.int32, (1, LANES), 1)
    m0_16 = jnp.where(lane1 < RW_HEAD_DIM, 1.0, 0.0).astype(BF16)
    m1_16 = jnp.where(lane1 < RW_HEAD_DIM, 0.0, 1.0).astype(BF16)

    def cast(xs):
        return [x.astype(BF16) for x in xs]

    def bd(q16):
        return jnp.concatenate([q16 * m0_16, q16 * m1_16], axis=0)

    def mm(p16, q16):
        return jnp.dot(p16, q16, preferred_element_type=F32)

    def mm_nt(p16, q16):
        return lax.dot_general(p16, q16, (((1,), (1,)), ((), ())), preferred_element_type=F32)

    a16, r16, v16 = cast(a_t), cast(r_t), cast(v)
    lhs = [jnp.concatenate([a16[i], r16[i]], axis=0) for i in idx]
    b16, k16 = cast(b_t), cast(k_t)
    bdv = [bd(x) for x in v16]
    g = [mm_nt(lhs[i], jnp.concatenate([bd(b16[i]), bd(k16[i])], axis=0)) for i in idx]
    x1 = [jnp.where(near, g[i][:CHUNK, :LANES], 0.0) for i in idx]
    x1_16 = cast(x1)
    l_e16 = [jnp.where(far, g[i][:CHUNK, :LANES], 0.0).astype(BF16) for i in idx]
    a_k16 = [jnp.concatenate([jnp.where(strict, g[i][:CHUNK, LANES:], 0.0),
                              jnp.where(incl, g[i][CHUNK:, LANES:], 0.0)], axis=0).astype(BF16) for i in idx]
    a_rb16 = [jnp.where(incl, g[i][CHUNK:, :LANES], 0.0).astype(BF16) for i in idx]

    def both(p16, q16):
        return jnp.concatenate([bd(p16), bd(q16)], axis=1)

    x2_16 = cast([mm(x1_16[i], bd(x1_16[i])) for i in idx])
    acc = [eye + x1[i] for i in idx]
    sq = [mm(x2_16[i], both(x2_16[i], acc[i].astype(BF16))) for i in idx]
    x4_16 = [sq[i][:, :LANES].astype(BF16) for i in idx]
    acc = [acc[i] + sq[i][:, LANES:] for i in idx]
    sq = [mm(x4_16[i], both(x4_16[i], acc[i].astype(BF16))) for i in idx]
    x8_16 = [sq[i][:, :LANES].astype(BF16) for i in idx]
    acc = [acc[i] + sq[i][:, LANES:] for i in idx]
    dinv = [acc[i] + mm(x8_16[i], bd(acc[i].astype(BF16))) for i in idx]
    dinv16 = cast(dinv)
    f16 = cast([mm(dinv16[i], bd(l_e16[i])) for i in idx])
    sq = [mm(f16[i], both(f16[i], dinv16[i])) for i in idx]
    f2_16 = [sq[i][:, :LANES].astype(BF16) for i in idx]
    hmat = [dinv[i] + sq[i][:, LANES:] for i in idx]
    tinv16 = cast([hmat[i] + mm(f2_16[i], bd(hmat[i].astype(BF16))) for i in idx])

    kv = [mm(a_k16[i], bdv[i]) for i in idx]
    wv16 = [kv[i][:CHUNK].astype(BF16) for i in idx]
    aw = [mm(tinv16[i], both(a16[i], wv16[i])) for i in idx]
    aw16 = cast(aw)
    qy = [mm(a_rb16[i], both(aw16[i][:, :LANES], aw16[i][:, LANES:]))
          + jnp.concatenate([r_t[i], kv[i][CHUNK:]], axis=1) for i in idx]
    be_ke_t = [jnp.concatenate([b_e[i], k_e[i]], axis=0).T.astype(BF16) for i in idx]
    rhs = [jnp.concatenate([aw16[i], jnp.concatenate([zero16, v16[i]], axis=1)], axis=0) for i in idx]
    mn = [jnp.where(bdmask2, mm(be_ke_t[i], rhs[i]), 0.0) for i in idx]
    qhat = [x[:, :LANES] for x in qy]
    yhat = [x[:, LANES:] for x in qy]
    mmat = [eye_bd * p_end[i] + mn[i][:, :LANES] for i in idx]
    nmat = [x[:, LANES:] for x in mn]
    return qhat, yhat, mmat, nmat


def _rwkv_kernel(rkv_ref, lr_ref, w0_ref, a0_ref, wwa_ref, gup_ref,
                 kk_ref, ka_ref, rk_ref, lnxg_ref, lnxb_ref, o_ref, state, yout_s, *, tb):
    n_chunks = tb // CHUNK
    n_pairs = RW_WIDTH // LANES

    @pl.when(pl.program_id(1) == 0)
    def _():
        state[...] = jnp.zeros_like(state)

    r = rkv_ref[0, :, :RW_WIDTH].astype(F32)
    k = rkv_ref[0, :, RW_WIDTH:2 * RW_WIDTH].astype(F32)
    v = rkv_ref[0, :, 2 * RW_WIDTH:].astype(F32)
    lr_m = lr_ref[0].astype(F32)

    wa = lr_m[:, :LANES]
    lane = lax.broadcasted_iota(jnp.int32, wa.shape, 1)
    wa = jnp.where(lane < RW_DECAY_RANK, jnp.tanh(wa), wa)
    wa_up = _dot(wa, wwa_ref[...])
    ld = -math.exp(-0.5) * _sigmoid(w0_ref[...] + wa_up[:, :RW_WIDTH])
    iclr = _sigmoid(a0_ref[...] + wa_up[:, RW_WIDTH:])
    gate = _dot(_sigmoid(lr_m[:, LANES:]), gup_ref[...])

    hr = lax.broadcasted_iota(jnp.int32, (MXU_DIM, MXU_DIM), 0) // RW_HEAD_DIM
    hc = lax.broadcasted_iota(jnp.int32, (MXU_DIM, MXU_DIM), 1) // RW_HEAD_DIM
    seg = jnp.where(hr == hc, 1.0 / RW_HEAD_DIM, 0.0).astype(BF16)

    def split3(x):
        hi = x.astype(BF16)
        rem = x - hi.astype(F32)
        mid = rem.astype(BF16)
        return hi, mid, (rem - mid.astype(F32)).astype(BF16)

    def seg_mean(x, terms=1):
        parts = split3(x)[:terms]
        cols = []
        for c in range(0, RW_WIDTH, MXU_DIM):
            cols.append(sum(jnp.dot(part[:, c:c + MXU_DIM], seg, preferred_element_type=F32) for part in parts))
        return jnp.concatenate(cols, axis=1)

    kk = k * kk_ref[...]
    ss = seg_mean(kk * kk) * RW_HEAD_DIM
    kk = kk * lax.rsqrt(jnp.maximum(ss, 1e-24))
    kh = k * (1.0 + (iclr - 1.0) * ka_ref[...])

    tr = lax.broadcasted_iota(jnp.int32, (MXU_DIM, MXU_DIM), 0)
    tc = lax.broadcasted_iota(jnp.int32, (MXU_DIM, MXU_DIM), 1)
    tri = jnp.where((tc <= tr) & (tc // CHUNK == tr // CHUNK), 1.0, 0.0).astype(BF16)
    ld_parts = split3(ld)
    lp = jnp.concatenate(
        [sum(jnp.dot(tri, part[t0:t0 + MXU_DIM], preferred_element_type=F32) for part in ld_parts)
         for t0 in range(0, tb, MXU_DIM)], axis=0)

    a_t, r_t, b_t, k_t, b_e, k_e, vs, p_end = [], [], [], [], [], [], [], []
    b = kk * iclr
    for c in range(n_chunks):
        rs = slice(c * CHUNK, (c + 1) * CHUNK)
        lp_c = lp[rs]
        lp_last = lp_c[CHUNK - 1:CHUNK]
        e_in = jnp.exp(lp_c)
        e_inv = jnp.exp(-lp_c)
        e_end = jnp.exp(lp_last - lp_c)
        a_c = -kk[rs] * jnp.exp(lp_c - ld[rs])
        r_c = r[rs] * e_in
        b_c = b[rs] * e_inv
        k_c = kh[rs] * e_inv
        be_c = b[rs] * e_end
        ke_c = kh[rs] * e_end
        pe_c = jnp.exp(lp_last)
        for p in range(n_pairs):
            cs = slice(p * LANES, (p + 1) * LANES)
            a_t.append(a_c[:, cs])
            r_t.append(r_c[:, cs])
            b_t.append(b_c[:, cs])
            k_t.append(k_c[:, cs])
            b_e.append(be_c[:, cs])
            k_e.append(ke_c[:, cs])
            vs.append(v[rs, cs])
            p_end.append(pe_c[:, cs])

    qhat, yhat, mmat, nmat = _rwkv_chunk_terms(a_t, r_t, b_t, k_t, b_e, k_e, vs, p_end)

    s = [state[p] for p in range(n_pairs)]
    for c in range(n_chunks):
        ym = [_dot(jnp.concatenate([qhat[c * n_pairs + p], mmat[c * n_pairs + p]], axis=0), s[p])
              for p in range(n_pairs)]
        for p in range(n_pairs):
            i = c * n_pairs + p
            yout_s[c * CHUNK:(c + 1) * CHUNK, p * LANES:(p + 1) * LANES] = ym[p][:CHUNK] + yhat[i]
            s[p] = ym[p][CHUNK:] + nmat[i]
    for p in range(n_pairs):
        state[p] = s[p]

    y = yout_s[...]
    mean = seg_mean(y, terms=2)
    yc = y - mean
    var = seg_mean(yc * yc)
    y = yc * lax.rsqrt(var + RW_GN_EPS) * lnxg_ref[...] + lnxb_ref[...]
    bonus = seg_mean(r * kh * rk_ref[...]) * RW_HEAD_DIM * v
    o_ref[0] = ((y + bonus) * gate).astype(o_ref.dtype)


def _rwkv(rkv, lr, w0, a0, wwa, gup, k_k, k_a, r_k, lnx_g, lnx_b, *, tb=512):
    bsz, t, w3 = rkv.shape
    n_pairs = RW_WIDTH // LANES

    def full(shape):
        return pl.BlockSpec(shape, lambda b, i: (0,) * len(shape))

    return pl.pallas_call(
        functools.partial(_rwkv_kernel, tb=tb),
        grid=(bsz, t // tb),
        in_specs=[pl.BlockSpec((1, tb, w3), lambda b, i: (b, i, 0)),
                  pl.BlockSpec((1, tb, LR_PAD), lambda b, i: (b, i, 0)),
                  full((1, RW_WIDTH)), full((1, RW_WIDTH)),
                  full((LANES, 2 * RW_WIDTH)), full((LR_PAD - LANES, RW_WIDTH)),
                  full((1, RW_WIDTH)), full((1, RW_WIDTH)), full((1, RW_WIDTH)),
                  full((1, RW_WIDTH)), full((1, RW_WIDTH))],
        out_specs=pl.BlockSpec((1, tb, RW_WIDTH), lambda b, i: (b, i, 0)),
        out_shape=jax.ShapeDtypeStruct((bsz, t, RW_WIDTH), BF16),
        scratch_shapes=[pltpu.VMEM((n_pairs, LANES, LANES), F32),
                        pltpu.VMEM((tb, RW_WIDTH), F32)],
        compiler_params=_cparams(2),
        name="rwkv",
    )(rkv, lr, w0, a0, wwa, gup, k_k, k_a, r_k, lnx_g, lnx_b)


def _out_ln_kernel(ya_ref, yb_ref, w_ref, x_ref, mod_ref, lng_ref, lnb_ref, o_ref, *, gate_row):
    ka = ya_ref.shape[-1]
    y = (jnp.dot(ya_ref[0], w_ref[:ka], preferred_element_type=F32)
         + jnp.dot(yb_ref[0], w_ref[ka:], preferred_element_type=F32))
    g = mod_ref[0][gate_row:gate_row + 1]
    z = DEEPNORM_ALPHA * x_ref[0] + (1.0 + g) * y
    o_ref[0] = _layer_norm_rows(z, lng_ref[...], lnb_ref[...])


def _out_ln(ya, yb, w, x, mod, ln_g, ln_b, *, gate_row, tm=512):
    bsz, t, d = x.shape
    return pl.pallas_call(
        functools.partial(_out_ln_kernel, gate_row=gate_row),
        grid=(bsz, t // tm),
        in_specs=[pl.BlockSpec((1, tm, ya.shape[-1]), lambda b, i: (b, i, 0)),
                  pl.BlockSpec((1, tm, yb.shape[-1]), lambda b, i: (b, i, 0)),
                  pl.BlockSpec(w.shape, lambda b, i: (0, 0)),
                  pl.BlockSpec((1, tm, d), lambda b, i: (b, i, 0)),
                  pl.BlockSpec((1, 6, d), lambda b, i: (b, 0, 0)),
                  pl.BlockSpec((1, d), lambda b, i: (0, 0)),
                  pl.BlockSpec((1, d), lambda b, i: (0, 0))],
        out_specs=pl.BlockSpec((1, tm, d), lambda b, i: (b, i, 0)),
        out_shape=jax.ShapeDtypeStruct((bsz, t, d), F32),
        compiler_params=_cparams(2),
        name="out_ln",
    )(ya, yb, w, x, mod, ln_g, ln_b)


def _attn_kernel(q_ref, kp_ref, kc_ref, vp_ref, vc_ref, bias_ref, o_ref, lse_ref, *, nb, rb):
    step = pl.program_id(1)
    n_blk = rb // BLOCK
    lane = lax.broadcasted_iota(jnp.int32, (1, LANES), 1)
    scale = DIL_HEAD_DIM ** -0.5
    qmask = (jnp.where(lane < DIL_HEAD_DIM, scale, 0.0).astype(BF16),
             jnp.where(lane >= DIL_HEAD_DIM, scale, 0.0).astype(BF16))
    lane_o = lax.broadcasted_iota(jnp.int32, (BLOCK, LANES), 1)
    head0 = lane_o < DIL_HEAD_DIM

    ones = jnp.ones((BLOCK, LANES), BF16)

    def attend(rows, q, parts):
        heads = range(DIL_HEADS)
        lanes = [slice((h // 2) * LANES, (h // 2 + 1) * LANES) for h in heads]
        qh = [q[:, lanes[h]] * qmask[h % 2] for h in heads]
        scores = []
        for h in heads:
            row = []
            for k_blk, _, col0, pen in parts:
                s = lax.dot_general(qh[h], k_blk[:, lanes[h]], (((1,), (1,)), ((), ())),
                                    preferred_element_type=F32) + bias_ref[h, :, col0:col0 + BLOCK]
                row.append(s if pen is None else s + pen)
            scores.append(row)
        ms = []
        for h in heads:
            m = jnp.max(scores[h][0], axis=-1, keepdims=True)
            for s in scores[h][1:]:
                m = jnp.maximum(m, jnp.max(s, axis=-1, keepdims=True))
            ms.append(m)
        es = [[jnp.exp((s - ms[h]).astype(BF16)) for s in scores[h]] for h in heads]
        vaug = [[jnp.where(head0 if h % 2 == 0 else jnp.logical_not(head0), part[1][:, lanes[h]], ones)
                 for part in parts] for h in heads]
        res = [sum(jnp.dot(e, v, preferred_element_type=F32) for e, v in zip(es[h], vaug[h])) for h in heads]
        m_all = jnp.zeros((BLOCK, LANES), F32)
        den_all = jnp.ones((BLOCK, LANES), F32)
        for p in range(DIL_WIDTH // LANES):
            r0, r1 = res[2 * p], res[2 * p + 1]
            swapped = jnp.where(head0, r1, r0)
            den = pltpu.roll(swapped, DIL_HEAD_DIM, 1)
            o_ref[rows, p * LANES:(p + 1) * LANES] = jnp.where(head0, r0, r1) / den
            m_all = jnp.where(lane_o == 2 * p, ms[2 * p], jnp.where(lane_o == 2 * p + 1, ms[2 * p + 1], m_all))
            den_all = jnp.where(lane_o == 2 * p, den, jnp.where(lane_o == 2 * p + 1, swapped, den_all))
        lse_ref[rows, :] = m_all + jnp.log(den_all)

    for j in range(n_blk):
        rows = slice(j * BLOCK, (j + 1) * BLOCK)
        q = q_ref[rows, :]
        cur = (kc_ref[rows, :], vc_ref[rows, :], BLOCK, None)
        if n_blk % nb == 0:
            prev = "none" if j % nb == 0 else "inside"
        else:
            prev = "inside" if j > 0 else ("none" if nb == n_blk else "before")
        if prev == "none":
            attend(rows, q, [cur])
        elif prev == "inside":
            before = slice((j - 1) * BLOCK, j * BLOCK)
            attend(rows, q, [(kc_ref[before, :], vc_ref[before, :], 0, None), cur])
        else:
            has_prev = (step % (nb // n_blk)) != 0
            pen = jnp.where(has_prev, 0.0, MASKED)
            attend(rows, q, [(kp_ref[...], vp_ref[...], 0, pen), cur])


def _qkv_proj_kernel(x_ref, mod_ref, w_ref, o0_ref, o1_ref, o2_ref, xs_ref, *, tm):
    m = mod_ref[0]
    scale = 1.0 + m[1:2]
    shift = m[0:1]
    j = pl.program_id(2)
    n_col = x_ref.shape[-1] // LANES

    @pl.when(j == 0)
    def _():
        for c in range(n_col):
            xs_ref[c] = x_ref[0, :, c * LANES:(c + 1) * LANES]

    def group(o_ref, dil):
        rows = tm // dil
        if dil == 1:
            xp = x_ref[0]
        else:
            xp = jnp.concatenate(
                [jnp.concatenate([xs_ref[c, pl.ds(r, rows, stride=dil), :] for r in range(dil)], axis=0)
                 for c in range(n_col)], axis=1)
        u = (xp * scale + shift).astype(BF16)
        res = jnp.dot(u, w_ref[...], preferred_element_type=F32).astype(BF16)
        for r in range(dil):
            o_ref[0, r] = res[r * rows:(r + 1) * rows]

    for g, o_ref in enumerate((o0_ref, o1_ref, o2_ref)):
        pl.when(j == g)(functools.partial(group, o_ref, DIL_PATTERNS[g][1]))


def _qkv_proj(x, mod, w, *, tm=1024):
    bsz, t, d = x.shape
    gw = 3 * DIL_WIDTH
    out_specs, out_shape = [], []
    for _, dil in DIL_PATTERNS:
        out_specs.append(pl.BlockSpec((1, dil, tm // dil, gw), lambda b, i, j: (b, 0, i, 0)))
        out_shape.append(jax.ShapeDtypeStruct((bsz, dil, t // dil, gw), BF16))
    return pl.pallas_call(
        functools.partial(_qkv_proj_kernel, tm=tm),
        grid=(bsz, t // tm, N_GROUPS),
        in_specs=[pl.BlockSpec((1, tm, d), lambda b, i, j: (b, i, 0)),
                  pl.BlockSpec((1, 6, d), lambda b, i, j: (b, 0, 0)),
                  pl.BlockSpec((d, gw), lambda b, i, j: (0, j))],
        out_specs=out_specs,
        out_shape=out_shape,
        scratch_shapes=[pltpu.VMEM((d // LANES, tm, LANES), F32)],
        compiler_params=_cparams(3),
        name="qkv_proj",
    )(x, mod, w)


def _dilated_attention_group(qkv, bias, g, *, rb=512):
    bsz, dil, length, width = qkv.shape
    t = dil * length
    nb = length // BLOCK
    n_blk = rb // BLOCK
    assert nb % n_blk == 0 or n_blk % nb == 0
    flat = qkv.reshape(bsz, t, width)

    def cur(j):
        return lambda b, s: (b, s, j)

    def prev(j):
        return lambda b, s: (b, jnp.maximum(s * n_blk - 1, 0), j)

    blk = (None, rb, DIL_WIDTH)
    pblk = (None, BLOCK, DIL_WIDTH)
    o, lse = pl.pallas_call(
        functools.partial(_attn_kernel, nb=nb, rb=rb),
        grid=(bsz, t // rb),
        in_specs=[pl.BlockSpec(blk, cur(0)),
                  pl.BlockSpec(pblk, prev(1)), pl.BlockSpec(blk, cur(1)),
                  pl.BlockSpec(pblk, prev(2)), pl.BlockSpec(blk, cur(2)),
                  pl.BlockSpec((DIL_HEADS, BLOCK, 2 * BLOCK), lambda b, s: (0, 0, 0))],
        out_specs=[pl.BlockSpec(blk, lambda b, s: (b, s, 0)),
                   pl.BlockSpec((None, rb, LANES), lambda b, s: (b, s, 0))],
        out_shape=[jax.ShapeDtypeStruct((bsz, t, DIL_WIDTH), F32),
                   jax.ShapeDtypeStruct((bsz, t, LANES), F32)],
        compiler_params=_cparams(2),
        name=f"dilated_attn_g{g}",
    )(flat, flat, flat, flat, flat, bias)
    return o.reshape(bsz, dil, length, DIL_WIDTH), lse.reshape(bsz, dil, length, LANES)


def _merge_out_ln_kernel(o0_ref, o1_ref, o2_ref, l0_ref, l1_ref, l2_ref, w_ref, x_ref, mod_ref,
                         lng_ref, lnb_ref, o_ref, o_nat, l_nat, *, gate_row, tm):
    def natural(src_ref, scr, dil):
        if dil == 1:
            return src_ref[0, 0]
        n_col = src_ref.shape[-1] // LANES
        for c in range(n_col):
            for r in range(dil):
                scr[c, pl.ds(r, tm // dil, stride=dil), :] = src_ref[0, r, :, c * LANES:(c + 1) * LANES]
        return jnp.concatenate([scr[c] for c in range(n_col)], axis=1)

    dils = [dil for _, dil in DIL_PATTERNS]
    lses = [natural(ref, l_nat, dil) for ref, dil in zip((l0_ref, l1_ref, l2_ref), dils)]
    m = jnp.maximum(jnp.maximum(lses[0], lses[1]), lses[2])
    es = [jnp.exp(l - m) for l in lses]
    inv = 1.0 / (es[0] + es[1] + es[2])
    hr = lax.broadcasted_iota(jnp.int32, (LANES, DIL_WIDTH), 0)
    hc = lax.broadcasted_iota(jnp.int32, (LANES, DIL_WIDTH), 1) // DIL_HEAD_DIM
    expand = jnp.where(hr == hc, 1.0, 0.0).astype(BF16)

    def widen(wt):
        hi = wt.astype(BF16)
        lo = (wt - hi.astype(F32)).astype(BF16)
        return (jnp.dot(hi, expand, preferred_element_type=F32)
                + jnp.dot(lo, expand, preferred_element_type=F32))

    merged = None
    for e, ref, dil in zip(es, (o0_ref, o1_ref, o2_ref), dils):
        term = widen(e * inv) * natural(ref, o_nat, dil)
        merged = term if merged is None else merged + term
    y = jnp.dot(merged.astype(BF16), w_ref[...], preferred_element_type=F32)
    g = mod_ref[0][gate_row:gate_row + 1]
    z = DEEPNORM_ALPHA * x_ref[0] + (1.0 + g) * y
    o_ref[0] = _layer_norm_rows(z, lng_ref[...], lnb_ref[...])


def _merge_out_ln(outs, lses, w, x, mod, ln_g, ln_b, *, gate_row, tm=512):
    bsz, t, d = x.shape
    ospecs = [pl.BlockSpec((1, dil, tm // dil, DIL_WIDTH), lambda b, i: (b, 0, i, 0)) for _, dil in DIL_PATTERNS]
    lspecs = [pl.BlockSpec((1, dil, tm // dil, LANES), lambda b, i: (b, 0, i, 0)) for _, dil in DIL_PATTERNS]
    return pl.pallas_call(
        functools.partial(_merge_out_ln_kernel, gate_row=gate_row, tm=tm),
        grid=(bsz, t // tm),
        in_specs=ospecs + lspecs + [
            pl.BlockSpec((DIL_WIDTH, d), lambda b, i: (0, 0)),
            pl.BlockSpec((1, tm, d), lambda b, i: (b, i, 0)),
            pl.BlockSpec((1, 6, d), lambda b, i: (b, 0, 0)),
            pl.BlockSpec((1, d), lambda b, i: (0, 0)),
            pl.BlockSpec((1, d), lambda b, i: (0, 0))],
        out_specs=pl.BlockSpec((1, tm, d), lambda b, i: (b, i, 0)),
        out_shape=jax.ShapeDtypeStruct((bsz, t, d), F32),
        scratch_shapes=[pltpu.VMEM((DIL_WIDTH // LANES, tm, LANES), F32), pltpu.VMEM((1, tm, LANES), F32)],
        compiler_params=_cparams(2),
        name="merge_out_ln",
    )(*outs, *lses, w, x, mod, ln_g, ln_b)


def _t5_bucket(dist):
    exact = N_BUCKETS // 2
    logd = jnp.log(jnp.maximum(dist, 1).astype(F32) / exact) / math.log(MAX_DISTANCE / exact)
    large = jnp.minimum(exact + (logd * (N_BUCKETS - exact)).astype(jnp.int32), N_BUCKETS - 1)
    return jnp.where(dist < exact, dist, large)


def _bias_tables(rel_bias):
    rel = jnp.arange(BLOCK + 1)
    tables = []
    for g, (window, dil) in enumerate(DIL_PATTERNS):
        span = window // dil
        bucket = _t5_bucket(jnp.clip(rel, 0, span) * dil)
        vec = rel_bias[bucket][:, g * DIL_HEADS:(g + 1) * DIL_HEADS].T.astype(F32)
        ext = jnp.concatenate([vec[:, ::-1], jnp.zeros((DIL_HEADS, BLOCK), F32)], axis=1)
        flat = jnp.tile(ext, (1, BLOCK))[:, :BLOCK * 2 * BLOCK]
        tables.append(flat.reshape(DIL_HEADS, BLOCK, 2 * BLOCK))
    qi = np.arange(BLOCK)[:, None]
    ki = np.arange(2 * BLOCK)[None, :]
    band = (ki >= qi) & (ki <= qi + BLOCK)
    return jnp.where(band, jnp.stack(tables), MASKED)


def _mlp_kernel(x_ref, mod_ref, w1_ref, w2_ref, lng_ref, lnb_ref, o_ref, *, shift_row, scale_row, gate_row, ff_chunk):
    m = mod_ref[0]
    x = x_ref[0]
    u = (x * (1.0 + m[scale_row:scale_row + 1]) + m[shift_row:shift_row + 1]).astype(BF16)
    acc = None
    for c in range(D_FF // ff_chunk):
        h = jnp.dot(u, w1_ref[:, c * ff_chunk:(c + 1) * ff_chunk], preferred_element_type=F32)
        h = jnp.maximum(h, 0.0)
        h = (h * h).astype(BF16)
        part = jnp.dot(h, w2_ref[c * ff_chunk:(c + 1) * ff_chunk, :], preferred_element_type=F32)
        acc = part if acc is None else acc + part
    z = DEEPNORM_ALPHA * x + (1.0 + m[gate_row:gate_row + 1]) * acc
    o_ref[0] = _layer_norm_rows(z, lng_ref[...], lnb_ref[...])


def _mlp_ln(x, mod, w1, w2, ln_g, ln_b, *, tm=512, ff_chunk=1024):
    bsz, t, d = x.shape
    return pl.pallas_call(
        functools.partial(_mlp_kernel, shift_row=3, scale_row=4, gate_row=5, ff_chunk=ff_chunk),
        grid=(bsz, t // tm),
        in_specs=[pl.BlockSpec((1, tm, d), lambda b, i: (b, i, 0)),
                  pl.BlockSpec((1, 6, d), lambda b, i: (b, 0, 0)),
                  pl.BlockSpec((d, D_FF), lambda b, i: (0, 0)),
                  pl.BlockSpec((D_FF, d), lambda b, i: (0, 0)),
                  pl.BlockSpec((1, d), lambda b, i: (0, 0)),
                  pl.BlockSpec((1, d), lambda b, i: (0, 0))],
        out_specs=pl.BlockSpec((1, tm, d), lambda b, i: (b, i, 0)),
        out_shape=jax.ShapeDtypeStruct((bsz, t, d), F32),
        compiler_params=_cparams(2),
        name="mlp_ln",
    )(x, mod, w1, w2, ln_g, ln_b)


def _prep_even_weights(w_in, mu, w_up, a_up, g_up):
    w3 = 3 * RW_WIDTH
    pad = LR_PAD - (RW_PROJ - w3)
    d = w_in.shape[0]
    w_perm = jnp.concatenate([w_in[:, :w3], w_in[:, RW_PROJ:], w_in[:, w3:RW_PROJ],
                              jnp.zeros((d, pad), w_in.dtype)], axis=1).astype(BF16)
    mu_pad = jnp.concatenate([mu, jnp.zeros((pad,), mu.dtype)]).reshape(1, w3 + LR_PAD)
    zeros = jnp.zeros((RW_DECAY_RANK, RW_WIDTH), w_up.dtype)
    wwa = jnp.concatenate([jnp.concatenate([w_up, zeros], axis=1),
                           jnp.concatenate([zeros, a_up], axis=1)], axis=0).astype(BF16)
    gup = jnp.concatenate([g_up, jnp.zeros((pad, RW_WIDTH), g_up.dtype)], axis=0).astype(BF16)
    return w_perm, mu_pad, wwa, gup


def kernel(x, c, ada_w, ada_b, ln_g, ln_b, ab_w_in, rw_mu, rw_w0, rw_w_up, rw_a0, rw_a_up, rw_g_up,
           rw_k_k, rw_k_a, rw_r_k, rw_lnx_g, rw_lnx_b, sc_conv_w, ab_w_out, dil_w_qkv, dil_w_out,
           rel_bias, mlp_w1, mlp_w2):
    bsz, t, d = x.shape
    mods = _adaln(c, ada_w, ada_b).reshape(DEPTH, bsz, 6, d)
    bias = None
    row = lambda a: a.reshape(1, -1)
    for i in range(DEPTH):
        mod = mods[i]
        j = i // 2
        if i % 2 == 0:
            w_perm, mu_pad, wwa, gup = _prep_even_weights(
                ab_w_in[j], rw_mu[j], rw_w_up[j], rw_a_up[j], rw_g_up[j])
            rkv, lr, yb = _ab_proj(x, mod, w_perm, mu_pad, sc_conv_w[j])
            ya = _rwkv(rkv, lr, row(rw_w0[j]), row(rw_a0[j]), wwa, gup,
                       row(rw_k_k[j]), row(rw_k_a[j]), row(rw_r_k[j]),
                       row(rw_lnx_g[j]), row(rw_lnx_b[j]))
            x = _out_ln(ya, yb, ab_w_out[j].astype(BF16), x, mod, row(ln_g[i, 0]), row(ln_b[i, 0]), gate_row=2)
        else:
            if bias is None:
                bias = _bias_tables(rel_bias)
            qkvs = _qkv_proj(x, mod, dil_w_qkv[j].astype(BF16))
            outs, lses = [], []
            for g in range(N_GROUPS):
                o, lse = _dilated_attention_group(qkvs[g], bias[g], g)
                outs.append(o)
                lses.append(lse)
            x = _merge_out_ln(outs, lses, dil_w_out[j].astype(BF16), x, mod,
                              row(ln_g[i, 0]), row(ln_b[i, 0]), gate_row=2)
        x = _mlp_ln(x, mod, mlp_w1[i].astype(BF16), mlp_w2[i].astype(BF16),
                    row(ln_g[i, 1]), row(ln_b[i, 1]))
    return x
```

```python
import functools
import math

import jax
import jax.numpy as jnp
import numpy as np
from jax import lax
from jax.experimental import pallas as pl
from jax.experimental.pallas import tpu as pltpu

F32 = jnp.float32
BF16 = jnp.bfloat16

D_MODEL = 1024
DEPTH = 2
RW_HEADS = 8
RW_HEAD_DIM = 64
RW_WIDTH = 512
RW_DECAY_RANK = 64
RW_ICLR_RANK = 64
RW_GATE_RANK = 160
RW_GN_EPS = 64e-5
RW_PROJ = 3 * RW_WIDTH + RW_DECAY_RANK + RW_ICLR_RANK + RW_GATE_RANK
SC_WIDTH = 512
SC_CONV = 3
DIL_PATTERNS = ((128, 1), (512, 4), (2048, 16))
N_GROUPS = 3
DIL_HEADS = 8
DIL_HEAD_DIM = 64
DIL_WIDTH = 512
BLOCK = 128
N_BUCKETS = 32
MAX_DISTANCE = 2048
D_FF = 4 * D_MODEL
DEEPNORM_ALPHA = (2 * DEPTH) ** 0.25
LN_EPS = 1e-5

LANES = 128
MXU_DIM = 256
CHUNK = 64
LR_PAD = 384
VMEM_LIMIT = 56 * 1024 * 1024
MASKED = -1e30


def _cparams(n_axes):
    return pltpu.CompilerParams(dimension_semantics=("arbitrary",) * n_axes,
                                vmem_limit_bytes=VMEM_LIMIT)


def _sigmoid(x):
    return 1.0 / (1.0 + jnp.exp(-x))


def _dot(a, b):
    return jnp.dot(a.astype(BF16), b.astype(BF16), preferred_element_type=F32)


def _layer_norm_rows(z, g, b):
    mu = jnp.mean(z, axis=-1, keepdims=True)
    zc = z - mu
    var = jnp.mean(zc * zc, axis=-1, keepdims=True)
    return zc * lax.rsqrt(var + LN_EPS) * g + b


def _adaln_kernel(c_ref, w_ref, b_ref, o_ref):
    c = c_ref[...]
    cond = c * _sigmoid(c)
    o_ref[0] = _dot(cond, w_ref[0]) + b_ref[0]


def _adaln(c, ada_w, ada_b):
    depth, d, n = ada_w.shape
    bsz = c.shape[0]
    tn = 1536
    return pl.pallas_call(
        _adaln_kernel,
        grid=(depth, n // tn),
        in_specs=[pl.BlockSpec((bsz, d), lambda i, j: (0, 0)),
                  pl.BlockSpec((1, d, tn), lambda i, j: (i, 0, j)),
                  pl.BlockSpec((1, 1, tn), lambda i, j: (i, 0, j))],
        out_specs=pl.BlockSpec((1, bsz, tn), lambda i, j: (i, 0, j)),
        out_shape=jax.ShapeDtypeStruct((depth, bsz, n), F32),
        compiler_params=_cparams(2),
        name="adaln",
    )(c, ada_w, ada_b.reshape(depth, 1, n))


def _shift_rows(x, carry, k):
    rolled = pltpu.roll(x, k, 0)
    head = rolled[:8]
    row = lax.broadcasted_iota(jnp.int32, head.shape, 0)
    for i in range(k):
        head = jnp.where(row == i, carry[8 - k + i:8 - k + i + 1], head)
    return jnp.concatenate([head, rolled[8:]], axis=0)


def _ab_proj_kernel(x_ref, mod_ref, w_ref, mu_ref, convw_ref, rkv_ref, lr_ref, yb_ref, carry_p, carry_z, *, tm):
    @pl.when(pl.program_id(1) == 0)
    def _():
        carry_p[...] = jnp.zeros_like(carry_p)
        carry_z[...] = jnp.zeros_like(carry_z)

    m = mod_ref[0]
    u = (x_ref[0] * (1.0 + m[1:2]) + m[0:1]).astype(BF16)

    def proj(c0, width):
        return jnp.dot(u, w_ref[:, c0:c0 + width], preferred_element_type=F32)

    def mixed(c0, m0, width):
        p = proj(c0, width)
        prev = _shift_rows(p, carry_p[:, m0:m0 + width], 1)
        carry_p[:, m0:m0 + width] = p[tm - 8:]
        return (p + mu_ref[:, m0:m0 + width] * (prev - p)).astype(BF16)

    w3 = 3 * RW_WIDTH
    for c in range(3):
        rkv_ref[0, :, c * RW_WIDTH:(c + 1) * RW_WIDTH] = mixed(c * RW_WIDTH, c * RW_WIDTH, RW_WIDTH)
    lr_ref[0] = mixed(2 * w3, w3, LR_PAD)

    h = proj(w3, SC_WIDTH)
    z = proj(w3 + 2 * SC_WIDTH, SC_WIDTH) * h
    zc = carry_z[...]
    cw = convw_ref[...]
    conv = cw[2:3] * z + cw[1:2] * _shift_rows(z, zc, 1) + cw[0:1] * _shift_rows(z, zc, 2)
    carry_z[...] = z[tm - 8:]
    yb_ref[0] = (proj(w3 + SC_WIDTH, SC_WIDTH) * conv).astype(BF16)


def _ab_proj(x, mod, w, mu, conv_w, *, tm=1024):
    bsz, t, d = x.shape
    w3 = 3 * RW_WIDTH

    def const(shape):
        return pl.BlockSpec(shape, lambda b, i: (0,) * len(shape), pipeline_mode=pl.Buffered(1))

    def rows(width):
        return pl.BlockSpec((1, tm, width), lambda b, i: (b, i, 0))

    return pl.pallas_call(
        functools.partial(_ab_proj_kernel, tm=tm),
        grid=(bsz, t // tm),
        in_specs=[rows(d), pl.BlockSpec((1, 6, d), lambda b, i: (b, 0, 0)),
                  const(w.shape), const((1, w3 + LR_PAD)), const((SC_CONV, SC_WIDTH))],
        out_specs=[rows(w3), rows(LR_PAD), rows(SC_WIDTH)],
        out_shape=[jax.ShapeDtypeStruct((bsz, t, w3), BF16), jax.ShapeDtypeStruct((bsz, t, LR_PAD), BF16),
                   jax.ShapeDtypeStruct((bsz, t, SC_WIDTH), BF16)],
        scratch_shapes=[pltpu.VMEM((8, w3 + LR_PAD), F32), pltpu.VMEM((8, SC_WIDTH), F32)],
        compiler_params=_cparams(2),
        name="ab_proj",
    )(x, mod, w, mu, conv_w)


def _rwkv_chunk_terms(a_t, r_t, b_t, k_t, b_e, k_e, v, p_end):
    n = len(a_t)
    idx = range(n)
    row = lax.broadcasted_iota(jnp.int32, (CHUNK, LANES), 0)
    col = lax.broadcasted_iota(jnp.int32, (CHUNK, LANES), 1)
    s_idx = col % CHUNK
    head0 = col < RW_HEAD_DIM
    strict = s_idx < row
    incl = s_idx <= row
    near = strict & ((s_idx // 16) == (row // 16))
    far = strict & ((s_idx // 16) != (row // 16))
    eye = jnp.where(s_idx == row, 1.0, 0.0)
    r2 = lax.broadcasted_iota(jnp.int32, (LANES, 2 * LANES), 0)
    c2 = lax.broadcasted_iota(jnp.int32, (LANES, 2 * LANES), 1)
    bdmask2 = (r2 // RW_HEAD_DIM) == ((c2 % LANES) // RW_HEAD_DIM)
    eye_bd = jnp.where(lax.broadcasted_iota(jnp.int32, (LANES, LANES), 0)
                       == lax.broadcasted_iota(jnp.int32, (LANES, LANES), 1), 1.0, 0.0)
    zero16 = jnp.zeros((CHUNK, LANES), BF16)
    lane1 = lax.broadcasted_iota(jnp.int32, (1, LANES), 1)
    m0_16 = jnp.where(lane1 < RW_HEAD_DIM, 1.0, 0.0).astype(BF16)
    m1_16 = jnp.where(lane1 < RW_HEAD_DIM, 0.0, 1.0).astype(BF16)

    def cast(xs):
        return [x.astype(BF16) for x in xs]

    def bd(q16):
        return jnp.concatenate([q16 * m0_16, q16 * m1_16], axis=0)

    def mm(p16, q16):
        return jnp.dot(p16, q16, preferred_element_type=F32)

    def mm_nt(p16, q16):
        return lax.dot_general(p16, q16, (((1,), (1,)), ((), ())), preferred_element_type=F32)

    a16, r16, v16 = cast(a_t), cast(r_t), cast(v)
    lhs = [jnp.concatenate([a16[i], r16[i]], axis=0) for i in idx]
    b16, k16 = cast(b_t), cast(k_t)
    bdv = [bd(x) for x in v16]
    g = [mm_nt(lhs[i], jnp.concatenate([bd(b16[i]), bd(k16[i])], axis=0)) for i in idx]
    x1 = [jnp.where(near, g[i][:CHUNK, :LANES], 0.0) for i in idx]
    x1_16 = cast(x1)
    l_e16 = [jnp.where(far, g[i][:CHUNK, :LANES], 0.0).astype(BF16) for i in idx]
    a_k16 = [jnp.concatenate([jnp.where(strict, g[i][:CHUNK, LANES:], 0.0),
                              jnp.where(incl, g[i][CHUNK:, LANES:], 0.0)], axis=0).astype(BF16) for i in idx]
    a_rb16 = [jnp.where(incl, g[i][CHUNK:, :LANES], 0.0).astype(BF16) for i in idx]

    def both(p16, q16):
        return jnp.concatenate([bd(p16), bd(q16)], axis=1)

    x2_16 = cast([mm(x1_16[i], bd(x1_16[i])) for i in idx])
    acc = [eye + x1[i] for i in idx]
    sq = [mm(x2_16[i], both(x2_16[i], acc[i].astype(BF16))) for i in idx]
    x4_16 = [sq[i][:, :LANES].astype(BF16) for i in idx]
    acc = [acc[i] + sq[i][:, LANES:] for i in idx]
    sq = [mm(x4_16[i], both(x4_16[i], acc[i].astype(BF16))) for i in idx]
    x8_16 = [sq[i][:, :LANES].astype(BF16) for i in idx]
    acc = [acc[i] + sq[i][:, LANES:] for i in idx]
    dinv = [acc[i] + mm(x8_16[i], bd(acc[i].astype(BF16))) for i in idx]
    dinv16 = cast(dinv)
    f16 = cast([mm(dinv16[i], bd(l_e16[i])) for i in idx])
    sq = [mm(f16[i], both(f16[i], dinv16[i])) for i in idx]
    f2_16 = [sq[i][:, :LANES].astype(BF16) for i in idx]
    hmat = [dinv[i] + sq[i][:, LANES:] for i in idx]
    tinv16 = cast([hmat[i] + mm(f2_16[i], bd(hmat[i].astype(BF16))) for i in idx])

    kv = [mm(a_k16[i], bdv[i]) for i in idx]
    wv16 = [kv[i][:CHUNK].astype(BF16) for i in idx]
    aw = [mm(tinv16[i], both(a16[i], wv16[i])) for i in idx]
    aw16 = cast(aw)
    qy = [mm(a_rb16[i], both(aw16[i][:, :LANES], aw16[i][:, LANES:]))
          + jnp.concatenate([r_t[i], kv[i][CHUNK:]], axis=1) for i in idx]
    be_ke_t = [jnp.concatenate([b_e[i], k_e[i]], axis=0).T.astype(BF16) for i in idx]
    rhs = [jnp.concatenate([aw16[i], jnp.concatenate([zero16, v16[i]], axis=1)], axis=0) for i in idx]
    mn = [jnp.where(bdmask2, mm(be_ke_t[i], rhs[i]), 0.0) for i in idx]
    qhat = [x[:, :LANES] for x in qy]
    yhat = [x[:, LANES:] for x in qy]
    mmat = [eye_bd * p_end[i] + mn[i][:, :LANES] for i in idx]
    nmat = [x[:, LANES:] for x in mn]
    return qhat, yhat, mmat, nmat


def _rwkv_kernel(rkv_ref, lr_ref, w0_ref, a0_ref, wwa_ref, gup_ref,
                 kk_ref, ka_ref, rk_ref, lnxg_ref, lnxb_ref, o_ref, state, yout_s, *, tb):
    n_chunks = tb // CHUNK
    n_pairs = RW_WIDTH // LANES

    @pl.when(pl.program_id(1) == 0)
    def _():
        state[...] = jnp.zeros_like(state)

    r = rkv_ref[0, :, :RW_WIDTH].astype(F32)
    k = rkv_ref[0, :, RW_WIDTH:2 * RW_WIDTH].astype(F32)
    v = rkv_ref[0, :, 2 * RW_WIDTH:].astype(F32)
    lr_m = lr_ref[0].astype(F32)

    wa = lr_m[:, :LANES]
    lane = lax.broadcasted_iota(jnp.int32, wa.shape, 1)
    wa = jnp.where(lane < RW_DECAY_RANK, jnp.tanh(wa), wa)
    wa_up = _dot(wa, wwa_ref[...])
    ld = -math.exp(-0.5) * _sigmoid(w0_ref[...] + wa_up[:, :RW_WIDTH])
    iclr = _sigmoid(a0_ref[...] + wa_up[:, RW_WIDTH:])
    gate = _dot(_sigmoid(lr_m[:, LANES:]), gup_ref[...])

    hr = lax.broadcasted_iota(jnp.int32, (MXU_DIM, MXU_DIM), 0) // RW_HEAD_DIM
    hc = lax.broadcasted_iota(jnp.int32, (MXU_DIM, MXU_DIM), 1) // RW_HEAD_DIM
    seg = jnp.where(hr == hc, 1.0 / RW_HEAD_DIM, 0.0).astype(BF16)

    def split3(x):
        hi = x.astype(BF16)
        rem = x - hi.astype(F32)
        mid = rem.astype(BF16)
        return hi, mid, (rem - mid.astype(F32)).astype(BF16)

    def seg_mean(x, terms=1):
        parts = split3(x)[:terms]
        cols = []
        for c in range(0, RW_WIDTH, MXU_DIM):
            cols.append(sum(jnp.dot(part[:, c:c + MXU_DIM], seg, preferred_element_type=F32) for part in parts))
        return jnp.concatenate(cols, axis=1)

    kk = k * kk_ref[...]
    ss = seg_mean(kk * kk) * RW_HEAD_DIM
    kk = kk * lax.rsqrt(jnp.maximum(ss, 1e-24))
    kh = k * (1.0 + (iclr - 1.0) * ka_ref[...])

    tr = lax.broadcasted_iota(jnp.int32, (MXU_DIM, MXU_DIM), 0)
    tc = lax.broadcasted_iota(jnp.int32, (MXU_DIM, MXU_DIM), 1)
    tri = jnp.where((tc <= tr) & (tc // CHUNK == tr // CHUNK), 1.0, 0.0).astype(BF16)
    ld_parts = split3(ld)
    lp = jnp.concatenate(
        [sum(jnp.dot(tri, part[t0:t0 + MXU_DIM], preferred_element_type=F32) for part in ld_parts)
         for t0 in range(0, tb, MXU_DIM)], axis=0)

    a_t, r_t, b_t, k_t, b_e, k_e, vs, p_end = [], [], [], [], [], [], [], []
    b = kk * iclr
    for c in range(n_chunks):
        rs = slice(c * CHUNK, (c + 1) * CHUNK)
        lp_c = lp[rs]
        lp_last = lp_c[CHUNK - 1:CHUNK]
        e_in = jnp.exp(lp_c)
        e_inv = jnp.exp(-lp_c)
        e_end = jnp.exp(lp_last - lp_c)
        a_c = -kk[rs] * jnp.exp(lp_c - ld[rs])
        r_c = r[rs] * e_in
        b_c = b[rs] * e_inv
        k_c = kh[rs] * e_inv
        be_c = b[rs] * e_end
        ke_c = kh[rs] * e_end
        pe_c = jnp.exp(lp_last)
        for p in range(n_pairs):
            cs = slice(p * LANES, (p + 1) * LANES)
            a_t.append(a_c[:, cs])
            r_t.append(r_c[:, cs])
            b_t.append(b_c[:, cs])
            k_t.append(k_c[:, cs])
            b_e.append(be_c[:, cs])
            k_e.append(ke_c[:, cs])
            vs.append(v[rs, cs])
            p_end.append(pe_c[:, cs])

    qhat, yhat, mmat, nmat = _rwkv_chunk_terms(a_t, r_t, b_t, k_t, b_e, k_e, vs, p_end)

    s = [state[p] for p in range(n_pairs)]
    for c in range(n_chunks):
        ym = [_dot(jnp.concatenate([qhat[c * n_pairs + p], mmat[c * n_pairs + p]], axis=0), s[p])
              for p in range(n_pairs)]
        for p in range(n_pairs):
            i = c * n_pairs + p
            yout_s[c * CHUNK:(c + 1) * CHUNK, p * LANES:(p + 1) * LANES] = ym[p][:CHUNK] + yhat[i]
            s[p] = ym[p][CHUNK:] + nmat[i]
    for p in range(n_pairs):
        state[p] = s[p]

    y = yout_s[...]
    mean = seg_mean(y, terms=2)
    yc = y - mean
    var = seg_mean(yc * yc)
    y = yc * lax.rsqrt(var + RW_GN_EPS) * lnxg_ref[...] + lnxb_ref[...]
    bonus = seg_mean(r * kh * rk_ref[...]) * RW_HEAD_DIM * v
    o_ref[0] = ((y + bonus) * gate).astype(o_ref.dtype)


def _rwkv(rkv, lr, w0, a0, wwa, gup, k_k, k_a, r_k, lnx_g, lnx_b, *, tb=512):
    bsz, t, w3 = rkv.shape
    n_pairs = RW_WIDTH // LANES

    def full(shape):
        return pl.BlockSpec(shape, lambda b, i: (0,) * len(shape))

    return pl.pallas_call(
        functools.partial(_rwkv_kernel, tb=tb),
        grid=(bsz, t // tb),
        in_specs=[pl.BlockSpec((1, tb, w3), lambda b, i: (b, i, 0)),
                  pl.BlockSpec((1, tb, LR_PAD), lambda b, i: (b, i, 0)),
                  full((1, RW_WIDTH)), full((1, RW_WIDTH)),
                  full((LANES, 2 * RW_WIDTH)), full((LR_PAD - LANES, RW_WIDTH)),
                  full((1, RW_WIDTH)), full((1, RW_WIDTH)), full((1, RW_WIDTH)),
                  full((1, RW_WIDTH)), full((1, RW_WIDTH))],
        out_specs=pl.BlockSpec((1, tb, RW_WIDTH), lambda b, i: (b, i, 0)),
        out_shape=jax.ShapeDtypeStruct((bsz, t, RW_WIDTH), BF16),
        scratch_shapes=[pltpu.VMEM((n_pairs, LANES, LANES), F32),
                        pltpu.VMEM((tb, RW_WIDTH), F32)],
        compiler_params=_cparams(2),
        name="rwkv",
    )(rkv, lr, w0, a0, wwa, gup, k_k, k_a, r_k, lnx_g, lnx_b)


def _out_ln_kernel(ya_ref, yb_ref, w_ref, x_ref, mod_ref, lng_ref, lnb_ref, o_ref, *, gate_row):
    ka = ya_ref.shape[-1]
    y = (jnp.dot(ya_ref[0], w_ref[:ka], preferred_element_type=F32)
         + jnp.dot(yb_ref[0], w_ref[ka:], preferred_element_type=F32))
    g = mod_ref[0][gate_row:gate_row + 1]
    z = DEEPNORM_ALPHA * x_ref[0] + (1.0 + g) * y
    o_ref[0] = _layer_norm_rows(z, lng_ref[...], lnb_ref[...])


def _out_ln(ya, yb, w, x, mod, ln_g, ln_b, *, gate_row, tm=512):
    bsz, t, d = x.shape
    return pl.pallas_call(
        functools.partial(_out_ln_kernel, gate_row=gate_row),
        grid=(bsz, t // tm),
        in_specs=[pl.BlockSpec((1, tm, ya.shape[-1]), lambda b, i: (b, i, 0)),
                  pl.BlockSpec((1, tm, yb.shape[-1]), lambda b, i: (b, i, 0)),
                  pl.BlockSpec(w.shape, lambda b, i: (0, 0)),
                  pl.BlockSpec((1, tm, d), lambda b, i: (b, i, 0)),
                  pl.BlockSpec((1, 6, d), lambda b, i: (b, 0, 0)),
                  pl.BlockSpec((1, d), lambda b, i: (0, 0)),
                  pl.BlockSpec((1, d), lambda b, i: (0, 0))],
        out_specs=pl.BlockSpec((1, tm, d), lambda b, i: (b, i, 0)),
        out_shape=jax.ShapeDtypeStruct((bsz, t, d), F32),
        compiler_params=_cparams(2),
        name="out_ln",
    )(ya, yb, w, x, mod, ln_g, ln_b)


def _attn_kernel(q_ref, kp_ref, kc_ref, vp_ref, vc_ref, bias_ref, o_ref, lse_ref, *, nb, rb):
    step = pl.program_id(1)
    n_blk = rb // BLOCK
    scale = DIL_HEAD_DIM ** -0.5
    n_pairs = DIL_WIDTH // LANES
    krow = lax.broadcasted_iota(jnp.int32, (2 * BLOCK, LANES), 0) // BLOCK
    klane = lax.broadcasted_iota(jnp.int32, (2 * BLOCK, LANES), 1) // DIL_HEAD_DIM
    own = krow == klane
    zeros = jnp.zeros((2 * BLOCK, LANES), BF16)
    ones_bd = jnp.where(own, 1.0, 0.0).astype(BF16)
    lane_o = lax.broadcasted_iota(jnp.int32, (BLOCK, LANES), 1)

    def stack_bd(x):
        return jnp.where(own, jnp.concatenate([x, x], axis=0), zeros)

    def attend(blocks):
        units = [(rows, q, parts, p) for rows, q, parts in blocks for p in range(n_pairs)]
        lanes = [slice(p * LANES, (p + 1) * LANES) for p in range(n_pairs)]
        scores = []
        for _, q, parts, p in units:
            qs = q[:, lanes[p]] * scale
            row = []
            for k_blk, _, col0, pen in parts:
                s = lax.dot_general(qs, stack_bd(k_blk[:, lanes[p]]), (((1,), (1,)), ((), ())),
                                    preferred_element_type=F32)
                bias = jnp.concatenate([bias_ref[2 * p, :, col0:col0 + BLOCK],
                                        bias_ref[2 * p + 1, :, col0:col0 + BLOCK]], axis=1)
                s = s + bias
                row.append(s if pen is None else s + pen)
            scores.append(row)
        ms = []
        for u in range(len(units)):
            halves = []
            for hh in range(2):
                cols = slice(hh * BLOCK, (hh + 1) * BLOCK)
                m = jnp.max(scores[u][0][:, cols], axis=-1, keepdims=True)
                for s in scores[u][1:]:
                    m = jnp.maximum(m, jnp.max(s[:, cols], axis=-1, keepdims=True))
                halves.append(m)
            ms.append(halves)
        es = [[jnp.exp((s - jnp.concatenate([jnp.broadcast_to(ms[u][0], (BLOCK, BLOCK)),
                                             jnp.broadcast_to(ms[u][1], (BLOCK, BLOCK))], axis=1)).astype(BF16))
               for s in scores[u]] for u in range(len(units))]
        res = []
        for u, (_, _, parts, p) in enumerate(units):
            res.append(sum(jnp.dot(e, jnp.concatenate([stack_bd(part[1][:, lanes[p]]), ones_bd], axis=1),
                                   preferred_element_type=F32) for e, part in zip(es[u], parts)))
        for b, (rows, _, _) in enumerate(blocks):
            m_all = jnp.zeros((BLOCK, LANES), F32)
            den_all = jnp.ones((BLOCK, LANES), F32)
            for p in range(n_pairs):
                u = b * n_pairs + p
                num, den = res[u][:, :LANES], res[u][:, LANES:]
                o_ref[rows, lanes[p]] = num / den
                m_all = jnp.where(lane_o == 2 * p, ms[u][0], jnp.where(lane_o == 2 * p + 1, ms[u][1], m_all))
                den_all = jnp.where((lane_o == 2 * p), den,
                                    jnp.where(lane_o == 2 * p + 1, pltpu.roll(den, DIL_HEAD_DIM, 1), den_all))
            lse_ref[rows, :] = m_all + jnp.log(den_all)

    blocks = []
    for j in range(n_blk):
        rows = slice(j * BLOCK, (j + 1) * BLOCK)
        cur = (kc_ref[rows, :], vc_ref[rows, :], BLOCK, None)
        if n_blk % nb == 0:
            prev = "none" if j % nb == 0 else "inside"
        else:
            prev = "inside" if j > 0 else ("none" if nb == n_blk else "before")
        if prev == "none":
            parts = [cur]
        elif prev == "inside":
            before = slice((j - 1) * BLOCK, j * BLOCK)
            parts = [(kc_ref[before, :], vc_ref[before, :], 0, None), cur]
        else:
            has_prev = (step % (nb // n_blk)) != 0
            parts = [(kp_ref[...], vp_ref[...], 0, jnp.where(has_prev, 0.0, MASKED)), cur]
        blocks.append((rows, q_ref[rows, :], parts))
    attend(blocks)


def _qkv_proj_kernel(x_ref, mod_ref, w_ref, o0_ref, o1_ref, o2_ref, xs_ref, *, tm):
    m = mod_ref[0]
    scale = 1.0 + m[1:2]
    shift = m[0:1]
    j = pl.program_id(2)
    n_col = x_ref.shape[-1] // LANES

    @pl.when(j == 0)
    def _():
        for c in range(n_col):
            xs_ref[c] = x_ref[0, :, c * LANES:(c + 1) * LANES]

    def group(o_ref, dil):
        rows = tm // dil
        if dil == 1:
            xp = x_ref[0]
        else:
            xp = jnp.concatenate(
                [jnp.concatenate([xs_ref[c, pl.ds(r, rows, stride=dil), :] for r in range(dil)], axis=0)
                 for c in range(n_col)], axis=1)
        u = (xp * scale + shift).astype(BF16)
        res = jnp.dot(u, w_ref[...], preferred_element_type=F32).astype(BF16)
        for r in range(dil):
            o_ref[0, r] = res[r * rows:(r + 1) * rows]

    for g, o_ref in enumerate((o0_ref, o1_ref, o2_ref)):
        pl.when(j == g)(functools.partial(group, o_ref, DIL_PATTERNS[g][1]))


def _qkv_proj(x, mod, w, *, tm=1024):
    bsz, t, d = x.shape
    gw = 3 * DIL_WIDTH
    out_specs, out_shape = [], []
    for _, dil in DIL_PATTERNS:
        out_specs.append(pl.BlockSpec((1, dil, tm // dil, gw), lambda b, i, j: (b, 0, i, 0)))
        out_shape.append(jax.ShapeDtypeStruct((bsz, dil, t // dil, gw), BF16))
    return pl.pallas_call(
        functools.partial(_qkv_proj_kernel, tm=tm),
        grid=(bsz, t // tm, N_GROUPS),
        in_specs=[pl.BlockSpec((1, tm, d), lambda b, i, j: (b, i, 0)),
                  pl.BlockSpec((1, 6, d), lambda b, i, j: (b, 0, 0)),
                  pl.BlockSpec((d, gw), lambda b, i, j: (0, j))],
        out_specs=out_specs,
        out_shape=out_shape,
        scratch_shapes=[pltpu.VMEM((d // LANES, tm, LANES), F32)],
        compiler_params=_cparams(3),
        name="qkv_proj",
    )(x, mod, w)


def _dilated_attention_group(qkv, bias, g, *, rb=512):
    bsz, dil, length, width = qkv.shape
    t = dil * length
    nb = length // BLOCK
    n_blk = rb // BLOCK
    assert nb % n_blk == 0 or n_blk % nb == 0
    flat = qkv.reshape(bsz, t, width)

    def cur(j):
        return lambda b, s: (b, s, j)

    def prev(j):
        return lambda b, s: (b, jnp.maximum(s * n_blk - 1, 0), j)

    blk = (None, rb, DIL_WIDTH)
    pblk = (None, BLOCK, DIL_WIDTH)
    o, lse = pl.pallas_call(
        functools.partial(_attn_kernel, nb=nb, rb=rb),
        grid=(bsz, t // rb),
        in_specs=[pl.BlockSpec(blk, cur(0)),
                  pl.BlockSpec(pblk, prev(1)), pl.BlockSpec(blk, cur(1)),
                  pl.BlockSpec(pblk, prev(2)), pl.BlockSpec(blk, cur(2)),
                  pl.BlockSpec((DIL_HEADS, BLOCK, 2 * BLOCK), lambda b, s: (0, 0, 0))],
        out_specs=[pl.BlockSpec(blk, lambda b, s: (b, s, 0)),
                   pl.BlockSpec((None, rb, LANES), lambda b, s: (b, s, 0))],
        out_shape=[jax.ShapeDtypeStruct((bsz, t, DIL_WIDTH), F32),
                   jax.ShapeDtypeStruct((bsz, t, LANES), F32)],
        compiler_params=_cparams(2),
        name=f"dilated_attn_g{g}",
    )(flat, flat, flat, flat, flat, bias)
    return o.reshape(bsz, dil, length, DIL_WIDTH), lse.reshape(bsz, dil, length, LANES)


def _merge_out_ln_kernel(o0_ref, o1_ref, o2_ref, l0_ref, l1_ref, l2_ref, w_ref, x_ref, mod_ref,
                         lng_ref, lnb_ref, o_ref, o_nat, l_nat, *, gate_row, tm):
    def natural(src_ref, scr, dil):
        if dil == 1:
            return src_ref[0, 0]
        n_col = src_ref.shape[-1] // LANES
        for c in range(n_col):
            for r in range(dil):
                scr[c, pl.ds(r, tm // dil, stride=dil), :] = src_ref[0, r, :, c * LANES:(c + 1) * LANES]
        return jnp.concatenate([scr[c] for c in range(n_col)], axis=1)

    dils = [dil for _, dil in DIL_PATTERNS]
    lses = [natural(ref, l_nat, dil) for ref, dil in zip((l0_ref, l1_ref, l2_ref), dils)]
    m = jnp.maximum(jnp.maximum(lses[0], lses[1]), lses[2])
    es = [jnp.exp(l - m) for l in lses]
    inv = 1.0 / (es[0] + es[1] + es[2])
    hr = lax.broadcasted_iota(jnp.int32, (LANES, DIL_WIDTH), 0)
    hc = lax.broadcasted_iota(jnp.int32, (LANES, DIL_WIDTH), 1) // DIL_HEAD_DIM
    expand = jnp.where(hr == hc, 1.0, 0.0).astype(BF16)

    def widen(wt):
        hi = wt.astype(BF16)
        lo = (wt - hi.astype(F32)).astype(BF16)
        return (jnp.dot(hi, expand, preferred_element_type=F32)
                + jnp.dot(lo, expand, preferred_element_type=F32))

    merged = None
    for e, ref, dil in zip(es, (o0_ref, o1_ref, o2_ref), dils):
        term = widen(e * inv) * natural(ref, o_nat, dil)
        merged = term if merged is None else merged + term
    y = jnp.dot(merged.astype(BF16), w_ref[...], preferred_element_type=F32)
    g = mod_ref[0][gate_row:gate_row + 1]
    z = DEEPNORM_ALPHA * x_ref[0] + (1.0 + g) * y
    o_ref[0] = _layer_norm_rows(z, lng_ref[...], lnb_ref[...])


def _merge_out_ln(outs, lses, w, x, mod, ln_g, ln_b, *, gate_row, tm=512):
    bsz, t, d = x.shape
    ospecs = [pl.BlockSpec((1, dil, tm // dil, DIL_WIDTH), lambda b, i: (b, 0, i, 0)) for _, dil in DIL_PATTERNS]
    lspecs = [pl.BlockSpec((1, dil, tm // dil, LANES), lambda b, i: (b, 0, i, 0)) for _, dil in DIL_PATTERNS]
    return pl.pallas_call(
        functools.partial(_merge_out_ln_kernel, gate_row=gate_row, tm=tm),
        grid=(bsz, t // tm),
        in_specs=ospecs + lspecs + [
            pl.BlockSpec((DIL_WIDTH, d), lambda b, i: (0, 0)),
            pl.BlockSpec((1, tm, d), lambda b, i: (b, i, 0)),
            pl.BlockSpec((1, 6, d), lambda b, i: (b, 0, 0)),
            pl.BlockSpec((1, d), lambda b, i: (0, 0)),
            pl.BlockSpec((1, d), lambda b, i: (0, 0))],
        out_specs=pl.BlockSpec((1, tm, d), lambda b, i: (b, i, 0)),
        out_shape=jax.ShapeDtypeStruct((bsz, t, d), F32),
        scratch_shapes=[pltpu.VMEM((DIL_WIDTH // LANES, tm, LANES), F32), pltpu.VMEM((1, tm, LANES), F32)],
        compiler_params=_cparams(2),
        name="merge_out_ln",
    )(*outs, *lses, w, x, mod, ln_g, ln_b)


def _t5_bucket(dist):
    exact = N_BUCKETS // 2
    logd = jnp.log(jnp.maximum(dist, 1).astype(F32) / exact) / math.log(MAX_DISTANCE / exact)
    large = jnp.minimum(exact + (logd * (N_BUCKETS - exact)).astype(jnp.int32), N_BUCKETS - 1)
    return jnp.where(dist < exact, dist, large)


def _bias_tables(rel_bias):
    rel = jnp.arange(BLOCK + 1)
    tables = []
    for g, (window, dil) in enumerate(DIL_PATTERNS):
        span = window // dil
        bucket = _t5_bucket(jnp.clip(rel, 0, span) * dil)
        vec = rel_bias[bucket][:, g * DIL_HEADS:(g + 1) * DIL_HEADS].T.astype(F32)
        ext = jnp.concatenate([vec[:, ::-1], jnp.zeros((DIL_HEADS, BLOCK), F32)], axis=1)
        flat = jnp.tile(ext, (1, BLOCK))[:, :BLOCK * 2 * BLOCK]
        tables.append(flat.reshape(DIL_HEADS, BLOCK, 2 * BLOCK))
    qi = np.arange(BLOCK)[:, None]
    ki = np.arange(2 * BLOCK)[None, :]
    band = (ki >= qi) & (ki <= qi + BLOCK)
    return jnp.where(band, jnp.stack(tables), MASKED)


def _mlp_kernel(x_ref, mod_ref, w1_ref, w2_ref, lng_ref, lnb_ref, o_ref, *, shift_row, scale_row, gate_row, ff_chunk):
    m = mod_ref[0]
    x = x_ref[0]
    u = (x * (1.0 + m[scale_row:scale_row + 1]) + m[shift_row:shift_row + 1]).astype(BF16)
    acc = None
    for c in range(D_FF // ff_chunk):
        h = jnp.dot(u, w1_ref[:, c * ff_chunk:(c + 1) * ff_chunk], preferred_element_type=F32)
        h = jnp.maximum(h, 0.0)
        h = (h * h).astype(BF16)
        part = jnp.dot(h, w2_ref[c * ff_chunk:(c + 1) * ff_chunk, :], preferred_element_type=F32)
        acc = part if acc is None else acc + part
    z = DEEPNORM_ALPHA * x + (1.0 + m[gate_row:gate_row + 1]) * acc
    o_ref[0] = _layer_norm_rows(z, lng_ref[...], lnb_ref[...])


def _mlp_ln(x, mod, w1, w2, ln_g, ln_b, *, tm=512, ff_chunk=1024):
    bsz, t, d = x.shape
    return pl.pallas_call(
        functools.partial(_mlp_kernel, shift_row=3, scale_row=4, gate_row=5, ff_chunk=ff_chunk),
        grid=(bsz, t // tm),
        in_specs=[pl.BlockSpec((1, tm, d), lambda b, i: (b, i, 0)),
                  pl.BlockSpec((1, 6, d), lambda b, i: (b, 0, 0)),
                  pl.BlockSpec((d, D_FF), lambda b, i: (0, 0)),
                  pl.BlockSpec((D_FF, d), lambda b, i: (0, 0)),
                  pl.BlockSpec((1, d), lambda b, i: (0, 0)),
                  pl.BlockSpec((1, d), lambda b, i: (0, 0))],
        out_specs=pl.BlockSpec((1, tm, d), lambda b, i: (b, i, 0)),
        out_shape=jax.ShapeDtypeStruct((bsz, t, d), F32),
        compiler_params=_cparams(2),
        name="mlp_ln",
    )(x, mod, w1, w2, ln_g, ln_b)


def _prep_even_weights(w_in, mu, w_up, a_up, g_up):
    w3 = 3 * RW_WIDTH
    pad = LR_PAD - (RW_PROJ - w3)
    d = w_in.shape[0]
    w_perm = jnp.concatenate([w_in[:, :w3], w_in[:, RW_PROJ:], w_in[:, w3:RW_PROJ],
                              jnp.zeros((d, pad), w_in.dtype)], axis=1).astype(BF16)
    mu_pad = jnp.concatenate([mu, jnp.zeros((pad,), mu.dtype)]).reshape(1, w3 + LR_PAD)
    zeros = jnp.zeros((RW_DECAY_RANK, RW_WIDTH), w_up.dtype)
    wwa = jnp.concatenate([jnp.concatenate([w_up, zeros], axis=1),
                           jnp.concatenate([zeros, a_up], axis=1)], axis=0).astype(BF16)
    gup = jnp.concatenate([g_up, jnp.zeros((pad, RW_WIDTH), g_up.dtype)], axis=0).astype(BF16)
    return w_perm, mu_pad, wwa, gup


def kernel(x, c, ada_w, ada_b, ln_g, ln_b, ab_w_in, rw_mu, rw_w0, rw_w_up, rw_a0, rw_a_up, rw_g_up,
           rw_k_k, rw_k_a, rw_r_k, rw_lnx_g, rw_lnx_b, sc_conv_w, ab_w_out, dil_w_qkv, dil_w_out,
           rel_bias, mlp_w1, mlp_w2):
    bsz, t, d = x.shape
    mods = _adaln(c, ada_w, ada_b).reshape(DEPTH, bsz, 6, d)
    bias = None
    row = lambda a: a.reshape(1, -1)
    for i in range(DEPTH):
        mod = mods[i]
        j = i // 2
        if i % 2 == 0:
            w_perm, mu_pad, wwa, gup = _prep_even_weights(
                ab_w_in[j], rw_mu[j], rw_w_up[j], rw_a_up[j], rw_g_up[j])
            rkv, lr, yb = _ab_proj(x, mod, w_perm, mu_pad, sc_conv_w[j])
            ya = _rwkv(rkv, lr, row(rw_w0[j]), row(rw_a0[j]), wwa, gup,
                       row(rw_k_k[j]), row(rw_k_a[j]), row(rw_r_k[j]),
                       row(rw_lnx_g[j]), row(rw_lnx_b[j]))
            x = _out_ln(ya, yb, ab_w_out[j].astype(BF16), x, mod, row(ln_g[i, 0]), row(ln_b[i, 0]), gate_row=2)
        else:
            if bias is None:
                bias = _bias_tables(rel_bias)
            qkvs = _qkv_proj(x, mod, dil_w_qkv[j].astype(BF16))
            outs, lses = [], []
            for g in range(N_GROUPS):
                o, lse = _dilated_attention_group(qkvs[g], bias[g], g)
                outs.append(o)
                lses.append(lse)
            x = _merge_out_ln(outs, lses, dil_w_out[j].astype(BF16), x, mod,
                              row(ln_g[i, 0]), row(ln_b[i, 0]), gate_row=2)
        x = _mlp_ln(x, mod, mlp_w1[i].astype(BF16), mlp_w2[i].astype(BF16),
                    row(ln_g[i, 1]), row(ln_b[i, 1]))
    return x
```

```python
import functools
import math

import jax
import jax.numpy as jnp
import numpy as np
from jax import lax
from jax.experimental import pallas as pl
from jax.experimental.pallas import tpu as pltpu

F32 = jnp.float32
BF16 = jnp.bfloat16

D_MODEL = 1024
DEPTH = 2
RW_HEADS = 8
RW_HEAD_DIM = 64
RW_WIDTH = 512
RW_DECAY_RANK = 64
RW_ICLR_RANK = 64
RW_GATE_RANK = 160
RW_GN_EPS = 64e-5
RW_PROJ = 3 * RW_WIDTH + RW_DECAY_RANK + RW_ICLR_RANK + RW_GATE_RANK
SC_WIDTH = 512
SC_CONV = 3
DIL_PATTERNS = ((128, 1), (512, 4), (2048, 16))
N_GROUPS = 3
DIL_HEADS = 8
DIL_HEAD_DIM = 64
DIL_WIDTH = 512
BLOCK = 128
N_BUCKETS = 32
MAX_DISTANCE = 2048
D_FF = 4 * D_MODEL
DEEPNORM_ALPHA = (2 * DEPTH) ** 0.25
LN_EPS = 1e-5

LANES = 128
MXU_DIM = 256
CHUNK = 64
LR_PAD = 384
VMEM_LIMIT = 56 * 1024 * 1024
MASKED = -1e30


def _cparams(n_axes):
    return pltpu.CompilerParams(dimension_semantics=("arbitrary",) * n_axes,
                                vmem_limit_bytes=VMEM_LIMIT)


def _sigmoid(x):
    return 1.0 / (1.0 + jnp.exp(-x))


def _dot(a, b):
    return jnp.dot(a.astype(BF16), b.astype(BF16), preferred_element_type=F32)


def _layer_norm_rows(z, g, b):
    mu = jnp.mean(z, axis=-1, keepdims=True)
    zc = z - mu
    var = jnp.mean(zc * zc, axis=-1, keepdims=True)
    return zc * lax.rsqrt(var + LN_EPS) * g + b


def _adaln_kernel(c_ref, w_ref, b_ref, o_ref):
    c = c_ref[...]
    cond = c * _sigmoid(c)
    o_ref[0] = _dot(cond, w_ref[0]) + b_ref[0]


def _adaln(c, ada_w, ada_b):
    depth, d, n = ada_w.shape
    bsz = c.shape[0]
    tn = 1536
    return pl.pallas_call(
        _adaln_kernel,
        grid=(depth, n // tn),
        in_specs=[pl.BlockSpec((bsz, d), lambda i, j: (0, 0)),
                  pl.BlockSpec((1, d, tn), lambda i, j: (i, 0, j)),
                  pl.BlockSpec((1, 1, tn), lambda i, j: (i, 0, j))],
        out_specs=pl.BlockSpec((1, bsz, tn), lambda i, j: (i, 0, j)),
        out_shape=jax.ShapeDtypeStruct((depth, bsz, n), F32),
        compiler_params=_cparams(2),
        name="adaln",
    )(c, ada_w, ada_b.reshape(depth, 1, n))


def _shift_rows(x, carry, k):
    rolled = pltpu.roll(x, k, 0)
    head = rolled[:8]
    row = lax.broadcasted_iota(jnp.int32, head.shape, 0)
    for i in range(k):
        head = jnp.where(row == i, carry[8 - k + i:8 - k + i + 1], head)
    return jnp.concatenate([head, rolled[8:]], axis=0)


def _ab_proj_kernel(x_ref, mod_ref, w_ref, mu_ref, convw_ref, rkv_ref, lr_ref, yb_ref, carry_p, carry_z, *, tm):
    @pl.when(pl.program_id(1) == 0)
    def _():
        carry_p[...] = jnp.zeros_like(carry_p)
        carry_z[...] = jnp.zeros_like(carry_z)

    m = mod_ref[0]
    u = (x_ref[0] * (1.0 + m[1:2]) + m[0:1]).astype(BF16)

    def proj(c0, width):
        return jnp.dot(u, w_ref[:, c0:c0 + width], preferred_element_type=F32)

    def mixed(c0, m0, width):
        p = proj(c0, width)
        prev = _shift_rows(p, carry_p[:, m0:m0 + width], 1)
        carry_p[:, m0:m0 + width] = p[tm - 8:]
        return (p + mu_ref[:, m0:m0 + width] * (prev - p)).astype(BF16)

    w3 = 3 * RW_WIDTH
    for c in range(3):
        rkv_ref[0, :, c * RW_WIDTH:(c + 1) * RW_WIDTH] = mixed(c * RW_WIDTH, c * RW_WIDTH, RW_WIDTH)
    lr_ref[0] = mixed(2 * w3, w3, LR_PAD)

    h = proj(w3, SC_WIDTH)
    z = proj(w3 + 2 * SC_WIDTH, SC_WIDTH) * h
    zc = carry_z[...]
    cw = convw_ref[...]
    conv = cw[2:3] * z + cw[1:2] * _shift_rows(z, zc, 1) + cw[0:1] * _shift_rows(z, zc, 2)
    carry_z[...] = z[tm - 8:]
    yb_ref[0] = (proj(w3 + SC_WIDTH, SC_WIDTH) * conv).astype(BF16)


def _ab_proj(x, mod, w, mu, conv_w, *, tm=1024):
    bsz, t, d = x.shape
    w3 = 3 * RW_WIDTH

    def const(shape):
        return pl.BlockSpec(shape, lambda b, i: (0,) * len(shape), pipeline_mode=pl.Buffered(1))

    def rows(width):
        return pl.BlockSpec((1, tm, width), lambda b, i: (b, i, 0))

    return pl.pallas_call(
        functools.partial(_ab_proj_kernel, tm=tm),
        grid=(bsz, t // tm),
        in_specs=[rows(d), pl.BlockSpec((1, 6, d), lambda b, i: (b, 0, 0)),
                  const(w.shape), const((1, w3 + LR_PAD)), const((SC_CONV, SC_WIDTH))],
        out_specs=[rows(w3), rows(LR_PAD), rows(SC_WIDTH)],
        out_shape=[jax.ShapeDtypeStruct((bsz, t, w3), BF16), jax.ShapeDtypeStruct((bsz, t, LR_PAD), BF16),
                   jax.ShapeDtypeStruct((bsz, t, SC_WIDTH), BF16)],
        scratch_shapes=[pltpu.VMEM((8, w3 + LR_PAD), F32), pltpu.VMEM((8, SC_WIDTH), F32)],
        compiler_params=_cparams(2),
        name="ab_proj",
    )(x, mod, w, mu, conv_w)


def _rwkv_chunk_terms(a_t, r_t, b_t, k_t, b_e, k_e, v, p_end):
    n = len(a_t)
    idx = range(n)
    row = lax.broadcasted_iota(jnp.int32, (CHUNK, LANES), 0)
    col = lax.broadcasted_iota(jnp.int32, (CHUNK, LANES), 1)
    s_idx = col % CHUNK
    head0 = col < RW_HEAD_DIM
    strict = s_idx < row
    incl = s_idx <= row
    near = strict & ((s_idx // 16) == (row // 16))
    far = strict & ((s_idx // 16) != (row // 16))
    eye = jnp.where(s_idx == row, 1.0, 0.0)
    r2 = lax.broadcasted_iota(jnp.int32, (LANES, 2 * LANES), 0)
    c2 = lax.broadcasted_iota(jnp.int32, (LANES, 2 * LANES), 1)
    bdmask2 = (r2 // RW_HEAD_DIM) == ((c2 % LANES) // RW_HEAD_DIM)
    eye_bd = jnp.where(lax.broadcasted_iota(jnp.int32, (LANES, LANES), 0)
                       == lax.broadcasted_iota(jnp.int32, (LANES, LANES), 1), 1.0, 0.0)
    zero16 = jnp.zeros((CHUNK, LANES), BF16)
    lane1 = lax.broadcasted_iota(jnp.int32, (1, LANES), 1)
    m0_16 = jnp.where(lane1 < RW_HEAD_DIM, 1.0, 0.0).astype(BF16)
    m1_16 = jnp.where(lane1 < RW_HEAD_DIM, 0.0, 1.0).astype(BF16)

    def cast(xs):
        return [x.astype(BF16) for x in xs]

    def bd(q16):
        return jnp.concatenate([q16 * m0_16, q16 * m1_16], axis=0)

    def mm(p16, q16):
        return jnp.dot(p16, q16, preferred_element_type=F32)

    def mm_nt(p16, q16):
        return lax.dot_general(p16, q16, (((1,), (1,)), ((), ())), preferred_element_type=F32)

    a16, r16, v16 = cast(a_t), cast(r_t), cast(v)
    lhs = [jnp.concatenate([a16[i], r16[i]], axis=0) for i in idx]
    b16, k16 = cast(b_t), cast(k_t)
    bdv = [bd(x) for x in v16]
    g = [mm_nt(lhs[i], jnp.concatenate([bd(b16[i]), bd(k16[i])], axis=0)) for i in idx]
    x1 = [jnp.where(near, g[i][:CHUNK, :LANES], 0.0) for i in idx]
    x1_16 = cast(x1)
    l_e16 = [jnp.where(far, g[i][:CHUNK, :LANES], 0.0).astype(BF16) for i in idx]
    a_k16 = [jnp.concatenate([jnp.where(strict, g[i][:CHUNK, LANES:], 0.0),
                              jnp.where(incl, g[i][CHUNK:, LANES:], 0.0)], axis=0).astype(BF16) for i in idx]
    a_rb16 = [jnp.where(incl, g[i][CHUNK:, :LANES], 0.0).astype(BF16) for i in idx]

    def both(p16, q16):
        return jnp.concatenate([bd(p16), bd(q16)], axis=1)

    x2_16 = cast([mm(x1_16[i], bd(x1_16[i])) for i in idx])
    acc = [eye + x1[i] for i in idx]
    sq = [mm(x2_16[i], both(x2_16[i], acc[i].astype(BF16))) for i in idx]
    x4_16 = [sq[i][:, :LANES].astype(BF16) for i in idx]
    acc = [acc[i] + sq[i][:, LANES:] for i in idx]
    sq = [mm(x4_16[i], both(x4_16[i], acc[i].astype(BF16))) for i in idx]
    x8_16 = [sq[i][:, :LANES].astype(BF16) for i in idx]
    acc = [acc[i] + sq[i][:, LANES:] for i in idx]
    dinv = [acc[i] + mm(x8_16[i], bd(acc[i].astype(BF16))) for i in idx]
    dinv16 = cast(dinv)
    f16 = cast([mm(dinv16[i], bd(l_e16[i])) for i in idx])
    sq = [mm(f16[i], both(f16[i], dinv16[i])) for i in idx]
    f2_16 = [sq[i][:, :LANES].astype(BF16) for i in idx]
    hmat = [dinv[i] + sq[i][:, LANES:] for i in idx]
    tinv16 = cast([hmat[i] + mm(f2_16[i], bd(hmat[i].astype(BF16))) for i in idx])

    kv = [mm(a_k16[i], bdv[i]) for i in idx]
    wv16 = [kv[i][:CHUNK].astype(BF16) for i in idx]
    aw = [mm(tinv16[i], both(a16[i], wv16[i])) for i in idx]
    aw16 = cast(aw)
    qy = [mm(a_rb16[i], both(aw16[i][:, :LANES], aw16[i][:, LANES:]))
          + jnp.concatenate([r_t[i], kv[i][CHUNK:]], axis=1) for i in idx]
    be_ke_t = [jnp.concatenate([b_e[i], k_e[i]], axis=0).T.astype(BF16) for i in idx]
    rhs = [jnp.concatenate([aw16[i], jnp.concatenate([zero16, v16[i]], axis=1)], axis=0) for i in idx]
    mn = [jnp.where(bdmask2, mm(be_ke_t[i], rhs[i]), 0.0) for i in idx]
    qhat = [x[:, :LANES] for x in qy]
    yhat = [x[:, LANES:] for x in qy]
    mmat = [eye_bd * p_end[i] + mn[i][:, :LANES] for i in idx]
    nmat = [x[:, LANES:] for x in mn]
    return qhat, yhat, mmat, nmat


def _rwkv_kernel(rkv_ref, lr_ref, w0_ref, a0_ref, wwa_ref, gup_ref,
                 kk_ref, ka_ref, rk_ref, lnxg_ref, lnxb_ref, o_ref, state, yout_s, *, tb):
    n_chunks = tb // CHUNK
    n_pairs = RW_WIDTH // LANES

    @pl.when(pl.program_id(1) == 0)
    def _():
        state[...] = jnp.zeros_like(state)

    r = rkv_ref[0, :, :RW_WIDTH].astype(F32)
    k = rkv_ref[0, :, RW_WIDTH:2 * RW_WIDTH].astype(F32)
    v = rkv_ref[0, :, 2 * RW_WIDTH:].astype(F32)
    lr_m = lr_ref[0].astype(F32)

    wa = lr_m[:, :LANES]
    lane = lax.broadcasted_iota(jnp.int32, wa.shape, 1)
    wa = jnp.where(lane < RW_DECAY_RANK, jnp.tanh(wa), wa)
    wa_up = _dot(wa, wwa_ref[...])
    ld = -math.exp(-0.5) * _sigmoid(w0_ref[...] + wa_up[:, :RW_WIDTH])
    iclr = _sigmoid(a0_ref[...] + wa_up[:, RW_WIDTH:])
    gate = _dot(_sigmoid(lr_m[:, LANES:]), gup_ref[...])

    hr = lax.broadcasted_iota(jnp.int32, (MXU_DIM, MXU_DIM), 0) // RW_HEAD_DIM
    hc = lax.broadcasted_iota(jnp.int32, (MXU_DIM, MXU_DIM), 1) // RW_HEAD_DIM
    seg = jnp.where(hr == hc, 1.0 / RW_HEAD_DIM, 0.0).astype(BF16)

    def split3(x):
        hi = x.astype(BF16)
        rem = x - hi.astype(F32)
        mid = rem.astype(BF16)
        return hi, mid, (rem - mid.astype(F32)).astype(BF16)

    def seg_mean(x, terms=1):
        parts = split3(x)[:terms]
        cols = []
        for c in range(0, RW_WIDTH, MXU_DIM):
            cols.append(sum(jnp.dot(part[:, c:c + MXU_DIM], seg, preferred_element_type=F32) for part in parts))
        return jnp.concatenate(cols, axis=1)

    kk = k * kk_ref[...]
    ss = seg_mean(kk * kk) * RW_HEAD_DIM
    kk = kk * lax.rsqrt(jnp.maximum(ss, 1e-24))
    kh = k * (1.0 + (iclr - 1.0) * ka_ref[...])

    tr = lax.broadcasted_iota(jnp.int32, (MXU_DIM, MXU_DIM), 0)
    tc = lax.broadcasted_iota(jnp.int32, (MXU_DIM, MXU_DIM), 1)
    tri = jnp.where((tc <= tr) & (tc // CHUNK == tr // CHUNK), 1.0, 0.0).astype(BF16)
    ld_parts = split3(ld)[:2]
    lp = jnp.concatenate(
        [sum(jnp.dot(tri, part[t0:t0 + MXU_DIM], preferred_element_type=F32) for part in ld_parts)
         for t0 in range(0, tb, MXU_DIM)], axis=0)

    a_t, r_t, b_t, k_t, b_e, k_e, vs, p_end = [], [], [], [], [], [], [], []
    b = kk * iclr
    for c in range(n_chunks):
        rs = slice(c * CHUNK, (c + 1) * CHUNK)
        lp_c = lp[rs]
        lp_last = lp_c[CHUNK - 1:CHUNK]
        e_in = jnp.exp(lp_c)
        e_inv = jnp.exp(-lp_c)
        e_end = jnp.exp(lp_last - lp_c)
        a_c = -kk[rs] * jnp.exp(lp_c - ld[rs])
        r_c = r[rs] * e_in
        b_c = b[rs] * e_inv
        k_c = kh[rs] * e_inv
        be_c = b[rs] * e_end
        ke_c = kh[rs] * e_end
        pe_c = jnp.exp(lp_last)
        for p in range(n_pairs):
            cs = slice(p * LANES, (p + 1) * LANES)
            a_t.append(a_c[:, cs])
            r_t.append(r_c[:, cs])
            b_t.append(b_c[:, cs])
            k_t.append(k_c[:, cs])
            b_e.append(be_c[:, cs])
            k_e.append(ke_c[:, cs])
            vs.append(v[rs, cs])
            p_end.append(pe_c[:, cs])

    qhat, yhat, mmat, nmat = _rwkv_chunk_terms(a_t, r_t, b_t, k_t, b_e, k_e, vs, p_end)

    s = [state[p] for p in range(n_pairs)]
    for c in range(n_chunks):
        ym = [_dot(jnp.concatenate([qhat[c * n_pairs + p], mmat[c * n_pairs + p]], axis=0), s[p])
              for p in range(n_pairs)]
        for p in range(n_pairs):
            i = c * n_pairs + p
            yout_s[c * CHUNK:(c + 1) * CHUNK, p * LANES:(p + 1) * LANES] = ym[p][:CHUNK] + yhat[i]
            s[p] = ym[p][CHUNK:] + nmat[i]
    for p in range(n_pairs):
        state[p] = s[p]

    y = yout_s[...]
    mean = seg_mean(y)
    yc = y - mean
    var = seg_mean(yc * yc)
    y = yc * lax.rsqrt(var + RW_GN_EPS) * lnxg_ref[...] + lnxb_ref[...]
    bonus = seg_mean(r * kh * rk_ref[...]) * RW_HEAD_DIM * v
    o_ref[0] = ((y + bonus) * gate).astype(o_ref.dtype)


def _rwkv(rkv, lr, w0, a0, wwa, gup, k_k, k_a, r_k, lnx_g, lnx_b, *, tb=512):
    bsz, t, w3 = rkv.shape
    n_pairs = RW_WIDTH // LANES

    def full(shape):
        return pl.BlockSpec(shape, lambda b, i: (0,) * len(shape))

    return pl.pallas_call(
        functools.partial(_rwkv_kernel, tb=tb),
        grid=(bsz, t // tb),
        in_specs=[pl.BlockSpec((1, tb, w3), lambda b, i: (b, i, 0)),
                  pl.BlockSpec((1, tb, LR_PAD), lambda b, i: (b, i, 0)),
                  full((1, RW_WIDTH)), full((1, RW_WIDTH)),
                  full((LANES, 2 * RW_WIDTH)), full((LR_PAD - LANES, RW_WIDTH)),
                  full((1, RW_WIDTH)), full((1, RW_WIDTH)), full((1, RW_WIDTH)),
                  full((1, RW_WIDTH)), full((1, RW_WIDTH))],
        out_specs=pl.BlockSpec((1, tb, RW_WIDTH), lambda b, i: (b, i, 0)),
        out_shape=jax.ShapeDtypeStruct((bsz, t, RW_WIDTH), BF16),
        scratch_shapes=[pltpu.VMEM((n_pairs, LANES, LANES), F32),
                        pltpu.VMEM((tb, RW_WIDTH), F32)],
        compiler_params=_cparams(2),
        name="rwkv",
    )(rkv, lr, w0, a0, wwa, gup, k_k, k_a, r_k, lnx_g, lnx_b)


def _attn_kernel(q_ref, kp_ref, kc_ref, vp_ref, vc_ref, bias_ref, o_ref, lse_ref, *, nb, rb):
    step = pl.program_id(1)
    n_blk = rb // BLOCK
    scale = DIL_HEAD_DIM ** -0.5
    n_pairs = DIL_WIDTH // LANES
    krow = lax.broadcasted_iota(jnp.int32, (2 * BLOCK, LANES), 0) // BLOCK
    klane = lax.broadcasted_iota(jnp.int32, (2 * BLOCK, LANES), 1) // DIL_HEAD_DIM
    own = krow == klane
    zeros = jnp.zeros((2 * BLOCK, LANES), BF16)
    ones_bd = jnp.where(own, 1.0, 0.0).astype(BF16)
    lane_o = lax.broadcasted_iota(jnp.int32, (BLOCK, LANES), 1)

    def stack_bd(x):
        return jnp.where(own, jnp.concatenate([x, x], axis=0), zeros)

    def attend(blocks):
        units = [(rows, q, parts, p) for rows, q, parts in blocks for p in range(n_pairs)]
        lanes = [slice(p * LANES, (p + 1) * LANES) for p in range(n_pairs)]
        scores = []
        for _, q, parts, p in units:
            qs = q[:, lanes[p]] * scale
            row = []
            for k_blk, _, col0, pen in parts:
                s = lax.dot_general(qs, stack_bd(k_blk[:, lanes[p]]), (((1,), (1,)), ((), ())),
                                    preferred_element_type=F32)
                bias = jnp.concatenate([bias_ref[2 * p, :, col0:col0 + BLOCK],
                                        bias_ref[2 * p + 1, :, col0:col0 + BLOCK]], axis=1)
                s = s + bias
                row.append(s if pen is None else s + pen)
            scores.append(row)
        ms = []
        for u in range(len(units)):
            halves = []
            for hh in range(2):
                cols = slice(hh * BLOCK, (hh + 1) * BLOCK)
                m = jnp.max(scores[u][0][:, cols], axis=-1, keepdims=True)
                for s in scores[u][1:]:
                    m = jnp.maximum(m, jnp.max(s[:, cols], axis=-1, keepdims=True))
                halves.append(m)
            ms.append(halves)
        es = [[jnp.exp((s - jnp.concatenate([jnp.broadcast_to(ms[u][0], (BLOCK, BLOCK)),
                                             jnp.broadcast_to(ms[u][1], (BLOCK, BLOCK))], axis=1)).astype(BF16))
               for s in scores[u]] for u in range(len(units))]
        res = []
        for u, (_, _, parts, p) in enumerate(units):
            res.append(sum(jnp.dot(e, jnp.concatenate([stack_bd(part[1][:, lanes[p]]), ones_bd], axis=1),
                                   preferred_element_type=F32) for e, part in zip(es[u], parts)))
        for b, (rows, _, _) in enumerate(blocks):
            m_all = jnp.zeros((BLOCK, LANES), F32)
            den_all = jnp.ones((BLOCK, LANES), F32)
            for p in range(n_pairs):
                u = b * n_pairs + p
                num, den = res[u][:, :LANES], res[u][:, LANES:]
                o_ref[rows, lanes[p]] = (num / den).astype(o_ref.dtype)
                m_all = jnp.where(lane_o == 2 * p, ms[u][0], jnp.where(lane_o == 2 * p + 1, ms[u][1], m_all))
                den_all = jnp.where((lane_o == 2 * p), den,
                                    jnp.where(lane_o == 2 * p + 1, pltpu.roll(den, DIL_HEAD_DIM, 1), den_all))
            lse_ref[rows, :] = m_all + jnp.log(den_all)

    blocks = []
    for j in range(n_blk):
        rows = slice(j * BLOCK, (j + 1) * BLOCK)
        cur = (kc_ref[rows, :], vc_ref[rows, :], BLOCK, None)
        if n_blk % nb == 0:
            prev = "none" if j % nb == 0 else "inside"
        else:
            prev = "inside" if j > 0 else ("none" if nb == n_blk else "before")
        if prev == "none":
            parts = [cur]
        elif prev == "inside":
            before = slice((j - 1) * BLOCK, j * BLOCK)
            parts = [(kc_ref[before, :], vc_ref[before, :], 0, None), cur]
        else:
            has_prev = (step % (nb // n_blk)) != 0
            parts = [(kp_ref[...], vp_ref[...], 0, jnp.where(has_prev, 0.0, MASKED)), cur]
        blocks.append((rows, q_ref[rows, :], parts))
    attend(blocks)


def _qkv_proj_kernel(x_ref, mod_ref, w_ref, o0_ref, o1_ref, o2_ref, xs_ref, *, tm):
    m = mod_ref[0]
    scale = 1.0 + m[1:2]
    shift = m[0:1]
    n_col = x_ref.shape[-1] // LANES
    gw = 3 * DIL_WIDTH

    for c in range(n_col):
        xs_ref[c] = x_ref[0, :, c * LANES:(c + 1) * LANES]

    for g, o_ref in enumerate((o0_ref, o1_ref, o2_ref)):
        dil = DIL_PATTERNS[g][1]
        rows = tm // dil
        if dil == 1:
            xp = x_ref[0]
        else:
            xp = jnp.concatenate(
                [jnp.concatenate([xs_ref[c, pl.ds(r, rows, stride=dil), :] for r in range(dil)], axis=0)
                 for c in range(n_col)], axis=1)
        u = (xp * scale + shift).astype(BF16)
        res = jnp.dot(u, w_ref[:, g * gw:(g + 1) * gw], preferred_element_type=F32).astype(BF16)
        for r in range(dil):
            o_ref[0, r] = res[r * rows:(r + 1) * rows]


def _qkv_proj(x, mod, w, *, tm=1024):
    bsz, t, d = x.shape
    gw = 3 * DIL_WIDTH
    out_specs, out_shape = [], []
    for _, dil in DIL_PATTERNS:
        out_specs.append(pl.BlockSpec((1, dil, tm // dil, gw), lambda b, i: (b, 0, i, 0)))
        out_shape.append(jax.ShapeDtypeStruct((bsz, dil, t // dil, gw), BF16))
    return pl.pallas_call(
        functools.partial(_qkv_proj_kernel, tm=tm),
        grid=(bsz, t // tm),
        in_specs=[pl.BlockSpec((1, tm, d), lambda b, i: (b, i, 0)),
                  pl.BlockSpec((1, 6, d), lambda b, i: (b, 0, 0)),
                  pl.BlockSpec(w.shape, lambda b, i: (0, 0), pipeline_mode=pl.Buffered(1))],
        out_specs=out_specs,
        out_shape=out_shape,
        scratch_shapes=[pltpu.VMEM((d // LANES, tm, LANES), F32)],
        compiler_params=_cparams(2),
        name="qkv_proj",
    )(x, mod, w)


def _dilated_attention_group(qkv, bias, g, *, rb=512):
    bsz, dil, length, width = qkv.shape
    t = dil * length
    nb = length // BLOCK
    n_blk = rb // BLOCK
    assert nb % n_blk == 0 or n_blk % nb == 0
    flat = qkv.reshape(bsz, t, width)

    def cur(j):
        return lambda b, s: (b, s, j)

    def prev(j):
        return lambda b, s: (b, jnp.maximum(s * n_blk - 1, 0), j)

    blk = (None, rb, DIL_WIDTH)
    pblk = (None, BLOCK, DIL_WIDTH)
    o, lse = pl.pallas_call(
        functools.partial(_attn_kernel, nb=nb, rb=rb),
        grid=(bsz, t // rb),
        in_specs=[pl.BlockSpec(blk, cur(0)),
                  pl.BlockSpec(pblk, prev(1)), pl.BlockSpec(blk, cur(1)),
                  pl.BlockSpec(pblk, prev(2)), pl.BlockSpec(blk, cur(2)),
                  pl.BlockSpec((DIL_HEADS, BLOCK, 2 * BLOCK), lambda b, s: (0, 0, 0))],
        out_specs=[pl.BlockSpec(blk, lambda b, s: (b, s, 0)),
                   pl.BlockSpec((None, rb, LANES), lambda b, s: (b, s, 0))],
        out_shape=[jax.ShapeDtypeStruct((bsz, t, DIL_WIDTH), BF16),
                   jax.ShapeDtypeStruct((bsz, t, LANES), F32)],
        compiler_params=_cparams(2),
        name=f"dilated_attn_g{g}",
    )(flat, flat, flat, flat, flat, bias)
    return o.reshape(bsz, dil, length, DIL_WIDTH), lse.reshape(bsz, dil, length, LANES)


def _t5_bucket(dist):
    exact = N_BUCKETS // 2
    logd = jnp.log(jnp.maximum(dist, 1).astype(F32) / exact) / math.log(MAX_DISTANCE / exact)
    large = jnp.minimum(exact + (logd * (N_BUCKETS - exact)).astype(jnp.int32), N_BUCKETS - 1)
    return jnp.where(dist < exact, dist, large)


def _bias_tables(rel_bias):
    rel = jnp.arange(BLOCK + 1)
    tables = []
    for g, (window, dil) in enumerate(DIL_PATTERNS):
        span = window // dil
        bucket = _t5_bucket(jnp.clip(rel, 0, span) * dil)
        vec = rel_bias[bucket][:, g * DIL_HEADS:(g + 1) * DIL_HEADS].T.astype(F32)
        ext = jnp.concatenate([vec[:, ::-1], jnp.zeros((DIL_HEADS, BLOCK), F32)], axis=1)
        flat = jnp.tile(ext, (1, BLOCK))[:, :BLOCK * 2 * BLOCK]
        tables.append(flat.reshape(DIL_HEADS, BLOCK, 2 * BLOCK))
    qi = np.arange(BLOCK)[:, None]
    ki = np.arange(2 * BLOCK)[None, :]
    band = (ki >= qi) & (ki <= qi + BLOCK)
    return jnp.where(band, jnp.stack(tables), MASKED)


def _tail(mix_fn, wout_ref, x_ref, mod_ref, lng_ref, lnb_ref, w1_ref, w2_ref, o_ref, *, tm, n_sub, ff_chunk):
    m = mod_ref[0]
    sub = tm // n_sub
    tiles = [slice(i * sub, (i + 1) * sub) for i in range(n_sub)]
    x1s, us = [], []
    for rows in tiles:
        y = None
        for piece, k0 in mix_fn(rows):
            part = jnp.dot(piece, wout_ref[k0:k0 + piece.shape[-1], :], preferred_element_type=F32)
            y = part if y is None else y + part
        x1 = _layer_norm_rows(DEEPNORM_ALPHA * x_ref[0, rows, :] + (1.0 + m[2:3]) * y, lng_ref[0:1], lnb_ref[0:1])
        x1s.append(x1)
        us.append((x1 * (1.0 + m[4:5]) + m[3:4]).astype(BF16))
    accs = [None] * n_sub
    for c in range(D_FF // ff_chunk):
        cols = slice(c * ff_chunk, (c + 1) * ff_chunk)
        for i in range(n_sub):
            h = jnp.maximum(jnp.dot(us[i], w1_ref[:, cols], preferred_element_type=F32), 0.0)
            part = jnp.dot((h * h).astype(BF16), w2_ref[cols, :], preferred_element_type=F32)
            accs[i] = part if accs[i] is None else accs[i] + part
    for i, rows in enumerate(tiles):
        z = DEEPNORM_ALPHA * x1s[i] + (1.0 + m[5:6]) * accs[i]
        o_ref[0, rows, :] = _layer_norm_rows(z, lng_ref[1:2], lnb_ref[1:2])


def _even_tail_kernel(ya_ref, yb_ref, wout_ref, x_ref, mod_ref, lng_ref, lnb_ref, w1_ref, w2_ref, o_ref, **kw):
    def mix(rows):
        return [(ya_ref[0, rows, :], 0), (yb_ref[0, rows, :], ya_ref.shape[-1])]

    _tail(mix, wout_ref, x_ref, mod_ref, lng_ref, lnb_ref, w1_ref, w2_ref, o_ref, **kw)


def _odd_tail_kernel(o0_ref, o1_ref, o2_ref, l0_ref, l1_ref, l2_ref, wout_ref, x_ref, mod_ref, lng_ref, lnb_ref,
                     w1_ref, w2_ref, o_ref, o_nat, l_nat, mix_s, *, tm, **kw):
    def natural(src_ref, scr, dil):
        if dil == 1:
            return src_ref[0, 0].astype(F32)
        n_col = src_ref.shape[-1] // LANES
        for c in range(n_col):
            for r in range(dil):
                scr[c, pl.ds(r, tm // dil, stride=dil), :] = src_ref[0, r, :, c * LANES:(c + 1) * LANES].astype(F32)
        return jnp.concatenate([scr[c] for c in range(n_col)], axis=1)

    dils = [dil for _, dil in DIL_PATTERNS]
    lses = [natural(ref, l_nat, dil) for ref, dil in zip((l0_ref, l1_ref, l2_ref), dils)]
    mx = jnp.maximum(jnp.maximum(lses[0], lses[1]), lses[2])
    es = [jnp.exp(l - mx) for l in lses]
    inv = 1.0 / (es[0] + es[1] + es[2])
    hr = lax.broadcasted_iota(jnp.int32, (LANES, DIL_WIDTH), 0)
    hc = lax.broadcasted_iota(jnp.int32, (LANES, DIL_WIDTH), 1) // DIL_HEAD_DIM
    expand = jnp.where(hr == hc, 1.0, 0.0).astype(BF16)

    def widen(wt):
        hi = wt.astype(BF16)
        lo = (wt - hi.astype(F32)).astype(BF16)
        return (jnp.dot(hi, expand, preferred_element_type=F32)
                + jnp.dot(lo, expand, preferred_element_type=F32))

    merged = None
    for e, ref, dil in zip(es, (o0_ref, o1_ref, o2_ref), dils):
        term = widen(e * inv) * natural(ref, o_nat, dil)
        merged = term if merged is None else merged + term
    mix_s[...] = merged.astype(BF16)
    _tail(lambda rows: [(mix_s[rows, :], 0)], wout_ref, x_ref, mod_ref, lng_ref, lnb_ref, w1_ref, w2_ref, o_ref,
          tm=tm, **kw)


def _layer_tail(mix, w_out, x, mod, ln_g, ln_b, w1, w2, *, tm=512, n_sub=2, ff_chunk=1024):
    bsz, t, d = x.shape

    def const(shape):
        return pl.BlockSpec(shape, lambda b, i: (0,) * len(shape), pipeline_mode=pl.Buffered(1))

    tail_specs = [const(w_out.shape),
                  pl.BlockSpec((1, tm, d), lambda b, i: (b, i, 0)),
                  pl.BlockSpec((1, 6, d), lambda b, i: (b, 0, 0)),
                  const((2, d)), const((2, d)), const((d, D_FF)), const((D_FF, d))]
    tail_args = (w_out, x, mod, ln_g, ln_b, w1, w2)
    kw = dict(tm=tm, n_sub=n_sub, ff_chunk=ff_chunk)
    if isinstance(mix, tuple):
        outs, lses = mix
        mix_specs = ([pl.BlockSpec((1, dil, tm // dil, DIL_WIDTH), lambda b, i: (b, 0, i, 0)) for _, dil in DIL_PATTERNS]
                     + [pl.BlockSpec((1, dil, tm // dil, LANES), lambda b, i: (b, 0, i, 0)) for _, dil in DIL_PATTERNS])
        mix_args = (*outs, *lses)
        body = functools.partial(_odd_tail_kernel, **kw)
        scratch = [pltpu.VMEM((DIL_WIDTH // LANES, tm, LANES), F32), pltpu.VMEM((1, tm, LANES), F32),
                   pltpu.VMEM((tm, DIL_WIDTH), BF16)]
    else:
        mix_specs = [pl.BlockSpec((1, tm, piece.shape[-1]), lambda b, i: (b, i, 0)) for piece in mix]
        mix_args = tuple(mix)
        body = functools.partial(_even_tail_kernel, **kw)
        scratch = []
    return pl.pallas_call(
        body,
        grid=(bsz, t // tm),
        in_specs=mix_specs + tail_specs,
        out_specs=pl.BlockSpec((1, tm, d), lambda b, i: (b, i, 0)),
        out_shape=jax.ShapeDtypeStruct((bsz, t, d), F32),
        scratch_shapes=scratch,
        compiler_params=_cparams(2),
        name="layer_tail",
    )(*mix_args, *tail_args)


def _prep_even_weights(w_in, mu, w_up, a_up, g_up):
    w3 = 3 * RW_WIDTH
    pad = LR_PAD - (RW_PROJ - w3)
    d = w_in.shape[0]
    w_perm = jnp.concatenate([w_in[:, :w3], w_in[:, RW_PROJ:], w_in[:, w3:RW_PROJ],
                              jnp.zeros((d, pad), w_in.dtype)], axis=1).astype(BF16)
    mu_pad = jnp.concatenate([mu, jnp.zeros((pad,), mu.dtype)]).reshape(1, w3 + LR_PAD)
    zeros = jnp.zeros((RW_DECAY_RANK, RW_WIDTH), w_up.dtype)
    wwa = jnp.concatenate([jnp.concatenate([w_up, zeros], axis=1),
                           jnp.concatenate([zeros, a_up], axis=1)], axis=0).astype(BF16)
    gup = jnp.concatenate([g_up, jnp.zeros((pad, RW_WIDTH), g_up.dtype)], axis=0).astype(BF16)
    return w_perm, mu_pad, wwa, gup


def kernel(x, c, ada_w, ada_b, ln_g, ln_b, ab_w_in, rw_mu, rw_w0, rw_w_up, rw_a0, rw_a_up, rw_g_up,
           rw_k_k, rw_k_a, rw_r_k, rw_lnx_g, rw_lnx_b, sc_conv_w, ab_w_out, dil_w_qkv, dil_w_out,
           rel_bias, mlp_w1, mlp_w2):
    bsz, t, d = x.shape
    mods = _adaln(c, ada_w, ada_b).reshape(DEPTH, bsz, 6, d)
    bias = None
    row = lambda a: a.reshape(1, -1)
    for i in range(DEPTH):
        mod = mods[i]
        j = i // 2
        if i % 2 == 0:
            w_perm, mu_pad, wwa, gup = _prep_even_weights(
                ab_w_in[j], rw_mu[j], rw_w_up[j], rw_a_up[j], rw_g_up[j])
            rkv, lr, yb = _ab_proj(x, mod, w_perm, mu_pad, sc_conv_w[j])
            ya = _rwkv(rkv, lr, row(rw_w0[j]), row(rw_a0[j]), wwa, gup,
                       row(rw_k_k[j]), row(rw_k_a[j]), row(rw_r_k[j]),
                       row(rw_lnx_g[j]), row(rw_lnx_b[j]))
            mix, w_out = [ya, yb], ab_w_out[j]
        else:
            if bias is None:
                bias = _bias_tables(rel_bias)
            qkvs = _qkv_proj(x, mod, dil_w_qkv[j].astype(BF16))
            outs, lses = [], []
            for g in range(N_GROUPS):
                o, lse = _dilated_attention_group(qkvs[g], bias[g], g)
                outs.append(o)
                lses.append(lse)
            mix, w_out = (outs, lses), dil_w_out[j]
        x = _layer_tail(mix, w_out.astype(BF16), x, mod, ln_g[i], ln_b[i],
                        mlp_w1[i].astype(BF16), mlp_w2[i].astype(BF16))
    return x
```

```python
import functools
import math

import jax
import jax.numpy as jnp
import numpy as np
from jax import lax
from jax.experimental import pallas as pl
from jax.experimental.pallas import tpu as pltpu

F32 = jnp.float32
BF16 = jnp.bfloat16

D_MODEL = 1024
DEPTH = 2
RW_HEADS = 8
RW_HEAD_DIM = 64
RW_WIDTH = 512
RW_DECAY_RANK = 64
RW_ICLR_RANK = 64
RW_GATE_RANK = 160
RW_GN_EPS = 64e-5
RW_PROJ = 3 * RW_WIDTH + RW_DECAY_RANK + RW_ICLR_RANK + RW_GATE_RANK
SC_WIDTH = 512
SC_CONV = 3
DIL_PATTERNS = ((128, 1), (512, 4), (2048, 16))
N_GROUPS = 3
DIL_HEADS = 8
DIL_HEAD_DIM = 64
DIL_WIDTH = 512
BLOCK = 128
N_BUCKETS = 32
MAX_DISTANCE = 2048
D_FF = 4 * D_MODEL
DEEPNORM_ALPHA = (2 * DEPTH) ** 0.25
LN_EPS = 1e-5

LANES = 128
MXU_DIM = 256
CHUNK = 64
LR_PAD = 384
VMEM_LIMIT = 56 * 1024 * 1024
MASKED = -1e30


def _cparams(n_axes):
    return pltpu.CompilerParams(dimension_semantics=("arbitrary",) * n_axes,
                                vmem_limit_bytes=VMEM_LIMIT)


def _sigmoid(x):
    return 1.0 / (1.0 + jnp.exp(-x))


def _dot(a, b):
    return jnp.dot(a.astype(BF16), b.astype(BF16), preferred_element_type=F32)


def _layer_norm_rows(z, g, b):
    mu = jnp.mean(z, axis=-1, keepdims=True)
    zc = z - mu
    var = jnp.mean(zc * zc, axis=-1, keepdims=True)
    return zc * lax.rsqrt(var + LN_EPS) * g + b


def _adaln_kernel(c_ref, w_ref, b_ref, o_ref):
    c = c_ref[...]
    cond = c * _sigmoid(c)
    o_ref[0] = _dot(cond, w_ref[0]) + b_ref[0]


def _adaln(c, ada_w, ada_b):
    depth, d, n = ada_w.shape
    bsz = c.shape[0]
    tn = 1536
    return pl.pallas_call(
        _adaln_kernel,
        grid=(depth, n // tn),
        in_specs=[pl.BlockSpec((bsz, d), lambda i, j: (0, 0)),
                  pl.BlockSpec((1, d, tn), lambda i, j: (i, 0, j)),
                  pl.BlockSpec((1, 1, tn), lambda i, j: (i, 0, j))],
        out_specs=pl.BlockSpec((1, bsz, tn), lambda i, j: (i, 0, j)),
        out_shape=jax.ShapeDtypeStruct((depth, bsz, n), F32),
        compiler_params=_cparams(2),
        name="adaln",
    )(c, ada_w, ada_b.reshape(depth, 1, n))


def _shift_rows(x, carry, k):
    rolled = pltpu.roll(x, k, 0)
    head = rolled[:8]
    row = lax.broadcasted_iota(jnp.int32, head.shape, 0)
    for i in range(k):
        head = jnp.where(row == i, carry[8 - k + i:8 - k + i + 1], head)
    return jnp.concatenate([head, rolled[8:]], axis=0)


def _ab_proj_kernel(x_ref, mod_ref, w_ref, mu_ref, convw_ref, rkv_ref, lr_ref, yb_ref, carry_p, carry_z, *, tm):
    @pl.when(pl.program_id(1) == 0)
    def _():
        carry_p[...] = jnp.zeros_like(carry_p)
        carry_z[...] = jnp.zeros_like(carry_z)

    m = mod_ref[0]
    u = (x_ref[0] * (1.0 + m[1:2]) + m[0:1]).astype(BF16)

    def proj(c0, width):
        return jnp.dot(u, w_ref[:, c0:c0 + width], preferred_element_type=F32)

    def mixed(c0, m0, width):
        p = proj(c0, width)
        prev = _shift_rows(p, carry_p[:, m0:m0 + width], 1)
        carry_p[:, m0:m0 + width] = p[tm - 8:]
        return (p + mu_ref[:, m0:m0 + width] * (prev - p)).astype(BF16)

    w3 = 3 * RW_WIDTH
    for c in range(3):
        rkv_ref[0, :, c * RW_WIDTH:(c + 1) * RW_WIDTH] = mixed(c * RW_WIDTH, c * RW_WIDTH, RW_WIDTH)
    lr_ref[0] = mixed(2 * w3, w3, LR_PAD)

    h = proj(w3, SC_WIDTH)
    z = proj(w3 + 2 * SC_WIDTH, SC_WIDTH) * h
    zc = carry_z[...]
    cw = convw_ref[...]
    conv = cw[2:3] * z + cw[1:2] * _shift_rows(z, zc, 1) + cw[0:1] * _shift_rows(z, zc, 2)
    carry_z[...] = z[tm - 8:]
    yb_ref[0] = (proj(w3 + SC_WIDTH, SC_WIDTH) * conv).astype(BF16)


def _ab_proj(x, mod, w, mu, conv_w, *, tm=1024):
    bsz, t, d = x.shape
    w3 = 3 * RW_WIDTH

    def const(shape):
        return pl.BlockSpec(shape, lambda b, i: (0,) * len(shape), pipeline_mode=pl.Buffered(1))

    def rows(width):
        return pl.BlockSpec((1, tm, width), lambda b, i: (b, i, 0))

    return pl.pallas_call(
        functools.partial(_ab_proj_kernel, tm=tm),
        grid=(bsz, t // tm),
        in_specs=[rows(d), pl.BlockSpec((1, 6, d), lambda b, i: (b, 0, 0)),
                  const(w.shape), const((1, w3 + LR_PAD)), const((SC_CONV, SC_WIDTH))],
        out_specs=[rows(w3), rows(LR_PAD), rows(SC_WIDTH)],
        out_shape=[jax.ShapeDtypeStruct((bsz, t, w3), BF16), jax.ShapeDtypeStruct((bsz, t, LR_PAD), BF16),
                   jax.ShapeDtypeStruct((bsz, t, SC_WIDTH), BF16)],
        scratch_shapes=[pltpu.VMEM((8, w3 + LR_PAD), F32), pltpu.VMEM((8, SC_WIDTH), F32)],
        compiler_params=_cparams(2),
        name="ab_proj",
    )(x, mod, w, mu, conv_w)


def _rwkv_chunk_terms(a_t, r_t, b_t, k_t, b_e, k_e, v, p_end):
    n = len(a_t)
    idx = range(n)
    row = lax.broadcasted_iota(jnp.int32, (CHUNK, LANES), 0)
    col = lax.broadcasted_iota(jnp.int32, (CHUNK, LANES), 1)
    s_idx = col % CHUNK
    head0 = col < RW_HEAD_DIM
    strict = s_idx < row
    incl = s_idx <= row
    near = strict & ((s_idx // 16) == (row // 16))
    far = strict & ((s_idx // 16) != (row // 16))
    eye = jnp.where(s_idx == row, 1.0, 0.0)
    r2 = lax.broadcasted_iota(jnp.int32, (LANES, 2 * LANES), 0)
    c2 = lax.broadcasted_iota(jnp.int32, (LANES, 2 * LANES), 1)
    bdmask2 = (r2 // RW_HEAD_DIM) == ((c2 % LANES) // RW_HEAD_DIM)
    eye_bd = jnp.where(lax.broadcasted_iota(jnp.int32, (LANES, LANES), 0)
                       == lax.broadcasted_iota(jnp.int32, (LANES, LANES), 1), 1.0, 0.0)
    zero16 = jnp.zeros((CHUNK, LANES), BF16)
    lane1 = lax.broadcasted_iota(jnp.int32, (1, LANES), 1)
    m0_16 = jnp.where(lane1 < RW_HEAD_DIM, 1.0, 0.0).astype(BF16)
    m1_16 = jnp.where(lane1 < RW_HEAD_DIM, 0.0, 1.0).astype(BF16)

    def cast(xs):
        return [x.astype(BF16) for x in xs]

    def bd(q16):
        return jnp.concatenate([q16 * m0_16, q16 * m1_16], axis=0)

    def mm(p16, q16):
        return jnp.dot(p16, q16, preferred_element_type=F32)

    def mm_nt(p16, q16):
        return lax.dot_general(p16, q16, (((1,), (1,)), ((), ())), preferred_element_type=F32)

    a16, r16, v16 = cast(a_t), cast(r_t), cast(v)
    lhs = [jnp.concatenate([a16[i], r16[i]], axis=0) for i in idx]
    b16, k16 = cast(b_t), cast(k_t)
    bdv = [bd(x) for x in v16]
    g = [mm_nt(lhs[i], jnp.concatenate([bd(b16[i]), bd(k16[i])], axis=0)) for i in idx]
    x1 = [jnp.where(near, g[i][:CHUNK, :LANES], 0.0) for i in idx]
    x1_16 = cast(x1)
    l_e16 = [jnp.where(far, g[i][:CHUNK, :LANES], 0.0).astype(BF16) for i in idx]
    a_k16 = [jnp.concatenate([jnp.where(strict, g[i][:CHUNK, LANES:], 0.0),
                              jnp.where(incl, g[i][CHUNK:, LANES:], 0.0)], axis=0).astype(BF16) for i in idx]
    a_rb16 = [jnp.where(incl, g[i][CHUNK:, :LANES], 0.0).astype(BF16) for i in idx]

    def both(p16, q16):
        return jnp.concatenate([bd(p16), bd(q16)], axis=1)

    x2_16 = cast([mm(x1_16[i], bd(x1_16[i])) for i in idx])
    acc = [eye + x1[i] for i in idx]
    sq = [mm(x2_16[i], both(x2_16[i], acc[i].astype(BF16))) for i in idx]
    x4_16 = [sq[i][:, :LANES].astype(BF16) for i in idx]
    acc = [acc[i] + sq[i][:, LANES:] for i in idx]
    sq = [mm(x4_16[i], both(x4_16[i], acc[i].astype(BF16))) for i in idx]
    x8_16 = [sq[i][:, :LANES].astype(BF16) for i in idx]
    acc = [acc[i] + sq[i][:, LANES:] for i in idx]
    dinv = [acc[i] + mm(x8_16[i], bd(acc[i].astype(BF16))) for i in idx]
    dinv16 = cast(dinv)
    f16 = cast([mm(dinv16[i], bd(l_e16[i])) for i in idx])
    sq = [mm(f16[i], both(f16[i], dinv16[i])) for i in idx]
    f2_16 = [sq[i][:, :LANES].astype(BF16) for i in idx]
    hmat = [dinv[i] + sq[i][:, LANES:] for i in idx]
    tinv16 = cast([hmat[i] + mm(f2_16[i], bd(hmat[i].astype(BF16))) for i in idx])

    kv = [mm(a_k16[i], bdv[i]) for i in idx]
    wv16 = [kv[i][:CHUNK].astype(BF16) for i in idx]
    aw = [mm(tinv16[i], both(a16[i], wv16[i])) for i in idx]
    aw16 = cast(aw)
    qy = [mm(a_rb16[i], both(aw16[i][:, :LANES], aw16[i][:, LANES:]))
          + jnp.concatenate([r_t[i], kv[i][CHUNK:]], axis=1) for i in idx]
    be_ke_t = [jnp.concatenate([b_e[i], k_e[i]], axis=0).T.astype(BF16) for i in idx]
    rhs = [jnp.concatenate([aw16[i], jnp.concatenate([zero16, v16[i]], axis=1)], axis=0) for i in idx]
    mn = [jnp.where(bdmask2, mm(be_ke_t[i], rhs[i]), 0.0) for i in idx]
    qhat = [x[:, :LANES] for x in qy]
    yhat = [x[:, LANES:] for x in qy]
    mmat = [eye_bd * p_end[i] + mn[i][:, :LANES] for i in idx]
    nmat = [x[:, LANES:] for x in mn]
    return qhat, yhat, mmat, nmat


def _rwkv_kernel(rkv_ref, lr_ref, w0_ref, a0_ref, wwa_ref, gup_ref,
                 kk_ref, ka_ref, rk_ref, lnxg_ref, lnxb_ref, o_ref, state, yout_s, *, tb):
    n_chunks = tb // CHUNK
    n_pairs = RW_WIDTH // LANES

    @pl.when(pl.program_id(1) == 0)
    def _():
        state[...] = jnp.zeros_like(state)

    r = rkv_ref[0, :, :RW_WIDTH].astype(F32)
    k = rkv_ref[0, :, RW_WIDTH:2 * RW_WIDTH].astype(F32)
    v = rkv_ref[0, :, 2 * RW_WIDTH:].astype(F32)
    lr_m = lr_ref[0].astype(F32)

    wa = lr_m[:, :LANES]
    lane = lax.broadcasted_iota(jnp.int32, wa.shape, 1)
    wa = jnp.where(lane < RW_DECAY_RANK, jnp.tanh(wa), wa)
    wa_up = _dot(wa, wwa_ref[...])
    ld = -math.exp(-0.5) * _sigmoid(w0_ref[...] + wa_up[:, :RW_WIDTH])
    iclr = _sigmoid(a0_ref[...] + wa_up[:, RW_WIDTH:])
    gate = _dot(_sigmoid(lr_m[:, LANES:]), gup_ref[...])

    hr = lax.broadcasted_iota(jnp.int32, (MXU_DIM, MXU_DIM), 0) // RW_HEAD_DIM
    hc = lax.broadcasted_iota(jnp.int32, (MXU_DIM, MXU_DIM), 1) // RW_HEAD_DIM
    seg = jnp.where(hr == hc, 1.0 / RW_HEAD_DIM, 0.0).astype(BF16)

    def split3(x):
        hi = x.astype(BF16)
        rem = x - hi.astype(F32)
        mid = rem.astype(BF16)
        return hi, mid, (rem - mid.astype(F32)).astype(BF16)

    def seg_mean(x, terms=1):
        parts = split3(x)[:terms]
        cols = []
        for c in range(0, RW_WIDTH, MXU_DIM):
            cols.append(sum(jnp.dot(part[:, c:c + MXU_DIM], seg, preferred_element_type=F32) for part in parts))
        return jnp.concatenate(cols, axis=1)

    kk = k * kk_ref[...]
    ss = seg_mean(kk * kk) * RW_HEAD_DIM
    kk = kk * lax.rsqrt(jnp.maximum(ss, 1e-24))
    kh = k * (1.0 + (iclr - 1.0) * ka_ref[...])

    tr = lax.broadcasted_iota(jnp.int32, (MXU_DIM, MXU_DIM), 0)
    tc = lax.broadcasted_iota(jnp.int32, (MXU_DIM, MXU_DIM), 1)
    tri = jnp.where((tc <= tr) & (tc // CHUNK == tr // CHUNK), 1.0, 0.0).astype(BF16)
    ld_parts = split3(ld)[:2]
    lp = jnp.concatenate(
        [sum(jnp.dot(tri, part[t0:t0 + MXU_DIM], preferred_element_type=F32) for part in ld_parts)
         for t0 in range(0, tb, MXU_DIM)], axis=0)

    a_t, r_t, b_t, k_t, b_e, k_e, vs, p_end = [], [], [], [], [], [], [], []
    b = kk * iclr
    for c in range(n_chunks):
        rs = slice(c * CHUNK, (c + 1) * CHUNK)
        lp_c = lp[rs]
        lp_last = lp_c[CHUNK - 1:CHUNK]
        e_in = jnp.exp(lp_c)
        e_inv = jnp.exp(-lp_c)
        e_end = jnp.exp(lp_last - lp_c)
        a_c = -kk[rs] * jnp.exp(lp_c - ld[rs])
        r_c = r[rs] * e_in
        b_c = b[rs] * e_inv
        k_c = kh[rs] * e_inv
        be_c = b[rs] * e_end
        ke_c = kh[rs] * e_end
        pe_c = jnp.exp(lp_last)
        for p in range(n_pairs):
            cs = slice(p * LANES, (p + 1) * LANES)
            a_t.append(a_c[:, cs])
            r_t.append(r_c[:, cs])
            b_t.append(b_c[:, cs])
            k_t.append(k_c[:, cs])
            b_e.append(be_c[:, cs])
            k_e.append(ke_c[:, cs])
            vs.append(v[rs, cs])
            p_end.append(pe_c[:, cs])

    qhat, yhat, mmat, nmat = _rwkv_chunk_terms(a_t, r_t, b_t, k_t, b_e, k_e, vs, p_end)

    s = [state[p] for p in range(n_pairs)]
    for c in range(n_chunks):
        ym = [_dot(jnp.concatenate([qhat[c * n_pairs + p], mmat[c * n_pairs + p]], axis=0), s[p])
              for p in range(n_pairs)]
        for p in range(n_pairs):
            i = c * n_pairs + p
            yout_s[c * CHUNK:(c + 1) * CHUNK, p * LANES:(p + 1) * LANES] = ym[p][:CHUNK] + yhat[i]
            s[p] = ym[p][CHUNK:] + nmat[i]
    for p in range(n_pairs):
        state[p] = s[p]

    y = yout_s[...]
    mean = seg_mean(y)
    yc = y - mean
    var = seg_mean(yc * yc)
    y = yc * lax.rsqrt(var + RW_GN_EPS) * lnxg_ref[...] + lnxb_ref[...]
    bonus = seg_mean(r * kh * rk_ref[...]) * RW_HEAD_DIM * v
    o_ref[0] = ((y + bonus) * gate).astype(o_ref.dtype)


def _rwkv(rkv, lr, w0, a0, wwa, gup, k_k, k_a, r_k, lnx_g, lnx_b, *, tb=512):
    bsz, t, w3 = rkv.shape
    n_pairs = RW_WIDTH // LANES

    def full(shape):
        return pl.BlockSpec(shape, lambda b, i: (0,) * len(shape))

    return pl.pallas_call(
        functools.partial(_rwkv_kernel, tb=tb),
        grid=(bsz, t // tb),
        in_specs=[pl.BlockSpec((1, tb, w3), lambda b, i: (b, i, 0)),
                  pl.BlockSpec((1, tb, LR_PAD), lambda b, i: (b, i, 0)),
                  full((1, RW_WIDTH)), full((1, RW_WIDTH)),
                  full((LANES, 2 * RW_WIDTH)), full((LR_PAD - LANES, RW_WIDTH)),
                  full((1, RW_WIDTH)), full((1, RW_WIDTH)), full((1, RW_WIDTH)),
                  full((1, RW_WIDTH)), full((1, RW_WIDTH))],
        out_specs=pl.BlockSpec((1, tb, RW_WIDTH), lambda b, i: (b, i, 0)),
        out_shape=jax.ShapeDtypeStruct((bsz, t, RW_WIDTH), BF16),
        scratch_shapes=[pltpu.VMEM((n_pairs, LANES, LANES), F32),
                        pltpu.VMEM((tb, RW_WIDTH), F32)],
        compiler_params=_cparams(2),
        name="rwkv",
    )(rkv, lr, w0, a0, wwa, gup, k_k, k_a, r_k, lnx_g, lnx_b)


def _attn_kernel(q_ref, kp_ref, kc_ref, vp_ref, vc_ref, bias_ref, o_ref, lse_ref, *, nb, rb):
    step = pl.program_id(1)
    n_blk = rb // BLOCK
    scale = DIL_HEAD_DIM ** -0.5
    n_pairs = DIL_WIDTH // LANES
    krow = lax.broadcasted_iota(jnp.int32, (2 * BLOCK, LANES), 0) // BLOCK
    klane = lax.broadcasted_iota(jnp.int32, (2 * BLOCK, LANES), 1) // DIL_HEAD_DIM
    own = krow == klane
    zeros = jnp.zeros((2 * BLOCK, LANES), BF16)
    ones_bd = jnp.where(own, 1.0, 0.0).astype(BF16)
    lane_o = lax.broadcasted_iota(jnp.int32, (BLOCK, LANES), 1)

    def stack_bd(x):
        return jnp.where(own, jnp.concatenate([x, x], axis=0), zeros)

    def attend(blocks):
        units = [(rows, q, parts, p) for rows, q, parts in blocks for p in range(n_pairs)]
        lanes = [slice(p * LANES, (p + 1) * LANES) for p in range(n_pairs)]
        scores = []
        for _, q, parts, p in units:
            qs = q[:, lanes[p]] * scale
            row = []
            for k_blk, _, col0, pen in parts:
                s = lax.dot_general(qs, stack_bd(k_blk[:, lanes[p]]), (((1,), (1,)), ((), ())),
                                    preferred_element_type=F32)
                bias = jnp.concatenate([bias_ref[2 * p, :, col0:col0 + BLOCK],
                                        bias_ref[2 * p + 1, :, col0:col0 + BLOCK]], axis=1)
                s = s + bias
                row.append(s if pen is None else s + pen)
            scores.append(row)
        ms = []
        for u in range(len(units)):
            halves = []
            for hh in range(2):
                cols = slice(hh * BLOCK, (hh + 1) * BLOCK)
                m = jnp.max(scores[u][0][:, cols], axis=-1, keepdims=True)
                for s in scores[u][1:]:
                    m = jnp.maximum(m, jnp.max(s[:, cols], axis=-1, keepdims=True))
                halves.append(m)
            ms.append(halves)
        es = [[jnp.exp((s - jnp.concatenate([jnp.broadcast_to(ms[u][0], (BLOCK, BLOCK)),
                                             jnp.broadcast_to(ms[u][1], (BLOCK, BLOCK))], axis=1)).astype(BF16))
               for s in scores[u]] for u in range(len(units))]
        res = []
        for u, (_, _, parts, p) in enumerate(units):
            res.append(sum(jnp.dot(e, jnp.concatenate([stack_bd(part[1][:, lanes[p]]), ones_bd], axis=1),
                                   preferred_element_type=F32) for e, part in zip(es[u], parts)))
        for b, (rows, _, _) in enumerate(blocks):
            m_all = jnp.zeros((BLOCK, LANES), F32)
            den_all = jnp.ones((BLOCK, LANES), F32)
            for p in range(n_pairs):
                u = b * n_pairs + p
                num, den = res[u][:, :LANES], res[u][:, LANES:]
                o_ref[rows, lanes[p]] = (num / den).astype(o_ref.dtype)
                m_all = jnp.where(lane_o == 2 * p, ms[u][0], jnp.where(lane_o == 2 * p + 1, ms[u][1], m_all))
                den_all = jnp.where((lane_o == 2 * p), den,
                                    jnp.where(lane_o == 2 * p + 1, pltpu.roll(den, DIL_HEAD_DIM, 1), den_all))
            lse_ref[rows, :] = m_all + jnp.log(den_all)

    blocks = []
    for j in range(n_blk):
        rows = slice(j * BLOCK, (j + 1) * BLOCK)
        cur = (kc_ref[rows, :], vc_ref[rows, :], BLOCK, None)
        if n_blk % nb == 0:
            prev = "none" if j % nb == 0 else "inside"
        else:
            prev = "inside" if j > 0 else ("none" if nb == n_blk else "before")
        if prev == "none":
            parts = [cur]
        elif prev == "inside":
            before = slice((j - 1) * BLOCK, j * BLOCK)
            parts = [(kc_ref[before, :], vc_ref[before, :], 0, None), cur]
        else:
            has_prev = (step % (nb // n_blk)) != 0
            parts = [(kp_ref[...], vp_ref[...], 0, jnp.where(has_prev, 0.0, MASKED)), cur]
        blocks.append((rows, q_ref[rows, :], parts))
    attend(blocks)


def _qkv_proj_kernel(x_ref, mod_ref, w_ref, o0_ref, o1_ref, o2_ref, xs_ref, *, tm):
    m = mod_ref[0]
    scale = 1.0 + m[1:2]
    shift = m[0:1]
    n_col = x_ref.shape[-1] // LANES
    gw = 3 * DIL_WIDTH

    for c in range(n_col):
        xs_ref[c] = x_ref[0, :, c * LANES:(c + 1) * LANES]

    for g, o_ref in enumerate((o0_ref, o1_ref, o2_ref)):
        dil = DIL_PATTERNS[g][1]
        rows = tm // dil
        if dil == 1:
            xp = x_ref[0]
        else:
            xp = jnp.concatenate(
                [jnp.concatenate([xs_ref[c, pl.ds(r, rows, stride=dil), :] for r in range(dil)], axis=0)
                 for c in range(n_col)], axis=1)
        u = (xp * scale + shift).astype(BF16)
        res = jnp.dot(u, w_ref[:, g * gw:(g + 1) * gw], preferred_element_type=F32).astype(BF16)
        for r in range(dil):
            o_ref[0, r] = res[r * rows:(r + 1) * rows]


def _qkv_proj(x, mod, w, *, tm=1024):
    bsz, t, d = x.shape
    gw = 3 * DIL_WIDTH
    out_specs, out_shape = [], []
    for _, dil in DIL_PATTERNS:
        out_specs.append(pl.BlockSpec((1, dil, tm // dil, gw), lambda b, i: (b, 0, i, 0)))
        out_shape.append(jax.ShapeDtypeStruct((bsz, dil, t // dil, gw), BF16))
    return pl.pallas_call(
        functools.partial(_qkv_proj_kernel, tm=tm),
        grid=(bsz, t // tm),
        in_specs=[pl.BlockSpec((1, tm, d), lambda b, i: (b, i, 0)),
                  pl.BlockSpec((1, 6, d), lambda b, i: (b, 0, 0)),
                  pl.BlockSpec(w.shape, lambda b, i: (0, 0), pipeline_mode=pl.Buffered(1))],
        out_specs=out_specs,
        out_shape=out_shape,
        scratch_shapes=[pltpu.VMEM((d // LANES, tm, LANES), F32)],
        compiler_params=_cparams(2),
        name="qkv_proj",
    )(x, mod, w)


def _dilated_attention_group(qkv, bias, g, *, rb=1024):
    bsz, dil, length, width = qkv.shape
    t = dil * length
    nb = length // BLOCK
    n_blk = rb // BLOCK
    assert nb % n_blk == 0 or n_blk % nb == 0
    flat = qkv.reshape(bsz, t, width)

    def cur(j):
        return lambda b, s: (b, s, j)

    def prev(j):
        return lambda b, s: (b, jnp.maximum(s * n_blk - 1, 0), j)

    blk = (None, rb, DIL_WIDTH)
    pblk = (None, BLOCK, DIL_WIDTH)
    o, lse = pl.pallas_call(
        functools.partial(_attn_kernel, nb=nb, rb=rb),
        grid=(bsz, t // rb),
        in_specs=[pl.BlockSpec(blk, cur(0)),
                  pl.BlockSpec(pblk, prev(1)), pl.BlockSpec(blk, cur(1)),
                  pl.BlockSpec(pblk, prev(2)), pl.BlockSpec(blk, cur(2)),
                  pl.BlockSpec((DIL_HEADS, BLOCK, 2 * BLOCK), lambda b, s: (0, 0, 0))],
        out_specs=[pl.BlockSpec(blk, lambda b, s: (b, s, 0)),
                   pl.BlockSpec((None, rb, LANES), lambda b, s: (b, s, 0))],
        out_shape=[jax.ShapeDtypeStruct((bsz, t, DIL_WIDTH), BF16),
                   jax.ShapeDtypeStruct((bsz, t, LANES), F32)],
        compiler_params=_cparams(2),
        name=f"dilated_attn_g{g}",
    )(flat, flat, flat, flat, flat, bias)
    return o.reshape(bsz, dil, length, DIL_WIDTH), lse.reshape(bsz, dil, length, LANES)


def _t5_bucket(dist):
    exact = N_BUCKETS // 2
    logd = jnp.log(jnp.maximum(dist, 1).astype(F32) / exact) / math.log(MAX_DISTANCE / exact)
    large = jnp.minimum(exact + (logd * (N_BUCKETS - exact)).astype(jnp.int32), N_BUCKETS - 1)
    return jnp.where(dist < exact, dist, large)


def _bias_tables(rel_bias):
    rel = jnp.arange(BLOCK + 1)
    tables = []
    for g, (window, dil) in enumerate(DIL_PATTERNS):
        span = window // dil
        bucket = _t5_bucket(jnp.clip(rel, 0, span) * dil)
        vec = rel_bias[bucket][:, g * DIL_HEADS:(g + 1) * DIL_HEADS].T.astype(F32)
        ext = jnp.concatenate([vec[:, ::-1], jnp.zeros((DIL_HEADS, BLOCK), F32)], axis=1)
        flat = jnp.tile(ext, (1, BLOCK))[:, :BLOCK * 2 * BLOCK]
        tables.append(flat.reshape(DIL_HEADS, BLOCK, 2 * BLOCK))
    qi = np.arange(BLOCK)[:, None]
    ki = np.arange(2 * BLOCK)[None, :]
    band = (ki >= qi) & (ki <= qi + BLOCK)
    return jnp.where(band, jnp.stack(tables), MASKED)


def _tail(mix_fn, wout_ref, x_ref, mod_ref, lng_ref, lnb_ref, w1_ref, w2_ref, o_ref, *, tm, n_sub, ff_chunk):
    m = mod_ref[0]
    sub = tm // n_sub
    tiles = [slice(i * sub, (i + 1) * sub) for i in range(n_sub)]
    x1s, us = [], []
    for rows in tiles:
        y = None
        for piece, k0 in mix_fn(rows):
            part = jnp.dot(piece, wout_ref[k0:k0 + piece.shape[-1], :], preferred_element_type=F32)
            y = part if y is None else y + part
        x1 = _layer_norm_rows(DEEPNORM_ALPHA * x_ref[0, rows, :] + (1.0 + m[2:3]) * y, lng_ref[0:1], lnb_ref[0:1])
        x1s.append(x1)
        us.append((x1 * (1.0 + m[4:5]) + m[3:4]).astype(BF16))
    accs = [None] * n_sub
    for c in range(D_FF // ff_chunk):
        cols = slice(c * ff_chunk, (c + 1) * ff_chunk)
        for i in range(n_sub):
            h = jnp.maximum(jnp.dot(us[i], w1_ref[:, cols], preferred_element_type=F32), 0.0)
            part = jnp.dot((h * h).astype(BF16), w2_ref[cols, :], preferred_element_type=F32)
            accs[i] = part if accs[i] is None else accs[i] + part
    for i, rows in enumerate(tiles):
        z = DEEPNORM_ALPHA * x1s[i] + (1.0 + m[5:6]) * accs[i]
        o_ref[0, rows, :] = _layer_norm_rows(z, lng_ref[1:2], lnb_ref[1:2])


def _even_tail_kernel(ya_ref, yb_ref, wout_ref, x_ref, mod_ref, lng_ref, lnb_ref, w1_ref, w2_ref, o_ref, **kw):
    def mix(rows):
        return [(ya_ref[0, rows, :], 0), (yb_ref[0, rows, :], ya_ref.shape[-1])]

    _tail(mix, wout_ref, x_ref, mod_ref, lng_ref, lnb_ref, w1_ref, w2_ref, o_ref, **kw)


def _odd_tail_kernel(o0_ref, o1_ref, o2_ref, l0_ref, l1_ref, l2_ref, wout_ref, x_ref, mod_ref, lng_ref, lnb_ref,
                     w1_ref, w2_ref, o_ref, o_nat, l_nat, mix_s, *, tm, **kw):
    def natural(src_ref, scr, dil):
        if dil == 1:
            return src_ref[0, 0].astype(F32)
        n_col = src_ref.shape[-1] // LANES
        for c in range(n_col):
            for r in range(dil):
                scr[c, pl.ds(r, tm // dil, stride=dil), :] = src_ref[0, r, :, c * LANES:(c + 1) * LANES].astype(F32)
        return jnp.concatenate([scr[c] for c in range(n_col)], axis=1)

    dils = [dil for _, dil in DIL_PATTERNS]
    lses = [natural(ref, l_nat, dil) for ref, dil in zip((l0_ref, l1_ref, l2_ref), dils)]
    mx = jnp.maximum(jnp.maximum(lses[0], lses[1]), lses[2])
    es = [jnp.exp(l - mx) for l in lses]
    inv = 1.0 / (es[0] + es[1] + es[2])
    hr = lax.broadcasted_iota(jnp.int32, (LANES, DIL_WIDTH), 0)
    hc = lax.broadcasted_iota(jnp.int32, (LANES, DIL_WIDTH), 1) // DIL_HEAD_DIM
    expand = jnp.where(hr == hc, 1.0, 0.0).astype(BF16)

    def widen(wt):
        hi = wt.astype(BF16)
        lo = (wt - hi.astype(F32)).astype(BF16)
        return (jnp.dot(hi, expand, preferred_element_type=F32)
                + jnp.dot(lo, expand, preferred_element_type=F32))

    merged = None
    for e, ref, dil in zip(es, (o0_ref, o1_ref, o2_ref), dils):
        term = widen(e * inv) * natural(ref, o_nat, dil)
        merged = term if merged is None else merged + term
    mix_s[...] = merged.astype(BF16)
    _tail(lambda rows: [(mix_s[rows, :], 0)], wout_ref, x_ref, mod_ref, lng_ref, lnb_ref, w1_ref, w2_ref, o_ref,
          tm=tm, **kw)


def _layer_tail(mix, w_out, x, mod, ln_g, ln_b, w1, w2, layer, *, tm=512, n_sub=2, ff_chunk=1024):
    bsz, t, d = x.shape

    def const(shape):
        return pl.BlockSpec(shape, lambda b, i: (0,) * len(shape), pipeline_mode=pl.Buffered(1))

    def of_layer(shape):
        return pl.BlockSpec((None,) + shape, lambda b, i: (layer,) + (0,) * len(shape), pipeline_mode=pl.Buffered(1))

    tail_specs = [const(w_out.shape),
                  pl.BlockSpec((1, tm, d), lambda b, i: (b, i, 0)),
                  pl.BlockSpec((1, 6, d), lambda b, i: (b, 0, 0)),
                  of_layer((2, d)), of_layer((2, d)), of_layer((d, D_FF)), of_layer((D_FF, d))]
    tail_args = (w_out, x, mod, ln_g, ln_b, w1, w2)
    kw = dict(tm=tm, n_sub=n_sub, ff_chunk=ff_chunk)
    if isinstance(mix, tuple):
        outs, lses = mix
        mix_specs = ([pl.BlockSpec((1, dil, tm // dil, DIL_WIDTH), lambda b, i: (b, 0, i, 0)) for _, dil in DIL_PATTERNS]
                     + [pl.BlockSpec((1, dil, tm // dil, LANES), lambda b, i: (b, 0, i, 0)) for _, dil in DIL_PATTERNS])
        mix_args = (*outs, *lses)
        body = functools.partial(_odd_tail_kernel, **kw)
        scratch = [pltpu.VMEM((DIL_WIDTH // LANES, tm, LANES), F32), pltpu.VMEM((1, tm, LANES), F32),
                   pltpu.VMEM((tm, DIL_WIDTH), BF16)]
    else:
        mix_specs = [pl.BlockSpec((1, tm, piece.shape[-1]), lambda b, i: (b, i, 0)) for piece in mix]
        mix_args = tuple(mix)
        body = functools.partial(_even_tail_kernel, **kw)
        scratch = []
    return pl.pallas_call(
        body,
        grid=(bsz, t // tm),
        in_specs=mix_specs + tail_specs,
        out_specs=pl.BlockSpec((1, tm, d), lambda b, i: (b, i, 0)),
        out_shape=jax.ShapeDtypeStruct((bsz, t, d), F32),
        scratch_shapes=scratch,
        compiler_params=_cparams(2),
        name="layer_tail",
    )(*mix_args, *tail_args)


def _prep_even_weights(w_in, mu, w_up, a_up, g_up):
    w3 = 3 * RW_WIDTH
    pad = LR_PAD - (RW_PROJ - w3)
    d = w_in.shape[0]
    w_perm = jnp.concatenate([w_in[:, :w3], w_in[:, RW_PROJ:], w_in[:, w3:RW_PROJ],
                              jnp.zeros((d, pad), w_in.dtype)], axis=1).astype(BF16)
    mu_pad = jnp.concatenate([mu, jnp.zeros((pad,), mu.dtype)]).reshape(1, w3 + LR_PAD)
    zeros = jnp.zeros((RW_DECAY_RANK, RW_WIDTH), w_up.dtype)
    wwa = jnp.concatenate([jnp.concatenate([w_up, zeros], axis=1),
                           jnp.concatenate([zeros, a_up], axis=1)], axis=0).astype(BF16)
    gup = jnp.concatenate([g_up, jnp.zeros((pad, RW_WIDTH), g_up.dtype)], axis=0).astype(BF16)
    return w_perm, mu_pad, wwa, gup


def kernel(x, c, ada_w, ada_b, ln_g, ln_b, ab_w_in, rw_mu, rw_w0, rw_w_up, rw_a0, rw_a_up, rw_g_up,
           rw_k_k, rw_k_a, rw_r_k, rw_lnx_g, rw_lnx_b, sc_conv_w, ab_w_out, dil_w_qkv, dil_w_out,
           rel_bias, mlp_w1, mlp_w2):
    bsz, t, d = x.shape
    mods = _adaln(c, ada_w, ada_b).reshape(DEPTH, bsz, 6, d)
    bias = None
    row = lambda a: a.reshape(1, -1)
    w1_all, w2_all = mlp_w1.astype(BF16), mlp_w2.astype(BF16)
    for i in range(DEPTH):
        mod = mods[i]
        j = i // 2
        if i % 2 == 0:
            w_perm, mu_pad, wwa, gup = _prep_even_weights(
                ab_w_in[j], rw_mu[j], rw_w_up[j], rw_a_up[j], rw_g_up[j])
            rkv, lr, yb = _ab_proj(x, mod, w_perm, mu_pad, sc_conv_w[j])
            ya = _rwkv(rkv, lr, row(rw_w0[j]), row(rw_a0[j]), wwa, gup,
                       row(rw_k_k[j]), row(rw_k_a[j]), row(rw_r_k[j]),
                       row(rw_lnx_g[j]), row(rw_lnx_b[j]))
            mix, w_out = [ya, yb], ab_w_out[j]
        else:
            if bias is None:
                bias = _bias_tables(rel_bias)
            qkvs = _qkv_proj(x, mod, dil_w_qkv[j].astype(BF16))
            outs, lses = [], []
            for g in range(N_GROUPS):
                o, lse = _dilated_attention_group(qkvs[g], bias[g], g)
                outs.append(o)
                lses.append(lse)
            mix, w_out = (outs, lses), dil_w_out[j]
        x = _layer_tail(mix, w_out.astype(BF16), x, mod, ln_g, ln_b, w1_all, w2_all, i)
    return x
```

```python
import functools
import math

import jax
import jax.numpy as jnp
import numpy as np
from jax import lax
from jax.experimental import pallas as pl
from jax.experimental.pallas import tpu as pltpu

F32 = jnp.float32
BF16 = jnp.bfloat16

D_MODEL = 1024
DEPTH = 2
RW_HEADS = 8
RW_HEAD_DIM = 64
RW_WIDTH = 512
RW_DECAY_RANK = 64
RW_ICLR_RANK = 64
RW_GATE_RANK = 160
RW_GN_EPS = 64e-5
RW_PROJ = 3 * RW_WIDTH + RW_DECAY_RANK + RW_ICLR_RANK + RW_GATE_RANK
SC_WIDTH = 512
SC_CONV = 3
DIL_PATTERNS = ((128, 1), (512, 4), (2048, 16))
N_GROUPS = 3
DIL_HEADS = 8
DIL_HEAD_DIM = 64
DIL_WIDTH = 512
BLOCK = 128
N_BUCKETS = 32
MAX_DISTANCE = 2048
D_FF = 4 * D_MODEL
DEEPNORM_ALPHA = (2 * DEPTH) ** 0.25
LN_EPS = 1e-5

LANES = 128
MXU_DIM = 256
CHUNK = 64
LR_PAD = 384
VMEM_LIMIT = 56 * 1024 * 1024
MASKED = -1e30


def _cparams(n_axes):
    return pltpu.CompilerParams(dimension_semantics=("arbitrary",) * n_axes,
                                vmem_limit_bytes=VMEM_LIMIT)


def _sigmoid(x):
    return 1.0 / (1.0 + jnp.exp(-x))


def _dot(a, b):
    return jnp.dot(a.astype(BF16), b.astype(BF16), preferred_element_type=F32)


def _layer_norm_rows(z, g, b):
    mu = jnp.mean(z, axis=-1, keepdims=True)
    zc = z - mu
    var = jnp.mean(zc * zc, axis=-1, keepdims=True)
    return zc * lax.rsqrt(var + LN_EPS) * g + b


def _adaln_kernel(c_ref, w_ref, b_ref, o_ref):
    c = c_ref[...]
    cond = c * _sigmoid(c)
    o_ref[0] = _dot(cond, w_ref[0]) + b_ref[0]


def _adaln(c, ada_w, ada_b):
    depth, d, n = ada_w.shape
    bsz = c.shape[0]
    tn = 1536
    return pl.pallas_call(
        _adaln_kernel,
        grid=(depth, n // tn),
        in_specs=[pl.BlockSpec((bsz, d), lambda i, j: (0, 0)),
                  pl.BlockSpec((1, d, tn), lambda i, j: (i, 0, j)),
                  pl.BlockSpec((1, 1, tn), lambda i, j: (i, 0, j))],
        out_specs=pl.BlockSpec((1, bsz, tn), lambda i, j: (i, 0, j)),
        out_shape=jax.ShapeDtypeStruct((depth, bsz, n), F32),
        compiler_params=_cparams(2),
        name="adaln",
    )(c, ada_w, ada_b.reshape(depth, 1, n))


def _shift_rows(x, carry, k):
    rolled = pltpu.roll(x, k, 0)
    head = rolled[:8]
    row = lax.broadcasted_iota(jnp.int32, head.shape, 0)
    for i in range(k):
        head = jnp.where(row == i, carry[8 - k + i:8 - k + i + 1], head)
    return jnp.concatenate([head, rolled[8:]], axis=0)


def _ab_proj_kernel(x_ref, mod_ref, w_ref, mu_ref, convw_ref, rkv_ref, lr_ref, yb_ref, carry_p, carry_z, *, tm):
    @pl.when(pl.program_id(1) == 0)
    def _():
        carry_p[...] = jnp.zeros_like(carry_p)
        carry_z[...] = jnp.zeros_like(carry_z)

    m = mod_ref[0]
    u = (x_ref[0] * (1.0 + m[1:2]) + m[0:1]).astype(BF16)

    def proj(c0, width):
        return jnp.dot(u, w_ref[:, c0:c0 + width], preferred_element_type=F32)

    def mixed(c0, m0, width):
        p = proj(c0, width)
        prev = _shift_rows(p, carry_p[:, m0:m0 + width], 1)
        carry_p[:, m0:m0 + width] = p[tm - 8:]
        return (p + mu_ref[:, m0:m0 + width] * (prev - p)).astype(BF16)

    w3 = 3 * RW_WIDTH
    for c in range(3):
        rkv_ref[0, :, c * RW_WIDTH:(c + 1) * RW_WIDTH] = mixed(c * RW_WIDTH, c * RW_WIDTH, RW_WIDTH)
    lr_ref[0] = mixed(2 * w3, w3, LR_PAD)

    h = proj(w3, SC_WIDTH)
    z = proj(w3 + 2 * SC_WIDTH, SC_WIDTH) * h
    zc = carry_z[...]
    cw = convw_ref[...]
    conv = cw[2:3] * z + cw[1:2] * _shift_rows(z, zc, 1) + cw[0:1] * _shift_rows(z, zc, 2)
    carry_z[...] = z[tm - 8:]
    yb_ref[0] = (proj(w3 + SC_WIDTH, SC_WIDTH) * conv).astype(BF16)


def _ab_proj(x, mod, w, mu, conv_w, *, tm=1024):
    bsz, t, d = x.shape
    w3 = 3 * RW_WIDTH

    def const(shape):
        return pl.BlockSpec(shape, lambda b, i: (0,) * len(shape), pipeline_mode=pl.Buffered(1))

    def rows(width):
        return pl.BlockSpec((1, tm, width), lambda b, i: (b, i, 0))

    return pl.pallas_call(
        functools.partial(_ab_proj_kernel, tm=tm),
        grid=(bsz, t // tm),
        in_specs=[rows(d), pl.BlockSpec((1, 6, d), lambda b, i: (b, 0, 0)),
                  const(w.shape), const((1, w3 + LR_PAD)), const((SC_CONV, SC_WIDTH))],
        out_specs=[rows(w3), rows(LR_PAD), rows(SC_WIDTH)],
        out_shape=[jax.ShapeDtypeStruct((bsz, t, w3), BF16), jax.ShapeDtypeStruct((bsz, t, LR_PAD), BF16),
                   jax.ShapeDtypeStruct((bsz, t, SC_WIDTH), BF16)],
        scratch_shapes=[pltpu.VMEM((8, w3 + LR_PAD), F32), pltpu.VMEM((8, SC_WIDTH), F32)],
        compiler_params=_cparams(2),
        name="ab_proj",
    )(x, mod, w, mu, conv_w)


def _rwkv_chunk_terms(a_t, r_t, b_t, k_t, b_e, k_e, v, p_end):
    n = len(a_t)
    idx = range(n)
    row = lax.broadcasted_iota(jnp.int32, (CHUNK, LANES), 0)
    col = lax.broadcasted_iota(jnp.int32, (CHUNK, LANES), 1)
    s_idx = col % CHUNK
    head0 = col < RW_HEAD_DIM
    strict = s_idx < row
    incl = s_idx <= row
    near = strict & ((s_idx // 16) == (row // 16))
    far = strict & ((s_idx // 16) != (row // 16))
    eye = jnp.where(s_idx == row, 1.0, 0.0)
    r2 = lax.broadcasted_iota(jnp.int32, (LANES, 2 * LANES), 0)
    c2 = lax.broadcasted_iota(jnp.int32, (LANES, 2 * LANES), 1)
    bdmask2 = (r2 // RW_HEAD_DIM) == ((c2 % LANES) // RW_HEAD_DIM)
    eye_bd = jnp.where(lax.broadcasted_iota(jnp.int32, (LANES, LANES), 0)
                       == lax.broadcasted_iota(jnp.int32, (LANES, LANES), 1), 1.0, 0.0)
    zero16 = jnp.zeros((CHUNK, LANES), BF16)
    lane1 = lax.broadcasted_iota(jnp.int32, (1, LANES), 1)
    m0_16 = jnp.where(lane1 < RW_HEAD_DIM, 1.0, 0.0).astype(BF16)
    m1_16 = jnp.where(lane1 < RW_HEAD_DIM, 0.0, 1.0).astype(BF16)

    def cast(xs):
        return [x.astype(BF16) for x in xs]

    def bd(q16):
        return jnp.concatenate([q16 * m0_16, q16 * m1_16], axis=0)

    def mm(p16, q16):
        return jnp.dot(p16, q16, preferred_element_type=F32)

    def mm_nt(p16, q16):
        return lax.dot_general(p16, q16, (((1,), (1,)), ((), ())), preferred_element_type=F32)

    a16, r16, v16 = cast(a_t), cast(r_t), cast(v)
    lhs = [jnp.concatenate([a16[i], r16[i]], axis=0) for i in idx]
    b16, k16 = cast(b_t), cast(k_t)
    bdv = [bd(x) for x in v16]
    g = [mm_nt(lhs[i], jnp.concatenate([bd(b16[i]), bd(k16[i])], axis=0)) for i in idx]
    x1 = [jnp.where(near, g[i][:CHUNK, :LANES], 0.0) for i in idx]
    x1_16 = cast(x1)
    l_e16 = [jnp.where(far, g[i][:CHUNK, :LANES], 0.0).astype(BF16) for i in idx]
    a_k16 = [jnp.concatenate([jnp.where(strict, g[i][:CHUNK, LANES:], 0.0),
                              jnp.where(incl, g[i][CHUNK:, LANES:], 0.0)], axis=0).astype(BF16) for i in idx]
    a_rb16 = [jnp.where(incl, g[i][CHUNK:, :LANES], 0.0).astype(BF16) for i in idx]

    def both(p16, q16):
        return jnp.concatenate([bd(p16), bd(q16)], axis=1)

    x2_16 = cast([mm(x1_16[i], bd(x1_16[i])) for i in idx])
    acc = [eye + x1[i] for i in idx]
    sq = [mm(x2_16[i], both(x2_16[i], acc[i].astype(BF16))) for i in idx]
    x4_16 = [sq[i][:, :LANES].astype(BF16) for i in idx]
    acc = [acc[i] + sq[i][:, LANES:] for i in idx]
    sq = [mm(x4_16[i], both(x4_16[i], acc[i].astype(BF16))) for i in idx]
    x8_16 = [sq[i][:, :LANES].astype(BF16) for i in idx]
    acc = [acc[i] + sq[i][:, LANES:] for i in idx]
    dinv = [acc[i] + mm(x8_16[i], bd(acc[i].astype(BF16))) for i in idx]
    dinv16 = cast(dinv)
    f16 = cast([mm(dinv16[i], bd(l_e16[i])) for i in idx])
    sq = [mm(f16[i], both(f16[i], dinv16[i])) for i in idx]
    f2_16 = [sq[i][:, :LANES].astype(BF16) for i in idx]
    hmat = [dinv[i] + sq[i][:, LANES:] for i in idx]
    tinv16 = cast([hmat[i] + mm(f2_16[i], bd(hmat[i].astype(BF16))) for i in idx])

    kv = [mm(a_k16[i], bdv[i]) for i in idx]
    wv16 = [kv[i][:CHUNK].astype(BF16) for i in idx]
    aw = [mm(tinv16[i], both(a16[i], wv16[i])) for i in idx]
    aw16 = cast(aw)
    qy = [mm(a_rb16[i], both(aw16[i][:, :LANES], aw16[i][:, LANES:]))
          + jnp.concatenate([r_t[i], kv[i][CHUNK:]], axis=1) for i in idx]
    be_ke_t = [jnp.concatenate([b_e[i], k_e[i]], axis=0).T.astype(BF16) for i in idx]
    rhs = [jnp.concatenate([aw16[i], jnp.concatenate([zero16, v16[i]], axis=1)], axis=0) for i in idx]
    mn = [jnp.where(bdmask2, mm(be_ke_t[i], rhs[i]), 0.0) for i in idx]
    qhat = [x[:, :LANES] for x in qy]
    yhat = [x[:, LANES:] for x in qy]
    mmat = [eye_bd * p_end[i] + mn[i][:, :LANES] for i in idx]
    nmat = [x[:, LANES:] for x in mn]
    return qhat, yhat, mmat, nmat


def _rwkv_kernel(rkv_ref, lr_ref, w0_ref, a0_ref, wwa_ref, gup_ref,
                 kk_ref, ka_ref, rk_ref, lnxg_ref, lnxb_ref, o_ref, state, yout_s, *, tb):
    n_chunks = tb // CHUNK
    n_pairs = RW_WIDTH // LANES

    @pl.when(pl.program_id(1) == 0)
    def _():
        state[...] = jnp.zeros_like(state)

    r = rkv_ref[0, :, :RW_WIDTH].astype(F32)
    k = rkv_ref[0, :, RW_WIDTH:2 * RW_WIDTH].astype(F32)
    v = rkv_ref[0, :, 2 * RW_WIDTH:].astype(F32)
    lr_m = lr_ref[0].astype(F32)

    wa = lr_m[:, :LANES]
    lane = lax.broadcasted_iota(jnp.int32, wa.shape, 1)
    wa = jnp.where(lane < RW_DECAY_RANK, jnp.tanh(wa), wa)
    wa_up = _dot(wa, wwa_ref[...])
    ld = -math.exp(-0.5) * _sigmoid(w0_ref[...] + wa_up[:, :RW_WIDTH])
    iclr = _sigmoid(a0_ref[...] + wa_up[:, RW_WIDTH:])
    gate = _dot(_sigmoid(lr_m[:, LANES:]), gup_ref[...])

    hr = lax.broadcasted_iota(jnp.int32, (MXU_DIM, MXU_DIM), 0) // RW_HEAD_DIM
    hc = lax.broadcasted_iota(jnp.int32, (MXU_DIM, MXU_DIM), 1) // RW_HEAD_DIM
    seg = jnp.where(hr == hc, 1.0 / RW_HEAD_DIM, 0.0).astype(BF16)

    def split3(x):
        hi = x.astype(BF16)
        rem = x - hi.astype(F32)
        mid = rem.astype(BF16)
        return hi, mid, (rem - mid.astype(F32)).astype(BF16)

    def seg_mean(x, terms=1):
        parts = split3(x)[:terms]
        cols = []
        for c in range(0, RW_WIDTH, MXU_DIM):
            cols.append(sum(jnp.dot(part[:, c:c + MXU_DIM], seg, preferred_element_type=F32) for part in parts))
        return jnp.concatenate(cols, axis=1)

    kk = k * kk_ref[...]
    ss = seg_mean(kk * kk) * RW_HEAD_DIM
    kk = kk * lax.rsqrt(jnp.maximum(ss, 1e-24))
    kh = k * (1.0 + (iclr - 1.0) * ka_ref[...])

    tr = lax.broadcasted_iota(jnp.int32, (MXU_DIM, MXU_DIM), 0)
    tc = lax.broadcasted_iota(jnp.int32, (MXU_DIM, MXU_DIM), 1)
    tri = jnp.where((tc <= tr) & (tc // CHUNK == tr // CHUNK), 1.0, 0.0).astype(BF16)
    ld_parts = split3(ld)[:2]
    lp = jnp.concatenate(
        [sum(jnp.dot(tri, part[t0:t0 + MXU_DIM], preferred_element_type=F32) for part in ld_parts)
         for t0 in range(0, tb, MXU_DIM)], axis=0)

    a_t, r_t, b_t, k_t, b_e, k_e, vs, p_end = [], [], [], [], [], [], [], []
    b = kk * iclr
    for c in range(n_chunks):
        rs = slice(c * CHUNK, (c + 1) * CHUNK)
        lp_c = lp[rs]
        lp_last = lp_c[CHUNK - 1:CHUNK]
        e_in = jnp.exp(lp_c)
        e_inv = jnp.exp(-lp_c)
        e_end = jnp.exp(lp_last - lp_c)
        a_c = -kk[rs] * jnp.exp(lp_c - ld[rs])
        r_c = r[rs] * e_in
        b_c = b[rs] * e_inv
        k_c = kh[rs] * e_inv
        be_c = b[rs] * e_end
        ke_c = kh[rs] * e_end
        pe_c = jnp.exp(lp_last)
        for p in range(n_pairs):
            cs = slice(p * LANES, (p + 1) * LANES)
            a_t.append(a_c[:, cs])
            r_t.append(r_c[:, cs])
            b_t.append(b_c[:, cs])
            k_t.append(k_c[:, cs])
            b_e.append(be_c[:, cs])
            k_e.append(ke_c[:, cs])
            vs.append(v[rs, cs])
            p_end.append(pe_c[:, cs])

    qhat, yhat, mmat, nmat = _rwkv_chunk_terms(a_t, r_t, b_t, k_t, b_e, k_e, vs, p_end)

    s = [state[p] for p in range(n_pairs)]
    for c in range(n_chunks):
        ym = [_dot(jnp.concatenate([qhat[c * n_pairs + p], mmat[c * n_pairs + p]], axis=0), s[p])
              for p in range(n_pairs)]
        for p in range(n_pairs):
            i = c * n_pairs + p
            yout_s[c * CHUNK:(c + 1) * CHUNK, p * LANES:(p + 1) * LANES] = ym[p][:CHUNK] + yhat[i]
            s[p] = ym[p][CHUNK:] + nmat[i]
    for p in range(n_pairs):
        state[p] = s[p]

    y = yout_s[...]
    mean = seg_mean(y)
    yc = y - mean
    var = seg_mean(yc * yc)
    y = yc * lax.rsqrt(var + RW_GN_EPS) * lnxg_ref[...] + lnxb_ref[...]
    bonus = seg_mean(r * kh * rk_ref[...]) * RW_HEAD_DIM * v
    o_ref[0] = ((y + bonus) * gate).astype(o_ref.dtype)


def _rwkv(rkv, lr, w0, a0, wwa, gup, k_k, k_a, r_k, lnx_g, lnx_b, *, tb=512):
    bsz, t, w3 = rkv.shape
    n_pairs = RW_WIDTH // LANES

    def full(shape):
        return pl.BlockSpec(shape, lambda b, i: (0,) * len(shape))

    return pl.pallas_call(
        functools.partial(_rwkv_kernel, tb=tb),
        grid=(bsz, t // tb),
        in_specs=[pl.BlockSpec((1, tb, w3), lambda b, i: (b, i, 0)),
                  pl.BlockSpec((1, tb, LR_PAD), lambda b, i: (b, i, 0)),
                  full((1, RW_WIDTH)), full((1, RW_WIDTH)),
                  full((LANES, 2 * RW_WIDTH)), full((LR_PAD - LANES, RW_WIDTH)),
                  full((1, RW_WIDTH)), full((1, RW_WIDTH)), full((1, RW_WIDTH)),
                  full((1, RW_WIDTH)), full((1, RW_WIDTH))],
        out_specs=pl.BlockSpec((1, tb, RW_WIDTH), lambda b, i: (b, i, 0)),
        out_shape=jax.ShapeDtypeStruct((bsz, t, RW_WIDTH), BF16),
        scratch_shapes=[pltpu.VMEM((n_pairs, LANES, LANES), F32),
                        pltpu.VMEM((tb, RW_WIDTH), F32)],
        compiler_params=_cparams(2),
        name="rwkv",
    )(rkv, lr, w0, a0, wwa, gup, k_k, k_a, r_k, lnx_g, lnx_b)


def _attn_kernel(q_ref, kp_ref, kc_ref, vp_ref, vc_ref, bias_ref, o_ref, lse_ref, *, nb, rb):
    step = pl.program_id(1)
    n_blk = rb // BLOCK
    scale = DIL_HEAD_DIM ** -0.5
    n_pairs = DIL_WIDTH // LANES
    krow = lax.broadcasted_iota(jnp.int32, (2 * BLOCK, LANES), 0) // BLOCK
    klane = lax.broadcasted_iota(jnp.int32, (2 * BLOCK, LANES), 1) // DIL_HEAD_DIM
    own = krow == klane
    zeros = jnp.zeros((2 * BLOCK, LANES), BF16)
    ones_bd = jnp.where(own, 1.0, 0.0).astype(BF16)
    lane_o = lax.broadcasted_iota(jnp.int32, (BLOCK, LANES), 1)

    def stack_bd(x):
        return jnp.where(own, jnp.concatenate([x, x], axis=0), zeros)

    def attend(blocks):
        units = [(rows, q, parts, p) for rows, q, parts in blocks for p in range(n_pairs)]
        lanes = [slice(p * LANES, (p + 1) * LANES) for p in range(n_pairs)]
        scores = []
        for _, q, parts, p in units:
            qs = q[:, lanes[p]] * scale
            row = []
            for k_blk, _, col0, pen in parts:
                s = lax.dot_general(qs, stack_bd(k_blk[:, lanes[p]]), (((1,), (1,)), ((), ())),
                                    preferred_element_type=F32)
                bias = jnp.concatenate([bias_ref[2 * p, :, col0:col0 + BLOCK],
                                        bias_ref[2 * p + 1, :, col0:col0 + BLOCK]], axis=1)
                s = s + bias
                row.append(s if pen is None else s + pen)
            scores.append(row)
        ms = []
        for u in range(len(units)):
            halves = []
            for hh in range(2):
                cols = slice(hh * BLOCK, (hh + 1) * BLOCK)
                m = jnp.max(scores[u][0][:, cols], axis=-1, keepdims=True)
                for s in scores[u][1:]:
                    m = jnp.maximum(m, jnp.max(s[:, cols], axis=-1, keepdims=True))
                halves.append(m)
            ms.append(halves)
        es = [[jnp.exp((s - jnp.concatenate([jnp.broadcast_to(ms[u][0], (BLOCK, BLOCK)),
                                             jnp.broadcast_to(ms[u][1], (BLOCK, BLOCK))], axis=1)).astype(BF16))
               for s in scores[u]] for u in range(len(units))]
        res = []
        for u, (_, _, parts, p) in enumerate(units):
            res.append(sum(jnp.dot(e, jnp.concatenate([stack_bd(part[1][:, lanes[p]]), ones_bd], axis=1),
                                   preferred_element_type=F32) for e, part in zip(es[u], parts)))
        for b, (rows, _, _) in enumerate(blocks):
            m_all = jnp.zeros((BLOCK, LANES), F32)
            den_all = jnp.ones((BLOCK, LANES), F32)
            for p in range(n_pairs):
                u = b * n_pairs + p
                num, den = res[u][:, :LANES], res[u][:, LANES:]
                o_ref[rows, lanes[p]] = (num / den).astype(o_ref.dtype)
                m_all = jnp.where(lane_o == 2 * p, ms[u][0], jnp.where(lane_o == 2 * p + 1, ms[u][1], m_all))
                den_all = jnp.where((lane_o == 2 * p), den,
                                    jnp.where(lane_o == 2 * p + 1, pltpu.roll(den, DIL_HEAD_DIM, 1), den_all))
            lse_ref[rows, :] = m_all + jnp.log(den_all)

    blocks = []
    for j in range(n_blk):
        rows = slice(j * BLOCK, (j + 1) * BLOCK)
        cur = (kc_ref[rows, :], vc_ref[rows, :], BLOCK, None)
        if n_blk % nb == 0:
            prev = "none" if j % nb == 0 else "inside"
        else:
            prev = "inside" if j > 0 else ("none" if nb == n_blk else "before")
        if prev == "none":
            parts = [cur]
        elif prev == "inside":
            before = slice((j - 1) * BLOCK, j * BLOCK)
            parts = [(kc_ref[before, :], vc_ref[before, :], 0, None), cur]
        else:
            has_prev = (step % (nb // n_blk)) != 0
            parts = [(kp_ref[...], vp_ref[...], 0, jnp.where(has_prev, 0.0, MASKED)), cur]
        blocks.append((rows, q_ref[rows, :], parts))
    attend(blocks)


def _qkv_proj_kernel(x_ref, mod_ref, w_ref, o0_ref, o1_ref, o2_ref, xs_ref, xs4_ref, *, tm):
    m = mod_ref[0]
    scale = 1.0 + m[1:2]
    shift = m[0:1]
    n_col = x_ref.shape[-1] // LANES
    gw = 3 * DIL_WIDTH
    step = DIL_PATTERNS[1][1]
    assert [dil for _, dil in DIL_PATTERNS] == [1, step, step * step]

    for c in range(n_col):
        xs_ref[c] = x_ref[0, :, c * LANES:(c + 1) * LANES]
    rows4 = tm // step
    x4 = [jnp.concatenate([xs_ref[c, pl.ds(a, rows4, stride=step), :] for a in range(step)], axis=0)
          for c in range(n_col)]
    for c in range(n_col):
        xs4_ref[c] = x4[c]
    rows16 = rows4 // step
    x16 = [jnp.concatenate([xs4_ref[c, pl.ds(a * rows4 + b, rows16, stride=step), :]
                            for a in range(step) for b in range(step)], axis=0) for c in range(n_col)]
    inputs = [x_ref[0], jnp.concatenate(x4, axis=1), jnp.concatenate(x16, axis=1)]
    residue_of_group = [[0], list(range(step)), [a + step * b for a in range(step) for b in range(step)]]

    for g, o_ref in enumerate((o0_ref, o1_ref, o2_ref)):
        u = (inputs[g] * scale + shift).astype(BF16)
        res = jnp.dot(u, w_ref[:, g * gw:(g + 1) * gw], preferred_element_type=F32).astype(BF16)
        rows = tm // len(residue_of_group[g])
        for grp, r in enumerate(residue_of_group[g]):
            o_ref[0, r] = res[grp * rows:(grp + 1) * rows]


def _qkv_proj(x, mod, w, *, tm=1024):
    bsz, t, d = x.shape
    gw = 3 * DIL_WIDTH
    out_specs, out_shape = [], []
    for _, dil in DIL_PATTERNS:
        out_specs.append(pl.BlockSpec((1, dil, tm // dil, gw), lambda b, i: (b, 0, i, 0)))
        out_shape.append(jax.ShapeDtypeStruct((bsz, dil, t // dil, gw), BF16))
    return pl.pallas_call(
        functools.partial(_qkv_proj_kernel, tm=tm),
        grid=(bsz, t // tm),
        in_specs=[pl.BlockSpec((1, tm, d), lambda b, i: (b, i, 0)),
                  pl.BlockSpec((1, 6, d), lambda b, i: (b, 0, 0)),
                  pl.BlockSpec(w.shape, lambda b, i: (0, 0), pipeline_mode=pl.Buffered(1))],
        out_specs=out_specs,
        out_shape=out_shape,
        scratch_shapes=[pltpu.VMEM((d // LANES, tm, LANES), F32), pltpu.VMEM((d // LANES, tm, LANES), F32)],
        compiler_params=_cparams(2),
        name="qkv_proj",
    )(x, mod, w)


def _dilated_attention_group(qkv, bias, g, *, rb=1024):
    bsz, dil, length, width = qkv.shape
    t = dil * length
    nb = length // BLOCK
    n_blk = rb // BLOCK
    assert nb % n_blk == 0 or n_blk % nb == 0
    flat = qkv.reshape(bsz, t, width)

    def cur(j):
        return lambda b, s: (b, s, j)

    def prev(j):
        return lambda b, s: (b, jnp.maximum(s * n_blk - 1, 0), j)

    blk = (None, rb, DIL_WIDTH)
    pblk = (None, BLOCK, DIL_WIDTH)
    o, lse = pl.pallas_call(
        functools.partial(_attn_kernel, nb=nb, rb=rb),
        grid=(bsz, t // rb),
        in_specs=[pl.BlockSpec(blk, cur(0)),
                  pl.BlockSpec(pblk, prev(1)), pl.BlockSpec(blk, cur(1)),
                  pl.BlockSpec(pblk, prev(2)), pl.BlockSpec(blk, cur(2)),
                  pl.BlockSpec((DIL_HEADS, BLOCK, 2 * BLOCK), lambda b, s: (0, 0, 0))],
        out_specs=[pl.BlockSpec(blk, lambda b, s: (b, s, 0)),
                   pl.BlockSpec((None, rb, LANES), lambda b, s: (b, s, 0))],
        out_shape=[jax.ShapeDtypeStruct((bsz, t, DIL_WIDTH), BF16),
                   jax.ShapeDtypeStruct((bsz, t, LANES), F32)],
        compiler_params=_cparams(2),
        name=f"dilated_attn_g{g}",
    )(flat, flat, flat, flat, flat, bias)
    return o.reshape(bsz, dil, length, DIL_WIDTH), lse.reshape(bsz, dil, length, LANES)


def _t5_bucket(dist):
    exact = N_BUCKETS // 2
    logd = jnp.log(jnp.maximum(dist, 1).astype(F32) / exact) / math.log(MAX_DISTANCE / exact)
    large = jnp.minimum(exact + (logd * (N_BUCKETS - exact)).astype(jnp.int32), N_BUCKETS - 1)
    return jnp.where(dist < exact, dist, large)


def _bias_tables(rel_bias):
    rel = jnp.arange(BLOCK + 1)
    tables = []
    for g, (window, dil) in enumerate(DIL_PATTERNS):
        span = window // dil
        bucket = _t5_bucket(jnp.clip(rel, 0, span) * dil)
        vec = rel_bias[bucket][:, g * DIL_HEADS:(g + 1) * DIL_HEADS].T.astype(F32)
        ext = jnp.concatenate([vec[:, ::-1], jnp.zeros((DIL_HEADS, BLOCK), F32)], axis=1)
        flat = jnp.tile(ext, (1, BLOCK))[:, :BLOCK * 2 * BLOCK]
        tables.append(flat.reshape(DIL_HEADS, BLOCK, 2 * BLOCK))
    qi = np.arange(BLOCK)[:, None]
    ki = np.arange(2 * BLOCK)[None, :]
    band = (ki >= qi) & (ki <= qi + BLOCK)
    return jnp.where(band, jnp.stack(tables), MASKED)


def _tail(mix_fn, wout_ref, x_ref, mod_ref, lng_ref, lnb_ref, w1_ref, w2_ref, o_ref, *, tm, n_sub, ff_chunk):
    m = mod_ref[0]
    sub = tm // n_sub
    tiles = [slice(i * sub, (i + 1) * sub) for i in range(n_sub)]
    x1s, us = [], []
    for rows in tiles:
        y = None
        for piece, k0 in mix_fn(rows):
            part = jnp.dot(piece, wout_ref[k0:k0 + piece.shape[-1], :], preferred_element_type=F32)
            y = part if y is None else y + part
        x1 = _layer_norm_rows(DEEPNORM_ALPHA * x_ref[0, rows, :] + (1.0 + m[2:3]) * y, lng_ref[0:1], lnb_ref[0:1])
        x1s.append(x1)
        us.append((x1 * (1.0 + m[4:5]) + m[3:4]).astype(BF16))
    accs = [None] * n_sub
    for c in range(D_FF // ff_chunk):
        cols = slice(c * ff_chunk, (c + 1) * ff_chunk)
        for i in range(n_sub):
            h = jnp.maximum(jnp.dot(us[i], w1_ref[:, cols], preferred_element_type=F32), 0.0)
            part = jnp.dot((h * h).astype(BF16), w2_ref[cols, :], preferred_element_type=F32)
            accs[i] = part if accs[i] is None else accs[i] + part
    for i, rows in enumerate(tiles):
        z = DEEPNORM_ALPHA * x1s[i] + (1.0 + m[5:6]) * accs[i]
        o_ref[0, rows, :] = _layer_norm_rows(z, lng_ref[1:2], lnb_ref[1:2])


def _even_tail_kernel(ya_ref, yb_ref, wout_ref, x_ref, mod_ref, lng_ref, lnb_ref, w1_ref, w2_ref, o_ref, **kw):
    def mix(rows):
        return [(ya_ref[0, rows, :], 0), (yb_ref[0, rows, :], ya_ref.shape[-1])]

    _tail(mix, wout_ref, x_ref, mod_ref, lng_ref, lnb_ref, w1_ref, w2_ref, o_ref, **kw)


def _odd_tail_kernel(o0_ref, o1_ref, o2_ref, l0_ref, l1_ref, l2_ref, wout_ref, x_ref, mod_ref, lng_ref, lnb_ref,
                     w1_ref, w2_ref, o_ref, o_nat, l_nat, mix_s, *, tm, **kw):
    def natural(src_ref, scr, dil):
        if dil == 1:
            return src_ref[0, 0].astype(F32)
        n_col = src_ref.shape[-1] // LANES
        for c in range(n_col):
            for r in range(dil):
                scr[c, pl.ds(r, tm // dil, stride=dil), :] = src_ref[0, r, :, c * LANES:(c + 1) * LANES].astype(F32)
        return jnp.concatenate([scr[c] for c in range(n_col)], axis=1)

    dils = [dil for _, dil in DIL_PATTERNS]
    lses = [natural(ref, l_nat, dil) for ref, dil in zip((l0_ref, l1_ref, l2_ref), dils)]
    mx = jnp.maximum(jnp.maximum(lses[0], lses[1]), lses[2])
    es = [jnp.exp(l - mx) for l in lses]
    inv = 1.0 / (es[0] + es[1] + es[2])
    hr = lax.broadcasted_iota(jnp.int32, (LANES, DIL_WIDTH), 0)
    hc = lax.broadcasted_iota(jnp.int32, (LANES, DIL_WIDTH), 1) // DIL_HEAD_DIM
    expand = jnp.where(hr == hc, 1.0, 0.0).astype(BF16)

    def widen(wt):
        return jnp.dot(wt.astype(BF16), expand, preferred_element_type=F32)

    merged = None
    for e, ref, dil in zip(es, (o0_ref, o1_ref, o2_ref), dils):
        term = widen(e * inv) * natural(ref, o_nat, dil)
        merged = term if merged is None else merged + term
    mix_s[...] = merged.astype(BF16)
    _tail(lambda rows: [(mix_s[rows, :], 0)], wout_ref, x_ref, mod_ref, lng_ref, lnb_ref, w1_ref, w2_ref, o_ref,
          tm=tm, **kw)


def _layer_tail(mix, w_out, x, mod, ln_g, ln_b, w1, w2, layer, *, tm=512, n_sub=2, ff_chunk=1024):
    bsz, t, d = x.shape

    def const(shape):
        return pl.BlockSpec(shape, lambda b, i: (0,) * len(shape), pipeline_mode=pl.Buffered(1))

    def of_layer(shape):
        return pl.BlockSpec((None,) + shape, lambda b, i: (layer,) + (0,) * len(shape), pipeline_mode=pl.Buffered(1))

    tail_specs = [const(w_out.shape),
                  pl.BlockSpec((1, tm, d), lambda b, i: (b, i, 0)),
                  pl.BlockSpec((1, 6, d), lambda b, i: (b, 0, 0)),
                  of_layer((2, d)), of_layer((2, d)), of_layer((d, D_FF)), of_layer((D_FF, d))]
    tail_args = (w_out, x, mod, ln_g, ln_b, w1, w2)
    kw = dict(tm=tm, n_sub=n_sub, ff_chunk=ff_chunk)
    if isinstance(mix, tuple):
        outs, lses = mix
        mix_specs = ([pl.BlockSpec((1, dil, tm // dil, DIL_WIDTH), lambda b, i: (b, 0, i, 0)) for _, dil in DIL_PATTERNS]
                     + [pl.BlockSpec((1, dil, tm // dil, LANES), lambda b, i: (b, 0, i, 0)) for _, dil in DIL_PATTERNS])
        mix_args = (*outs, *lses)
        body = functools.partial(_odd_tail_kernel, **kw)
        scratch = [pltpu.VMEM((DIL_WIDTH // LANES, tm, LANES), F32), pltpu.VMEM((1, tm, LANES), F32),
                   pltpu.VMEM((tm, DIL_WIDTH), BF16)]
    else:
        mix_specs = [pl.BlockSpec((1, tm, piece.shape[-1]), lambda b, i: (b, i, 0)) for piece in mix]
        mix_args = tuple(mix)
        body = functools.partial(_even_tail_kernel, **kw)
        scratch = []
    return pl.pallas_call(
        body,
        grid=(bsz, t // tm),
        in_specs=mix_specs + tail_specs,
        out_specs=pl.BlockSpec((1, tm, d), lambda b, i: (b, i, 0)),
        out_shape=jax.ShapeDtypeStruct((bsz, t, d), F32),
        scratch_shapes=scratch,
        compiler_params=_cparams(2),
        name="layer_tail",
    )(*mix_args, *tail_args)


def _prep_even_weights(w_in, mu, w_up, a_up, g_up):
    w3 = 3 * RW_WIDTH
    pad = LR_PAD - (RW_PROJ - w3)
    d = w_in.shape[0]
    w_perm = jnp.concatenate([w_in[:, :w3], w_in[:, RW_PROJ:], w_in[:, w3:RW_PROJ],
                              jnp.zeros((d, pad), w_in.dtype)], axis=1).astype(BF16)
    mu_pad = jnp.concatenate([mu, jnp.zeros((pad,), mu.dtype)]).reshape(1, w3 + LR_PAD)
    zeros = jnp.zeros((RW_DECAY_RANK, RW_WIDTH), w_up.dtype)
    wwa = jnp.concatenate([jnp.concatenate([w_up, zeros], axis=1),
                           jnp.concatenate([zeros, a_up], axis=1)], axis=0).astype(BF16)
    gup = jnp.concatenate([g_up, jnp.zeros((pad, RW_WIDTH), g_up.dtype)], axis=0).astype(BF16)
    return w_perm, mu_pad, wwa, gup


def kernel(x, c, ada_w, ada_b, ln_g, ln_b, ab_w_in, rw_mu, rw_w0, rw_w_up, rw_a0, rw_a_up, rw_g_up,
           rw_k_k, rw_k_a, rw_r_k, rw_lnx_g, rw_lnx_b, sc_conv_w, ab_w_out, dil_w_qkv, dil_w_out,
           rel_bias, mlp_w1, mlp_w2):
    bsz, t, d = x.shape
    mods = _adaln(c, ada_w, ada_b).reshape(DEPTH, bsz, 6, d)
    bias = None
    row = lambda a: a.reshape(1, -1)
    w1_all, w2_all = mlp_w1.astype(BF16), mlp_w2.astype(BF16)
    for i in range(DEPTH):
        mod = mods[i]
        j = i // 2
        if i % 2 == 0:
            w_perm, mu_pad, wwa, gup = _prep_even_weights(
                ab_w_in[j], rw_mu[j], rw_w_up[j], rw_a_up[j], rw_g_up[j])
            rkv, lr, yb = _ab_proj(x, mod, w_perm, mu_pad, sc_conv_w[j])
            ya = _rwkv(rkv, lr, row(rw_w0[j]), row(rw_a0[j]), wwa, gup,
                       row(rw_k_k[j]), row(rw_k_a[j]), row(rw_r_k[j]),
                       row(rw_lnx_g[j]), row(rw_lnx_b[j]))
            mix, w_out = [ya, yb], ab_w_out[j]
        else:
            if bias is None:
                bias = _bias_tables(rel_bias)
            qkvs = _qkv_proj(x, mod, dil_w_qkv[j].astype(BF16))
            outs, lses = [], []
            for g in range(N_GROUPS):
                o, lse = _dilated_attention_group(qkvs[g], bias[g], g)
                outs.append(o)
                lses.append(lse)
            mix, w_out = (outs, lses), dil_w_out[j]
        x = _layer_tail(mix, w_out.astype(BF16), x, mod, ln_g, ln_b, w1_all, w2_all, i)
    return x
```

```python
import functools
import math

import jax
import jax.numpy as jnp
import numpy as np
from jax import lax
from jax.experimental import pallas as pl
from jax.experimental.pallas import tpu as pltpu

F32 = jnp.float32
BF16 = jnp.bfloat16

D_MODEL = 1024
DEPTH = 2
RW_HEADS = 8
RW_HEAD_DIM = 64
RW_WIDTH = 512
RW_DECAY_RANK = 64
RW_ICLR_RANK = 64
RW_GATE_RANK = 160
RW_GN_EPS = 64e-5
RW_PROJ = 3 * RW_WIDTH + RW_DECAY_RANK + RW_ICLR_RANK + RW_GATE_RANK
SC_WIDTH = 512
SC_CONV = 3
DIL_PATTERNS = ((128, 1), (512, 4), (2048, 16))
N_GROUPS = 3
DIL_HEADS = 8
DIL_HEAD_DIM = 64
DIL_WIDTH = 512
BLOCK = 128
N_BUCKETS = 32
MAX_DISTANCE = 2048
D_FF = 4 * D_MODEL
DEEPNORM_ALPHA = (2 * DEPTH) ** 0.25
LN_EPS = 1e-5

LANES = 128
MXU_DIM = 256
CHUNK = 64
LR_PAD = 384
VMEM_LIMIT = 56 * 1024 * 1024
TAIL_VMEM_LIMIT = 60 * 1024 * 1024
MASKED = -1e30


def _cparams(n_axes, vmem_limit=VMEM_LIMIT):
    return pltpu.CompilerParams(dimension_semantics=("arbitrary",) * n_axes,
                                vmem_limit_bytes=vmem_limit)


def _sigmoid(x):
    return 1.0 / (1.0 + jnp.exp(-x))


def _dot(a, b):
    return jnp.dot(a.astype(BF16), b.astype(BF16), preferred_element_type=F32)


def _layer_norm_rows(z, g, b):
    mu = jnp.mean(z, axis=-1, keepdims=True)
    zc = z - mu
    var = jnp.mean(zc * zc, axis=-1, keepdims=True)
    return zc * lax.rsqrt(var + LN_EPS) * g + b


def _adaln_kernel(c_ref, w_ref, b_ref, o_ref):
    c = c_ref[...]
    cond = c * _sigmoid(c)
    o_ref[0] = _dot(cond, w_ref[0]) + b_ref[0]


def _adaln(c, ada_w, ada_b):
    depth, d, n = ada_w.shape
    bsz = c.shape[0]
    tn = 1536
    return pl.pallas_call(
        _adaln_kernel,
        grid=(depth, n // tn),
        in_specs=[pl.BlockSpec((bsz, d), lambda i, j: (0, 0)),
                  pl.BlockSpec((1, d, tn), lambda i, j: (i, 0, j)),
                  pl.BlockSpec((1, 1, tn), lambda i, j: (i, 0, j))],
        out_specs=pl.BlockSpec((1, bsz, tn), lambda i, j: (i, 0, j)),
        out_shape=jax.ShapeDtypeStruct((depth, bsz, n), F32),
        compiler_params=_cparams(2),
        name="adaln",
    )(c, ada_w, ada_b.reshape(depth, 1, n))


def _shift_rows(x, carry, k):
    rolled = pltpu.roll(x, k, 0)
    head = rolled[:8]
    row = lax.broadcasted_iota(jnp.int32, head.shape, 0)
    for i in range(k):
        head = jnp.where(row == i, carry[8 - k + i:8 - k + i + 1], head)
    return jnp.concatenate([head, rolled[8:]], axis=0)


def _ab_proj_kernel(x_ref, mod_ref, w_ref, mu_ref, convw_ref, rkv_ref, lr_ref, yb_ref, carry_p, carry_z, *, tm):
    @pl.when(pl.program_id(1) == 0)
    def _():
        carry_p[...] = jnp.zeros_like(carry_p)
        carry_z[...] = jnp.zeros_like(carry_z)

    m = mod_ref[0]
    u = (x_ref[0] * (1.0 + m[1:2]) + m[0:1]).astype(BF16)

    def proj(c0, width):
        return jnp.dot(u, w_ref[:, c0:c0 + width], preferred_element_type=F32)

    def mixed(c0, m0, width):
        p = proj(c0, width)
        prev = _shift_rows(p, carry_p[:, m0:m0 + width], 1)
        carry_p[:, m0:m0 + width] = p[tm - 8:]
        return (p + mu_ref[:, m0:m0 + width] * (prev - p)).astype(BF16)

    w3 = 3 * RW_WIDTH
    for c in range(3):
        rkv_ref[0, :, c * RW_WIDTH:(c + 1) * RW_WIDTH] = mixed(c * RW_WIDTH, c * RW_WIDTH, RW_WIDTH)
    lr_ref[0] = mixed(2 * w3, w3, LR_PAD)

    h = proj(w3, SC_WIDTH)
    z = proj(w3 + 2 * SC_WIDTH, SC_WIDTH) * h
    zc = carry_z[...]
    cw = convw_ref[...]
    conv = cw[2:3] * z + cw[1:2] * _shift_rows(z, zc, 1) + cw[0:1] * _shift_rows(z, zc, 2)
    carry_z[...] = z[tm - 8:]
    yb_ref[0] = (proj(w3 + SC_WIDTH, SC_WIDTH) * conv).astype(BF16)


def _ab_proj(x, mod, w, mu, conv_w, *, tm=1024):
    bsz, t, d = x.shape
    w3 = 3 * RW_WIDTH

    def const(shape):
        return pl.BlockSpec(shape, lambda b, i: (0,) * len(shape), pipeline_mode=pl.Buffered(1))

    def rows(width):
        return pl.BlockSpec((1, tm, width), lambda b, i: (b, i, 0))

    return pl.pallas_call(
        functools.partial(_ab_proj_kernel, tm=tm),
        grid=(bsz, t // tm),
        in_specs=[rows(d), pl.BlockSpec((1, 6, d), lambda b, i: (b, 0, 0)),
                  const(w.shape), const((1, w3 + LR_PAD)), const((SC_CONV, SC_WIDTH))],
        out_specs=[rows(w3), rows(LR_PAD), rows(SC_WIDTH)],
        out_shape=[jax.ShapeDtypeStruct((bsz, t, w3), BF16), jax.ShapeDtypeStruct((bsz, t, LR_PAD), BF16),
                   jax.ShapeDtypeStruct((bsz, t, SC_WIDTH), BF16)],
        scratch_shapes=[pltpu.VMEM((8, w3 + LR_PAD), F32), pltpu.VMEM((8, SC_WIDTH), F32)],
        compiler_params=_cparams(2),
        name="ab_proj",
    )(x, mod, w, mu, conv_w)


def _rwkv_chunk_terms(a_t, r_t, b_t, k_t, b_e, k_e, v, p_end):
    n = len(a_t)
    idx = range(n)
    row = lax.broadcasted_iota(jnp.int32, (CHUNK, LANES), 0)
    col = lax.broadcasted_iota(jnp.int32, (CHUNK, LANES), 1)
    s_idx = col % CHUNK
    head0 = col < RW_HEAD_DIM
    strict = s_idx < row
    incl = s_idx <= row
    near = strict & ((s_idx // 16) == (row // 16))
    far = strict & ((s_idx // 16) != (row // 16))
    eye = jnp.where(s_idx == row, 1.0, 0.0)
    r2 = lax.broadcasted_iota(jnp.int32, (LANES, 2 * LANES), 0)
    c2 = lax.broadcasted_iota(jnp.int32, (LANES, 2 * LANES), 1)
    bdmask2 = (r2 // RW_HEAD_DIM) == ((c2 % LANES) // RW_HEAD_DIM)
    eye_bd = jnp.where(lax.broadcasted_iota(jnp.int32, (LANES, LANES), 0)
                       == lax.broadcasted_iota(jnp.int32, (LANES, LANES), 1), 1.0, 0.0)
    zero16 = jnp.zeros((CHUNK, LANES), BF16)
    lane1 = lax.broadcasted_iota(jnp.int32, (1, LANES), 1)
    m0_16 = jnp.where(lane1 < RW_HEAD_DIM, 1.0, 0.0).astype(BF16)
    m1_16 = jnp.where(lane1 < RW_HEAD_DIM, 0.0, 1.0).astype(BF16)

    def cast(xs):
        return [x.astype(BF16) for x in xs]

    def bd(q16):
        return jnp.concatenate([q16 * m0_16, q16 * m1_16], axis=0)

    def mm(p16, q16):
        return jnp.dot(p16, q16, preferred_element_type=F32)

    def mm_nt(p16, q16):
        return lax.dot_general(p16, q16, (((1,), (1,)), ((), ())), preferred_element_type=F32)

    a16, r16, v16 = cast(a_t), cast(r_t), cast(v)
    lhs = [jnp.concatenate([a16[i], r16[i]], axis=0) for i in idx]
    b16, k16 = cast(b_t), cast(k_t)
    bdv = [bd(x) for x in v16]
    g = [mm_nt(lhs[i], jnp.concatenate([bd(b16[i]), bd(k16[i])], axis=0)) for i in idx]
    x1 = [jnp.where(near, g[i][:CHUNK, :LANES], 0.0) for i in idx]
    x1_16 = cast(x1)
    l_e16 = [jnp.where(far, g[i][:CHUNK, :LANES], 0.0).astype(BF16) for i in idx]
    a_k16 = [jnp.concatenate([jnp.where(strict, g[i][:CHUNK, LANES:], 0.0),
                              jnp.where(incl, g[i][CHUNK:, LANES:], 0.0)], axis=0).astype(BF16) for i in idx]
    a_rb16 = [jnp.where(incl, g[i][CHUNK:, :LANES], 0.0).astype(BF16) for i in idx]

    def both(p16, q16):
        return jnp.concatenate([bd(p16), bd(q16)], axis=1)

    x2_16 = cast([mm(x1_16[i], bd(x1_16[i])) for i in idx])
    acc = [eye + x1[i] for i in idx]
    sq = [mm(x2_16[i], both(x2_16[i], acc[i].astype(BF16))) for i in idx]
    x4_16 = [sq[i][:, :LANES].astype(BF16) for i in idx]
    acc = [acc[i] + sq[i][:, LANES:] for i in idx]
    sq = [mm(x4_16[i], both(x4_16[i], acc[i].astype(BF16))) for i in idx]
    x8_16 = [sq[i][:, :LANES].astype(BF16) for i in idx]
    acc = [acc[i] + sq[i][:, LANES:] for i in idx]
    dinv = [acc[i] + mm(x8_16[i], bd(acc[i].astype(BF16))) for i in idx]
    dinv16 = cast(dinv)
    f16 = cast([mm(dinv16[i], bd(l_e16[i])) for i in idx])
    sq = [mm(f16[i], both(f16[i], dinv16[i])) for i in idx]
    f2_16 = [sq[i][:, :LANES].astype(BF16) for i in idx]
    hmat = [dinv[i] + sq[i][:, LANES:] for i in idx]
    tinv16 = cast([hmat[i] + mm(f2_16[i], bd(hmat[i].astype(BF16))) for i in idx])

    kv = [mm(a_k16[i], bdv[i]) for i in idx]
    wv16 = [kv[i][:CHUNK].astype(BF16) for i in idx]
    aw = [mm(tinv16[i], both(a16[i], wv16[i])) for i in idx]
    aw16 = cast(aw)
    qy = [mm(a_rb16[i], both(aw16[i][:, :LANES], aw16[i][:, LANES:]))
          + jnp.concatenate([r_t[i], kv[i][CHUNK:]], axis=1) for i in idx]
    be_ke_t = [jnp.concatenate([b_e[i], k_e[i]], axis=0).T.astype(BF16) for i in idx]
    rhs = [jnp.concatenate([aw16[i], jnp.concatenate([zero16, v16[i]], axis=1)], axis=0) for i in idx]
    mn = [jnp.where(bdmask2, mm(be_ke_t[i], rhs[i]), 0.0) for i in idx]
    qhat = [x[:, :LANES] for x in qy]
    yhat = [x[:, LANES:] for x in qy]
    mmat = [eye_bd * p_end[i] + mn[i][:, :LANES] for i in idx]
    nmat = [x[:, LANES:] for x in mn]
    return qhat, yhat, mmat, nmat


def _rwkv_kernel(rkv_ref, lr_ref, w0_ref, a0_ref, wwa_ref, gup_ref,
                 kk_ref, ka_ref, rk_ref, lnxg_ref, lnxb_ref, o_ref, state, yout_s, *, tb):
    n_chunks = tb // CHUNK
    n_pairs = RW_WIDTH // LANES

    @pl.when(pl.program_id(1) == 0)
    def _():
        state[...] = jnp.zeros_like(state)

    r = rkv_ref[0, :, :RW_WIDTH].astype(F32)
    k = rkv_ref[0, :, RW_WIDTH:2 * RW_WIDTH].astype(F32)
    v = rkv_ref[0, :, 2 * RW_WIDTH:].astype(F32)
    lr_m = lr_ref[0].astype(F32)

    wa = lr_m[:, :LANES]
    lane = lax.broadcasted_iota(jnp.int32, wa.shape, 1)
    wa = jnp.where(lane < RW_DECAY_RANK, jnp.tanh(wa), wa)
    wa_up = _dot(wa, wwa_ref[...])
    ld = -math.exp(-0.5) * _sigmoid(w0_ref[...] + wa_up[:, :RW_WIDTH])
    iclr = _sigmoid(a0_ref[...] + wa_up[:, RW_WIDTH:])
    gate = _dot(_sigmoid(lr_m[:, LANES:]), gup_ref[...])

    hr = lax.broadcasted_iota(jnp.int32, (MXU_DIM, MXU_DIM), 0) // RW_HEAD_DIM
    hc = lax.broadcasted_iota(jnp.int32, (MXU_DIM, MXU_DIM), 1) // RW_HEAD_DIM
    seg = jnp.where(hr == hc, 1.0 / RW_HEAD_DIM, 0.0).astype(BF16)

    def split3(x):
        hi = x.astype(BF16)
        rem = x - hi.astype(F32)
        mid = rem.astype(BF16)
        return hi, mid, (rem - mid.astype(F32)).astype(BF16)

    def seg_mean(x, terms=1):
        parts = split3(x)[:terms]
        cols = []
        for c in range(0, RW_WIDTH, MXU_DIM):
            cols.append(sum(jnp.dot(part[:, c:c + MXU_DIM], seg, preferred_element_type=F32) for part in parts))
        return jnp.concatenate(cols, axis=1)

    kk = k * kk_ref[...]
    ss = seg_mean(kk * kk) * RW_HEAD_DIM
    kk = kk * lax.rsqrt(jnp.maximum(ss, 1e-24))
    kh = k * (1.0 + (iclr - 1.0) * ka_ref[...])

    tr = lax.broadcasted_iota(jnp.int32, (MXU_DIM, MXU_DIM), 0)
    tc = lax.broadcasted_iota(jnp.int32, (MXU_DIM, MXU_DIM), 1)
    tri = jnp.where((tc <= tr) & (tc // CHUNK == tr // CHUNK), 1.0, 0.0).astype(BF16)
    ld_parts = split3(ld)[:2]
    lp = jnp.concatenate(
        [sum(jnp.dot(tri, part[t0:t0 + MXU_DIM], preferred_element_type=F32) for part in ld_parts)
         for t0 in range(0, tb, MXU_DIM)], axis=0)

    a_t, r_t, b_t, k_t, b_e, k_e, vs, p_end = [], [], [], [], [], [], [], []
    b = kk * iclr
    for c in range(n_chunks):
        rs = slice(c * CHUNK, (c + 1) * CHUNK)
        lp_c = lp[rs]
        lp_last = lp_c[CHUNK - 1:CHUNK]
        e_in = jnp.exp(lp_c)
        e_inv = jnp.exp(-lp_c)
        e_end = jnp.exp(lp_last - lp_c)
        a_c = -kk[rs] * jnp.exp(lp_c - ld[rs])
        r_c = r[rs] * e_in
        b_c = b[rs] * e_inv
        k_c = kh[rs] * e_inv
        be_c = b[rs] * e_end
        ke_c = kh[rs] * e_end
        pe_c = jnp.exp(lp_last)
        for p in range(n_pairs):
            cs = slice(p * LANES, (p + 1) * LANES)
            a_t.append(a_c[:, cs])
            r_t.append(r_c[:, cs])
            b_t.append(b_c[:, cs])
            k_t.append(k_c[:, cs])
            b_e.append(be_c[:, cs])
            k_e.append(ke_c[:, cs])
            vs.append(v[rs, cs])
            p_end.append(pe_c[:, cs])

    qhat, yhat, mmat, nmat = _rwkv_chunk_terms(a_t, r_t, b_t, k_t, b_e, k_e, vs, p_end)

    s = [state[p] for p in range(n_pairs)]
    for c in range(n_chunks):
        ym = [_dot(jnp.concatenate([qhat[c * n_pairs + p], mmat[c * n_pairs + p]], axis=0), s[p])
              for p in range(n_pairs)]
        for p in range(n_pairs):
            i = c * n_pairs + p
            yout_s[c * CHUNK:(c + 1) * CHUNK, p * LANES:(p + 1) * LANES] = ym[p][:CHUNK] + yhat[i]
            s[p] = ym[p][CHUNK:] + nmat[i]
    for p in range(n_pairs):
        state[p] = s[p]

    y = yout_s[...]
    mean = seg_mean(y)
    yc = y - mean
    var = seg_mean(yc * yc)
    y = yc * lax.rsqrt(var + RW_GN_EPS) * lnxg_ref[...] + lnxb_ref[...]
    bonus = seg_mean(r * kh * rk_ref[...]) * RW_HEAD_DIM * v
    o_ref[0] = ((y + bonus) * gate).astype(o_ref.dtype)


def _rwkv(rkv, lr, w0, a0, wwa, gup, k_k, k_a, r_k, lnx_g, lnx_b, *, tb=512):
    bsz, t, w3 = rkv.shape
    n_pairs = RW_WIDTH // LANES

    def full(shape):
        return pl.BlockSpec(shape, lambda b, i: (0,) * len(shape))

    return pl.pallas_call(
        functools.partial(_rwkv_kernel, tb=tb),
        grid=(bsz, t // tb),
        in_specs=[pl.BlockSpec((1, tb, w3), lambda b, i: (b, i, 0)),
                  pl.BlockSpec((1, tb, LR_PAD), lambda b, i: (b, i, 0)),
                  full((1, RW_WIDTH)), full((1, RW_WIDTH)),
                  full((LANES, 2 * RW_WIDTH)), full((LR_PAD - LANES, RW_WIDTH)),
                  full((1, RW_WIDTH)), full((1, RW_WIDTH)), full((1, RW_WIDTH)),
                  full((1, RW_WIDTH)), full((1, RW_WIDTH))],
        out_specs=pl.BlockSpec((1, tb, RW_WIDTH), lambda b, i: (b, i, 0)),
        out_shape=jax.ShapeDtypeStruct((bsz, t, RW_WIDTH), BF16),
        scratch_shapes=[pltpu.VMEM((n_pairs, LANES, LANES), F32),
                        pltpu.VMEM((tb, RW_WIDTH), F32)],
        compiler_params=_cparams(2),
        name="rwkv",
    )(rkv, lr, w0, a0, wwa, gup, k_k, k_a, r_k, lnx_g, lnx_b)


def _attn_kernel(q_ref, kp_ref, kc_ref, vp_ref, vc_ref, bias_ref, o_ref, lse_ref, *, nb, rb):
    step = pl.program_id(1)
    n_blk = rb // BLOCK
    scale = DIL_HEAD_DIM ** -0.5
    n_pairs = DIL_WIDTH // LANES
    krow = lax.broadcasted_iota(jnp.int32, (2 * BLOCK, LANES), 0) // BLOCK
    klane = lax.broadcasted_iota(jnp.int32, (2 * BLOCK, LANES), 1) // DIL_HEAD_DIM
    own = krow == klane
    zeros = jnp.zeros((2 * BLOCK, LANES), BF16)
    ones_bd = jnp.where(own, 1.0, 0.0).astype(BF16)
    lane_o = lax.broadcasted_iota(jnp.int32, (BLOCK, LANES), 1)

    def stack_bd(x):
        return jnp.where(own, jnp.concatenate([x, x], axis=0), zeros)

    def attend(blocks):
        units = [(rows, q, parts, p) for rows, q, parts in blocks for p in range(n_pairs)]
        lanes = [slice(p * LANES, (p + 1) * LANES) for p in range(n_pairs)]
        scores = []
        for _, q, parts, p in units:
            qs = q[:, lanes[p]] * scale
            row = []
            for k_blk, _, col0, pen in parts:
                s = lax.dot_general(qs, stack_bd(k_blk[:, lanes[p]]), (((1,), (1,)), ((), ())),
                                    preferred_element_type=F32)
                bias = jnp.concatenate([bias_ref[2 * p, :, col0:col0 + BLOCK],
                                        bias_ref[2 * p + 1, :, col0:col0 + BLOCK]], axis=1)
                s = s + bias
                row.append(s if pen is None else s + pen)
            scores.append(row)
        ms = []
        for u in range(len(units)):
            halves = []
            for hh in range(2):
                cols = slice(hh * BLOCK, (hh + 1) * BLOCK)
                m = jnp.max(scores[u][0][:, cols], axis=-1, keepdims=True)
                for s in scores[u][1:]:
                    m = jnp.maximum(m, jnp.max(s[:, cols], axis=-1, keepdims=True))
                halves.append(m)
            ms.append(halves)
        es = [[jnp.exp((s - jnp.concatenate([jnp.broadcast_to(ms[u][0], (BLOCK, BLOCK)),
                                             jnp.broadcast_to(ms[u][1], (BLOCK, BLOCK))], axis=1)).astype(BF16))
               for s in scores[u]] for u in range(len(units))]
        res = []
        for u, (_, _, parts, p) in enumerate(units):
            res.append(sum(jnp.dot(e, jnp.concatenate([stack_bd(part[1][:, lanes[p]]), ones_bd], axis=1),
                                   preferred_element_type=F32) for e, part in zip(es[u], parts)))
        for b, (rows, _, _) in enumerate(blocks):
            m_all = jnp.zeros((BLOCK, LANES), F32)
            den_all = jnp.ones((BLOCK, LANES), F32)
            for p in range(n_pairs):
                u = b * n_pairs + p
                num, den = res[u][:, :LANES], res[u][:, LANES:]
                o_ref[rows, lanes[p]] = (num / den).astype(o_ref.dtype)
                m_all = jnp.where(lane_o == 2 * p, ms[u][0], jnp.where(lane_o == 2 * p + 1, ms[u][1], m_all))
                den_all = jnp.where((lane_o == 2 * p), den,
                                    jnp.where(lane_o == 2 * p + 1, pltpu.roll(den, DIL_HEAD_DIM, 1), den_all))
            lse_ref[rows, :] = m_all + jnp.log(den_all)

    blocks = []
    for j in range(n_blk):
        rows = slice(j * BLOCK, (j + 1) * BLOCK)
        cur = (kc_ref[rows, :], vc_ref[rows, :], BLOCK, None)
        if n_blk % nb == 0:
            prev = "none" if j % nb == 0 else "inside"
        else:
            prev = "inside" if j > 0 else ("none" if nb == n_blk else "before")
        if prev == "none":
            parts = [cur]
        elif prev == "inside":
            before = slice((j - 1) * BLOCK, j * BLOCK)
            parts = [(kc_ref[before, :], vc_ref[before, :], 0, None), cur]
        else:
            has_prev = (step % (nb // n_blk)) != 0
            parts = [(kp_ref[...], vp_ref[...], 0, jnp.where(has_prev, 0.0, MASKED)), cur]
        blocks.append((rows, q_ref[rows, :], parts))
    attend(blocks)


def _qkv_proj_kernel(x_ref, mod_ref, w_ref, o0_ref, o1_ref, o2_ref, xs_ref, xs4_ref, *, tm):
    m = mod_ref[0]
    scale = 1.0 + m[1:2]
    shift = m[0:1]
    n_col = x_ref.shape[-1] // LANES
    gw = 3 * DIL_WIDTH
    step = DIL_PATTERNS[1][1]
    assert [dil for _, dil in DIL_PATTERNS] == [1, step, step * step]

    for c in range(n_col):
        xs_ref[c] = x_ref[0, :, c * LANES:(c + 1) * LANES]
    rows4 = tm // step
    x4 = [jnp.concatenate([xs_ref[c, pl.ds(a, rows4, stride=step), :] for a in range(step)], axis=0)
          for c in range(n_col)]
    for c in range(n_col):
        xs4_ref[c] = x4[c]
    rows16 = rows4 // step
    x16 = [jnp.concatenate([xs4_ref[c, pl.ds(a * rows4 + b, rows16, stride=step), :]
                            for a in range(step) for b in range(step)], axis=0) for c in range(n_col)]
    inputs = [x_ref[0], jnp.concatenate(x4, axis=1), jnp.concatenate(x16, axis=1)]
    residue_of_group = [[0], list(range(step)), [a + step * b for a in range(step) for b in range(step)]]

    for g, o_ref in enumerate((o0_ref, o1_ref, o2_ref)):
        u = (inputs[g] * scale + shift).astype(BF16)
        res = jnp.dot(u, w_ref[:, g * gw:(g + 1) * gw], preferred_element_type=F32).astype(BF16)
        rows = tm // len(residue_of_group[g])
        for grp, r in enumerate(residue_of_group[g]):
            o_ref[0, r] = res[grp * rows:(grp + 1) * rows]


def _qkv_proj(x, mod, w, *, tm=1024):
    bsz, t, d = x.shape
    gw = 3 * DIL_WIDTH
    out_specs, out_shape = [], []
    for _, dil in DIL_PATTERNS:
        out_specs.append(pl.BlockSpec((1, dil, tm // dil, gw), lambda b, i: (b, 0, i, 0)))
        out_shape.append(jax.ShapeDtypeStruct((bsz, dil, t // dil, gw), BF16))
    return pl.pallas_call(
        functools.partial(_qkv_proj_kernel, tm=tm),
        grid=(bsz, t // tm),
        in_specs=[pl.BlockSpec((1, tm, d), lambda b, i: (b, i, 0)),
                  pl.BlockSpec((1, 6, d), lambda b, i: (b, 0, 0)),
                  pl.BlockSpec(w.shape, lambda b, i: (0, 0), pipeline_mode=pl.Buffered(1))],
        out_specs=out_specs,
        out_shape=out_shape,
        scratch_shapes=[pltpu.VMEM((d // LANES, tm, LANES), F32), pltpu.VMEM((d // LANES, tm, LANES), F32)],
        compiler_params=_cparams(2),
        name="qkv_proj",
    )(x, mod, w)


def _dilated_attention_group(qkv, bias, g, *, rb=1024):
    bsz, dil, length, width = qkv.shape
    t = dil * length
    nb = length // BLOCK
    n_blk = rb // BLOCK
    assert nb % n_blk == 0 or n_blk % nb == 0
    flat = qkv.reshape(bsz, t, width)

    def cur(j):
        return lambda b, s: (b, s, j)

    def prev(j):
        return lambda b, s: (b, jnp.maximum(s * n_blk - 1, 0), j)

    blk = (None, rb, DIL_WIDTH)
    pblk = (None, BLOCK, DIL_WIDTH)
    o, lse = pl.pallas_call(
        functools.partial(_attn_kernel, nb=nb, rb=rb),
        grid=(bsz, t // rb),
        in_specs=[pl.BlockSpec(blk, cur(0)),
                  pl.BlockSpec(pblk, prev(1)), pl.BlockSpec(blk, cur(1)),
                  pl.BlockSpec(pblk, prev(2)), pl.BlockSpec(blk, cur(2)),
                  pl.BlockSpec((DIL_HEADS, BLOCK, 2 * BLOCK), lambda b, s: (0, 0, 0))],
        out_specs=[pl.BlockSpec(blk, lambda b, s: (b, s, 0)),
                   pl.BlockSpec((None, rb, LANES), lambda b, s: (b, s, 0))],
        out_shape=[jax.ShapeDtypeStruct((bsz, t, DIL_WIDTH), BF16),
                   jax.ShapeDtypeStruct((bsz, t, LANES), F32)],
        compiler_params=_cparams(2),
        name=f"dilated_attn_g{g}",
    )(flat, flat, flat, flat, flat, bias)
    return o.reshape(bsz, dil, length, DIL_WIDTH), lse.reshape(bsz, dil, length, LANES)


def _t5_bucket(dist):
    exact = N_BUCKETS // 2
    logd = jnp.log(jnp.maximum(dist, 1).astype(F32) / exact) / math.log(MAX_DISTANCE / exact)
    large = jnp.minimum(exact + (logd * (N_BUCKETS - exact)).astype(jnp.int32), N_BUCKETS - 1)
    return jnp.where(dist < exact, dist, large)


def _bias_tables(rel_bias):
    rel = jnp.arange(BLOCK + 1)
    tables = []
    for g, (window, dil) in enumerate(DIL_PATTERNS):
        span = window // dil
        bucket = _t5_bucket(jnp.clip(rel, 0, span) * dil)
        vec = rel_bias[bucket][:, g * DIL_HEADS:(g + 1) * DIL_HEADS].T.astype(F32)
        ext = jnp.concatenate([vec[:, ::-1], jnp.zeros((DIL_HEADS, BLOCK), F32)], axis=1)
        flat = jnp.tile(ext, (1, BLOCK))[:, :BLOCK * 2 * BLOCK]
        tables.append(flat.reshape(DIL_HEADS, BLOCK, 2 * BLOCK))
    qi = np.arange(BLOCK)[:, None]
    ki = np.arange(2 * BLOCK)[None, :]
    band = (ki >= qi) & (ki <= qi + BLOCK)
    return jnp.where(band, jnp.stack(tables), MASKED)


def _tail(mix_fn, wout_ref, x_ref, mod_ref, lng_ref, lnb_ref, w1_ref, w2_ref, o_ref, *, tm, n_sub, ff_chunk):
    m = mod_ref[0]
    sub = tm // n_sub
    tiles = [slice(i * sub, (i + 1) * sub) for i in range(n_sub)]
    x1s, us = [], []
    for rows in tiles:
        y = None
        for piece, k0 in mix_fn(rows):
            part = jnp.dot(piece, wout_ref[k0:k0 + piece.shape[-1], :], preferred_element_type=F32)
            y = part if y is None else y + part
        x1 = _layer_norm_rows(DEEPNORM_ALPHA * x_ref[0, rows, :] + (1.0 + m[2:3]) * y, lng_ref[0:1], lnb_ref[0:1])
        x1s.append(x1)
        us.append((x1 * (1.0 + m[4:5]) + m[3:4]).astype(BF16))
    accs = [None] * n_sub
    for c in range(D_FF // ff_chunk):
        cols = slice(c * ff_chunk, (c + 1) * ff_chunk)
        w1c = w1_ref[:, cols].astype(BF16)
        w2c = w2_ref[cols, :].astype(BF16)
        for i in range(n_sub):
            h = jnp.maximum(jnp.dot(us[i], w1c, preferred_element_type=F32), 0.0)
            part = jnp.dot((h * h).astype(BF16), w2c, preferred_element_type=F32)
            accs[i] = part if accs[i] is None else accs[i] + part
    for i, rows in enumerate(tiles):
        z = DEEPNORM_ALPHA * x1s[i] + (1.0 + m[5:6]) * accs[i]
        o_ref[0, rows, :] = _layer_norm_rows(z, lng_ref[1:2], lnb_ref[1:2])


def _even_tail_kernel(ya_ref, yb_ref, wout_ref, x_ref, mod_ref, lng_ref, lnb_ref, w1_ref, w2_ref, o_ref, **kw):
    def mix(rows):
        return [(ya_ref[0, rows, :], 0), (yb_ref[0, rows, :], ya_ref.shape[-1])]

    _tail(mix, wout_ref, x_ref, mod_ref, lng_ref, lnb_ref, w1_ref, w2_ref, o_ref, **kw)


def _odd_tail_kernel(o0_ref, o1_ref, o2_ref, l0_ref, l1_ref, l2_ref, wout_ref, x_ref, mod_ref, lng_ref, lnb_ref,
                     w1_ref, w2_ref, o_ref, o_nat, l_nat, mix_s, *, tm, **kw):
    def natural(src_ref, scr, dil):
        if dil == 1:
            return src_ref[0, 0].astype(F32)
        n_col = src_ref.shape[-1] // LANES
        for c in range(n_col):
            for r in range(dil):
                scr[c, pl.ds(r, tm // dil, stride=dil), :] = src_ref[0, r, :, c * LANES:(c + 1) * LANES].astype(F32)
        return jnp.concatenate([scr[c] for c in range(n_col)], axis=1)

    dils = [dil for _, dil in DIL_PATTERNS]
    lses = [natural(ref, l_nat, dil) for ref, dil in zip((l0_ref, l1_ref, l2_ref), dils)]
    mx = jnp.maximum(jnp.maximum(lses[0], lses[1]), lses[2])
    es = [jnp.exp(l - mx) for l in lses]
    inv = 1.0 / (es[0] + es[1] + es[2])
    hr = lax.broadcasted_iota(jnp.int32, (LANES, DIL_WIDTH), 0)
    hc = lax.broadcasted_iota(jnp.int32, (LANES, DIL_WIDTH), 1) // DIL_HEAD_DIM
    expand = jnp.where(hr == hc, 1.0, 0.0).astype(BF16)

    def widen(wt):
        return jnp.dot(wt.astype(BF16), expand, preferred_element_type=F32)

    merged = None
    for e, ref, dil in zip(es, (o0_ref, o1_ref, o2_ref), dils):
        term = widen(e * inv) * natural(ref, o_nat, dil)
        merged = term if merged is None else merged + term
    mix_s[...] = merged.astype(BF16)
    _tail(lambda rows: [(mix_s[rows, :], 0)], wout_ref, x_ref, mod_ref, lng_ref, lnb_ref, w1_ref, w2_ref, o_ref,
          tm=tm, **kw)


def _layer_tail(mix, w_out, x, mod, ln_g, ln_b, w1, w2, layer, *, tm=512, n_sub=2, ff_chunk=1024):
    bsz, t, d = x.shape

    def const(shape):
        return pl.BlockSpec(shape, lambda b, i: (0,) * len(shape), pipeline_mode=pl.Buffered(1))

    def of_layer(shape):
        return pl.BlockSpec((None,) + shape, lambda b, i: (layer,) + (0,) * len(shape), pipeline_mode=pl.Buffered(1))

    tail_specs = [const(w_out.shape),
                  pl.BlockSpec((1, tm, d), lambda b, i: (b, i, 0)),
                  pl.BlockSpec((1, 6, d), lambda b, i: (b, 0, 0)),
                  of_layer((2, d)), of_layer((2, d)), of_layer((d, D_FF)), of_layer((D_FF, d))]
    tail_args = (w_out, x, mod, ln_g, ln_b, w1, w2)
    kw = dict(tm=tm, n_sub=n_sub, ff_chunk=ff_chunk)
    if isinstance(mix, tuple):
        outs, lses = mix
        mix_specs = ([pl.BlockSpec((1, dil, tm // dil, DIL_WIDTH), lambda b, i: (b, 0, i, 0)) for _, dil in DIL_PATTERNS]
                     + [pl.BlockSpec((1, dil, tm // dil, LANES), lambda b, i: (b, 0, i, 0)) for _, dil in DIL_PATTERNS])
        mix_args = (*outs, *lses)
        body = functools.partial(_odd_tail_kernel, **kw)
        scratch = [pltpu.VMEM((DIL_WIDTH // LANES, tm, LANES), F32), pltpu.VMEM((1, tm, LANES), F32),
                   pltpu.VMEM((tm, DIL_WIDTH), BF16)]
    else:
        mix_specs = [pl.BlockSpec((1, tm, piece.shape[-1]), lambda b, i: (b, i, 0)) for piece in mix]
        mix_args = tuple(mix)
        body = functools.partial(_even_tail_kernel, **kw)
        scratch = []
    return pl.pallas_call(
        body,
        grid=(bsz, t // tm),
        in_specs=mix_specs + tail_specs,
        out_specs=pl.BlockSpec((1, tm, d), lambda b, i: (b, i, 0)),
        out_shape=jax.ShapeDtypeStruct((bsz, t, d), F32),
        scratch_shapes=scratch,
        compiler_params=_cparams(2, TAIL_VMEM_LIMIT),
        name="layer_tail",
    )(*mix_args, *tail_args)


def _prep_even_weights(w_in, mu, w_up, a_up, g_up):
    w3 = 3 * RW_WIDTH
    pad = LR_PAD - (RW_PROJ - w3)
    d = w_in.shape[0]
    w_perm = jnp.concatenate([w_in[:, :w3], w_in[:, RW_PROJ:], w_in[:, w3:RW_PROJ],
                              jnp.zeros((d, pad), w_in.dtype)], axis=1).astype(BF16)
    mu_pad = jnp.concatenate([mu, jnp.zeros((pad,), mu.dtype)]).reshape(1, w3 + LR_PAD)
    zeros = jnp.zeros((RW_DECAY_RANK, RW_WIDTH), w_up.dtype)
    wwa = jnp.concatenate([jnp.concatenate([w_up, zeros], axis=1),
                           jnp.concatenate([zeros, a_up], axis=1)], axis=0).astype(BF16)
    gup = jnp.concatenate([g_up, jnp.zeros((pad, RW_WIDTH), g_up.dtype)], axis=0).astype(BF16)
    return w_perm, mu_pad, wwa, gup


def kernel(x, c, ada_w, ada_b, ln_g, ln_b, ab_w_in, rw_mu, rw_w0, rw_w_up, rw_a0, rw_a_up, rw_g_up,
           rw_k_k, rw_k_a, rw_r_k, rw_lnx_g, rw_lnx_b, sc_conv_w, ab_w_out, dil_w_qkv, dil_w_out,
           rel_bias, mlp_w1, mlp_w2):
    bsz, t, d = x.shape
    mods = _adaln(c, ada_w, ada_b).reshape(DEPTH, bsz, 6, d)
    bias = None
    row = lambda a: a.reshape(1, -1)
    w1_all, w2_all = mlp_w1, mlp_w2
    for i in range(DEPTH):
        mod = mods[i]
        j = i // 2
        if i % 2 == 0:
            w_perm, mu_pad, wwa, gup = _prep_even_weights(
                ab_w_in[j], rw_mu[j], rw_w_up[j], rw_a_up[j], rw_g_up[j])
            rkv, lr, yb = _ab_proj(x, mod, w_perm, mu_pad, sc_conv_w[j])
            ya = _rwkv(rkv, lr, row(rw_w0[j]), row(rw_a0[j]), wwa, gup,
                       row(rw_k_k[j]), row(rw_k_a[j]), row(rw_r_k[j]),
                       row(rw_lnx_g[j]), row(rw_lnx_b[j]))
            mix, w_out = [ya, yb], ab_w_out[j]
        else:
            if bias is None:
                bias = _bias_tables(rel_bias)
            qkvs = _qkv_proj(x, mod, dil_w_qkv[j].astype(BF16))
            outs, lses = [], []
            for g in range(N_GROUPS):
                o, lse = _dilated_attention_group(qkvs[g], bias[g], g)
                outs.append(o)
                lses.append(lse)
            mix, w_out = (outs, lses), dil_w_out[j]
        x = _layer_tail(mix, w_out.astype(BF16), x, mod, ln_g, ln_b, w1_all, w2_all, i)
    return x
```

```python
import functools
import math

import jax
import jax.numpy as jnp
import numpy as np
from jax import lax
from jax.experimental import pallas as pl
from jax.experimental.pallas import tpu as pltpu

F32 = jnp.float32
BF16 = jnp.bfloat16

D_MODEL = 1024
DEPTH = 2
RW_HEADS = 8
RW_HEAD_DIM = 64
RW_WIDTH = 512
RW_DECAY_RANK = 64
RW_ICLR_RANK = 64
RW_GATE_RANK = 160
RW_GN_EPS = 64e-5
RW_PROJ = 3 * RW_WIDTH + RW_DECAY_RANK + RW_ICLR_RANK + RW_GATE_RANK
SC_WIDTH = 512
SC_CONV = 3
DIL_PATTERNS = ((128, 1), (512, 4), (2048, 16))
N_GROUPS = 3
DIL_HEADS = 8
DIL_HEAD_DIM = 64
DIL_WIDTH = 512
BLOCK = 128
N_BUCKETS = 32
MAX_DISTANCE = 2048
D_FF = 4 * D_MODEL
DEEPNORM_ALPHA = (2 * DEPTH) ** 0.25
LN_EPS = 1e-5

LANES = 128
MXU_DIM = 256
CHUNK = 64
LR_PAD = 384
VMEM_LIMIT = 56 * 1024 * 1024
TAIL_VMEM_LIMIT = 60 * 1024 * 1024
MASKED = -1e30


def _cparams(n_axes, vmem_limit=VMEM_LIMIT):
    return pltpu.CompilerParams(dimension_semantics=("arbitrary",) * n_axes,
                                vmem_limit_bytes=vmem_limit)


def _sigmoid(x):
    return 1.0 / (1.0 + jnp.exp(-x))


def _dot(a, b):
    return jnp.dot(a.astype(BF16), b.astype(BF16), preferred_element_type=F32)


def _layer_norm_rows(z, g, b):
    mu = jnp.mean(z, axis=-1, keepdims=True)
    zc = z - mu
    var = jnp.mean(zc * zc, axis=-1, keepdims=True)
    return zc * lax.rsqrt(var + LN_EPS) * g + b


def _adaln_kernel(c_ref, w_ref, b_ref, o_ref):
    c = c_ref[...]
    cond = c * _sigmoid(c)
    o_ref[0] = _dot(cond, w_ref[0]) + b_ref[0]


def _adaln(c, ada_w, ada_b):
    depth, d, n = ada_w.shape
    bsz = c.shape[0]
    tn = 1536
    return pl.pallas_call(
        _adaln_kernel,
        grid=(depth, n // tn),
        in_specs=[pl.BlockSpec((bsz, d), lambda i, j: (0, 0)),
                  pl.BlockSpec((1, d, tn), lambda i, j: (i, 0, j)),
                  pl.BlockSpec((1, 1, tn), lambda i, j: (i, 0, j))],
        out_specs=pl.BlockSpec((1, bsz, tn), lambda i, j: (i, 0, j)),
        out_shape=jax.ShapeDtypeStruct((depth, bsz, n), F32),
        compiler_params=_cparams(2),
        name="adaln",
    )(c, ada_w, ada_b.reshape(depth, 1, n))


def _shift_rows(x, carry, k):
    rolled = pltpu.roll(x, k, 0)
    head = rolled[:8]
    row = lax.broadcasted_iota(jnp.int32, head.shape, 0)
    for i in range(k):
        head = jnp.where(row == i, carry[8 - k + i:8 - k + i + 1], head)
    return jnp.concatenate([head, rolled[8:]], axis=0)


def _ab_proj_kernel(x_ref, mod_ref, w_rkv_ref, w_lr_ref, w_cv_ref, mu_ref, convw_ref, rkv_ref, lr_ref, yb_ref,
                    carry_p, carry_z, *, tm):
    @pl.when(pl.program_id(1) == 0)
    def _():
        carry_p[...] = jnp.zeros_like(carry_p)
        carry_z[...] = jnp.zeros_like(carry_z)

    m = mod_ref[0]
    u = (x_ref[0] * (1.0 + m[1:2]) + m[0:1]).astype(BF16)

    def proj(w_ref, c0, width):
        return jnp.dot(u, w_ref[:, c0:c0 + width], preferred_element_type=F32)

    def mixed(w_ref, c0, m0, width):
        p = proj(w_ref, c0, width)
        prev = _shift_rows(p, carry_p[:, m0:m0 + width], 1)
        carry_p[:, m0:m0 + width] = p[tm - 8:]
        return (p + mu_ref[:, m0:m0 + width] * (prev - p)).astype(BF16)

    for c in range(3):
        cols = slice(c * RW_WIDTH, (c + 1) * RW_WIDTH)
        rkv_ref[0, :, cols] = mixed(w_rkv_ref, c * RW_WIDTH, c * RW_WIDTH, RW_WIDTH)
    lr_ref[0] = mixed(w_lr_ref, 0, 3 * RW_WIDTH, LR_PAD)

    h = proj(w_cv_ref, 0, SC_WIDTH)
    z = proj(w_cv_ref, 2 * SC_WIDTH, SC_WIDTH) * h
    zc = carry_z[...]
    cw = convw_ref[...]
    conv = cw[2:3] * z + cw[1:2] * _shift_rows(z, zc, 1) + cw[0:1] * _shift_rows(z, zc, 2)
    carry_z[...] = z[tm - 8:]
    yb_ref[0] = (proj(w_cv_ref, SC_WIDTH, SC_WIDTH) * conv).astype(BF16)


def _ab_proj(x, mod, w, mu, conv_w, *, tm=1024):
    bsz, t, d = x.shape
    w3 = 3 * RW_WIDTH

    def const(shape):
        return pl.BlockSpec(shape, lambda b, i: (0,) * len(shape), pipeline_mode=pl.Buffered(1))

    def rows(width):
        return pl.BlockSpec((1, tm, width), lambda b, i: (b, i, 0))

    return pl.pallas_call(
        functools.partial(_ab_proj_kernel, tm=tm),
        grid=(bsz, t // tm),
        in_specs=[rows(d), pl.BlockSpec((1, 6, d), lambda b, i: (b, 0, 0)),
                  *[const(part.shape) for part in w], const((1, w3 + LR_PAD)), const((SC_CONV, SC_WIDTH))],
        out_specs=[rows(w3), rows(LR_PAD), rows(SC_WIDTH)],
        out_shape=[jax.ShapeDtypeStruct((bsz, t, w3), BF16), jax.ShapeDtypeStruct((bsz, t, LR_PAD), BF16),
                   jax.ShapeDtypeStruct((bsz, t, SC_WIDTH), BF16)],
        scratch_shapes=[pltpu.VMEM((8, w3 + LR_PAD), F32), pltpu.VMEM((8, SC_WIDTH), F32)],
        compiler_params=_cparams(2),
        name="ab_proj",
    )(x, mod, *w, mu, conv_w)


def _rwkv_chunk_terms(a_t, r_t, b_t, k_t, b_e, k_e, v, p_end):
    n = len(a_t)
    idx = range(n)
    row = lax.broadcasted_iota(jnp.int32, (CHUNK, LANES), 0)
    col = lax.broadcasted_iota(jnp.int32, (CHUNK, LANES), 1)
    s_idx = col % CHUNK
    head0 = col < RW_HEAD_DIM
    strict = s_idx < row
    incl = s_idx <= row
    near = strict & ((s_idx // 16) == (row // 16))
    far = strict & ((s_idx // 16) != (row // 16))
    eye = jnp.where(s_idx == row, 1.0, 0.0)
    r2 = lax.broadcasted_iota(jnp.int32, (LANES, 2 * LANES), 0)
    c2 = lax.broadcasted_iota(jnp.int32, (LANES, 2 * LANES), 1)
    bdmask2 = (r2 // RW_HEAD_DIM) == ((c2 % LANES) // RW_HEAD_DIM)
    eye_bd = jnp.where(lax.broadcasted_iota(jnp.int32, (LANES, LANES), 0)
                       == lax.broadcasted_iota(jnp.int32, (LANES, LANES), 1), 1.0, 0.0)
    zero16 = jnp.zeros((CHUNK, LANES), BF16)
    lane1 = lax.broadcasted_iota(jnp.int32, (1, LANES), 1)
    m0_16 = jnp.where(lane1 < RW_HEAD_DIM, 1.0, 0.0).astype(BF16)
    m1_16 = jnp.where(lane1 < RW_HEAD_DIM, 0.0, 1.0).astype(BF16)

    def cast(xs):
        return [x.astype(BF16) for x in xs]

    def bd(q16):
        return jnp.concatenate([q16 * m0_16, q16 * m1_16], axis=0)

    def mm(p16, q16):
        return jnp.dot(p16, q16, preferred_element_type=F32)

    def mm_nt(p16, q16):
        return lax.dot_general(p16, q16, (((1,), (1,)), ((), ())), preferred_element_type=F32)

    a16, r16, v16 = cast(a_t), cast(r_t), cast(v)
    lhs = [jnp.concatenate([a16[i], r16[i]], axis=0) for i in idx]
    b16, k16 = cast(b_t), cast(k_t)
    bdv = [bd(x) for x in v16]
    g = [mm_nt(lhs[i], jnp.concatenate([bd(b16[i]), bd(k16[i])], axis=0)) for i in idx]
    x1 = [jnp.where(near, g[i][:CHUNK, :LANES], 0.0) for i in idx]
    x1_16 = cast(x1)
    l_e16 = [jnp.where(far, g[i][:CHUNK, :LANES], 0.0).astype(BF16) for i in idx]
    a_k16 = [jnp.concatenate([jnp.where(strict, g[i][:CHUNK, LANES:], 0.0),
                              jnp.where(incl, g[i][CHUNK:, LANES:], 0.0)], axis=0).astype(BF16) for i in idx]
    a_rb16 = [jnp.where(incl, g[i][CHUNK:, :LANES], 0.0).astype(BF16) for i in idx]

    def both(p16, q16):
        return jnp.concatenate([bd(p16), bd(q16)], axis=1)

    x2_16 = cast([mm(x1_16[i], bd(x1_16[i])) for i in idx])
    acc = [eye + x1[i] for i in idx]
    sq = [mm(x2_16[i], both(x2_16[i], acc[i].astype(BF16))) for i in idx]
    x4_16 = [sq[i][:, :LANES].astype(BF16) for i in idx]
    acc = [acc[i] + sq[i][:, LANES:] for i in idx]
    sq = [mm(x4_16[i], both(x4_16[i], acc[i].astype(BF16))) for i in idx]
    x8_16 = [sq[i][:, :LANES].astype(BF16) for i in idx]
    acc = [acc[i] + sq[i][:, LANES:] for i in idx]
    dinv = [acc[i] + mm(x8_16[i], bd(acc[i].astype(BF16))) for i in idx]
    dinv16 = cast(dinv)
    f16 = cast([mm(dinv16[i], bd(l_e16[i])) for i in idx])
    sq = [mm(f16[i], both(f16[i], dinv16[i])) for i in idx]
    f2_16 = [sq[i][:, :LANES].astype(BF16) for i in idx]
    hmat = [dinv[i] + sq[i][:, LANES:] for i in idx]
    tinv16 = cast([hmat[i] + mm(f2_16[i], bd(hmat[i].astype(BF16))) for i in idx])

    kv = [mm(a_k16[i], bdv[i]) for i in idx]
    wv16 = [kv[i][:CHUNK].astype(BF16) for i in idx]
    aw = [mm(tinv16[i], both(a16[i], wv16[i])) for i in idx]
    aw16 = cast(aw)
    qy = [mm(a_rb16[i], both(aw16[i][:, :LANES], aw16[i][:, LANES:]))
          + jnp.concatenate([r_t[i], kv[i][CHUNK:]], axis=1) for i in idx]
    be_ke_t = [jnp.concatenate([b_e[i], k_e[i]], axis=0).T.astype(BF16) for i in idx]
    rhs = [jnp.concatenate([aw16[i], jnp.concatenate([zero16, v16[i]], axis=1)], axis=0) for i in idx]
    mn = [jnp.where(bdmask2, mm(be_ke_t[i], rhs[i]), 0.0) for i in idx]
    qhat = [x[:, :LANES] for x in qy]
    yhat = [x[:, LANES:] for x in qy]
    mmat = [eye_bd * p_end[i] + mn[i][:, :LANES] for i in idx]
    nmat = [x[:, LANES:] for x in mn]
    return qhat, yhat, mmat, nmat


def _rwkv_kernel(rkv_ref, lr_ref, w0_ref, a0_ref, wwa_ref, gup_ref,
                 kk_ref, ka_ref, rk_ref, lnxg_ref, lnxb_ref, o_ref, state, yout_s, *, tb):
    n_chunks = tb // CHUNK
    n_pairs = RW_WIDTH // LANES

    @pl.when(pl.program_id(1) == 0)
    def _():
        state[...] = jnp.zeros_like(state)

    r = rkv_ref[0, :, :RW_WIDTH].astype(F32)
    k = rkv_ref[0, :, RW_WIDTH:2 * RW_WIDTH].astype(F32)
    v = rkv_ref[0, :, 2 * RW_WIDTH:].astype(F32)
    lr_m = lr_ref[0].astype(F32)

    wa = lr_m[:, :LANES]
    lane = lax.broadcasted_iota(jnp.int32, wa.shape, 1)
    wa = jnp.where(lane < RW_DECAY_RANK, jnp.tanh(wa), wa)
    wa_up = _dot(wa, wwa_ref[...])
    ld = -math.exp(-0.5) * _sigmoid(w0_ref[...] + wa_up[:, :RW_WIDTH])
    iclr = _sigmoid(a0_ref[...] + wa_up[:, RW_WIDTH:])
    gate = _dot(_sigmoid(lr_m[:, LANES:]), gup_ref[...])

    hr = lax.broadcasted_iota(jnp.int32, (MXU_DIM, MXU_DIM), 0) // RW_HEAD_DIM
    hc = lax.broadcasted_iota(jnp.int32, (MXU_DIM, MXU_DIM), 1) // RW_HEAD_DIM
    seg = jnp.where(hr == hc, 1.0 / RW_HEAD_DIM, 0.0).astype(BF16)

    def split3(x):
        hi = x.astype(BF16)
        rem = x - hi.astype(F32)
        mid = rem.astype(BF16)
        return hi, mid, (rem - mid.astype(F32)).astype(BF16)

    def seg_mean(x, terms=1):
        parts = split3(x)[:terms]
        cols = []
        for c in range(0, RW_WIDTH, MXU_DIM):
            cols.append(sum(jnp.dot(part[:, c:c + MXU_DIM], seg, preferred_element_type=F32) for part in parts))
        return jnp.concatenate(cols, axis=1)

    kk = k * kk_ref[...]
    ss = seg_mean(kk * kk) * RW_HEAD_DIM
    kk = kk * lax.rsqrt(jnp.maximum(ss, 1e-24))
    kh = k * (1.0 + (iclr - 1.0) * ka_ref[...])

    tr = lax.broadcasted_iota(jnp.int32, (MXU_DIM, MXU_DIM), 0)
    tc = lax.broadcasted_iota(jnp.int32, (MXU_DIM, MXU_DIM), 1)
    tri = jnp.where((tc <= tr) & (tc // CHUNK == tr // CHUNK), 1.0, 0.0).astype(BF16)
    ld_parts = split3(ld)[:2]
    lp = jnp.concatenate(
        [sum(jnp.dot(tri, part[t0:t0 + MXU_DIM], preferred_element_type=F32) for part in ld_parts)
         for t0 in range(0, tb, MXU_DIM)], axis=0)

    a_t, r_t, b_t, k_t, b_e, k_e, vs, p_end = [], [], [], [], [], [], [], []
    b = kk * iclr
    for c in range(n_chunks):
        rs = slice(c * CHUNK, (c + 1) * CHUNK)
        lp_c = lp[rs]
        lp_last = lp_c[CHUNK - 1:CHUNK]
        e_in = jnp.exp(lp_c)
        e_inv = jnp.exp(-lp_c)
        e_end = jnp.exp(lp_last - lp_c)
        a_c = -kk[rs] * jnp.exp(lp_c - ld[rs])
        r_c = r[rs] * e_in
        b_c = b[rs] * e_inv
        k_c = kh[rs] * e_inv
        be_c = b[rs] * e_end
        ke_c = kh[rs] * e_end
        pe_c = jnp.exp(lp_last)
        for p in range(n_pairs):
            cs = slice(p * LANES, (p + 1) * LANES)
            a_t.append(a_c[:, cs])
            r_t.append(r_c[:, cs])
            b_t.append(b_c[:, cs])
            k_t.append(k_c[:, cs])
            b_e.append(be_c[:, cs])
            k_e.append(ke_c[:, cs])
            vs.append(v[rs, cs])
            p_end.append(pe_c[:, cs])

    qhat, yhat, mmat, nmat = _rwkv_chunk_terms(a_t, r_t, b_t, k_t, b_e, k_e, vs, p_end)

    s = [state[p] for p in range(n_pairs)]
    for c in range(n_chunks):
        ym = [_dot(jnp.concatenate([qhat[c * n_pairs + p], mmat[c * n_pairs + p]], axis=0), s[p])
              for p in range(n_pairs)]
        for p in range(n_pairs):
            i = c * n_pairs + p
            yout_s[c * CHUNK:(c + 1) * CHUNK, p * LANES:(p + 1) * LANES] = ym[p][:CHUNK] + yhat[i]
            s[p] = ym[p][CHUNK:] + nmat[i]
    for p in range(n_pairs):
        state[p] = s[p]

    y = yout_s[...]
    mean = seg_mean(y)
    yc = y - mean
    var = seg_mean(yc * yc)
    y = yc * lax.rsqrt(var + RW_GN_EPS) * lnxg_ref[...] + lnxb_ref[...]
    bonus = seg_mean(r * kh * rk_ref[...]) * RW_HEAD_DIM * v
    o_ref[0] = ((y + bonus) * gate).astype(o_ref.dtype)


def _rwkv(rkv, lr, w0, a0, wwa, gup, k_k, k_a, r_k, lnx_g, lnx_b, *, tb=512):
    bsz, t, w3 = rkv.shape
    n_pairs = RW_WIDTH // LANES

    def full(shape):
        return pl.BlockSpec(shape, lambda b, i: (0,) * len(shape))

    return pl.pallas_call(
        functools.partial(_rwkv_kernel, tb=tb),
        grid=(bsz, t // tb),
        in_specs=[pl.BlockSpec((1, tb, w3), lambda b, i: (b, i, 0)),
                  pl.BlockSpec((1, tb, LR_PAD), lambda b, i: (b, i, 0)),
                  full((1, RW_WIDTH)), full((1, RW_WIDTH)),
                  full((LANES, 2 * RW_WIDTH)), full((LR_PAD - LANES, RW_WIDTH)),
                  full((1, RW_WIDTH)), full((1, RW_WIDTH)), full((1, RW_WIDTH)),
                  full((1, RW_WIDTH)), full((1, RW_WIDTH))],
        out_specs=pl.BlockSpec((1, tb, RW_WIDTH), lambda b, i: (b, i, 0)),
        out_shape=jax.ShapeDtypeStruct((bsz, t, RW_WIDTH), BF16),
        scratch_shapes=[pltpu.VMEM((n_pairs, LANES, LANES), F32),
                        pltpu.VMEM((tb, RW_WIDTH), F32)],
        compiler_params=_cparams(2),
        name="rwkv",
    )(rkv, lr, w0, a0, wwa, gup, k_k, k_a, r_k, lnx_g, lnx_b)


def _attn_kernel(q_ref, kp_ref, kc_ref, vp_ref, vc_ref, bias_ref, o_ref, lse_ref, *, nb, rb):
    step = pl.program_id(1)
    n_blk = rb // BLOCK
    scale = DIL_HEAD_DIM ** -0.5
    n_pairs = DIL_WIDTH // LANES
    krow = lax.broadcasted_iota(jnp.int32, (2 * BLOCK, LANES), 0) // BLOCK
    klane = lax.broadcasted_iota(jnp.int32, (2 * BLOCK, LANES), 1) // DIL_HEAD_DIM
    own = krow == klane
    zeros = jnp.zeros((2 * BLOCK, LANES), BF16)
    ones_bd = jnp.where(own, 1.0, 0.0).astype(BF16)
    lane_o = lax.broadcasted_iota(jnp.int32, (BLOCK, LANES), 1)

    def stack_bd(x):
        return jnp.where(own, jnp.concatenate([x, x], axis=0), zeros)

    def attend(blocks):
        units = [(rows, q, parts, p) for rows, q, parts in blocks for p in range(n_pairs)]
        lanes = [slice(p * LANES, (p + 1) * LANES) for p in range(n_pairs)]
        scores = []
        for _, q, parts, p in units:
            qs = q[:, lanes[p]] * scale
            row = []
            for k_blk, _, col0, pen in parts:
                s = lax.dot_general(qs, stack_bd(k_blk[:, lanes[p]]), (((1,), (1,)), ((), ())),
                                    preferred_element_type=F32)
                bias = jnp.concatenate([bias_ref[2 * p, :, col0:col0 + BLOCK],
                                        bias_ref[2 * p + 1, :, col0:col0 + BLOCK]], axis=1)
                s = s + bias
                row.append(s if pen is None else s + pen)
            scores.append(row)
        ms = []
        for u in range(len(units)):
            halves = []
            for hh in range(2):
                cols = slice(hh * BLOCK, (hh + 1) * BLOCK)
                m = jnp.max(scores[u][0][:, cols], axis=-1, keepdims=True)
                for s in scores[u][1:]:
                    m = jnp.maximum(m, jnp.max(s[:, cols], axis=-1, keepdims=True))
                halves.append(m)
            ms.append(halves)
        es = [[jnp.exp((s - jnp.concatenate([jnp.broadcast_to(ms[u][0], (BLOCK, BLOCK)),
                                             jnp.broadcast_to(ms[u][1], (BLOCK, BLOCK))], axis=1)).astype(BF16))
               for s in scores[u]] for u in range(len(units))]
        res = []
        for u, (_, _, parts, p) in enumerate(units):
            res.append(sum(jnp.dot(e, jnp.concatenate([stack_bd(part[1][:, lanes[p]]), ones_bd], axis=1),
                                   preferred_element_type=F32) for e, part in zip(es[u], parts)))
        for b, (rows, _, _) in enumerate(blocks):
            m_all = jnp.zeros((BLOCK, LANES), F32)
            den_all = jnp.ones((BLOCK, LANES), F32)
            for p in range(n_pairs):
                u = b * n_pairs + p
                num, den = res[u][:, :LANES], res[u][:, LANES:]
                o_ref[rows, lanes[p]] = (num / den).astype(o_ref.dtype)
                m_all = jnp.where(lane_o == 2 * p, ms[u][0], jnp.where(lane_o == 2 * p + 1, ms[u][1], m_all))
                den_all = jnp.where((lane_o == 2 * p), den,
                                    jnp.where(lane_o == 2 * p + 1, pltpu.roll(den, DIL_HEAD_DIM, 1), den_all))
            lse_ref[rows, :] = m_all + jnp.log(den_all)

    blocks = []
    for j in range(n_blk):
        rows = slice(j * BLOCK, (j + 1) * BLOCK)
        cur = (kc_ref[rows, :], vc_ref[rows, :], BLOCK, None)
        if n_blk % nb == 0:
            prev = "none" if j % nb == 0 else "inside"
        else:
            prev = "inside" if j > 0 else ("none" if nb == n_blk else "before")
        if prev == "none":
            parts = [cur]
        elif prev == "inside":
            before = slice((j - 1) * BLOCK, j * BLOCK)
            parts = [(kc_ref[before, :], vc_ref[before, :], 0, None), cur]
        else:
            has_prev = (step % (nb // n_blk)) != 0
            parts = [(kp_ref[...], vp_ref[...], 0, jnp.where(has_prev, 0.0, MASKED)), cur]
        blocks.append((rows, q_ref[rows, :], parts))
    attend(blocks)


def _qkv_proj_kernel(x_ref, mod_ref, w_ref, o0_ref, o1_ref, o2_ref, xs_ref, xs4_ref, *, tm):
    m = mod_ref[0]
    scale = 1.0 + m[1:2]
    shift = m[0:1]
    n_col = x_ref.shape[-1] // LANES
    gw = 3 * DIL_WIDTH
    step = DIL_PATTERNS[1][1]
    assert [dil for _, dil in DIL_PATTERNS] == [1, step, step * step]

    for c in range(n_col):
        xs_ref[c] = x_ref[0, :, c * LANES:(c + 1) * LANES]
    rows4 = tm // step
    x4 = [jnp.concatenate([xs_ref[c, pl.ds(a, rows4, stride=step), :] for a in range(step)], axis=0)
          for c in range(n_col)]
    for c in range(n_col):
        xs4_ref[c] = x4[c]
    rows16 = rows4 // step
    x16 = [jnp.concatenate([xs4_ref[c, pl.ds(a * rows4 + b, rows16, stride=step), :]
                            for a in range(step) for b in range(step)], axis=0) for c in range(n_col)]
    inputs = [x_ref[0], jnp.concatenate(x4, axis=1), jnp.concatenate(x16, axis=1)]
    residue_of_group = [[0], list(range(step)), [a + step * b for a in range(step) for b in range(step)]]

    for g, o_ref in enumerate((o0_ref, o1_ref, o2_ref)):
        u = (inputs[g] * scale + shift).astype(BF16)
        res = jnp.dot(u, w_ref[:, g * gw:(g + 1) * gw], preferred_element_type=F32).astype(BF16)
        rows = tm // len(residue_of_group[g])
        for grp, r in enumerate(residue_of_group[g]):
            o_ref[0, r] = res[grp * rows:(grp + 1) * rows]


def _qkv_proj(x, mod, w, *, tm=1024):
    bsz, t, d = x.shape
    gw = 3 * DIL_WIDTH
    out_specs, out_shape = [], []
    for _, dil in DIL_PATTERNS:
        out_specs.append(pl.BlockSpec((1, dil, tm // dil, gw), lambda b, i: (b, 0, i, 0)))
        out_shape.append(jax.ShapeDtypeStruct((bsz, dil, t // dil, gw), BF16))
    return pl.pallas_call(
        functools.partial(_qkv_proj_kernel, tm=tm),
        grid=(bsz, t // tm),
        in_specs=[pl.BlockSpec((1, tm, d), lambda b, i: (b, i, 0)),
                  pl.BlockSpec((1, 6, d), lambda b, i: (b, 0, 0)),
                  pl.BlockSpec(w.shape, lambda b, i: (0, 0), pipeline_mode=pl.Buffered(1))],
        out_specs=out_specs,
        out_shape=out_shape,
        scratch_shapes=[pltpu.VMEM((d // LANES, tm, LANES), F32), pltpu.VMEM((d // LANES, tm, LANES), F32)],
        compiler_params=_cparams(2),
        name="qkv_proj",
    )(x, mod, w)


def _dilated_attention_group(qkv, bias, g, *, rb=1024):
    bsz, dil, length, width = qkv.shape
    t = dil * length
    nb = length // BLOCK
    n_blk = rb // BLOCK
    assert nb % n_blk == 0 or n_blk % nb == 0
    flat = qkv.reshape(bsz, t, width)

    def cur(j):
        return lambda b, s: (b, s, j)

    def prev(j):
        return lambda b, s: (b, jnp.maximum(s * n_blk - 1, 0), j)

    blk = (None, rb, DIL_WIDTH)
    pblk = (None, BLOCK, DIL_WIDTH)
    o, lse = pl.pallas_call(
        functools.partial(_attn_kernel, nb=nb, rb=rb),
        grid=(bsz, t // rb),
        in_specs=[pl.BlockSpec(blk, cur(0)),
                  pl.BlockSpec(pblk, prev(1)), pl.BlockSpec(blk, cur(1)),
                  pl.BlockSpec(pblk, prev(2)), pl.BlockSpec(blk, cur(2)),
                  pl.BlockSpec((DIL_HEADS, BLOCK, 2 * BLOCK), lambda b, s: (0, 0, 0))],
        out_specs=[pl.BlockSpec(blk, lambda b, s: (b, s, 0)),
                   pl.BlockSpec((None, rb, LANES), lambda b, s: (b, s, 0))],
        out_shape=[jax.ShapeDtypeStruct((bsz, t, DIL_WIDTH), BF16),
                   jax.ShapeDtypeStruct((bsz, t, LANES), F32)],
        compiler_params=_cparams(2),
        name=f"dilated_attn_g{g}",
    )(flat, flat, flat, flat, flat, bias)
    return o.reshape(bsz, dil, length, DIL_WIDTH), lse.reshape(bsz, dil, length, LANES)


def _t5_bucket(dist):
    exact = N_BUCKETS // 2
    logd = jnp.log(jnp.maximum(dist, 1).astype(F32) / exact) / math.log(MAX_DISTANCE / exact)
    large = jnp.minimum(exact + (logd * (N_BUCKETS - exact)).astype(jnp.int32), N_BUCKETS - 1)
    return jnp.where(dist < exact, dist, large)


def _bias_tables(rel_bias):
    rel = jnp.arange(BLOCK + 1)
    tables = []
    for g, (window, dil) in enumerate(DIL_PATTERNS):
        span = window // dil
        bucket = _t5_bucket(jnp.clip(rel, 0, span) * dil)
        vec = rel_bias[bucket][:, g * DIL_HEADS:(g + 1) * DIL_HEADS].T.astype(F32)
        ext = jnp.concatenate([vec[:, ::-1], jnp.zeros((DIL_HEADS, BLOCK), F32)], axis=1)
        flat = jnp.tile(ext, (1, BLOCK))[:, :BLOCK * 2 * BLOCK]
        tables.append(flat.reshape(DIL_HEADS, BLOCK, 2 * BLOCK))
    qi = np.arange(BLOCK)[:, None]
    ki = np.arange(2 * BLOCK)[None, :]
    band = (ki >= qi) & (ki <= qi + BLOCK)
    return jnp.where(band, jnp.stack(tables), MASKED)


def _tail(mix_fn, wout_ref, x_ref, mod_ref, lng_ref, lnb_ref, w1_ref, w2_ref, o_ref, *, tm, n_sub, ff_chunk):
    m = mod_ref[0]
    sub = tm // n_sub
    tiles = [slice(i * sub, (i + 1) * sub) for i in range(n_sub)]
    x1s, us = [], []
    for rows in tiles:
        y = None
        for piece, k0 in mix_fn(rows):
            part = jnp.dot(piece, wout_ref[k0:k0 + piece.shape[-1], :], preferred_element_type=F32)
            y = part if y is None else y + part
        x1 = _layer_norm_rows(DEEPNORM_ALPHA * x_ref[0, rows, :] + (1.0 + m[2:3]) * y, lng_ref[0:1], lnb_ref[0:1])
        x1s.append(x1)
        us.append((x1 * (1.0 + m[4:5]) + m[3:4]).astype(BF16))
    accs = [None] * n_sub
    for c in range(D_FF // ff_chunk):
        cols = slice(c * ff_chunk, (c + 1) * ff_chunk)
        w1c = w1_ref[:, cols].astype(BF16)
        w2c = w2_ref[cols, :].astype(BF16)
        for i in range(n_sub):
            h = jnp.maximum(jnp.dot(us[i], w1c, preferred_element_type=F32), 0.0)
            part = jnp.dot((h * h).astype(BF16), w2c, preferred_element_type=F32)
            accs[i] = part if accs[i] is None else accs[i] + part
    for i, rows in enumerate(tiles):
        z = DEEPNORM_ALPHA * x1s[i] + (1.0 + m[5:6]) * accs[i]
        o_ref[0, rows, :] = _layer_norm_rows(z, lng_ref[1:2], lnb_ref[1:2])


def _even_tail_kernel(ya_ref, yb_ref, wout_ref, x_ref, mod_ref, lng_ref, lnb_ref, w1_ref, w2_ref, o_ref, **kw):
    def mix(rows):
        return [(ya_ref[0, rows, :], 0), (yb_ref[0, rows, :], ya_ref.shape[-1])]

    _tail(mix, wout_ref, x_ref, mod_ref, lng_ref, lnb_ref, w1_ref, w2_ref, o_ref, **kw)


def _odd_tail_kernel(o0_ref, o1_ref, o2_ref, l0_ref, l1_ref, l2_ref, wout_ref, x_ref, mod_ref, lng_ref, lnb_ref,
                     w1_ref, w2_ref, o_ref, o_nat, l_nat, mix_s, *, tm, **kw):
    def natural(src_ref, scr, dil):
        if dil == 1:
            return src_ref[0, 0].astype(F32)
        n_col = src_ref.shape[-1] // LANES
        for c in range(n_col):
            for r in range(dil):
                scr[c, pl.ds(r, tm // dil, stride=dil), :] = src_ref[0, r, :, c * LANES:(c + 1) * LANES].astype(F32)
        return jnp.concatenate([scr[c] for c in range(n_col)], axis=1)

    dils = [dil for _, dil in DIL_PATTERNS]
    lses = [natural(ref, l_nat, dil) for ref, dil in zip((l0_ref, l1_ref, l2_ref), dils)]
    mx = jnp.maximum(jnp.maximum(lses[0], lses[1]), lses[2])
    es = [jnp.exp(l - mx) for l in lses]
    inv = 1.0 / (es[0] + es[1] + es[2])
    hr = lax.broadcasted_iota(jnp.int32, (LANES, DIL_WIDTH), 0)
    hc = lax.broadcasted_iota(jnp.int32, (LANES, DIL_WIDTH), 1) // DIL_HEAD_DIM
    expand = jnp.where(hr == hc, 1.0, 0.0).astype(BF16)

    def widen(wt):
        return jnp.dot(wt.astype(BF16), expand, preferred_element_type=F32)

    merged = None
    for e, ref, dil in zip(es, (o0_ref, o1_ref, o2_ref), dils):
        term = widen(e * inv) * natural(ref, o_nat, dil)
        merged = term if merged is None else merged + term
    mix_s[...] = merged.astype(BF16)
    _tail(lambda rows: [(mix_s[rows, :], 0)], wout_ref, x_ref, mod_ref, lng_ref, lnb_ref, w1_ref, w2_ref, o_ref,
          tm=tm, **kw)


def _layer_tail(mix, w_out, x, mod, ln_g, ln_b, w1, w2, layer, *, tm=512, n_sub=2, ff_chunk=1024):
    bsz, t, d = x.shape

    def const(shape):
        return pl.BlockSpec(shape, lambda b, i: (0,) * len(shape), pipeline_mode=pl.Buffered(1))

    def of_layer(shape):
        return pl.BlockSpec((None,) + shape, lambda b, i: (layer,) + (0,) * len(shape), pipeline_mode=pl.Buffered(1))

    tail_specs = [const(w_out.shape),
                  pl.BlockSpec((1, tm, d), lambda b, i: (b, i, 0)),
                  pl.BlockSpec((1, 6, d), lambda b, i: (b, 0, 0)),
                  of_layer((2, d)), of_layer((2, d)), of_layer((d, D_FF)), of_layer((D_FF, d))]
    tail_args = (w_out, x, mod, ln_g, ln_b, w1, w2)
    kw = dict(tm=tm, n_sub=n_sub, ff_chunk=ff_chunk)
    if isinstance(mix, tuple):
        outs, lses = mix
        mix_specs = ([pl.BlockSpec((1, dil, tm // dil, DIL_WIDTH), lambda b, i: (b, 0, i, 0)) for _, dil in DIL_PATTERNS]
                     + [pl.BlockSpec((1, dil, tm // dil, LANES), lambda b, i: (b, 0, i, 0)) for _, dil in DIL_PATTERNS])
        mix_args = (*outs, *lses)
        body = functools.partial(_odd_tail_kernel, **kw)
        scratch = [pltpu.VMEM((DIL_WIDTH // LANES, tm, LANES), F32), pltpu.VMEM((1, tm, LANES), F32),
                   pltpu.VMEM((tm, DIL_WIDTH), BF16)]
    else:
        mix_specs = [pl.BlockSpec((1, tm, piece.shape[-1]), lambda b, i: (b, i, 0)) for piece in mix]
        mix_args = tuple(mix)
        body = functools.partial(_even_tail_kernel, **kw)
        scratch = []
    return pl.pallas_call(
        body,
        grid=(bsz, t // tm),
        in_specs=mix_specs + tail_specs,
        out_specs=pl.BlockSpec((1, tm, d), lambda b, i: (b, i, 0)),
        out_shape=jax.ShapeDtypeStruct((bsz, t, d), F32),
        scratch_shapes=scratch,
        compiler_params=_cparams(2, TAIL_VMEM_LIMIT),
        name="layer_tail",
    )(*mix_args, *tail_args)


def _prep_even_weights(w_in, mu, w_up, a_up, g_up):
    w3 = 3 * RW_WIDTH
    pad = LR_PAD - (RW_PROJ - w3)
    d = w_in.shape[0]
    w16 = w_in.astype(BF16)
    w_perm = (w16[:, :w3], jnp.concatenate([w16[:, w3:RW_PROJ], jnp.zeros((d, pad), BF16)], axis=1), w16[:, RW_PROJ:])
    mu_pad = jnp.concatenate([mu, jnp.zeros((pad,), mu.dtype)]).reshape(1, w3 + LR_PAD)
    zeros = jnp.zeros((RW_DECAY_RANK, RW_WIDTH), w_up.dtype)
    wwa = jnp.concatenate([jnp.concatenate([w_up, zeros], axis=1),
                           jnp.concatenate([zeros, a_up], axis=1)], axis=0).astype(BF16)
    gup = jnp.concatenate([g_up, jnp.zeros((pad, RW_WIDTH), g_up.dtype)], axis=0).astype(BF16)
    return w_perm, mu_pad, wwa, gup


def kernel(x, c, ada_w, ada_b, ln_g, ln_b, ab_w_in, rw_mu, rw_w0, rw_w_up, rw_a0, rw_a_up, rw_g_up,
           rw_k_k, rw_k_a, rw_r_k, rw_lnx_g, rw_lnx_b, sc_conv_w, ab_w_out, dil_w_qkv, dil_w_out,
           rel_bias, mlp_w1, mlp_w2):
    bsz, t, d = x.shape
    mods = _adaln(c, ada_w, ada_b).reshape(DEPTH, bsz, 6, d)
    bias = None
    row = lambda a: a.reshape(1, -1)
    w1_all, w2_all = mlp_w1, mlp_w2
    for i in range(DEPTH):
        mod = mods[i]
        j = i // 2
        if i % 2 == 0:
            w_perm, mu_pad, wwa, gup = _prep_even_weights(
                ab_w_in[j], rw_mu[j], rw_w_up[j], rw_a_up[j], rw_g_up[j])
            rkv, lr, yb = _ab_proj(x, mod, w_perm, mu_pad, sc_conv_w[j])
            ya = _rwkv(rkv, lr, row(rw_w0[j]), row(rw_a0[j]), wwa, gup,
                       row(rw_k_k[j]), row(rw_k_a[j]), row(rw_r_k[j]),
                       row(rw_lnx_g[j]), row(rw_lnx_b[j]))
            mix, w_out = [ya, yb], ab_w_out[j]
        else:
            if bias is None:
                bias = _bias_tables(rel_bias)
            qkvs = _qkv_proj(x, mod, dil_w_qkv[j].astype(BF16))
            outs, lses = [], []
            for g in range(N_GROUPS):
                o, lse = _dilated_attention_group(qkvs[g], bias[g], g)
                outs.append(o)
                lses.append(lse)
            mix, w_out = (outs, lses), dil_w_out[j]
        x = _layer_tail(mix, w_out.astype(BF16), x, mod, ln_g, ln_b, w1_all, w2_all, i)
    return x
```

```python
import functools
import math

import jax
import jax.numpy as jnp
import numpy as np
from jax import lax
from jax.experimental import pallas as pl
from jax.experimental.pallas import tpu as pltpu

F32 = jnp.float32
BF16 = jnp.bfloat16

D_MODEL = 1024
DEPTH = 2
RW_HEADS = 8
RW_HEAD_DIM = 64
RW_WIDTH = 512
RW_DECAY_RANK = 64
RW_ICLR_RANK = 64
RW_GATE_RANK = 160
RW_GN_EPS = 64e-5
RW_PROJ = 3 * RW_WIDTH + RW_DECAY_RANK + RW_ICLR_RANK + RW_GATE_RANK
SC_WIDTH = 512
SC_CONV = 3
DIL_PATTERNS = ((128, 1), (512, 4), (2048, 16))
N_GROUPS = 3
DIL_HEADS = 8
DIL_HEAD_DIM = 64
DIL_WIDTH = 512
BLOCK = 128
N_BUCKETS = 32
MAX_DISTANCE = 2048
D_FF = 4 * D_MODEL
DEEPNORM_ALPHA = (2 * DEPTH) ** 0.25
LN_EPS = 1e-5

LANES = 128
MXU_DIM = 256
CHUNK = 64
INV_BLOCK = 16
LR_PAD = 384
VMEM_LIMIT = 56 * 1024 * 1024
TAIL_VMEM_LIMIT = 60 * 1024 * 1024
MASKED = -1e30


def _cparams(n_axes, vmem_limit=VMEM_LIMIT):
    return pltpu.CompilerParams(dimension_semantics=("arbitrary",) * n_axes,
                                vmem_limit_bytes=vmem_limit)


def _sigmoid(x):
    return 1.0 / (1.0 + jnp.exp(-x))


def _dot(a, b):
    return jnp.dot(a.astype(BF16), b.astype(BF16), preferred_element_type=F32)


def _layer_norm_rows(z, g, b):
    mu = jnp.mean(z, axis=-1, keepdims=True)
    zc = z - mu
    var = jnp.mean(zc * zc, axis=-1, keepdims=True)
    return zc * lax.rsqrt(var + LN_EPS) * g + b


def _adaln_kernel(c_ref, w_ref, b_ref, o_ref):
    c = c_ref[...]
    cond = c * _sigmoid(c)
    o_ref[0] = _dot(cond, w_ref[0]) + b_ref[0]


def _adaln(c, ada_w, ada_b):
    depth, d, n = ada_w.shape
    bsz = c.shape[0]
    tn = 3072
    return pl.pallas_call(
        _adaln_kernel,
        grid=(depth, n // tn),
        in_specs=[pl.BlockSpec((bsz, d), lambda i, j: (0, 0)),
                  pl.BlockSpec((1, d, tn), lambda i, j: (i, 0, j)),
                  pl.BlockSpec((1, 1, tn), lambda i, j: (i, 0, j))],
        out_specs=pl.BlockSpec((1, bsz, tn), lambda i, j: (i, 0, j)),
        out_shape=jax.ShapeDtypeStruct((depth, bsz, n), F32),
        compiler_params=_cparams(2),
        name="adaln",
    )(c, ada_w, ada_b.reshape(depth, 1, n))


def _shift_rows(x, carry, k):
    rolled = pltpu.roll(x, k, 0)
    head = rolled[:8]
    row = lax.broadcasted_iota(jnp.int32, head.shape, 0)
    for i in range(k):
        head = jnp.where(row == i, carry[8 - k + i:8 - k + i + 1], head)
    return jnp.concatenate([head, rolled[8:]], axis=0)


def _ab_proj_kernel(x_ref, mod_ref, w_rkv_ref, w_lr_ref, w_cv_ref, mu_ref, convw_ref, rkv_ref, lr_ref, yb_ref,
                    carry_p, carry_z, *, tm):
    @pl.when(pl.program_id(1) == 0)
    def _():
        carry_p[...] = jnp.zeros_like(carry_p)
        carry_z[...] = jnp.zeros_like(carry_z)

    m = mod_ref[0]
    u = (x_ref[0] * (1.0 + m[1:2]) + m[0:1]).astype(BF16)

    def proj(w_ref, c0, width):
        return jnp.dot(u, w_ref[:, c0:c0 + width], preferred_element_type=F32)

    def mixed(w_ref, c0, m0, width):
        p = proj(w_ref, c0, width)
        prev = _shift_rows(p, carry_p[:, m0:m0 + width], 1)
        carry_p[:, m0:m0 + width] = p[tm - 8:]
        return (p + mu_ref[:, m0:m0 + width] * (prev - p)).astype(BF16)

    for c in range(3):
        cols = slice(c * RW_WIDTH, (c + 1) * RW_WIDTH)
        rkv_ref[0, :, cols] = mixed(w_rkv_ref, c * RW_WIDTH, c * RW_WIDTH, RW_WIDTH)
    lr_ref[0] = mixed(w_lr_ref, 0, 3 * RW_WIDTH, LR_PAD)

    h = proj(w_cv_ref, 0, SC_WIDTH)
    z = proj(w_cv_ref, 2 * SC_WIDTH, SC_WIDTH) * h
    zc = carry_z[...]
    cw = convw_ref[...]
    conv = cw[2:3] * z + cw[1:2] * _shift_rows(z, zc, 1) + cw[0:1] * _shift_rows(z, zc, 2)
    carry_z[...] = z[tm - 8:]
    yb_ref[0] = (proj(w_cv_ref, SC_WIDTH, SC_WIDTH) * conv).astype(BF16)


def _ab_proj(x, mod, w, mu, conv_w, *, tm=1024):
    bsz, t, d = x.shape
    w3 = 3 * RW_WIDTH

    def const(shape):
        return pl.BlockSpec(shape, lambda b, i: (0,) * len(shape), pipeline_mode=pl.Buffered(1))

    def rows(width):
        return pl.BlockSpec((1, tm, width), lambda b, i: (b, i, 0))

    return pl.pallas_call(
        functools.partial(_ab_proj_kernel, tm=tm),
        grid=(bsz, t // tm),
        in_specs=[rows(d), pl.BlockSpec((1, 6, d), lambda b, i: (b, 0, 0)),
                  *[const(part.shape) for part in w], const((1, w3 + LR_PAD)), const((SC_CONV, SC_WIDTH))],
        out_specs=[rows(w3), rows(LR_PAD), rows(SC_WIDTH)],
        out_shape=[jax.ShapeDtypeStruct((bsz, t, w3), BF16), jax.ShapeDtypeStruct((bsz, t, LR_PAD), BF16),
                   jax.ShapeDtypeStruct((bsz, t, SC_WIDTH), BF16)],
        scratch_shapes=[pltpu.VMEM((8, w3 + LR_PAD), F32), pltpu.VMEM((8, SC_WIDTH), F32)],
        compiler_params=_cparams(2),
        name="ab_proj",
    )(x, mod, *w, mu, conv_w)


def _rwkv_chunk_terms(a_t, r_t, b_t, k_t, b_e, k_e, v, p_end):
    n = len(a_t)
    idx = range(n)
    row = lax.broadcasted_iota(jnp.int32, (CHUNK, LANES), 0)
    col = lax.broadcasted_iota(jnp.int32, (CHUNK, LANES), 1)
    s_idx = col % CHUNK
    strict = s_idx < row
    incl = s_idx <= row
    near = strict & ((s_idx // INV_BLOCK) == (row // INV_BLOCK))
    far = strict & ((s_idx // INV_BLOCK) != (row // INV_BLOCK))
    eye = jnp.where(s_idx == row, 1.0, 0.0)
    r2 = lax.broadcasted_iota(jnp.int32, (LANES, 2 * LANES), 0)
    c2 = lax.broadcasted_iota(jnp.int32, (LANES, 2 * LANES), 1)
    bdmask2 = (r2 // RW_HEAD_DIM) == ((c2 % LANES) // RW_HEAD_DIM)
    eye_bd = jnp.where(lax.broadcasted_iota(jnp.int32, (LANES, LANES), 0)
                       == lax.broadcasted_iota(jnp.int32, (LANES, LANES), 1), 1.0, 0.0)
    zero16 = jnp.zeros((CHUNK, LANES), BF16)
    lane1 = lax.broadcasted_iota(jnp.int32, (1, LANES), 1)
    m0_16 = jnp.where(lane1 < RW_HEAD_DIM, 1.0, 0.0).astype(BF16)
    m1_16 = jnp.where(lane1 < RW_HEAD_DIM, 0.0, 1.0).astype(BF16)

    def cast(xs):
        return [x.astype(BF16) for x in xs]

    def bd(q16):
        return jnp.concatenate([q16 * m0_16, q16 * m1_16], axis=0)

    def mm(p16, q16):
        return jnp.dot(p16, q16, preferred_element_type=F32)

    def mm_nt(p16, q16):
        return lax.dot_general(p16, q16, (((1,), (1,)), ((), ())), preferred_element_type=F32)

    a16, r16, v16 = cast(a_t), cast(r_t), cast(v)
    lhs = [jnp.concatenate([a16[i], r16[i]], axis=0) for i in idx]
    b16, k16 = cast(b_t), cast(k_t)
    bdv = [bd(x) for x in v16]
    g = [mm_nt(lhs[i], jnp.concatenate([bd(b16[i]), bd(k16[i])], axis=0)) for i in idx]
    x1 = [jnp.where(near, g[i][:CHUNK, :LANES], 0.0) for i in idx]
    x1_16 = cast(x1)
    l_e16 = [jnp.where(far, g[i][:CHUNK, :LANES], 0.0).astype(BF16) for i in idx]
    a_k16 = [jnp.concatenate([jnp.where(strict, g[i][:CHUNK, LANES:], 0.0),
                              jnp.where(incl, g[i][CHUNK:, LANES:], 0.0)], axis=0).astype(BF16) for i in idx]
    a_rb16 = [jnp.where(incl, g[i][CHUNK:, :LANES], 0.0).astype(BF16) for i in idx]

    def both(p16, q16):
        return jnp.concatenate([bd(p16), bd(q16)], axis=1)

    x2_16 = cast([mm(x1_16[i], bd(x1_16[i])) for i in idx])
    acc = [eye + x1[i] for i in idx]
    sq = [mm(x2_16[i], both(x2_16[i], acc[i].astype(BF16))) for i in idx]
    x4_16 = [sq[i][:, :LANES].astype(BF16) for i in idx]
    acc = [acc[i] + sq[i][:, LANES:] for i in idx]
    sq = [mm(x4_16[i], both(x4_16[i], acc[i].astype(BF16))) for i in idx]
    x8_16 = [sq[i][:, :LANES].astype(BF16) for i in idx]
    acc = [acc[i] + sq[i][:, LANES:] for i in idx]
    dinv = [acc[i] + mm(x8_16[i], bd(acc[i].astype(BF16))) for i in idx]
    dinv16 = cast(dinv)
    f16 = cast([mm(dinv16[i], bd(l_e16[i])) for i in idx])
    sq = [mm(f16[i], both(f16[i], dinv16[i])) for i in idx]
    f2_16 = [sq[i][:, :LANES].astype(BF16) for i in idx]
    hmat = [dinv[i] + sq[i][:, LANES:] for i in idx]
    tinv16 = cast([hmat[i] + mm(f2_16[i], bd(hmat[i].astype(BF16))) for i in idx])

    kv = [mm(a_k16[i], bdv[i]) for i in idx]
    wv16 = [kv[i][:CHUNK].astype(BF16) for i in idx]
    aw = [mm(tinv16[i], both(a16[i], wv16[i])) for i in idx]
    aw16 = cast(aw)
    qy = [mm(a_rb16[i], both(aw16[i][:, :LANES], aw16[i][:, LANES:]))
          + jnp.concatenate([r_t[i], kv[i][CHUNK:]], axis=1) for i in idx]
    be_ke_t = [jnp.concatenate([b_e[i], k_e[i]], axis=0).T.astype(BF16) for i in idx]
    rhs = [jnp.concatenate([aw16[i], jnp.concatenate([zero16, v16[i]], axis=1)], axis=0) for i in idx]
    mn = [jnp.where(bdmask2, mm(be_ke_t[i], rhs[i]), 0.0) for i in idx]
    qhat = [x[:, :LANES] for x in qy]
    yhat = [x[:, LANES:] for x in qy]
    mmat = [eye_bd * p_end[i] + mn[i][:, :LANES] for i in idx]
    nmat = [x[:, LANES:] for x in mn]
    return qhat, yhat, mmat, nmat


def _rwkv_kernel(rkv_ref, lr_ref, w0_ref, a0_ref, wwa_ref, gup_ref,
                 kk_ref, ka_ref, rk_ref, lnxg_ref, lnxb_ref, o_ref, state, yout_s, *, tb):
    n_chunks = tb // CHUNK
    n_pairs = RW_WIDTH // LANES

    @pl.when(pl.program_id(1) == 0)
    def _():
        state[...] = jnp.zeros_like(state)

    r = rkv_ref[0, :, :RW_WIDTH].astype(F32)
    k = rkv_ref[0, :, RW_WIDTH:2 * RW_WIDTH].astype(F32)
    v = rkv_ref[0, :, 2 * RW_WIDTH:].astype(F32)
    lr_m = lr_ref[0].astype(F32)

    wa = lr_m[:, :LANES]
    lane = lax.broadcasted_iota(jnp.int32, wa.shape, 1)
    wa = jnp.where(lane < RW_DECAY_RANK, jnp.tanh(wa), wa)
    wa_up = _dot(wa, wwa_ref[...])
    ld = -math.exp(-0.5) * _sigmoid(w0_ref[...] + wa_up[:, :RW_WIDTH])
    iclr = _sigmoid(a0_ref[...] + wa_up[:, RW_WIDTH:])
    gate = _dot(_sigmoid(lr_m[:, LANES:]), gup_ref[...])

    hr = lax.broadcasted_iota(jnp.int32, (MXU_DIM, MXU_DIM), 0) // RW_HEAD_DIM
    hc = lax.broadcasted_iota(jnp.int32, (MXU_DIM, MXU_DIM), 1) // RW_HEAD_DIM
    seg = jnp.where(hr == hc, 1.0 / RW_HEAD_DIM, 0.0).astype(BF16)

    def seg_mean(x):
        x16 = x.astype(BF16)
        return jnp.concatenate([jnp.dot(x16[:, c:c + MXU_DIM], seg, preferred_element_type=F32)
                                for c in range(0, RW_WIDTH, MXU_DIM)], axis=1)

    kk = k * kk_ref[...]
    ss = seg_mean(kk * kk) * RW_HEAD_DIM
    kk = kk * lax.rsqrt(jnp.maximum(ss, 1e-24))
    kh = k * (1.0 + (iclr - 1.0) * ka_ref[...])

    tr = lax.broadcasted_iota(jnp.int32, (MXU_DIM, MXU_DIM), 0)
    tc = lax.broadcasted_iota(jnp.int32, (MXU_DIM, MXU_DIM), 1)
    tri = jnp.where((tc <= tr) & (tc // CHUNK == tr // CHUNK), 1.0, 0.0).astype(BF16)
    ld_hi = ld.astype(BF16)
    ld_lo = (ld - ld_hi.astype(F32)).astype(BF16)
    lp = jnp.concatenate(
        [jnp.dot(tri, ld_hi[t0:t0 + MXU_DIM], preferred_element_type=F32)
         + jnp.dot(tri, ld_lo[t0:t0 + MXU_DIM], preferred_element_type=F32)
         for t0 in range(0, tb, MXU_DIM)], axis=0)

    a_t, r_t, b_t, k_t, b_e, k_e, vs, p_end = [], [], [], [], [], [], [], []
    b = kk * iclr
    for c in range(n_chunks):
        rs = slice(c * CHUNK, (c + 1) * CHUNK)
        lp_c = lp[rs]
        lp_last = lp_c[CHUNK - 1:CHUNK]
        e_in = jnp.exp(lp_c)
        e_inv = jnp.exp(-lp_c)
        e_end = jnp.exp(lp_last - lp_c)
        a_c = -kk[rs] * jnp.exp(lp_c - ld[rs])
        r_c = r[rs] * e_in
        b_c = b[rs] * e_inv
        k_c = kh[rs] * e_inv
        be_c = b[rs] * e_end
        ke_c = kh[rs] * e_end
        pe_c = jnp.exp(lp_last)
        for p in range(n_pairs):
            cs = slice(p * LANES, (p + 1) * LANES)
            a_t.append(a_c[:, cs])
            r_t.append(r_c[:, cs])
            b_t.append(b_c[:, cs])
            k_t.append(k_c[:, cs])
            b_e.append(be_c[:, cs])
            k_e.append(ke_c[:, cs])
            vs.append(v[rs, cs])
            p_end.append(pe_c[:, cs])

    qhat, yhat, mmat, nmat = _rwkv_chunk_terms(a_t, r_t, b_t, k_t, b_e, k_e, vs, p_end)

    s = [state[p] for p in range(n_pairs)]
    for c in range(n_chunks):
        ym = [_dot(jnp.concatenate([qhat[c * n_pairs + p], mmat[c * n_pairs + p]], axis=0), s[p])
              for p in range(n_pairs)]
        for p in range(n_pairs):
            i = c * n_pairs + p
            yout_s[c * CHUNK:(c + 1) * CHUNK, p * LANES:(p + 1) * LANES] = ym[p][:CHUNK] + yhat[i]
            s[p] = ym[p][CHUNK:] + nmat[i]
    for p in range(n_pairs):
        state[p] = s[p]

    y = yout_s[...]
    mean = seg_mean(y)
    yc = y - mean
    var = seg_mean(yc * yc)
    y = yc * lax.rsqrt(var + RW_GN_EPS) * lnxg_ref[...] + lnxb_ref[...]
    bonus = seg_mean(r * kh * rk_ref[...]) * RW_HEAD_DIM * v
    o_ref[0] = ((y + bonus) * gate).astype(o_ref.dtype)


def _rwkv(rkv, lr, w0, a0, wwa, gup, k_k, k_a, r_k, lnx_g, lnx_b, *, tb=512):
    bsz, t, w3 = rkv.shape
    n_pairs = RW_WIDTH // LANES

    def full(shape):
        return pl.BlockSpec(shape, lambda b, i: (0,) * len(shape))

    return pl.pallas_call(
        functools.partial(_rwkv_kernel, tb=tb),
        grid=(bsz, t // tb),
        in_specs=[pl.BlockSpec((1, tb, w3), lambda b, i: (b, i, 0)),
                  pl.BlockSpec((1, tb, LR_PAD), lambda b, i: (b, i, 0)),
                  full((1, RW_WIDTH)), full((1, RW_WIDTH)),
                  full((LANES, 2 * RW_WIDTH)), full((LR_PAD - LANES, RW_WIDTH)),
                  full((1, RW_WIDTH)), full((1, RW_WIDTH)), full((1, RW_WIDTH)),
                  full((1, RW_WIDTH)), full((1, RW_WIDTH))],
        out_specs=pl.BlockSpec((1, tb, RW_WIDTH), lambda b, i: (b, i, 0)),
        out_shape=jax.ShapeDtypeStruct((bsz, t, RW_WIDTH), BF16),
        scratch_shapes=[pltpu.VMEM((n_pairs, LANES, LANES), F32),
                        pltpu.VMEM((tb, RW_WIDTH), F32)],
        compiler_params=_cparams(2),
        name="rwkv",
    )(rkv, lr, w0, a0, wwa, gup, k_k, k_a, r_k, lnx_g, lnx_b)


def _attn_kernel(q_ref, kp_ref, kc_ref, vp_ref, vc_ref, bias_ref, o_ref, lse_ref, *, nb, rb):
    step = pl.program_id(1)
    n_blk = rb // BLOCK
    scale = DIL_HEAD_DIM ** -0.5
    n_pairs = DIL_WIDTH // LANES
    krow = lax.broadcasted_iota(jnp.int32, (2 * BLOCK, LANES), 0) // BLOCK
    klane = lax.broadcasted_iota(jnp.int32, (2 * BLOCK, LANES), 1) // DIL_HEAD_DIM
    own = krow == klane
    zeros = jnp.zeros((2 * BLOCK, LANES), BF16)
    ones_bd = jnp.where(own, 1.0, 0.0).astype(BF16)
    lane_o = lax.broadcasted_iota(jnp.int32, (BLOCK, LANES), 1)

    def stack_bd(x):
        return jnp.where(own, jnp.concatenate([x, x], axis=0), zeros)

    def attend(blocks):
        units = [(rows, q, parts, p) for rows, q, parts in blocks for p in range(n_pairs)]
        lanes = [slice(p * LANES, (p + 1) * LANES) for p in range(n_pairs)]
        scores = []
        for _, q, parts, p in units:
            qs = q[:, lanes[p]] * scale
            row = []
            for k_blk, _, col0, pen in parts:
                s = lax.dot_general(qs, stack_bd(k_blk[:, lanes[p]]), (((1,), (1,)), ((), ())),
                                    preferred_element_type=F32)
                bias = jnp.concatenate([bias_ref[2 * p, :, col0:col0 + BLOCK],
                                        bias_ref[2 * p + 1, :, col0:col0 + BLOCK]], axis=1)
                s = s + bias
                row.append(s if pen is None else s + pen)
            scores.append(row)
        ms = []
        for u in range(len(units)):
            halves = []
            for hh in range(2):
                cols = slice(hh * BLOCK, (hh + 1) * BLOCK)
                m = jnp.max(scores[u][0][:, cols], axis=-1, keepdims=True)
                for s in scores[u][1:]:
                    m = jnp.maximum(m, jnp.max(s[:, cols], axis=-1, keepdims=True))
                halves.append(m)
            ms.append(halves)
        es = [[jnp.exp((s - jnp.concatenate([jnp.broadcast_to(ms[u][0], (BLOCK, BLOCK)),
                                             jnp.broadcast_to(ms[u][1], (BLOCK, BLOCK))], axis=1)).astype(BF16))
               for s in scores[u]] for u in range(len(units))]
        res = []
        for u, (_, _, parts, p) in enumerate(units):
            res.append(sum(jnp.dot(e, jnp.concatenate([stack_bd(part[1][:, lanes[p]]), ones_bd], axis=1),
                                   preferred_element_type=F32) for e, part in zip(es[u], parts)))
        for b, (rows, _, _) in enumerate(blocks):
            m_all = jnp.zeros((BLOCK, LANES), F32)
            den_all = jnp.ones((BLOCK, LANES), F32)
            for p in range(n_pairs):
                u = b * n_pairs + p
                num, den = res[u][:, :LANES], res[u][:, LANES:]
                o_ref[rows, lanes[p]] = (num / den).astype(o_ref.dtype)
                m_all = jnp.where(lane_o == 2 * p, ms[u][0], jnp.where(lane_o == 2 * p + 1, ms[u][1], m_all))
                den_all = jnp.where((lane_o == 2 * p), den,
                                    jnp.where(lane_o == 2 * p + 1, pltpu.roll(den, DIL_HEAD_DIM, 1), den_all))
            lse_ref[rows, :] = m_all + jnp.log(den_all)

    blocks = []
    for j in range(n_blk):
        rows = slice(j * BLOCK, (j + 1) * BLOCK)
        cur = (kc_ref[rows, :], vc_ref[rows, :], BLOCK, None)
        if n_blk % nb == 0:
            prev = "none" if j % nb == 0 else "inside"
        else:
            prev = "inside" if j > 0 else ("none" if nb == n_blk else "before")
        if prev == "none":
            parts = [cur]
        elif prev == "inside":
            before = slice((j - 1) * BLOCK, j * BLOCK)
            parts = [(kc_ref[before, :], vc_ref[before, :], 0, None), cur]
        else:
            has_prev = (step % (nb // n_blk)) != 0
            parts = [(kp_ref[...], vp_ref[...], 0, jnp.where(has_prev, 0.0, MASKED)), cur]
        blocks.append((rows, q_ref[rows, :], parts))
    attend(blocks)


def _qkv_proj_kernel(x_ref, mod_ref, w_ref, o0_ref, o1_ref, o2_ref, xs_ref, xs4_ref, *, tm):
    m = mod_ref[0]
    scale = 1.0 + m[1:2]
    shift = m[0:1]
    n_col = x_ref.shape[-1] // LANES
    gw = 3 * DIL_WIDTH
    step = DIL_PATTERNS[1][1]
    assert [dil for _, dil in DIL_PATTERNS] == [1, step, step * step]

    for c in range(n_col):
        xs_ref[c] = x_ref[0, :, c * LANES:(c + 1) * LANES]
    rows4 = tm // step
    x4 = [jnp.concatenate([xs_ref[c, pl.ds(a, rows4, stride=step), :] for a in range(step)], axis=0)
          for c in range(n_col)]
    for c in range(n_col):
        xs4_ref[c] = x4[c]
    rows16 = rows4 // step
    x16 = [jnp.concatenate([xs4_ref[c, pl.ds(a * rows4 + b, rows16, stride=step), :]
                            for a in range(step) for b in range(step)], axis=0) for c in range(n_col)]
    inputs = [x_ref[0], jnp.concatenate(x4, axis=1), jnp.concatenate(x16, axis=1)]
    residue_of_group = [[0], list(range(step)), [a + step * b for a in range(step) for b in range(step)]]

    for g, o_ref in enumerate((o0_ref, o1_ref, o2_ref)):
        u = (inputs[g] * scale + shift).astype(BF16)
        res = jnp.dot(u, w_ref[:, g * gw:(g + 1) * gw].astype(BF16), preferred_element_type=F32).astype(BF16)
        rows = tm // len(residue_of_group[g])
        for grp, r in enumerate(residue_of_group[g]):
            o_ref[0, r] = res[grp * rows:(grp + 1) * rows]


def _qkv_proj(x, mod, w, *, tm=1024):
    bsz, t, d = x.shape
    gw = 3 * DIL_WIDTH
    out_specs, out_shape = [], []
    for _, dil in DIL_PATTERNS:
        out_specs.append(pl.BlockSpec((1, dil, tm // dil, gw), lambda b, i: (b, 0, i, 0)))
        out_shape.append(jax.ShapeDtypeStruct((bsz, dil, t // dil, gw), BF16))
    return pl.pallas_call(
        functools.partial(_qkv_proj_kernel, tm=tm),
        grid=(bsz, t // tm),
        in_specs=[pl.BlockSpec((1, tm, d), lambda b, i: (b, i, 0)),
                  pl.BlockSpec((1, 6, d), lambda b, i: (b, 0, 0)),
                  pl.BlockSpec(w.shape, lambda b, i: (0, 0), pipeline_mode=pl.Buffered(1))],
        out_specs=out_specs,
        out_shape=out_shape,
        scratch_shapes=[pltpu.VMEM((d // LANES, tm, LANES), F32), pltpu.VMEM((d // LANES, tm, LANES), F32)],
        compiler_params=_cparams(2),
        name="qkv_proj",
    )(x, mod, w)


def _dilated_attention_group(qkv, bias, g, *, rb=1024):
    bsz, dil, length, width = qkv.shape
    t = dil * length
    nb = length // BLOCK
    n_blk = rb // BLOCK
    assert nb % n_blk == 0 or n_blk % nb == 0
    flat = qkv.reshape(bsz, t, width)

    def cur(j):
        return lambda b, s: (b, s, j)

    def prev(j):
        return lambda b, s: (b, jnp.maximum(s * n_blk - 1, 0), j)

    blk = (None, rb, DIL_WIDTH)
    pblk = (None, BLOCK, DIL_WIDTH)
    o, lse = pl.pallas_call(
        functools.partial(_attn_kernel, nb=nb, rb=rb),
        grid=(bsz, t // rb),
        in_specs=[pl.BlockSpec(blk, cur(0)),
                  pl.BlockSpec(pblk, prev(1)), pl.BlockSpec(blk, cur(1)),
                  pl.BlockSpec(pblk, prev(2)), pl.BlockSpec(blk, cur(2)),
                  pl.BlockSpec((DIL_HEADS, BLOCK, 2 * BLOCK), lambda b, s: (0, 0, 0))],
        out_specs=[pl.BlockSpec(blk, lambda b, s: (b, s, 0)),
                   pl.BlockSpec((None, rb, LANES), lambda b, s: (b, s, 0))],
        out_shape=[jax.ShapeDtypeStruct((bsz, t, DIL_WIDTH), BF16),
                   jax.ShapeDtypeStruct((bsz, t, LANES), F32)],
        compiler_params=_cparams(2),
        name=f"dilated_attn_g{g}",
    )(flat, flat, flat, flat, flat, bias)
    return o.reshape(bsz, dil, length, DIL_WIDTH), lse.reshape(bsz, dil, length, LANES)


def _t5_bucket(dist):
    exact = N_BUCKETS // 2
    logd = jnp.log(jnp.maximum(dist, 1).astype(F32) / exact) / math.log(MAX_DISTANCE / exact)
    large = jnp.minimum(exact + (logd * (N_BUCKETS - exact)).astype(jnp.int32), N_BUCKETS - 1)
    return jnp.where(dist < exact, dist, large)


def _bias_tables(rel_bias):
    rel = jnp.arange(BLOCK + 1)
    tables = []
    for g, (window, dil) in enumerate(DIL_PATTERNS):
        span = window // dil
        bucket = _t5_bucket(jnp.clip(rel, 0, span) * dil)
        vec = rel_bias[bucket][:, g * DIL_HEADS:(g + 1) * DIL_HEADS].T.astype(F32)
        ext = jnp.concatenate([vec[:, ::-1], jnp.zeros((DIL_HEADS, BLOCK), F32)], axis=1)
        flat = jnp.tile(ext, (1, BLOCK))[:, :BLOCK * 2 * BLOCK]
        tables.append(flat.reshape(DIL_HEADS, BLOCK, 2 * BLOCK))
    qi = np.arange(BLOCK)[:, None]
    ki = np.arange(2 * BLOCK)[None, :]
    band = (ki >= qi) & (ki <= qi + BLOCK)
    return jnp.where(band, jnp.stack(tables), MASKED)


def _tail(mix_fn, wout_ref, x_ref, mod_ref, lng_ref, lnb_ref, w1_ref, w2_ref, o_ref, *, tm, n_sub, ff_chunk):
    m = mod_ref[0]
    sub = tm // n_sub
    tiles = [slice(i * sub, (i + 1) * sub) for i in range(n_sub)]
    x1s, us = [], []
    for rows in tiles:
        y = None
        for piece, k0 in mix_fn(rows):
            part = jnp.dot(piece, wout_ref[k0:k0 + piece.shape[-1], :], preferred_element_type=F32)
            y = part if y is None else y + part
        x1 = _layer_norm_rows(DEEPNORM_ALPHA * x_ref[0, rows, :] + (1.0 + m[2:3]) * y, lng_ref[0:1], lnb_ref[0:1])
        x1s.append(x1)
        us.append((x1 * (1.0 + m[4:5]) + m[3:4]).astype(BF16))
    accs = [None] * n_sub
    for c in range(D_FF // ff_chunk):
        cols = slice(c * ff_chunk, (c + 1) * ff_chunk)
        w1c = w1_ref[:, cols].astype(BF16)
        w2c = w2_ref[cols, :].astype(BF16)
        for i in range(n_sub):
            h = jnp.maximum(jnp.dot(us[i], w1c, preferred_element_type=F32), 0.0)
            part = jnp.dot((h * h).astype(BF16), w2c, preferred_element_type=F32)
            accs[i] = part if accs[i] is None else accs[i] + part
    for i, rows in enumerate(tiles):
        z = DEEPNORM_ALPHA * x1s[i] + (1.0 + m[5:6]) * accs[i]
        o_ref[0, rows, :] = _layer_norm_rows(z, lng_ref[1:2], lnb_ref[1:2])


def _even_tail_kernel(ya_ref, yb_ref, wout_ref, x_ref, mod_ref, lng_ref, lnb_ref, w1_ref, w2_ref, o_ref, **kw):
    def mix(rows):
        return [(ya_ref[0, rows, :], 0), (yb_ref[0, rows, :], ya_ref.shape[-1])]

    _tail(mix, wout_ref, x_ref, mod_ref, lng_ref, lnb_ref, w1_ref, w2_ref, o_ref, **kw)


def _odd_tail_kernel(o0_ref, o1_ref, o2_ref, l0_ref, l1_ref, l2_ref, wout_ref, x_ref, mod_ref, lng_ref, lnb_ref,
                     w1_ref, w2_ref, o_ref, o_nat, l_nat, mix_s, *, tm, **kw):
    def natural(src_ref, scr, dil):
        if dil == 1:
            return src_ref[0, 0].astype(F32)
        n_col = src_ref.shape[-1] // LANES
        for c in range(n_col):
            for r in range(dil):
                scr[c, pl.ds(r, tm // dil, stride=dil), :] = src_ref[0, r, :, c * LANES:(c + 1) * LANES].astype(F32)
        return jnp.concatenate([scr[c] for c in range(n_col)], axis=1)

    dils = [dil for _, dil in DIL_PATTERNS]
    lses = [natural(ref, l_nat, dil) for ref, dil in zip((l0_ref, l1_ref, l2_ref), dils)]
    mx = jnp.maximum(jnp.maximum(lses[0], lses[1]), lses[2])
    es = [jnp.exp(l - mx) for l in lses]
    inv = 1.0 / (es[0] + es[1] + es[2])
    hr = lax.broadcasted_iota(jnp.int32, (LANES, DIL_WIDTH), 0)
    hc = lax.broadcasted_iota(jnp.int32, (LANES, DIL_WIDTH), 1) // DIL_HEAD_DIM
    expand = jnp.where(hr == hc, 1.0, 0.0).astype(BF16)

    def widen(wt):
        return jnp.dot(wt.astype(BF16), expand, preferred_element_type=F32)

    merged = None
    for e, ref, dil in zip(es, (o0_ref, o1_ref, o2_ref), dils):
        term = widen(e * inv) * natural(ref, o_nat, dil)
        merged = term if merged is None else merged + term
    mix_s[...] = merged.astype(BF16)
    _tail(lambda rows: [(mix_s[rows, :], 0)], wout_ref, x_ref, mod_ref, lng_ref, lnb_ref, w1_ref, w2_ref, o_ref,
          tm=tm, **kw)


def _layer_tail(mix, w_out, x, mod, ln_g, ln_b, w1, w2, layer, *, tm=512, n_sub=2, ff_chunk=1024):
    bsz, t, d = x.shape

    def const(shape):
        return pl.BlockSpec(shape, lambda b, i: (0,) * len(shape), pipeline_mode=pl.Buffered(1))

    def of_layer(shape):
        return pl.BlockSpec((None,) + shape, lambda b, i: (layer,) + (0,) * len(shape), pipeline_mode=pl.Buffered(1))

    tail_specs = [const(w_out.shape),
                  pl.BlockSpec((1, tm, d), lambda b, i: (b, i, 0)),
                  pl.BlockSpec((1, 6, d), lambda b, i: (b, 0, 0)),
                  of_layer((2, d)), of_layer((2, d)), of_layer((d, D_FF)), of_layer((D_FF, d))]
    tail_args = (w_out, x, mod, ln_g, ln_b, w1, w2)
    kw = dict(tm=tm, n_sub=n_sub, ff_chunk=ff_chunk)
    if isinstance(mix, tuple):
        outs, lses = mix
        mix_specs = ([pl.BlockSpec((1, dil, tm // dil, DIL_WIDTH), lambda b, i: (b, 0, i, 0)) for _, dil in DIL_PATTERNS]
                     + [pl.BlockSpec((1, dil, tm // dil, LANES), lambda b, i: (b, 0, i, 0)) for _, dil in DIL_PATTERNS])
        mix_args = (*outs, *lses)
        body = functools.partial(_odd_tail_kernel, **kw)
        scratch = [pltpu.VMEM((DIL_WIDTH // LANES, tm, LANES), F32), pltpu.VMEM((1, tm, LANES), F32),
                   pltpu.VMEM((tm, DIL_WIDTH), BF16)]
    else:
        mix_specs = [pl.BlockSpec((1, tm, piece.shape[-1]), lambda b, i: (b, i, 0)) for piece in mix]
        mix_args = tuple(mix)
        body = functools.partial(_even_tail_kernel, **kw)
        scratch = []
    return pl.pallas_call(
        body,
        grid=(bsz, t // tm),
        in_specs=mix_specs + tail_specs,
        out_specs=pl.BlockSpec((1, tm, d), lambda b, i: (b, i, 0)),
        out_shape=jax.ShapeDtypeStruct((bsz, t, d), F32),
        scratch_shapes=scratch,
        compiler_params=_cparams(2, TAIL_VMEM_LIMIT),
        name="layer_tail",
    )(*mix_args, *tail_args)


def _prep_even_weights(w_in, mu, w_up, a_up, g_up):
    w3 = 3 * RW_WIDTH
    pad = LR_PAD - (RW_PROJ - w3)
    d = w_in.shape[0]
    w16 = w_in.astype(BF16)
    w_perm = (w16[:, :w3], jnp.concatenate([w16[:, w3:RW_PROJ], jnp.zeros((d, pad), BF16)], axis=1), w16[:, RW_PROJ:])
    mu_pad = jnp.concatenate([mu, jnp.zeros((pad,), mu.dtype)]).reshape(1, w3 + LR_PAD)
    zeros = jnp.zeros((RW_DECAY_RANK, RW_WIDTH), w_up.dtype)
    wwa = jnp.concatenate([jnp.concatenate([w_up, zeros], axis=1),
                           jnp.concatenate([zeros, a_up], axis=1)], axis=0).astype(BF16)
    gup = jnp.concatenate([g_up, jnp.zeros((pad, RW_WIDTH), g_up.dtype)], axis=0).astype(BF16)
    return w_perm, mu_pad, wwa, gup


def kernel(x, c, ada_w, ada_b, ln_g, ln_b, ab_w_in, rw_mu, rw_w0, rw_w_up, rw_a0, rw_a_up, rw_g_up,
           rw_k_k, rw_k_a, rw_r_k, rw_lnx_g, rw_lnx_b, sc_conv_w, ab_w_out, dil_w_qkv, dil_w_out,
           rel_bias, mlp_w1, mlp_w2):
    bsz, t, d = x.shape
    mods = _adaln(c, ada_w, ada_b).reshape(DEPTH, bsz, 6, d)
    bias = None
    row = lambda a: a.reshape(1, -1)
    w1_all, w2_all = mlp_w1, mlp_w2
    for i in range(DEPTH):
        mod = mods[i]
        j = i // 2
        if i % 2 == 0:
            w_perm, mu_pad, wwa, gup = _prep_even_weights(
                ab_w_in[j], rw_mu[j], rw_w_up[j], rw_a_up[j], rw_g_up[j])
            rkv, lr, yb = _ab_proj(x, mod, w_perm, mu_pad, sc_conv_w[j])
            ya = _rwkv(rkv, lr, row(rw_w0[j]), row(rw_a0[j]), wwa, gup,
                       row(rw_k_k[j]), row(rw_k_a[j]), row(rw_r_k[j]),
                       row(rw_lnx_g[j]), row(rw_lnx_b[j]))
            mix, w_out = [ya, yb], ab_w_out[j]
        else:
            if bias is None:
                bias = _bias_tables(rel_bias)
            qkvs = _qkv_proj(x, mod, dil_w_qkv[j])
            outs, lses = [], []
            for g in range(N_GROUPS):
                o, lse = _dilated_attention_group(qkvs[g], bias[g], g)
                outs.append(o)
                lses.append(lse)
            mix, w_out = (outs, lses), dil_w_out[j]
        x = _layer_tail(mix, w_out.astype(BF16), x, mod, ln_g, ln_b, w1_all, w2_all, i)
    return x
```

```python
import functools
import math

import jax
import jax.numpy as jnp
import numpy as np
from jax import lax
from jax.experimental import pallas as pl
from jax.experimental.pallas import tpu as pltpu

F32 = jnp.float32
BF16 = jnp.bfloat16

D_MODEL = 1024
DEPTH = 2
RW_HEADS = 8
RW_HEAD_DIM = 64
RW_WIDTH = 512
RW_DECAY_RANK = 64
RW_ICLR_RANK = 64
RW_GATE_RANK = 160
RW_GN_EPS = 64e-5
RW_PROJ = 3 * RW_WIDTH + RW_DECAY_RANK + RW_ICLR_RANK + RW_GATE_RANK
SC_WIDTH = 512
SC_CONV = 3
DIL_PATTERNS = ((128, 1), (512, 4), (2048, 16))
N_GROUPS = 3
DIL_HEADS = 8
DIL_HEAD_DIM = 64
DIL_WIDTH = 512
BLOCK = 128
N_BUCKETS = 32
MAX_DISTANCE = 2048
D_FF = 4 * D_MODEL
DEEPNORM_ALPHA = (2 * DEPTH) ** 0.25
LN_EPS = 1e-5

LANES = 128
MXU_DIM = 256
CHUNK = 64
INV_BLOCK = 16
LR_PAD = 384
VMEM_LIMIT = 56 * 1024 * 1024
TAIL_VMEM_LIMIT = 60 * 1024 * 1024
MASKED = -1e30


def _cparams(n_axes, vmem_limit=VMEM_LIMIT):
    return pltpu.CompilerParams(dimension_semantics=("arbitrary",) * n_axes,
                                vmem_limit_bytes=vmem_limit)


def _sigmoid(x):
    return 1.0 / (1.0 + jnp.exp(-x))


def _dot(a, b):
    return jnp.dot(a.astype(BF16), b.astype(BF16), preferred_element_type=F32)


def _layer_norm_rows(z, g, b):
    mu = jnp.mean(z, axis=-1, keepdims=True)
    zc = z - mu
    var = jnp.mean(zc * zc, axis=-1, keepdims=True)
    return zc * lax.rsqrt(var + LN_EPS) * g + b


def _adaln_kernel(c_ref, w_ref, b_ref, o_ref):
    c = c_ref[...]
    cond = c * _sigmoid(c)
    o_ref[0] = _dot(cond, w_ref[0]) + b_ref[0]


def _adaln(c, ada_w, ada_b):
    depth, d, n = ada_w.shape
    bsz = c.shape[0]
    tn = 3072
    return pl.pallas_call(
        _adaln_kernel,
        grid=(depth, n // tn),
        in_specs=[pl.BlockSpec((bsz, d), lambda i, j: (0, 0)),
                  pl.BlockSpec((1, d, tn), lambda i, j: (i, 0, j)),
                  pl.BlockSpec((1, 1, tn), lambda i, j: (i, 0, j))],
        out_specs=pl.BlockSpec((1, bsz, tn), lambda i, j: (i, 0, j)),
        out_shape=jax.ShapeDtypeStruct((depth, bsz, n), F32),
        compiler_params=_cparams(2),
        name="adaln",
    )(c, ada_w, ada_b.reshape(depth, 1, n))


def _shift_rows(x, carry, k):
    rolled = pltpu.roll(x, k, 0)
    head = rolled[:8]
    row = lax.broadcasted_iota(jnp.int32, head.shape, 0)
    for i in range(k):
        head = jnp.where(row == i, carry[8 - k + i:8 - k + i + 1], head)
    return jnp.concatenate([head, rolled[8:]], axis=0)


def _ab_proj_kernel(x_ref, mod_ref, wt_ref, mu_ref, convw_ref, rkv_ref, lr_ref, yb_ref, carry_p, carry_z, *, tm):
    @pl.when(pl.program_id(1) == 0)
    def _():
        carry_p[...] = jnp.zeros_like(carry_p)
        carry_z[...] = jnp.zeros_like(carry_z)

    m = mod_ref[0]
    u = (x_ref[0] * (1.0 + m[1:2]) + m[0:1]).astype(BF16)

    def proj(c0, width):
        return lax.dot_general(u, wt_ref[c0:c0 + width, :].astype(BF16), (((1,), (1,)), ((), ())),
                               preferred_element_type=F32)

    def mixed(c0, width):
        p = proj(c0, width)
        prev = _shift_rows(p, carry_p[:, c0:c0 + width], 1)
        carry_p[:, c0:c0 + width] = p[tm - 8:]
        return (p + mu_ref[:, c0:c0 + width] * (prev - p)).astype(BF16)

    w3 = 3 * RW_WIDTH
    for c in range(3):
        rkv_ref[0, :, c * RW_WIDTH:(c + 1) * RW_WIDTH] = mixed(c * RW_WIDTH, RW_WIDTH)
    lr_ref[0] = mixed(w3, LR_PAD)

    h = proj(RW_PROJ, SC_WIDTH)
    z = proj(RW_PROJ + 2 * SC_WIDTH, SC_WIDTH) * h
    zc = carry_z[...]
    cw = convw_ref[...]
    conv = cw[2:3] * z + cw[1:2] * _shift_rows(z, zc, 1) + cw[0:1] * _shift_rows(z, zc, 2)
    carry_z[...] = z[tm - 8:]
    yb_ref[0] = (proj(RW_PROJ + SC_WIDTH, SC_WIDTH) * conv).astype(BF16)


def _ab_proj(x, mod, w_t, mu, conv_w, *, tm=1024):
    bsz, t, d = x.shape
    w3 = 3 * RW_WIDTH

    def const(shape):
        return pl.BlockSpec(shape, lambda b, i: (0,) * len(shape), pipeline_mode=pl.Buffered(1))

    def rows(width):
        return pl.BlockSpec((1, tm, width), lambda b, i: (b, i, 0))

    return pl.pallas_call(
        functools.partial(_ab_proj_kernel, tm=tm),
        grid=(bsz, t // tm),
        in_specs=[rows(d), pl.BlockSpec((1, 6, d), lambda b, i: (b, 0, 0)),
                  const(w_t.shape), const((1, w3 + LR_PAD)), const((SC_CONV, SC_WIDTH))],
        out_specs=[rows(w3), rows(LR_PAD), rows(SC_WIDTH)],
        out_shape=[jax.ShapeDtypeStruct((bsz, t, w3), BF16), jax.ShapeDtypeStruct((bsz, t, LR_PAD), BF16),
                   jax.ShapeDtypeStruct((bsz, t, SC_WIDTH), BF16)],
        scratch_shapes=[pltpu.VMEM((8, w3 + LR_PAD), F32), pltpu.VMEM((8, SC_WIDTH), F32)],
        compiler_params=_cparams(2),
        name="ab_proj",
    )(x, mod, w_t, mu, conv_w)


def _rwkv_chunk_terms(a_t, r_t, b_t, k_t, b_e, k_e, v, p_end):
    n = len(a_t)
    idx = range(n)
    row = lax.broadcasted_iota(jnp.int32, (CHUNK, LANES), 0)
    col = lax.broadcasted_iota(jnp.int32, (CHUNK, LANES), 1)
    s_idx = col % CHUNK
    strict = s_idx < row
    incl = s_idx <= row
    near = strict & ((s_idx // INV_BLOCK) == (row // INV_BLOCK))
    far = strict & ((s_idx // INV_BLOCK) != (row // INV_BLOCK))
    eye = jnp.where(s_idx == row, 1.0, 0.0)
    r2 = lax.broadcasted_iota(jnp.int32, (LANES, 2 * LANES), 0)
    c2 = lax.broadcasted_iota(jnp.int32, (LANES, 2 * LANES), 1)
    bdmask2 = (r2 // RW_HEAD_DIM) == ((c2 % LANES) // RW_HEAD_DIM)
    eye_bd = jnp.where(lax.broadcasted_iota(jnp.int32, (LANES, LANES), 0)
                       == lax.broadcasted_iota(jnp.int32, (LANES, LANES), 1), 1.0, 0.0)
    zero16 = jnp.zeros((CHUNK, LANES), BF16)
    lane1 = lax.broadcasted_iota(jnp.int32, (1, LANES), 1)
    m0_16 = jnp.where(lane1 < RW_HEAD_DIM, 1.0, 0.0).astype(BF16)
    m1_16 = jnp.where(lane1 < RW_HEAD_DIM, 0.0, 1.0).astype(BF16)

    def cast(xs):
        return [x.astype(BF16) for x in xs]

    def bd(q16):
        return jnp.concatenate([q16 * m0_16, q16 * m1_16], axis=0)

    def mm(p16, q16):
        return jnp.dot(p16, q16, preferred_element_type=F32)

    def mm_nt(p16, q16):
        return lax.dot_general(p16, q16, (((1,), (1,)), ((), ())), preferred_element_type=F32)

    a16, r16, v16 = cast(a_t), cast(r_t), cast(v)
    lhs = [jnp.concatenate([a16[i], r16[i]], axis=0) for i in idx]
    b16, k16 = cast(b_t), cast(k_t)
    bdv = [bd(x) for x in v16]
    g = [mm_nt(lhs[i], jnp.concatenate([bd(b16[i]), bd(k16[i])], axis=0)) for i in idx]
    x1 = [jnp.where(near, g[i][:CHUNK, :LANES], 0.0) for i in idx]
    x1_16 = cast(x1)
    l_e16 = [jnp.where(far, g[i][:CHUNK, :LANES], 0.0).astype(BF16) for i in idx]
    a_k16 = [jnp.concatenate([jnp.where(strict, g[i][:CHUNK, LANES:], 0.0),
                              jnp.where(incl, g[i][CHUNK:, LANES:], 0.0)], axis=0).astype(BF16) for i in idx]
    a_rb16 = [jnp.where(incl, g[i][CHUNK:, :LANES], 0.0).astype(BF16) for i in idx]

    def both(p16, q16):
        return jnp.concatenate([bd(p16), bd(q16)], axis=1)

    x2_16 = cast([mm(x1_16[i], bd(x1_16[i])) for i in idx])
    acc = [eye + x1[i] for i in idx]
    sq = [mm(x2_16[i], both(x2_16[i], acc[i].astype(BF16))) for i in idx]
    x4_16 = [sq[i][:, :LANES].astype(BF16) for i in idx]
    acc = [acc[i] + sq[i][:, LANES:] for i in idx]
    sq = [mm(x4_16[i], both(x4_16[i], acc[i].astype(BF16))) for i in idx]
    x8_16 = [sq[i][:, :LANES].astype(BF16) for i in idx]
    acc = [acc[i] + sq[i][:, LANES:] for i in idx]
    dinv = [acc[i] + mm(x8_16[i], bd(acc[i].astype(BF16))) for i in idx]
    dinv16 = cast(dinv)
    f16 = cast([mm(dinv16[i], bd(l_e16[i])) for i in idx])
    sq = [mm(f16[i], both(f16[i], dinv16[i])) for i in idx]
    f2_16 = [sq[i][:, :LANES].astype(BF16) for i in idx]
    hmat = [dinv[i] + sq[i][:, LANES:] for i in idx]
    tinv16 = cast([hmat[i] + mm(f2_16[i], bd(hmat[i].astype(BF16))) for i in idx])

    kv = [mm(a_k16[i], bdv[i]) for i in idx]
    wv16 = [kv[i][:CHUNK].astype(BF16) for i in idx]
    aw = [mm(tinv16[i], both(a16[i], wv16[i])) for i in idx]
    aw16 = cast(aw)
    qy = [mm(a_rb16[i], both(aw16[i][:, :LANES], aw16[i][:, LANES:]))
          + jnp.concatenate([r_t[i], kv[i][CHUNK:]], axis=1) for i in idx]
    be_ke_t = [jnp.concatenate([b_e[i], k_e[i]], axis=0).T.astype(BF16) for i in idx]
    rhs = [jnp.concatenate([aw16[i], jnp.concatenate([zero16, v16[i]], axis=1)], axis=0) for i in idx]
    mn = [jnp.where(bdmask2, mm(be_ke_t[i], rhs[i]), 0.0) for i in idx]
    qhat = [x[:, :LANES] for x in qy]
    yhat = [x[:, LANES:] for x in qy]
    mmat = [eye_bd * p_end[i] + mn[i][:, :LANES] for i in idx]
    nmat = [x[:, LANES:] for x in mn]
    return qhat, yhat, mmat, nmat


def _rwkv_kernel(rkv_ref, lr_ref, w0_ref, a0_ref, wwa_ref, gup_ref,
                 kk_ref, ka_ref, rk_ref, lnxg_ref, lnxb_ref, o_ref, state, yout_s, *, tb):
    n_chunks = tb // CHUNK
    n_pairs = RW_WIDTH // LANES

    @pl.when(pl.program_id(1) == 0)
    def _():
        state[...] = jnp.zeros_like(state)

    r = rkv_ref[0, :, :RW_WIDTH].astype(F32)
    k = rkv_ref[0, :, RW_WIDTH:2 * RW_WIDTH].astype(F32)
    v = rkv_ref[0, :, 2 * RW_WIDTH:].astype(F32)
    lr_m = lr_ref[0].astype(F32)

    wa = lr_m[:, :LANES]
    lane = lax.broadcasted_iota(jnp.int32, wa.shape, 1)
    wa = jnp.where(lane < RW_DECAY_RANK, jnp.tanh(wa), wa)
    wa_up = _dot(wa, wwa_ref[...])
    ld = -math.exp(-0.5) * _sigmoid(w0_ref[...] + wa_up[:, :RW_WIDTH])
    iclr = _sigmoid(a0_ref[...] + wa_up[:, RW_WIDTH:])
    gate = _dot(_sigmoid(lr_m[:, LANES:]), gup_ref[...])

    hr = lax.broadcasted_iota(jnp.int32, (MXU_DIM, MXU_DIM), 0) // RW_HEAD_DIM
    hc = lax.broadcasted_iota(jnp.int32, (MXU_DIM, MXU_DIM), 1) // RW_HEAD_DIM
    seg = jnp.where(hr == hc, 1.0 / RW_HEAD_DIM, 0.0).astype(BF16)

    def seg_mean(x):
        x16 = x.astype(BF16)
        return jnp.concatenate([jnp.dot(x16[:, c:c + MXU_DIM], seg, preferred_element_type=F32)
                                for c in range(0, RW_WIDTH, MXU_DIM)], axis=1)

    kk = k * kk_ref[...]
    ss = seg_mean(kk * kk) * RW_HEAD_DIM
    kk = kk * lax.rsqrt(jnp.maximum(ss, 1e-24))
    kh = k * (1.0 + (iclr - 1.0) * ka_ref[...])

    tr = lax.broadcasted_iota(jnp.int32, (MXU_DIM, MXU_DIM), 0)
    tc = lax.broadcasted_iota(jnp.int32, (MXU_DIM, MXU_DIM), 1)
    tri = jnp.where((tc <= tr) & (tc // CHUNK == tr // CHUNK), 1.0, 0.0).astype(BF16)
    ld_hi = ld.astype(BF16)
    ld_lo = (ld - ld_hi.astype(F32)).astype(BF16)
    lp = jnp.concatenate(
        [jnp.dot(tri, ld_hi[t0:t0 + MXU_DIM], preferred_element_type=F32)
         + jnp.dot(tri, ld_lo[t0:t0 + MXU_DIM], preferred_element_type=F32)
         for t0 in range(0, tb, MXU_DIM)], axis=0)

    a_t, r_t, b_t, k_t, b_e, k_e, vs, p_end = [], [], [], [], [], [], [], []
    b = kk * iclr
    for c in range(n_chunks):
        rs = slice(c * CHUNK, (c + 1) * CHUNK)
        lp_c = lp[rs]
        lp_last = lp_c[CHUNK - 1:CHUNK]
        e_in = jnp.exp(lp_c)
        e_inv = jnp.exp(-lp_c)
        e_end = jnp.exp(lp_last - lp_c)
        a_c = -kk[rs] * jnp.exp(lp_c - ld[rs])
        r_c = r[rs] * e_in
        b_c = b[rs] * e_inv
        k_c = kh[rs] * e_inv
        be_c = b[rs] * e_end
        ke_c = kh[rs] * e_end
        pe_c = jnp.exp(lp_last)
        for p in range(n_pairs):
            cs = slice(p * LANES, (p + 1) * LANES)
            a_t.append(a_c[:, cs])
            r_t.append(r_c[:, cs])
            b_t.append(b_c[:, cs])
            k_t.append(k_c[:, cs])
            b_e.append(be_c[:, cs])
            k_e.append(ke_c[:, cs])
            vs.append(v[rs, cs])
            p_end.append(pe_c[:, cs])

    qhat, yhat, mmat, nmat = _rwkv_chunk_terms(a_t, r_t, b_t, k_t, b_e, k_e, vs, p_end)

    s = [state[p] for p in range(n_pairs)]
    for c in range(n_chunks):
        ym = [_dot(jnp.concatenate([qhat[c * n_pairs + p], mmat[c * n_pairs + p]], axis=0), s[p])
              for p in range(n_pairs)]
        for p in range(n_pairs):
            i = c * n_pairs + p
            yout_s[c * CHUNK:(c + 1) * CHUNK, p * LANES:(p + 1) * LANES] = ym[p][:CHUNK] + yhat[i]
            s[p] = ym[p][CHUNK:] + nmat[i]
    for p in range(n_pairs):
        state[p] = s[p]

    y = yout_s[...]
    mean = seg_mean(y)
    yc = y - mean
    var = seg_mean(yc * yc)
    y = yc * lax.rsqrt(var + RW_GN_EPS) * lnxg_ref[...] + lnxb_ref[...]
    bonus = seg_mean(r * kh * rk_ref[...]) * RW_HEAD_DIM * v
    o_ref[0] = ((y + bonus) * gate).astype(o_ref.dtype)


def _rwkv(rkv, lr, w0, a0, wwa, gup, k_k, k_a, r_k, lnx_g, lnx_b, *, tb=512):
    bsz, t, w3 = rkv.shape
    n_pairs = RW_WIDTH // LANES

    def full(shape):
        return pl.BlockSpec(shape, lambda b, i: (0,) * len(shape))

    return pl.pallas_call(
        functools.partial(_rwkv_kernel, tb=tb),
        grid=(bsz, t // tb),
        in_specs=[pl.BlockSpec((1, tb, w3), lambda b, i: (b, i, 0)),
                  pl.BlockSpec((1, tb, LR_PAD), lambda b, i: (b, i, 0)),
                  full((1, RW_WIDTH)), full((1, RW_WIDTH)),
                  full((LANES, 2 * RW_WIDTH)), full((LR_PAD - LANES, RW_WIDTH)),
                  full((1, RW_WIDTH)), full((1, RW_WIDTH)), full((1, RW_WIDTH)),
                  full((1, RW_WIDTH)), full((1, RW_WIDTH))],
        out_specs=pl.BlockSpec((1, tb, RW_WIDTH), lambda b, i: (b, i, 0)),
        out_shape=jax.ShapeDtypeStruct((bsz, t, RW_WIDTH), BF16),
        scratch_shapes=[pltpu.VMEM((n_pairs, LANES, LANES), F32),
                        pltpu.VMEM((tb, RW_WIDTH), F32)],
        compiler_params=_cparams(2),
        name="rwkv",
    )(rkv, lr, w0, a0, wwa, gup, k_k, k_a, r_k, lnx_g, lnx_b)


def _attn_kernel(q_ref, kp_ref, kc_ref, vp_ref, vc_ref, bias_ref, o_ref, lse_ref, *, nb, rb):
    step = pl.program_id(1)
    n_blk = rb // BLOCK
    scale = DIL_HEAD_DIM ** -0.5
    n_pairs = DIL_WIDTH // LANES
    krow = lax.broadcasted_iota(jnp.int32, (2 * BLOCK, LANES), 0) // BLOCK
    klane = lax.broadcasted_iota(jnp.int32, (2 * BLOCK, LANES), 1) // DIL_HEAD_DIM
    own = krow == klane
    zeros = jnp.zeros((2 * BLOCK, LANES), BF16)
    ones_bd = jnp.where(own, 1.0, 0.0).astype(BF16)
    lane_o = lax.broadcasted_iota(jnp.int32, (BLOCK, LANES), 1)

    def stack_bd(x):
        return jnp.where(own, jnp.concatenate([x, x], axis=0), zeros)

    def attend(blocks):
        units = [(rows, q, parts, p) for rows, q, parts in blocks for p in range(n_pairs)]
        lanes = [slice(p * LANES, (p + 1) * LANES) for p in range(n_pairs)]
        scores = []
        for _, q, parts, p in units:
            qs = q[:, lanes[p]] * scale
            row = []
            for k_blk, _, col0, pen in parts:
                s = lax.dot_general(qs, stack_bd(k_blk[:, lanes[p]]), (((1,), (1,)), ((), ())),
                                    preferred_element_type=F32)
                bias = jnp.concatenate([bias_ref[2 * p, :, col0:col0 + BLOCK],
                                        bias_ref[2 * p + 1, :, col0:col0 + BLOCK]], axis=1)
                s = s + bias
                row.append(s if pen is None else s + pen)
            scores.append(row)
        ms = []
        for u in range(len(units)):
            halves = []
            for hh in range(2):
                cols = slice(hh * BLOCK, (hh + 1) * BLOCK)
                m = jnp.max(scores[u][0][:, cols], axis=-1, keepdims=True)
                for s in scores[u][1:]:
                    m = jnp.maximum(m, jnp.max(s[:, cols], axis=-1, keepdims=True))
                halves.append(m)
            ms.append(halves)
        es = [[jnp.exp((s - jnp.concatenate([jnp.broadcast_to(ms[u][0], (BLOCK, BLOCK)),
                                             jnp.broadcast_to(ms[u][1], (BLOCK, BLOCK))], axis=1)).astype(BF16))
               for s in scores[u]] for u in range(len(units))]
        res = []
        for u, (_, _, parts, p) in enumerate(units):
            res.append(sum(jnp.dot(e, jnp.concatenate([stack_bd(part[1][:, lanes[p]]), ones_bd], axis=1),
                                   preferred_element_type=F32) for e, part in zip(es[u], parts)))
        for b, (rows, _, _) in enumerate(blocks):
            m_all = jnp.zeros((BLOCK, LANES), F32)
            den_all = jnp.ones((BLOCK, LANES), F32)
            for p in range(n_pairs):
                u = b * n_pairs + p
                num, den = res[u][:, :LANES], res[u][:, LANES:]
                o_ref[rows, lanes[p]] = (num / den).astype(o_ref.dtype)
                m_all = jnp.where(lane_o == 2 * p, ms[u][0], jnp.where(lane_o == 2 * p + 1, ms[u][1], m_all))
                den_all = jnp.where((lane_o == 2 * p), den,
                                    jnp.where(lane_o == 2 * p + 1, pltpu.roll(den, DIL_HEAD_DIM, 1), den_all))
            lse_ref[rows, :] = m_all + jnp.log(den_all)

    blocks = []
    for j in range(n_blk):
        rows = slice(j * BLOCK, (j + 1) * BLOCK)
        cur = (kc_ref[rows, :], vc_ref[rows, :], BLOCK, None)
        if n_blk % nb == 0:
            prev = "none" if j % nb == 0 else "inside"
        else:
            prev = "inside" if j > 0 else ("none" if nb == n_blk else "before")
        if prev == "none":
            parts = [cur]
        elif prev == "inside":
            before = slice((j - 1) * BLOCK, j * BLOCK)
            parts = [(kc_ref[before, :], vc_ref[before, :], 0, None), cur]
        else:
            has_prev = (step % (nb // n_blk)) != 0
            parts = [(kp_ref[...], vp_ref[...], 0, jnp.where(has_prev, 0.0, MASKED)), cur]
        blocks.append((rows, q_ref[rows, :], parts))
    attend(blocks)


def _qkv_proj_kernel(x_ref, mod_ref, w_ref, o0_ref, o1_ref, o2_ref, xs_ref, xs4_ref, *, tm):
    m = mod_ref[0]
    scale = 1.0 + m[1:2]
    shift = m[0:1]
    n_col = x_ref.shape[-1] // LANES
    gw = 3 * DIL_WIDTH
    step = DIL_PATTERNS[1][1]
    assert [dil for _, dil in DIL_PATTERNS] == [1, step, step * step]

    for c in range(n_col):
        xs_ref[c] = x_ref[0, :, c * LANES:(c + 1) * LANES]
    rows4 = tm // step
    x4 = [jnp.concatenate([xs_ref[c, pl.ds(a, rows4, stride=step), :] for a in range(step)], axis=0)
          for c in range(n_col)]
    for c in range(n_col):
        xs4_ref[c] = x4[c]
    rows16 = rows4 // step
    x16 = [jnp.concatenate([xs4_ref[c, pl.ds(a * rows4 + b, rows16, stride=step), :]
                            for a in range(step) for b in range(step)], axis=0) for c in range(n_col)]
    inputs = [x_ref[0], jnp.concatenate(x4, axis=1), jnp.concatenate(x16, axis=1)]
    residue_of_group = [[0], list(range(step)), [a + step * b for a in range(step) for b in range(step)]]

    for g, o_ref in enumerate((o0_ref, o1_ref, o2_ref)):
        u = (inputs[g] * scale + shift).astype(BF16)
        res = jnp.dot(u, w_ref[:, g * gw:(g + 1) * gw].astype(BF16), preferred_element_type=F32).astype(BF16)
        rows = tm // len(residue_of_group[g])
        for grp, r in enumerate(residue_of_group[g]):
            o_ref[0, r] = res[grp * rows:(grp + 1) * rows]


def _qkv_proj(x, mod, w, *, tm=1024):
    bsz, t, d = x.shape
    gw = 3 * DIL_WIDTH
    out_specs, out_shape = [], []
    for _, dil in DIL_PATTERNS:
        out_specs.append(pl.BlockSpec((1, dil, tm // dil, gw), lambda b, i: (b, 0, i, 0)))
        out_shape.append(jax.ShapeDtypeStruct((bsz, dil, t // dil, gw), BF16))
    return pl.pallas_call(
        functools.partial(_qkv_proj_kernel, tm=tm),
        grid=(bsz, t // tm),
        in_specs=[pl.BlockSpec((1, tm, d), lambda b, i: (b, i, 0)),
                  pl.BlockSpec((1, 6, d), lambda b, i: (b, 0, 0)),
                  pl.BlockSpec(w.shape, lambda b, i: (0, 0), pipeline_mode=pl.Buffered(1))],
        out_specs=out_specs,
        out_shape=out_shape,
        scratch_shapes=[pltpu.VMEM((d // LANES, tm, LANES), F32), pltpu.VMEM((d // LANES, tm, LANES), F32)],
        compiler_params=_cparams(2),
        name="qkv_proj",
    )(x, mod, w)


def _dilated_attention_group(qkv, bias, g, *, rb=1024):
    bsz, dil, length, width = qkv.shape
    t = dil * length
    nb = length // BLOCK
    n_blk = rb // BLOCK
    assert nb % n_blk == 0 or n_blk % nb == 0
    flat = qkv.reshape(bsz, t, width)

    def cur(j):
        return lambda b, s: (b, s, j)

    def prev(j):
        return lambda b, s: (b, jnp.maximum(s * n_blk - 1, 0), j)

    blk = (None, rb, DIL_WIDTH)
    pblk = (None, BLOCK, DIL_WIDTH)
    o, lse = pl.pallas_call(
        functools.partial(_attn_kernel, nb=nb, rb=rb),
        grid=(bsz, t // rb),
        in_specs=[pl.BlockSpec(blk, cur(0)),
                  pl.BlockSpec(pblk, prev(1)), pl.BlockSpec(blk, cur(1)),
                  pl.BlockSpec(pblk, prev(2)), pl.BlockSpec(blk, cur(2)),
                  pl.BlockSpec((DIL_HEADS, BLOCK, 2 * BLOCK), lambda b, s: (0, 0, 0))],
        out_specs=[pl.BlockSpec(blk, lambda b, s: (b, s, 0)),
                   pl.BlockSpec((None, rb, LANES), lambda b, s: (b, s, 0))],
        out_shape=[jax.ShapeDtypeStruct((bsz, t, DIL_WIDTH), BF16),
                   jax.ShapeDtypeStruct((bsz, t, LANES), F32)],
        compiler_params=_cparams(2),
        name=f"dilated_attn_g{g}",
    )(flat, flat, flat, flat, flat, bias)
    return o.reshape(bsz, dil, length, DIL_WIDTH), lse.reshape(bsz, dil, length, LANES)


def _t5_bucket(dist):
    exact = N_BUCKETS // 2
    logd = jnp.log(jnp.maximum(dist, 1).astype(F32) / exact) / math.log(MAX_DISTANCE / exact)
    large = jnp.minimum(exact + (logd * (N_BUCKETS - exact)).astype(jnp.int32), N_BUCKETS - 1)
    return jnp.where(dist < exact, dist, large)


def _bias_tables(rel_bias):
    rel = jnp.arange(BLOCK + 1)
    tables = []
    for g, (window, dil) in enumerate(DIL_PATTERNS):
        span = window // dil
        bucket = _t5_bucket(jnp.clip(rel, 0, span) * dil)
        vec = rel_bias[bucket][:, g * DIL_HEADS:(g + 1) * DIL_HEADS].T.astype(F32)
        ext = jnp.concatenate([vec[:, ::-1], jnp.zeros((DIL_HEADS, BLOCK), F32)], axis=1)
        flat = jnp.tile(ext, (1, BLOCK))[:, :BLOCK * 2 * BLOCK]
        tables.append(flat.reshape(DIL_HEADS, BLOCK, 2 * BLOCK))
    qi = np.arange(BLOCK)[:, None]
    ki = np.arange(2 * BLOCK)[None, :]
    band = (ki >= qi) & (ki <= qi + BLOCK)
    return jnp.where(band, jnp.stack(tables), MASKED)


def _tail(mix_fn, wout_ref, x_ref, mod_ref, lng_ref, lnb_ref, w1_ref, w2_ref, o_ref, *, tm, n_sub, ff_chunk):
    m = mod_ref[0]
    sub = tm // n_sub
    tiles = [slice(i * sub, (i + 1) * sub) for i in range(n_sub)]
    x1s, us = [], []
    for rows in tiles:
        y = None
        for piece, k0 in mix_fn(rows):
            part = jnp.dot(piece, wout_ref[k0:k0 + piece.shape[-1], :], preferred_element_type=F32)
            y = part if y is None else y + part
        x1 = _layer_norm_rows(DEEPNORM_ALPHA * x_ref[0, rows, :] + (1.0 + m[2:3]) * y, lng_ref[0:1], lnb_ref[0:1])
        x1s.append(x1)
        us.append((x1 * (1.0 + m[4:5]) + m[3:4]).astype(BF16))
    accs = [None] * n_sub
    for c in range(D_FF // ff_chunk):
        cols = slice(c * ff_chunk, (c + 1) * ff_chunk)
        w1c = w1_ref[:, cols].astype(BF16)
        w2c = w2_ref[cols, :].astype(BF16)
        for i in range(n_sub):
            h = jnp.maximum(jnp.dot(us[i], w1c, preferred_element_type=F32), 0.0)
            part = jnp.dot((h * h).astype(BF16), w2c, preferred_element_type=F32)
            accs[i] = part if accs[i] is None else accs[i] + part
    for i, rows in enumerate(tiles):
        z = DEEPNORM_ALPHA * x1s[i] + (1.0 + m[5:6]) * accs[i]
        o_ref[0, rows, :] = _layer_norm_rows(z, lng_ref[1:2], lnb_ref[1:2])


def _even_tail_kernel(ya_ref, yb_ref, wout_ref, x_ref, mod_ref, lng_ref, lnb_ref, w1_ref, w2_ref, o_ref, **kw):
    def mix(rows):
        return [(ya_ref[0, rows, :], 0), (yb_ref[0, rows, :], ya_ref.shape[-1])]

    _tail(mix, wout_ref, x_ref, mod_ref, lng_ref, lnb_ref, w1_ref, w2_ref, o_ref, **kw)


def _odd_tail_kernel(o0_ref, o1_ref, o2_ref, l0_ref, l1_ref, l2_ref, wout_ref, x_ref, mod_ref, lng_ref, lnb_ref,
                     w1_ref, w2_ref, o_ref, o_nat, l_nat, mix_s, *, tm, **kw):
    def natural(src_ref, scr, dil):
        if dil == 1:
            return src_ref[0, 0].astype(F32)
        n_col = src_ref.shape[-1] // LANES
        for c in range(n_col):
            for r in range(dil):
                scr[c, pl.ds(r, tm // dil, stride=dil), :] = src_ref[0, r, :, c * LANES:(c + 1) * LANES].astype(F32)
        return jnp.concatenate([scr[c] for c in range(n_col)], axis=1)

    dils = [dil for _, dil in DIL_PATTERNS]
    lses = [natural(ref, l_nat, dil) for ref, dil in zip((l0_ref, l1_ref, l2_ref), dils)]
    mx = jnp.maximum(jnp.maximum(lses[0], lses[1]), lses[2])
    es = [jnp.exp(l - mx) for l in lses]
    inv = 1.0 / (es[0] + es[1] + es[2])
    hr = lax.broadcasted_iota(jnp.int32, (LANES, DIL_WIDTH), 0)
    hc = lax.broadcasted_iota(jnp.int32, (LANES, DIL_WIDTH), 1) // DIL_HEAD_DIM
    expand = jnp.where(hr == hc, 1.0, 0.0).astype(BF16)

    def widen(wt):
        return jnp.dot(wt.astype(BF16), expand, preferred_element_type=F32)

    merged = None
    for e, ref, dil in zip(es, (o0_ref, o1_ref, o2_ref), dils):
        term = widen(e * inv) * natural(ref, o_nat, dil)
        merged = term if merged is None else merged + term
    mix_s[...] = merged.astype(BF16)
    _tail(lambda rows: [(mix_s[rows, :], 0)], wout_ref, x_ref, mod_ref, lng_ref, lnb_ref, w1_ref, w2_ref, o_ref,
          tm=tm, **kw)


def _layer_tail(mix, w_out, x, mod, ln_g, ln_b, w1, w2, layer, *, tm=512, n_sub=2, ff_chunk=1024):
    bsz, t, d = x.shape

    def const(shape):
        return pl.BlockSpec(shape, lambda b, i: (0,) * len(shape), pipeline_mode=pl.Buffered(1))

    def of_layer(shape):
        return pl.BlockSpec((None,) + shape, lambda b, i: (layer,) + (0,) * len(shape), pipeline_mode=pl.Buffered(1))

    tail_specs = [const(w_out.shape),
                  pl.BlockSpec((1, tm, d), lambda b, i: (b, i, 0)),
                  pl.BlockSpec((1, 6, d), lambda b, i: (b, 0, 0)),
                  of_layer((2, d)), of_layer((2, d)), of_layer((d, D_FF)), of_layer((D_FF, d))]
    tail_args = (w_out, x, mod, ln_g, ln_b, w1, w2)
    kw = dict(tm=tm, n_sub=n_sub, ff_chunk=ff_chunk)
    if isinstance(mix, tuple):
        outs, lses = mix
        mix_specs = ([pl.BlockSpec((1, dil, tm // dil, DIL_WIDTH), lambda b, i: (b, 0, i, 0)) for _, dil in DIL_PATTERNS]
                     + [pl.BlockSpec((1, dil, tm // dil, LANES), lambda b, i: (b, 0, i, 0)) for _, dil in DIL_PATTERNS])
        mix_args = (*outs, *lses)
        body = functools.partial(_odd_tail_kernel, **kw)
        scratch = [pltpu.VMEM((DIL_WIDTH // LANES, tm, LANES), F32), pltpu.VMEM((1, tm, LANES), F32),
                   pltpu.VMEM((tm, DIL_WIDTH), BF16)]
    else:
        mix_specs = [pl.BlockSpec((1, tm, piece.shape[-1]), lambda b, i: (b, i, 0)) for piece in mix]
        mix_args = tuple(mix)
        body = functools.partial(_even_tail_kernel, **kw)
        scratch = []
    return pl.pallas_call(
        body,
        grid=(bsz, t // tm),
        in_specs=mix_specs + tail_specs,
        out_specs=pl.BlockSpec((1, tm, d), lambda b, i: (b, i, 0)),
        out_shape=jax.ShapeDtypeStruct((bsz, t, d), F32),
        scratch_shapes=scratch,
        compiler_params=_cparams(2, TAIL_VMEM_LIMIT),
        name="layer_tail",
    )(*mix_args, *tail_args)


def _prep_even_weights(mu, w_up, a_up, g_up):
    pad = LR_PAD - (RW_PROJ - 3 * RW_WIDTH)
    mu_pad = jnp.concatenate([mu, jnp.zeros((pad,), mu.dtype)]).reshape(1, 3 * RW_WIDTH + LR_PAD)
    zeros = jnp.zeros((RW_DECAY_RANK, RW_WIDTH), w_up.dtype)
    wwa = jnp.concatenate([jnp.concatenate([w_up, zeros], axis=1),
                           jnp.concatenate([zeros, a_up], axis=1)], axis=0).astype(BF16)
    gup = jnp.concatenate([g_up, jnp.zeros((pad, RW_WIDTH), g_up.dtype)], axis=0).astype(BF16)
    return mu_pad, wwa, gup


def kernel(x, c, ada_w, ada_b, ln_g, ln_b, ab_w_in, rw_mu, rw_w0, rw_w_up, rw_a0, rw_a_up, rw_g_up,
           rw_k_k, rw_k_a, rw_r_k, rw_lnx_g, rw_lnx_b, sc_conv_w, ab_w_out, dil_w_qkv, dil_w_out,
           rel_bias, mlp_w1, mlp_w2):
    bsz, t, d = x.shape
    mods = _adaln(c, ada_w, ada_b).reshape(DEPTH, bsz, 6, d)
    bias = None
    row = lambda a: a.reshape(1, -1)
    w1_all, w2_all = mlp_w1, mlp_w2
    for i in range(DEPTH):
        mod = mods[i]
        j = i // 2
        if i % 2 == 0:
            mu_pad, wwa, gup = _prep_even_weights(rw_mu[j], rw_w_up[j], rw_a_up[j], rw_g_up[j])
            rkv, lr, yb = _ab_proj(x, mod, ab_w_in[j].T, mu_pad, sc_conv_w[j])
            ya = _rwkv(rkv, lr, row(rw_w0[j]), row(rw_a0[j]), wwa, gup,
                       row(rw_k_k[j]), row(rw_k_a[j]), row(rw_r_k[j]),
                       row(rw_lnx_g[j]), row(rw_lnx_b[j]))
            mix, w_out = [ya, yb], ab_w_out[j]
        else:
            if bias is None:
                bias = _bias_tables(rel_bias)
            qkvs = _qkv_proj(x, mod, dil_w_qkv[j])
            outs, lses = [], []
            for g in range(N_GROUPS):
                o, lse = _dilated_attention_group(qkvs[g], bias[g], g)
                outs.append(o)
                lses.append(lse)
            mix, w_out = (outs, lses), dil_w_out[j]
        x = _layer_tail(mix, w_out.astype(BF16), x, mod, ln_g, ln_b, w1_all, w2_all, i)
    return x
```

```python
import functools
import math

import jax
import jax.numpy as jnp
import numpy as np
from jax import lax
from jax.experimental import pallas as pl
from jax.experimental.pallas import tpu as pltpu

F32 = jnp.float32
BF16 = jnp.bfloat16

D_MODEL = 1024
DEPTH = 2
RW_HEADS = 8
RW_HEAD_DIM = 64
RW_WIDTH = 512
RW_DECAY_RANK = 64
RW_ICLR_RANK = 64
RW_GATE_RANK = 160
RW_GN_EPS = 64e-5
RW_PROJ = 3 * RW_WIDTH + RW_DECAY_RANK + RW_ICLR_RANK + RW_GATE_RANK
SC_WIDTH = 512
SC_CONV = 3
DIL_PATTERNS = ((128, 1), (512, 4), (2048, 16))
N_GROUPS = 3
DIL_HEADS = 8
DIL_HEAD_DIM = 64
DIL_WIDTH = 512
BLOCK = 128
N_BUCKETS = 32
MAX_DISTANCE = 2048
D_FF = 4 * D_MODEL
DEEPNORM_ALPHA = (2 * DEPTH) ** 0.25
LN_EPS = 1e-5

LANES = 128
MXU_DIM = 256
CHUNK = 64
INV_BLOCK = 16
LR_PAD = 384
VMEM_LIMIT = 56 * 1024 * 1024
TAIL_VMEM_LIMIT = 60 * 1024 * 1024
MASKED = -1e30


def _cparams(n_axes, vmem_limit=VMEM_LIMIT):
    return pltpu.CompilerParams(dimension_semantics=("arbitrary",) * n_axes,
                                vmem_limit_bytes=vmem_limit)


def _sigmoid(x):
    return 1.0 / (1.0 + jnp.exp(-x))


def _dot(a, b):
    return jnp.dot(a.astype(BF16), b.astype(BF16), preferred_element_type=F32)


def _layer_norm_rows(z, g, b):
    mu = jnp.mean(z, axis=-1, keepdims=True)
    zc = z - mu
    var = jnp.mean(zc * zc, axis=-1, keepdims=True)
    return zc * lax.rsqrt(var + LN_EPS) * g + b


def _adaln_kernel(c_ref, w_ref, b_ref, o_ref):
    c = c_ref[...]
    cond = c * _sigmoid(c)
    o_ref[0] = _dot(cond, w_ref[0]) + b_ref[0]


def _adaln(c, ada_w, ada_b):
    depth, d, n = ada_w.shape
    bsz = c.shape[0]
    tn = 3072
    return pl.pallas_call(
        _adaln_kernel,
        grid=(depth, n // tn),
        in_specs=[pl.BlockSpec((bsz, d), lambda i, j: (0, 0)),
                  pl.BlockSpec((1, d, tn), lambda i, j: (i, 0, j)),
                  pl.BlockSpec((1, 1, tn), lambda i, j: (i, 0, j))],
        out_specs=pl.BlockSpec((1, bsz, tn), lambda i, j: (i, 0, j)),
        out_shape=jax.ShapeDtypeStruct((depth, bsz, n), F32),
        compiler_params=_cparams(2),
        name="adaln",
    )(c, ada_w, ada_b.reshape(depth, 1, n))


def _shift_rows(x, carry, k):
    rolled = pltpu.roll(x, k, 0)
    head = rolled[:8]
    row = lax.broadcasted_iota(jnp.int32, head.shape, 0)
    for i in range(k):
        head = jnp.where(row == i, carry[8 - k + i:8 - k + i + 1], head)
    return jnp.concatenate([head, rolled[8:]], axis=0)


def _ab_proj_kernel(x_ref, mod_ref, wt_ref, mu_ref, convw_ref, rkv_ref, lr_ref, yb_ref, carry_p, carry_z, *, tm):
    @pl.when(pl.program_id(1) == 0)
    def _():
        carry_p[...] = jnp.zeros_like(carry_p)
        carry_z[...] = jnp.zeros_like(carry_z)

    m = mod_ref[0]
    u = (x_ref[0] * (1.0 + m[1:2]) + m[0:1]).astype(BF16)

    def proj(c0, width):
        return lax.dot_general(u, wt_ref[c0:c0 + width, :].astype(BF16), (((1,), (1,)), ((), ())),
                               preferred_element_type=F32)

    def mixed(c0, width):
        p = proj(c0, width)
        prev = _shift_rows(p, carry_p[:, c0:c0 + width], 1)
        carry_p[:, c0:c0 + width] = p[tm - 8:]
        return (p + mu_ref[:, c0:c0 + width] * (prev - p)).astype(BF16)

    w3 = 3 * RW_WIDTH
    for c in range(3):
        rkv_ref[0, :, c * RW_WIDTH:(c + 1) * RW_WIDTH] = mixed(c * RW_WIDTH, RW_WIDTH)
    lr_ref[0] = mixed(w3, LR_PAD)

    h = proj(RW_PROJ, SC_WIDTH)
    z = proj(RW_PROJ + 2 * SC_WIDTH, SC_WIDTH) * h
    zc = carry_z[...]
    cw = convw_ref[...]
    conv = cw[2:3] * z + cw[1:2] * _shift_rows(z, zc, 1) + cw[0:1] * _shift_rows(z, zc, 2)
    carry_z[...] = z[tm - 8:]
    yb_ref[0] = (proj(RW_PROJ + SC_WIDTH, SC_WIDTH) * conv).astype(BF16)


def _ab_proj(x, mod, w_t, mu, conv_w, *, tm=1024):
    bsz, t, d = x.shape
    w3 = 3 * RW_WIDTH

    def const(shape):
        return pl.BlockSpec(shape, lambda b, i: (0,) * len(shape), pipeline_mode=pl.Buffered(1))

    def rows(width):
        return pl.BlockSpec((1, tm, width), lambda b, i: (b, i, 0))

    return pl.pallas_call(
        functools.partial(_ab_proj_kernel, tm=tm),
        grid=(bsz, t // tm),
        in_specs=[rows(d), pl.BlockSpec((1, 6, d), lambda b, i: (b, 0, 0)),
                  const(w_t.shape), const((1, w3 + LR_PAD)), const((SC_CONV, SC_WIDTH))],
        out_specs=[rows(w3), rows(LR_PAD), rows(SC_WIDTH)],
        out_shape=[jax.ShapeDtypeStruct((bsz, t, w3), BF16), jax.ShapeDtypeStruct((bsz, t, LR_PAD), BF16),
                   jax.ShapeDtypeStruct((bsz, t, SC_WIDTH), BF16)],
        scratch_shapes=[pltpu.VMEM((8, w3 + LR_PAD), F32), pltpu.VMEM((8, SC_WIDTH), F32)],
        compiler_params=_cparams(2),
        name="ab_proj",
    )(x, mod, w_t, mu, conv_w)


def _rwkv_chunk_terms(a_t, r_t, b_t, k_t, b_e, k_e, v, p_end, hooks=()):
    n = len(a_t)
    idx = range(n)
    hooks = list(hooks)

    def hook():
        if hooks:
            hooks.pop(0)()
    row = lax.broadcasted_iota(jnp.int32, (CHUNK, LANES), 0)
    col = lax.broadcasted_iota(jnp.int32, (CHUNK, LANES), 1)
    s_idx = col % CHUNK
    strict = s_idx < row
    incl = s_idx <= row
    near = strict & ((s_idx // INV_BLOCK) == (row // INV_BLOCK))
    far = strict & ((s_idx // INV_BLOCK) != (row // INV_BLOCK))
    eye = jnp.where(s_idx == row, 1.0, 0.0)
    r2 = lax.broadcasted_iota(jnp.int32, (LANES, 2 * LANES), 0)
    c2 = lax.broadcasted_iota(jnp.int32, (LANES, 2 * LANES), 1)
    bdmask2 = (r2 // RW_HEAD_DIM) == ((c2 % LANES) // RW_HEAD_DIM)
    eye_bd = jnp.where(lax.broadcasted_iota(jnp.int32, (LANES, LANES), 0)
                       == lax.broadcasted_iota(jnp.int32, (LANES, LANES), 1), 1.0, 0.0)
    zero16 = jnp.zeros((CHUNK, LANES), BF16)
    lane1 = lax.broadcasted_iota(jnp.int32, (1, LANES), 1)
    m0_16 = jnp.where(lane1 < RW_HEAD_DIM, 1.0, 0.0).astype(BF16)
    m1_16 = jnp.where(lane1 < RW_HEAD_DIM, 0.0, 1.0).astype(BF16)

    def cast(xs):
        return [x.astype(BF16) for x in xs]

    def bd(q16):
        return jnp.concatenate([q16 * m0_16, q16 * m1_16], axis=0)

    def mm(p16, q16):
        return jnp.dot(p16, q16, preferred_element_type=F32)

    def mm_nt(p16, q16):
        return lax.dot_general(p16, q16, (((1,), (1,)), ((), ())), preferred_element_type=F32)

    a16, r16, v16 = cast(a_t), cast(r_t), cast(v)
    lhs = [jnp.concatenate([a16[i], r16[i]], axis=0) for i in idx]
    b16, k16 = cast(b_t), cast(k_t)
    bdv = [bd(x) for x in v16]
    g = [mm_nt(lhs[i], jnp.concatenate([bd(b16[i]), bd(k16[i])], axis=0)) for i in idx]
    x1 = [jnp.where(near, g[i][:CHUNK, :LANES], 0.0) for i in idx]
    x1_16 = cast(x1)
    l_e16 = [jnp.where(far, g[i][:CHUNK, :LANES], 0.0).astype(BF16) for i in idx]
    a_k16 = [jnp.concatenate([jnp.where(strict, g[i][:CHUNK, LANES:], 0.0),
                              jnp.where(incl, g[i][CHUNK:, LANES:], 0.0)], axis=0).astype(BF16) for i in idx]
    a_rb16 = [jnp.where(incl, g[i][CHUNK:, :LANES], 0.0).astype(BF16) for i in idx]

    def both(p16, q16):
        return jnp.concatenate([bd(p16), bd(q16)], axis=1)

    x2_16 = cast([mm(x1_16[i], bd(x1_16[i])) for i in idx])
    hook()
    acc = [eye + x1[i] for i in idx]
    sq = [mm(x2_16[i], both(x2_16[i], acc[i].astype(BF16))) for i in idx]
    x4_16 = [sq[i][:, :LANES].astype(BF16) for i in idx]
    acc = [acc[i] + sq[i][:, LANES:] for i in idx]
    sq = [mm(x4_16[i], both(x4_16[i], acc[i].astype(BF16))) for i in idx]
    hook()
    x8_16 = [sq[i][:, :LANES].astype(BF16) for i in idx]
    acc = [acc[i] + sq[i][:, LANES:] for i in idx]
    dinv = [acc[i] + mm(x8_16[i], bd(acc[i].astype(BF16))) for i in idx]
    dinv16 = cast(dinv)
    hook()
    f16 = cast([mm(dinv16[i], bd(l_e16[i])) for i in idx])
    sq = [mm(f16[i], both(f16[i], dinv16[i])) for i in idx]
    f2_16 = [sq[i][:, :LANES].astype(BF16) for i in idx]
    hook()
    hmat = [dinv[i] + sq[i][:, LANES:] for i in idx]
    tinv16 = cast([hmat[i] + mm(f2_16[i], bd(hmat[i].astype(BF16))) for i in idx])

    kv = [mm(a_k16[i], bdv[i]) for i in idx]
    wv16 = [kv[i][:CHUNK].astype(BF16) for i in idx]
    hook()
    aw = [mm(tinv16[i], both(a16[i], wv16[i])) for i in idx]
    aw16 = cast(aw)
    hook()
    qy = [mm(a_rb16[i], both(aw16[i][:, :LANES], aw16[i][:, LANES:]))
          + jnp.concatenate([r_t[i], kv[i][CHUNK:]], axis=1) for i in idx]
    be_ke_t = [jnp.concatenate([b_e[i], k_e[i]], axis=0).T.astype(BF16) for i in idx]
    rhs = [jnp.concatenate([aw16[i], jnp.concatenate([zero16, v16[i]], axis=1)], axis=0) for i in idx]
    mn = [jnp.where(bdmask2, mm(be_ke_t[i], rhs[i]), 0.0) for i in idx]
    qhat = [x[:, :LANES] for x in qy]
    yhat = [x[:, LANES:] for x in qy]
    mmat = [eye_bd * p_end[i] + mn[i][:, :LANES] for i in idx]
    nmat = [x[:, LANES:] for x in mn]
    while hooks:
        hook()
    return qhat, yhat, mmat, nmat


def _rwkv_kernel(rkv_ref, lr_ref, w0_ref, a0_ref, wwa_ref, gup_ref,
                 kk_ref, ka_ref, rk_ref, lnxg_ref, lnxb_ref, o_ref, state, yout_s, *, tb):
    n_chunks = tb // CHUNK
    n_pairs = RW_WIDTH // LANES

    @pl.when(pl.program_id(1) == 0)
    def _():
        state[...] = jnp.zeros_like(state)

    r = rkv_ref[0, :, :RW_WIDTH].astype(F32)
    k = rkv_ref[0, :, RW_WIDTH:2 * RW_WIDTH].astype(F32)
    v = rkv_ref[0, :, 2 * RW_WIDTH:].astype(F32)
    lr_m = lr_ref[0].astype(F32)

    wa = lr_m[:, :LANES]
    lane = lax.broadcasted_iota(jnp.int32, wa.shape, 1)
    wa = jnp.where(lane < RW_DECAY_RANK, jnp.tanh(wa), wa)
    wa_up = _dot(wa, wwa_ref[...])
    ld = -math.exp(-0.5) * _sigmoid(w0_ref[...] + wa_up[:, :RW_WIDTH])
    iclr = _sigmoid(a0_ref[...] + wa_up[:, RW_WIDTH:])
    gate = _dot(_sigmoid(lr_m[:, LANES:]), gup_ref[...])

    hr = lax.broadcasted_iota(jnp.int32, (MXU_DIM, MXU_DIM), 0) // RW_HEAD_DIM
    hc = lax.broadcasted_iota(jnp.int32, (MXU_DIM, MXU_DIM), 1) // RW_HEAD_DIM
    seg = jnp.where(hr == hc, 1.0 / RW_HEAD_DIM, 0.0).astype(BF16)

    def seg_mean(x):
        x16 = x.astype(BF16)
        return jnp.concatenate([jnp.dot(x16[:, c:c + MXU_DIM], seg, preferred_element_type=F32)
                                for c in range(0, RW_WIDTH, MXU_DIM)], axis=1)

    kk = k * kk_ref[...]
    ss = seg_mean(kk * kk) * RW_HEAD_DIM
    kk = kk * lax.rsqrt(jnp.maximum(ss, 1e-24))
    kh = k * (1.0 + (iclr - 1.0) * ka_ref[...])

    tr = lax.broadcasted_iota(jnp.int32, (MXU_DIM, MXU_DIM), 0)
    tc = lax.broadcasted_iota(jnp.int32, (MXU_DIM, MXU_DIM), 1)
    tri = jnp.where((tc <= tr) & (tc // CHUNK == tr // CHUNK), 1.0, 0.0).astype(BF16)
    ld_hi = ld.astype(BF16)
    ld_lo = (ld - ld_hi.astype(F32)).astype(BF16)
    lp = jnp.concatenate(
        [jnp.dot(tri, ld_hi[t0:t0 + MXU_DIM], preferred_element_type=F32)
         + jnp.dot(tri, ld_lo[t0:t0 + MXU_DIM], preferred_element_type=F32)
         for t0 in range(0, tb, MXU_DIM)], axis=0)

    a_t, r_t, b_t, k_t, b_e, k_e, vs, p_end = [], [], [], [], [], [], [], []
    b = kk * iclr
    for c in range(n_chunks):
        rs = slice(c * CHUNK, (c + 1) * CHUNK)
        lp_c = lp[rs]
        lp_last = lp_c[CHUNK - 1:CHUNK]
        e_in = jnp.exp(lp_c)
        e_inv = jnp.exp(-lp_c)
        e_end = jnp.exp(lp_last - lp_c)
        a_c = -kk[rs] * jnp.exp(lp_c - ld[rs])
        r_c = r[rs] * e_in
        b_c = b[rs] * e_inv
        k_c = kh[rs] * e_inv
        be_c = b[rs] * e_end
        ke_c = kh[rs] * e_end
        pe_c = jnp.exp(lp_last)
        for p in range(n_pairs):
            cs = slice(p * LANES, (p + 1) * LANES)
            a_t.append(a_c[:, cs])
            r_t.append(r_c[:, cs])
            b_t.append(b_c[:, cs])
            k_t.append(k_c[:, cs])
            b_e.append(be_c[:, cs])
            k_e.append(ke_c[:, cs])
            vs.append(v[rs, cs])
            p_end.append(pe_c[:, cs])

    bonus = seg_mean(r * kh * rk_ref[...]) * RW_HEAD_DIM * v
    s = [state[p] for p in range(n_pairs)]

    def serial_step(c, terms, c0):
        qhat, yhat, mmat, nmat = terms
        base = (c - c0) * n_pairs
        ym = [_dot(jnp.concatenate([qhat[base + p], mmat[base + p]], axis=0), s[p]) for p in range(n_pairs)]
        for p in range(n_pairs):
            yout_s[c * CHUNK:(c + 1) * CHUNK, p * LANES:(p + 1) * LANES] = ym[p][:CHUNK] + yhat[base + p]
            s[p] = ym[p][CHUNK:] + nmat[base + p]

    def finish(rows):
        y = yout_s[rows, :]
        yc = y - seg_mean(y)
        var = seg_mean(yc * yc)
        y = yc * lax.rsqrt(var + RW_GN_EPS) * lnxg_ref[...] + lnxb_ref[...]
        o_ref[0, rows, :] = ((y + bonus[rows]) * gate[rows]).astype(o_ref.dtype)

    half = n_chunks // 2
    cut = half * n_pairs
    slabs = (a_t, r_t, b_t, k_t, b_e, k_e, vs, p_end)
    first = _rwkv_chunk_terms(*[x[:cut] for x in slabs])
    hooks = [functools.partial(serial_step, c, first, 0) for c in range(half)]
    hooks.append(functools.partial(finish, slice(0, half * CHUNK)))
    second = _rwkv_chunk_terms(*[x[cut:] for x in slabs], hooks=hooks)
    for c in range(half, n_chunks):
        serial_step(c, second, half)
    for p in range(n_pairs):
        state[p] = s[p]
    finish(slice(half * CHUNK, tb))


def _rwkv(rkv, lr, w0, a0, wwa, gup, k_k, k_a, r_k, lnx_g, lnx_b, *, tb=512):
    bsz, t, w3 = rkv.shape
    n_pairs = RW_WIDTH // LANES

    def full(shape):
        return pl.BlockSpec(shape, lambda b, i: (0,) * len(shape))

    return pl.pallas_call(
        functools.partial(_rwkv_kernel, tb=tb),
        grid=(bsz, t // tb),
        in_specs=[pl.BlockSpec((1, tb, w3), lambda b, i: (b, i, 0)),
                  pl.BlockSpec((1, tb, LR_PAD), lambda b, i: (b, i, 0)),
                  full((1, RW_WIDTH)), full((1, RW_WIDTH)),
                  full((LANES, 2 * RW_WIDTH)), full((LR_PAD - LANES, RW_WIDTH)),
                  full((1, RW_WIDTH)), full((1, RW_WIDTH)), full((1, RW_WIDTH)),
                  full((1, RW_WIDTH)), full((1, RW_WIDTH))],
        out_specs=pl.BlockSpec((1, tb, RW_WIDTH), lambda b, i: (b, i, 0)),
        out_shape=jax.ShapeDtypeStruct((bsz, t, RW_WIDTH), BF16),
        scratch_shapes=[pltpu.VMEM((n_pairs, LANES, LANES), F32),
                        pltpu.VMEM((tb, RW_WIDTH), F32)],
        compiler_params=_cparams(2),
        name="rwkv",
    )(rkv, lr, w0, a0, wwa, gup, k_k, k_a, r_k, lnx_g, lnx_b)


def _attn_kernel(q_ref, kp_ref, kc_ref, vp_ref, vc_ref, bias_ref, o_ref, lse_ref, *, nb, rb):
    step = pl.program_id(1)
    n_blk = rb // BLOCK
    scale = DIL_HEAD_DIM ** -0.5
    n_pairs = DIL_WIDTH // LANES
    krow = lax.broadcasted_iota(jnp.int32, (2 * BLOCK, LANES), 0) // BLOCK
    klane = lax.broadcasted_iota(jnp.int32, (2 * BLOCK, LANES), 1) // DIL_HEAD_DIM
    own = krow == klane
    zeros = jnp.zeros((2 * BLOCK, LANES), BF16)
    ones_bd = jnp.where(own, 1.0, 0.0).astype(BF16)
    lane_o = lax.broadcasted_iota(jnp.int32, (BLOCK, LANES), 1)

    def stack_bd(x):
        return jnp.where(own, jnp.concatenate([x, x], axis=0), zeros)

    def attend(blocks):
        units = [(rows, q, parts, p) for rows, q, parts in blocks for p in range(n_pairs)]
        lanes = [slice(p * LANES, (p + 1) * LANES) for p in range(n_pairs)]
        scores = []
        for _, q, parts, p in units:
            qs = q[:, lanes[p]] * scale
            row = []
            for k_blk, _, col0, pen in parts:
                s = lax.dot_general(qs, stack_bd(k_blk[:, lanes[p]]), (((1,), (1,)), ((), ())),
                                    preferred_element_type=F32)
                bias = jnp.concatenate([bias_ref[2 * p, :, col0:col0 + BLOCK],
                                        bias_ref[2 * p + 1, :, col0:col0 + BLOCK]], axis=1)
                s = s + bias
                row.append(s if pen is None else s + pen)
            scores.append(row)
        ms = []
        for u in range(len(units)):
            halves = []
            for hh in range(2):
                cols = slice(hh * BLOCK, (hh + 1) * BLOCK)
                m = jnp.max(scores[u][0][:, cols], axis=-1, keepdims=True)
                for s in scores[u][1:]:
                    m = jnp.maximum(m, jnp.max(s[:, cols], axis=-1, keepdims=True))
                halves.append(m)
            ms.append(halves)
        es = [[jnp.exp((s - jnp.concatenate([jnp.broadcast_to(ms[u][0], (BLOCK, BLOCK)),
                                             jnp.broadcast_to(ms[u][1], (BLOCK, BLOCK))], axis=1)).astype(BF16))
               for s in scores[u]] for u in range(len(units))]
        res = []
        for u, (_, _, parts, p) in enumerate(units):
            res.append(sum(jnp.dot(e, jnp.concatenate([stack_bd(part[1][:, lanes[p]]), ones_bd], axis=1),
                                   preferred_element_type=F32) for e, part in zip(es[u], parts)))
        for b, (rows, _, _) in enumerate(blocks):
            m_all = jnp.zeros((BLOCK, LANES), F32)
            den_all = jnp.ones((BLOCK, LANES), F32)
            for p in range(n_pairs):
                u = b * n_pairs + p
                num, den = res[u][:, :LANES], res[u][:, LANES:]
                o_ref[rows, lanes[p]] = (num / den).astype(o_ref.dtype)
                m_all = jnp.where(lane_o == 2 * p, ms[u][0], jnp.where(lane_o == 2 * p + 1, ms[u][1], m_all))
                den_all = jnp.where((lane_o == 2 * p), den,
                                    jnp.where(lane_o == 2 * p + 1, pltpu.roll(den, DIL_HEAD_DIM, 1), den_all))
            lse_ref[rows, :] = m_all + jnp.log(den_all)

    blocks = []
    for j in range(n_blk):
        rows = slice(j * BLOCK, (j + 1) * BLOCK)
        cur = (kc_ref[rows, :], vc_ref[rows, :], BLOCK, None)
        if n_blk % nb == 0:
            prev = "none" if j % nb == 0 else "inside"
        else:
            prev = "inside" if j > 0 else ("none" if nb == n_blk else "before")
        if prev == "none":
            parts = [cur]
        elif prev == "inside":
            before = slice((j - 1) * BLOCK, j * BLOCK)
            parts = [(kc_ref[before, :], vc_ref[before, :], 0, None), cur]
        else:
            has_prev = (step % (nb // n_blk)) != 0
            parts = [(kp_ref[...], vp_ref[...], 0, jnp.where(has_prev, 0.0, MASKED)), cur]
        blocks.append((rows, q_ref[rows, :], parts))
    attend(blocks)


def _qkv_proj_kernel(x_ref, mod_ref, w_ref, o0_ref, o1_ref, o2_ref, xs_ref, xs4_ref, *, tm):
    m = mod_ref[0]
    scale = 1.0 + m[1:2]
    shift = m[0:1]
    n_col = x_ref.shape[-1] // LANES
    gw = 3 * DIL_WIDTH
    step = DIL_PATTERNS[1][1]
    assert [dil for _, dil in DIL_PATTERNS] == [1, step, step * step]

    for c in range(n_col):
        xs_ref[c] = x_ref[0, :, c * LANES:(c + 1) * LANES]
    rows4 = tm // step
    x4 = [jnp.concatenate([xs_ref[c, pl.ds(a, rows4, stride=step), :] for a in range(step)], axis=0)
          for c in range(n_col)]
    for c in range(n_col):
        xs4_ref[c] = x4[c]
    rows16 = rows4 // step
    x16 = [jnp.concatenate([xs4_ref[c, pl.ds(a * rows4 + b, rows16, stride=step), :]
                            for a in range(step) for b in range(step)], axis=0) for c in range(n_col)]
    inputs = [x_ref[0], jnp.concatenate(x4, axis=1), jnp.concatenate(x16, axis=1)]
    residue_of_group = [[0], list(range(step)), [a + step * b for a in range(step) for b in range(step)]]

    for g, o_ref in enumerate((o0_ref, o1_ref, o2_ref)):
        u = (inputs[g] * scale + shift).astype(BF16)
        res = jnp.dot(u, w_ref[:, g * gw:(g + 1) * gw].astype(BF16), preferred_element_type=F32).astype(BF16)
        rows = tm // len(residue_of_group[g])
        for grp, r in enumerate(residue_of_group[g]):
            o_ref[0, r] = res[grp * rows:(grp + 1) * rows]


def _qkv_proj(x, mod, w, *, tm=1024):
    bsz, t, d = x.shape
    gw = 3 * DIL_WIDTH
    out_specs, out_shape = [], []
    for _, dil in DIL_PATTERNS:
        out_specs.append(pl.BlockSpec((1, dil, tm // dil, gw), lambda b, i: (b, 0, i, 0)))
        out_shape.append(jax.ShapeDtypeStruct((bsz, dil, t // dil, gw), BF16))
    return pl.pallas_call(
        functools.partial(_qkv_proj_kernel, tm=tm),
        grid=(bsz, t // tm),
        in_specs=[pl.BlockSpec((1, tm, d), lambda b, i: (b, i, 0)),
                  pl.BlockSpec((1, 6, d), lambda b, i: (b, 0, 0)),
                  pl.BlockSpec(w.shape, lambda b, i: (0, 0), pipeline_mode=pl.Buffered(1))],
        out_specs=out_specs,
        out_shape=out_shape,
        scratch_shapes=[pltpu.VMEM((d // LANES, tm, LANES), F32), pltpu.VMEM((d // LANES, tm, LANES), F32)],
        compiler_params=_cparams(2),
        name="qkv_proj",
    )(x, mod, w)


def _dilated_attention_group(qkv, bias, g, *, rb=2048):
    bsz, dil, length, width = qkv.shape
    t = dil * length
    nb = length // BLOCK
    n_blk = rb // BLOCK
    assert nb % n_blk == 0 or n_blk % nb == 0
    flat = qkv.reshape(bsz, t, width)

    def cur(j):
        return lambda b, s: (b, s, j)

    def prev(j):
        return lambda b, s: (b, jnp.maximum(s * n_blk - 1, 0), j)

    blk = (None, rb, DIL_WIDTH)
    pblk = (None, BLOCK, DIL_WIDTH)
    o, lse = pl.pallas_call(
        functools.partial(_attn_kernel, nb=nb, rb=rb),
        grid=(bsz, t // rb),
        in_specs=[pl.BlockSpec(blk, cur(0)),
                  pl.BlockSpec(pblk, prev(1)), pl.BlockSpec(blk, cur(1)),
                  pl.BlockSpec(pblk, prev(2)), pl.BlockSpec(blk, cur(2)),
                  pl.BlockSpec((DIL_HEADS, BLOCK, 2 * BLOCK), lambda b, s: (0, 0, 0))],
        out_specs=[pl.BlockSpec(blk, lambda b, s: (b, s, 0)),
                   pl.BlockSpec((None, rb, LANES), lambda b, s: (b, s, 0))],
        out_shape=[jax.ShapeDtypeStruct((bsz, t, DIL_WIDTH), BF16),
                   jax.ShapeDtypeStruct((bsz, t, LANES), F32)],
        compiler_params=_cparams(2),
        name=f"dilated_attn_g{g}",
    )(flat, flat, flat, flat, flat, bias)
    return o.reshape(bsz, dil, length, DIL_WIDTH), lse.reshape(bsz, dil, length, LANES)


def _t5_bucket(dist):
    exact = N_BUCKETS // 2
    logd = jnp.log(jnp.maximum(dist, 1).astype(F32) / exact) / math.log(MAX_DISTANCE / exact)
    large = jnp.minimum(exact + (logd * (N_BUCKETS - exact)).astype(jnp.int32), N_BUCKETS - 1)
    return jnp.where(dist < exact, dist, large)


def _bias_tables(rel_bias):
    rel = jnp.arange(BLOCK + 1)
    tables = []
    for g, (window, dil) in enumerate(DIL_PATTERNS):
        span = window // dil
        bucket = _t5_bucket(jnp.clip(rel, 0, span) * dil)
        vec = rel_bias[bucket][:, g * DIL_HEADS:(g + 1) * DIL_HEADS].T.astype(F32)
        ext = jnp.concatenate([vec[:, ::-1], jnp.zeros((DIL_HEADS, BLOCK), F32)], axis=1)
        flat = jnp.tile(ext, (1, BLOCK))[:, :BLOCK * 2 * BLOCK]
        tables.append(flat.reshape(DIL_HEADS, BLOCK, 2 * BLOCK))
    qi = np.arange(BLOCK)[:, None]
    ki = np.arange(2 * BLOCK)[None, :]
    band = (ki >= qi) & (ki <= qi + BLOCK)
    return jnp.where(band, jnp.stack(tables), MASKED)


def _tail(mix_fn, wout_ref, x_ref, mod_ref, lng_ref, lnb_ref, w1_ref, w2_ref, o_ref, *, tm, n_sub, ff_chunk):
    m = mod_ref[0]
    sub = tm // n_sub
    tiles = [slice(i * sub, (i + 1) * sub) for i in range(n_sub)]
    x1s, us = [], []
    for rows in tiles:
        y = None
        for piece, k0 in mix_fn(rows):
            part = jnp.dot(piece, wout_ref[k0:k0 + piece.shape[-1], :], preferred_element_type=F32)
            y = part if y is None else y + part
        x1 = _layer_norm_rows(DEEPNORM_ALPHA * x_ref[0, rows, :] + (1.0 + m[2:3]) * y, lng_ref[0:1], lnb_ref[0:1])
        x1s.append(x1)
        us.append((x1 * (1.0 + m[4:5]) + m[3:4]).astype(BF16))
    accs = [None] * n_sub
    for c in range(D_FF // ff_chunk):
        cols = slice(c * ff_chunk, (c + 1) * ff_chunk)
        w1c = w1_ref[:, cols].astype(BF16)
        w2c = w2_ref[cols, :].astype(BF16)
        for i in range(n_sub):
            h = jnp.maximum(jnp.dot(us[i], w1c, preferred_element_type=F32), 0.0)
            part = jnp.dot((h * h).astype(BF16), w2c, preferred_element_type=F32)
            accs[i] = part if accs[i] is None else accs[i] + part
    for i, rows in enumerate(tiles):
        z = DEEPNORM_ALPHA * x1s[i] + (1.0 + m[5:6]) * accs[i]
        o_ref[0, rows, :] = _layer_norm_rows(z, lng_ref[1:2], lnb_ref[1:2])


def _even_tail_kernel(ya_ref, yb_ref, wout_ref, x_ref, mod_ref, lng_ref, lnb_ref, w1_ref, w2_ref, o_ref, **kw):
    def mix(rows):
        return [(ya_ref[0, rows, :], 0), (yb_ref[0, rows, :], ya_ref.shape[-1])]

    _tail(mix, wout_ref, x_ref, mod_ref, lng_ref, lnb_ref, w1_ref, w2_ref, o_ref, **kw)


def _odd_tail_kernel(o0_ref, o1_ref, o2_ref, l0_ref, l1_ref, l2_ref, wout_ref, x_ref, mod_ref, lng_ref, lnb_ref,
                     w1_ref, w2_ref, o_ref, o_nat, l_nat, mix_s, *, tm, **kw):
    def natural(src_ref, scr, dil):
        if dil == 1:
            return src_ref[0, 0].astype(F32)
        n_col = src_ref.shape[-1] // LANES
        for c in range(n_col):
            for r in range(dil):
                scr[c, pl.ds(r, tm // dil, stride=dil), :] = src_ref[0, r, :, c * LANES:(c + 1) * LANES].astype(F32)
        return jnp.concatenate([scr[c] for c in range(n_col)], axis=1)

    dils = [dil for _, dil in DIL_PATTERNS]
    lses = [natural(ref, l_nat, dil) for ref, dil in zip((l0_ref, l1_ref, l2_ref), dils)]
    mx = jnp.maximum(jnp.maximum(lses[0], lses[1]), lses[2])
    es = [jnp.exp(l - mx) for l in lses]
    inv = 1.0 / (es[0] + es[1] + es[2])
    hr = lax.broadcasted_iota(jnp.int32, (LANES, DIL_WIDTH), 0)
    hc = lax.broadcasted_iota(jnp.int32, (LANES, DIL_WIDTH), 1) // DIL_HEAD_DIM
    expand = jnp.where(hr == hc, 1.0, 0.0).astype(BF16)

    def widen(wt):
        return jnp.dot(wt.astype(BF16), expand, preferred_element_type=F32)

    merged = None
    for e, ref, dil in zip(es, (o0_ref, o1_ref, o2_ref), dils):
        term = widen(e * inv) * natural(ref, o_nat, dil)
        merged = term if merged is None else merged + term
    mix_s[...] = merged.astype(BF16)
    _tail(lambda rows: [(mix_s[rows, :], 0)], wout_ref, x_ref, mod_ref, lng_ref, lnb_ref, w1_ref, w2_ref, o_ref,
          tm=tm, **kw)


def _layer_tail(mix, w_out, x, mod, ln_g, ln_b, w1, w2, layer, *, tm=512, n_sub=2, ff_chunk=1024):
    bsz, t, d = x.shape

    def const(shape):
        return pl.BlockSpec(shape, lambda b, i: (0,) * len(shape), pipeline_mode=pl.Buffered(1))

    def of_layer(shape):
        return pl.BlockSpec((None,) + shape, lambda b, i: (layer,) + (0,) * len(shape), pipeline_mode=pl.Buffered(1))

    tail_specs = [const(w_out.shape),
                  pl.BlockSpec((1, tm, d), lambda b, i: (b, i, 0)),
                  pl.BlockSpec((1, 6, d), lambda b, i: (b, 0, 0)),
                  of_layer((2, d)), of_layer((2, d)), of_layer((d, D_FF)), of_layer((D_FF, d))]
    tail_args = (w_out, x, mod, ln_g, ln_b, w1, w2)
    kw = dict(tm=tm, n_sub=n_sub, ff_chunk=ff_chunk)
    if isinstance(mix, tuple):
        outs, lses = mix
        mix_specs = ([pl.BlockSpec((1, dil, tm // dil, DIL_WIDTH), lambda b, i: (b, 0, i, 0)) for _, dil in DIL_PATTERNS]
                     + [pl.BlockSpec((1, dil, tm // dil, LANES), lambda b, i: (b, 0, i, 0)) for _, dil in DIL_PATTERNS])
        mix_args = (*outs, *lses)
        body = functools.partial(_odd_tail_kernel, **kw)
        scratch = [pltpu.VMEM((DIL_WIDTH // LANES, tm, LANES), F32), pltpu.VMEM((1, tm, LANES), F32),
                   pltpu.VMEM((tm, DIL_WIDTH), BF16)]
    else:
        mix_specs = [pl.BlockSpec((1, tm, piece.shape[-1]), lambda b, i: (b, i, 0)) for piece in mix]
        mix_args = tuple(mix)
        body = functools.partial(_even_tail_kernel, **kw)
        scratch = []
    return pl.pallas_call(
        body,
        grid=(bsz, t // tm),
        in_specs=mix_specs + tail_specs,
        out_specs=pl.BlockSpec((1, tm, d), lambda b, i: (b, i, 0)),
        out_shape=jax.ShapeDtypeStruct((bsz, t, d), F32),
        scratch_shapes=scratch,
        compiler_params=_cparams(2, TAIL_VMEM_LIMIT),
        name="layer_tail",
    )(*mix_args, *tail_args)


def _prep_even_weights(mu, w_up, a_up, g_up):
    pad = LR_PAD - (RW_PROJ - 3 * RW_WIDTH)
    mu_pad = jnp.concatenate([mu, jnp.zeros((pad,), mu.dtype)]).reshape(1, 3 * RW_WIDTH + LR_PAD)
    zeros = jnp.zeros((RW_DECAY_RANK, RW_WIDTH), w_up.dtype)
    wwa = jnp.concatenate([jnp.concatenate([w_up, zeros], axis=1),
                           jnp.concatenate([zeros, a_up], axis=1)], axis=0).astype(BF16)
    gup = jnp.concatenate([g_up, jnp.zeros((pad, RW_WIDTH), g_up.dtype)], axis=0).astype(BF16)
    return mu_pad, wwa, gup


def kernel(x, c, ada_w, ada_b, ln_g, ln_b, ab_w_in, rw_mu, rw_w0, rw_w_up, rw_a0, rw_a_up, rw_g_up,
           rw_k_k, rw_k_a, rw_r_k, rw_lnx_g, rw_lnx_b, sc_conv_w, ab_w_out, dil_w_qkv, dil_w_out,
           rel_bias, mlp_w1, mlp_w2):
    bsz, t, d = x.shape
    mods = _adaln(c, ada_w, ada_b).reshape(DEPTH, bsz, 6, d)
    bias = None
    row = lambda a: a.reshape(1, -1)
    w1_all, w2_all = mlp_w1, mlp_w2
    for i in range(DEPTH):
        mod = mods[i]
        j = i // 2
        if i % 2 == 0:
            mu_pad, wwa, gup = _prep_even_weights(rw_mu[j], rw_w_up[j], rw_a_up[j], rw_g_up[j])
            rkv, lr, yb = _ab_proj(x, mod, ab_w_in[j].T, mu_pad, sc_conv_w[j])
            ya = _rwkv(rkv, lr, row(rw_w0[j]), row(rw_a0[j]), wwa, gup,
                       row(rw_k_k[j]), row(rw_k_a[j]), row(rw_r_k[j]),
                       row(rw_lnx_g[j]), row(rw_lnx_b[j]))
            mix, w_out = [ya, yb], ab_w_out[j]
        else:
            if bias is None:
                bias = _bias_tables(rel_bias)
            qkvs = _qkv_proj(x, mod, dil_w_qkv[j])
            outs, lses = [], []
            for g in range(N_GROUPS):
                o, lse = _dilated_attention_group(qkvs[g], bias[g], g)
                outs.append(o)
                lses.append(lse)
            mix, w_out = (outs, lses), dil_w_out[j]
        x = _layer_tail(mix, w_out.astype(BF16), x, mod, ln_g, ln_b, w1_all, w2_all, i)
    return x
```

```python
import functools
import math

import jax
import jax.numpy as jnp
import numpy as np
from jax import lax
from jax.experimental import pallas as pl
from jax.experimental.pallas import tpu as pltpu

F32 = jnp.float32
BF16 = jnp.bfloat16

D_MODEL = 1024
DEPTH = 2
RW_HEADS = 8
RW_HEAD_DIM = 64
RW_WIDTH = 512
RW_DECAY_RANK = 64
RW_ICLR_RANK = 64
RW_GATE_RANK = 160
RW_GN_EPS = 64e-5
RW_PROJ = 3 * RW_WIDTH + RW_DECAY_RANK + RW_ICLR_RANK + RW_GATE_RANK
SC_WIDTH = 512
SC_CONV = 3
DIL_PATTERNS = ((128, 1), (512, 4), (2048, 16))
N_GROUPS = 3
DIL_HEADS = 8
DIL_HEAD_DIM = 64
DIL_WIDTH = 512
BLOCK = 128
N_BUCKETS = 32
MAX_DISTANCE = 2048
D_FF = 4 * D_MODEL
DEEPNORM_ALPHA = (2 * DEPTH) ** 0.25
LN_EPS = 1e-5

LANES = 128
MXU_DIM = 256
CHUNK = 64
INV_BLOCK = 16
LR_PAD = 384
VMEM_LIMIT = 56 * 1024 * 1024
TAIL_VMEM_LIMIT = 60 * 1024 * 1024
MASKED = -1e30


def _cparams(n_axes, vmem_limit=VMEM_LIMIT):
    return pltpu.CompilerParams(dimension_semantics=("arbitrary",) * n_axes,
                                vmem_limit_bytes=vmem_limit)


def _sigmoid(x):
    return 0.5 * jnp.tanh(0.5 * x) + 0.5


def _dot(a, b):
    return jnp.dot(a.astype(BF16), b.astype(BF16), preferred_element_type=F32)


def _layer_norm_rows(z, g, b):
    mu = jnp.mean(z, axis=-1, keepdims=True)
    zc = z - mu
    var = jnp.mean(zc * zc, axis=-1, keepdims=True)
    return zc * lax.rsqrt(var + LN_EPS) * g + b


def _adaln_kernel(c_ref, w_ref, b_ref, o_ref):
    c = c_ref[...]
    cond = c * _sigmoid(c)
    o_ref[0] = _dot(cond, w_ref[0]) + b_ref[0]


def _adaln(c, ada_w, ada_b):
    depth, d, n = ada_w.shape
    bsz = c.shape[0]
    tn = 3072
    return pl.pallas_call(
        _adaln_kernel,
        grid=(depth, n // tn),
        in_specs=[pl.BlockSpec((bsz, d), lambda i, j: (0, 0)),
                  pl.BlockSpec((1, d, tn), lambda i, j: (i, 0, j)),
                  pl.BlockSpec((1, 1, tn), lambda i, j: (i, 0, j))],
        out_specs=pl.BlockSpec((1, bsz, tn), lambda i, j: (i, 0, j)),
        out_shape=jax.ShapeDtypeStruct((depth, bsz, n), F32),
        compiler_params=_cparams(2),
        name="adaln",
    )(c, ada_w, ada_b.reshape(depth, 1, n))


def _shift_rows(x, carry, k):
    rolled = pltpu.roll(x, k, 0)
    head = rolled[:8]
    row = lax.broadcasted_iota(jnp.int32, head.shape, 0)
    for i in range(k):
        head = jnp.where(row == i, carry[8 - k + i:8 - k + i + 1], head)
    return jnp.concatenate([head, rolled[8:]], axis=0)


def _ab_proj_kernel(x_ref, mod_ref, wt_ref, mu_ref, convw_ref, rkv_ref, lr_ref, yb_ref, carry_p, carry_z, *, tm):
    @pl.when(pl.program_id(1) == 0)
    def _():
        carry_p[...] = jnp.zeros_like(carry_p)
        carry_z[...] = jnp.zeros_like(carry_z)

    m = mod_ref[0]
    u = (x_ref[0] * (1.0 + m[1:2]) + m[0:1]).astype(BF16)

    def proj(c0, width):
        return lax.dot_general(u, wt_ref[c0:c0 + width, :].astype(BF16), (((1,), (1,)), ((), ())),
                               preferred_element_type=F32)

    def mixed(c0, width):
        p = proj(c0, width)
        prev = _shift_rows(p, carry_p[:, c0:c0 + width], 1)
        carry_p[:, c0:c0 + width] = p[tm - 8:]
        return (p + mu_ref[:, c0:c0 + width] * (prev - p)).astype(BF16)

    w3 = 3 * RW_WIDTH
    for c in range(3):
        rkv_ref[0, :, c * RW_WIDTH:(c + 1) * RW_WIDTH] = mixed(c * RW_WIDTH, RW_WIDTH)
    lr_ref[0] = mixed(w3, LR_PAD)

    h = proj(RW_PROJ, SC_WIDTH)
    z = proj(RW_PROJ + 2 * SC_WIDTH, SC_WIDTH) * h
    zc = carry_z[...]
    cw = convw_ref[...]
    conv = cw[2:3] * z + cw[1:2] * _shift_rows(z, zc, 1) + cw[0:1] * _shift_rows(z, zc, 2)
    carry_z[...] = z[tm - 8:]
    yb_ref[0] = (proj(RW_PROJ + SC_WIDTH, SC_WIDTH) * conv).astype(BF16)


def _ab_proj(x, mod, w_t, mu, conv_w, *, tm=1024):
    bsz, t, d = x.shape
    w3 = 3 * RW_WIDTH

    def const(shape):
        return pl.BlockSpec(shape, lambda b, i: (0,) * len(shape), pipeline_mode=pl.Buffered(1))

    def rows(width):
        return pl.BlockSpec((1, tm, width), lambda b, i: (b, i, 0))

    return pl.pallas_call(
        functools.partial(_ab_proj_kernel, tm=tm),
        grid=(bsz, t // tm),
        in_specs=[rows(d), pl.BlockSpec((1, 6, d), lambda b, i: (b, 0, 0)),
                  const(w_t.shape), const((1, w3 + LR_PAD)), const((SC_CONV, SC_WIDTH))],
        out_specs=[rows(w3), rows(LR_PAD), rows(SC_WIDTH)],
        out_shape=[jax.ShapeDtypeStruct((bsz, t, w3), BF16), jax.ShapeDtypeStruct((bsz, t, LR_PAD), BF16),
                   jax.ShapeDtypeStruct((bsz, t, SC_WIDTH), BF16)],
        scratch_shapes=[pltpu.VMEM((8, w3 + LR_PAD), F32), pltpu.VMEM((8, SC_WIDTH), F32)],
        compiler_params=_cparams(2),
        name="ab_proj",
    )(x, mod, w_t, mu, conv_w)


def _rwkv_chunk_terms(a_t, r_t, b_t, k_t, b_e, k_e, v, p_end, hooks=()):
    n = len(a_t)
    idx = range(n)
    hooks = list(hooks)

    def hook():
        if hooks:
            hooks.pop(0)()
    row = lax.broadcasted_iota(jnp.int32, (CHUNK, LANES), 0)
    col = lax.broadcasted_iota(jnp.int32, (CHUNK, LANES), 1)
    s_idx = col % CHUNK
    strict = s_idx < row
    incl = s_idx <= row
    near = strict & ((s_idx // INV_BLOCK) == (row // INV_BLOCK))
    far = strict & ((s_idx // INV_BLOCK) != (row // INV_BLOCK))
    eye = jnp.where(s_idx == row, 1.0, 0.0)
    r2 = lax.broadcasted_iota(jnp.int32, (LANES, 2 * LANES), 0)
    c2 = lax.broadcasted_iota(jnp.int32, (LANES, 2 * LANES), 1)
    bdmask2 = (r2 // RW_HEAD_DIM) == ((c2 % LANES) // RW_HEAD_DIM)
    eye_bd = jnp.where(lax.broadcasted_iota(jnp.int32, (LANES, LANES), 0)
                       == lax.broadcasted_iota(jnp.int32, (LANES, LANES), 1), 1.0, 0.0)
    zero16 = jnp.zeros((CHUNK, LANES), BF16)
    lane1 = lax.broadcasted_iota(jnp.int32, (1, LANES), 1)
    m0_16 = jnp.where(lane1 < RW_HEAD_DIM, 1.0, 0.0).astype(BF16)
    m1_16 = jnp.where(lane1 < RW_HEAD_DIM, 0.0, 1.0).astype(BF16)

    def cast(xs):
        return [x.astype(BF16) for x in xs]

    def bd(q16):
        return jnp.concatenate([q16 * m0_16, q16 * m1_16], axis=0)

    def mm(p16, q16):
        return jnp.dot(p16, q16, preferred_element_type=F32)

    def mm_nt(p16, q16):
        return lax.dot_general(p16, q16, (((1,), (1,)), ((), ())), preferred_element_type=F32)

    a16, r16, v16 = cast(a_t), cast(r_t), cast(v)
    lhs = [jnp.concatenate([a16[i], r16[i]], axis=0) for i in idx]
    b16, k16 = cast(b_t), cast(k_t)
    bdv = [bd(x) for x in v16]
    g = [mm_nt(lhs[i], jnp.concatenate([bd(b16[i]), bd(k16[i])], axis=0)) for i in idx]
    x1 = [jnp.where(near, g[i][:CHUNK, :LANES], 0.0) for i in idx]
    x1_16 = cast(x1)
    l_e16 = [jnp.where(far, g[i][:CHUNK, :LANES], 0.0).astype(BF16) for i in idx]
    a_k16 = [jnp.concatenate([jnp.where(strict, g[i][:CHUNK, LANES:], 0.0),
                              jnp.where(incl, g[i][CHUNK:, LANES:], 0.0)], axis=0).astype(BF16) for i in idx]
    a_rb16 = [jnp.where(incl, g[i][CHUNK:, :LANES], 0.0).astype(BF16) for i in idx]

    def both(p16, q16):
        return jnp.concatenate([bd(p16), bd(q16)], axis=1)

    x2_16 = cast([mm(x1_16[i], bd(x1_16[i])) for i in idx])
    hook()
    acc = [eye + x1[i] for i in idx]
    sq = [mm(x2_16[i], both(x2_16[i], acc[i].astype(BF16))) for i in idx]
    x4_16 = [sq[i][:, :LANES].astype(BF16) for i in idx]
    acc = [acc[i] + sq[i][:, LANES:] for i in idx]
    sq = [mm(x4_16[i], both(x4_16[i], acc[i].astype(BF16))) for i in idx]
    hook()
    x8_16 = [sq[i][:, :LANES].astype(BF16) for i in idx]
    acc = [acc[i] + sq[i][:, LANES:] for i in idx]
    dinv = [acc[i] + mm(x8_16[i], bd(acc[i].astype(BF16))) for i in idx]
    dinv16 = cast(dinv)
    hook()
    f16 = cast([mm(dinv16[i], bd(l_e16[i])) for i in idx])
    sq = [mm(f16[i], both(f16[i], dinv16[i])) for i in idx]
    f2_16 = [sq[i][:, :LANES].astype(BF16) for i in idx]
    hook()
    hmat = [dinv[i] + sq[i][:, LANES:] for i in idx]
    tinv16 = cast([hmat[i] + mm(f2_16[i], bd(hmat[i].astype(BF16))) for i in idx])

    kv = [mm(a_k16[i], bdv[i]) for i in idx]
    wv16 = [kv[i][:CHUNK].astype(BF16) for i in idx]
    hook()
    aw = [mm(tinv16[i], both(a16[i], wv16[i])) for i in idx]
    aw16 = cast(aw)
    hook()
    qy = [mm(a_rb16[i], both(aw16[i][:, :LANES], aw16[i][:, LANES:]))
          + jnp.concatenate([r_t[i], kv[i][CHUNK:]], axis=1) for i in idx]
    be_ke_t = [jnp.concatenate([b_e[i], k_e[i]], axis=0).T.astype(BF16) for i in idx]
    rhs = [jnp.concatenate([aw16[i], jnp.concatenate([zero16, v16[i]], axis=1)], axis=0) for i in idx]
    mn = [jnp.where(bdmask2, mm(be_ke_t[i], rhs[i]), 0.0) for i in idx]
    qhat = [x[:, :LANES] for x in qy]
    yhat = [x[:, LANES:] for x in qy]
    mmat = [eye_bd * p_end[i] + mn[i][:, :LANES] for i in idx]
    nmat = [x[:, LANES:] for x in mn]
    while hooks:
        hook()
    return qhat, yhat, mmat, nmat


def _rwkv_kernel(rkv_ref, lr_ref, w0_ref, a0_ref, wwa_ref, gup_ref,
                 kk_ref, ka_ref, rk_ref, lnxg_ref, lnxb_ref, o_ref, state, yout_s, *, tb):
    n_chunks = tb // CHUNK
    n_pairs = RW_WIDTH // LANES

    @pl.when(pl.program_id(1) == 0)
    def _():
        state[...] = jnp.zeros_like(state)

    r = rkv_ref[0, :, :RW_WIDTH].astype(F32)
    k = rkv_ref[0, :, RW_WIDTH:2 * RW_WIDTH].astype(F32)
    v = rkv_ref[0, :, 2 * RW_WIDTH:].astype(F32)
    lr_m = lr_ref[0].astype(F32)

    wa = lr_m[:, :LANES]
    lane = lax.broadcasted_iota(jnp.int32, wa.shape, 1)
    wa = jnp.where(lane < RW_DECAY_RANK, jnp.tanh(wa), wa)
    wa_up = _dot(wa, wwa_ref[...])
    ld = -math.exp(-0.5) * _sigmoid(w0_ref[...] + wa_up[:, :RW_WIDTH])
    iclr = _sigmoid(a0_ref[...] + wa_up[:, RW_WIDTH:])
    gate = _dot(_sigmoid(lr_m[:, LANES:]), gup_ref[...])

    hr = lax.broadcasted_iota(jnp.int32, (MXU_DIM, MXU_DIM), 0) // RW_HEAD_DIM
    hc = lax.broadcasted_iota(jnp.int32, (MXU_DIM, MXU_DIM), 1) // RW_HEAD_DIM
    seg = jnp.where(hr == hc, 1.0 / RW_HEAD_DIM, 0.0).astype(BF16)

    def seg_mean(x):
        x16 = x.astype(BF16)
        return jnp.concatenate([jnp.dot(x16[:, c:c + MXU_DIM], seg, preferred_element_type=F32)
                                for c in range(0, RW_WIDTH, MXU_DIM)], axis=1)

    kk = k * kk_ref[...]
    ss = seg_mean(kk * kk) * RW_HEAD_DIM
    kk = kk * lax.rsqrt(jnp.maximum(ss, 1e-24))
    kh = k * (1.0 + (iclr - 1.0) * ka_ref[...])

    tr = lax.broadcasted_iota(jnp.int32, (MXU_DIM, MXU_DIM), 0)
    tc = lax.broadcasted_iota(jnp.int32, (MXU_DIM, MXU_DIM), 1)
    tri = jnp.where((tc <= tr) & (tc // CHUNK == tr // CHUNK), 1.0, 0.0).astype(BF16)
    ld_hi = ld.astype(BF16)
    ld_lo = (ld - ld_hi.astype(F32)).astype(BF16)
    lp = jnp.concatenate(
        [jnp.dot(tri, ld_hi[t0:t0 + MXU_DIM], preferred_element_type=F32)
         + jnp.dot(tri, ld_lo[t0:t0 + MXU_DIM], preferred_element_type=F32)
         for t0 in range(0, tb, MXU_DIM)], axis=0)

    a_t, r_t, b_t, k_t, b_e, k_e, vs, p_end = [], [], [], [], [], [], [], []
    b = kk * iclr
    for c in range(n_chunks):
        rs = slice(c * CHUNK, (c + 1) * CHUNK)
        lp_c = lp[rs]
        lp_last = lp_c[CHUNK - 1:CHUNK]
        e_in = jnp.exp(lp_c)
        e_inv = jnp.exp(-lp_c)
        pe_c = jnp.exp(lp_last)
        a_c = -kk[rs] * jnp.exp(lp_c - ld[rs])
        r_c = r[rs] * e_in
        b_c = b[rs] * e_inv
        k_c = kh[rs] * e_inv
        be_c = b_c * pe_c
        ke_c = k_c * pe_c
        for p in range(n_pairs):
            cs = slice(p * LANES, (p + 1) * LANES)
            a_t.append(a_c[:, cs])
            r_t.append(r_c[:, cs])
            b_t.append(b_c[:, cs])
            k_t.append(k_c[:, cs])
            b_e.append(be_c[:, cs])
            k_e.append(ke_c[:, cs])
            vs.append(v[rs, cs])
            p_end.append(pe_c[:, cs])

    bonus = seg_mean(r * kh * rk_ref[...]) * RW_HEAD_DIM * v
    s = [state[p] for p in range(n_pairs)]

    def serial_step(c, terms, c0):
        qhat, yhat, mmat, nmat = terms
        base = (c - c0) * n_pairs
        ym = [_dot(jnp.concatenate([qhat[base + p], mmat[base + p]], axis=0), s[p]) for p in range(n_pairs)]
        for p in range(n_pairs):
            yout_s[c * CHUNK:(c + 1) * CHUNK, p * LANES:(p + 1) * LANES] = ym[p][:CHUNK] + yhat[base + p]
            s[p] = ym[p][CHUNK:] + nmat[base + p]

    def finish(rows):
        y = yout_s[rows, :]
        yc = y - seg_mean(y)
        var = seg_mean(yc * yc)
        y = yc * lax.rsqrt(var + RW_GN_EPS) * lnxg_ref[...] + lnxb_ref[...]
        o_ref[0, rows, :] = ((y + bonus[rows]) * gate[rows]).astype(o_ref.dtype)

    half = n_chunks // 2
    cut = half * n_pairs
    slabs = (a_t, r_t, b_t, k_t, b_e, k_e, vs, p_end)
    first = _rwkv_chunk_terms(*[x[:cut] for x in slabs])
    hooks = [functools.partial(serial_step, c, first, 0) for c in range(half)]
    hooks.append(functools.partial(finish, slice(0, half * CHUNK)))
    second = _rwkv_chunk_terms(*[x[cut:] for x in slabs], hooks=hooks)
    for c in range(half, n_chunks):
        serial_step(c, second, half)
    for p in range(n_pairs):
        state[p] = s[p]
    finish(slice(half * CHUNK, tb))


def _rwkv(rkv, lr, w0, a0, wwa, gup, k_k, k_a, r_k, lnx_g, lnx_b, *, tb=512):
    bsz, t, w3 = rkv.shape
    n_pairs = RW_WIDTH // LANES

    def full(shape):
        return pl.BlockSpec(shape, lambda b, i: (0,) * len(shape))

    return pl.pallas_call(
        functools.partial(_rwkv_kernel, tb=tb),
        grid=(bsz, t // tb),
        in_specs=[pl.BlockSpec((1, tb, w3), lambda b, i: (b, i, 0)),
                  pl.BlockSpec((1, tb, LR_PAD), lambda b, i: (b, i, 0)),
                  full((1, RW_WIDTH)), full((1, RW_WIDTH)),
                  full((LANES, 2 * RW_WIDTH)), full((LR_PAD - LANES, RW_WIDTH)),
                  full((1, RW_WIDTH)), full((1, RW_WIDTH)), full((1, RW_WIDTH)),
                  full((1, RW_WIDTH)), full((1, RW_WIDTH))],
        out_specs=pl.BlockSpec((1, tb, RW_WIDTH), lambda b, i: (b, i, 0)),
        out_shape=jax.ShapeDtypeStruct((bsz, t, RW_WIDTH), BF16),
        scratch_shapes=[pltpu.VMEM((n_pairs, LANES, LANES), F32),
                        pltpu.VMEM((tb, RW_WIDTH), F32)],
        compiler_params=_cparams(2),
        name="rwkv",
    )(rkv, lr, w0, a0, wwa, gup, k_k, k_a, r_k, lnx_g, lnx_b)


def _attn_kernel(q_ref, kp_ref, kc_ref, vp_ref, vc_ref, bias_ref, o_ref, lse_ref, *, nb, rb):
    step = pl.program_id(1)
    n_blk = rb // BLOCK
    scale = DIL_HEAD_DIM ** -0.5
    n_pairs = DIL_WIDTH // LANES
    krow = lax.broadcasted_iota(jnp.int32, (2 * BLOCK, LANES), 0) // BLOCK
    klane = lax.broadcasted_iota(jnp.int32, (2 * BLOCK, LANES), 1) // DIL_HEAD_DIM
    own = krow == klane
    zeros = jnp.zeros((2 * BLOCK, LANES), BF16)
    ones_bd = jnp.where(own, 1.0, 0.0).astype(BF16)
    lane_o = lax.broadcasted_iota(jnp.int32, (BLOCK, LANES), 1)

    def stack_bd(x):
        return jnp.where(own, jnp.concatenate([x, x], axis=0), zeros)

    def attend(blocks):
        units = [(rows, q, parts, p) for rows, q, parts in blocks for p in range(n_pairs)]
        lanes = [slice(p * LANES, (p + 1) * LANES) for p in range(n_pairs)]
        scores = []
        for _, q, parts, p in units:
            qs = q[:, lanes[p]] * scale
            row = []
            for k_blk, _, col0, pen in parts:
                s = lax.dot_general(qs, stack_bd(k_blk[:, lanes[p]]), (((1,), (1,)), ((), ())),
                                    preferred_element_type=F32)
                bias = jnp.concatenate([bias_ref[2 * p, :, col0:col0 + BLOCK],
                                        bias_ref[2 * p + 1, :, col0:col0 + BLOCK]], axis=1)
                s = s + bias
                row.append(s if pen is None else s + pen)
            scores.append(row)
        ms = []
        for u in range(len(units)):
            halves = []
            for hh in range(2):
                cols = slice(hh * BLOCK, (hh + 1) * BLOCK)
                m = jnp.max(scores[u][0][:, cols], axis=-1, keepdims=True)
                for s in scores[u][1:]:
                    m = jnp.maximum(m, jnp.max(s[:, cols], axis=-1, keepdims=True))
                halves.append(m)
            ms.append(halves)
        es = [[jnp.exp((s - jnp.concatenate([jnp.broadcast_to(ms[u][0], (BLOCK, BLOCK)),
                                             jnp.broadcast_to(ms[u][1], (BLOCK, BLOCK))], axis=1)).astype(BF16))
               for s in scores[u]] for u in range(len(units))]
        res = []
        for u, (_, _, parts, p) in enumerate(units):
            res.append(sum(jnp.dot(e, jnp.concatenate([stack_bd(part[1][:, lanes[p]]), ones_bd], axis=1),
                                   preferred_element_type=F32) for e, part in zip(es[u], parts)))
        for b, (rows, _, _) in enumerate(blocks):
            m_all = jnp.zeros((BLOCK, LANES), F32)
            den_all = jnp.ones((BLOCK, LANES), F32)
            for p in range(n_pairs):
                u = b * n_pairs + p
                num, den = res[u][:, :LANES], res[u][:, LANES:]
                o_ref[rows, lanes[p]] = (num / den).astype(o_ref.dtype)
                m_all = jnp.where(lane_o == 2 * p, ms[u][0], jnp.where(lane_o == 2 * p + 1, ms[u][1], m_all))
                den_all = jnp.where((lane_o == 2 * p), den,
                                    jnp.where(lane_o == 2 * p + 1, pltpu.roll(den, DIL_HEAD_DIM, 1), den_all))
            lse_ref[rows, :] = m_all + jnp.log(den_all)

    blocks = []
    for j in range(n_blk):
        rows = slice(j * BLOCK, (j + 1) * BLOCK)
        cur = (kc_ref[rows, :], vc_ref[rows, :], BLOCK, None)
        if n_blk % nb == 0:
            prev = "none" if j % nb == 0 else "inside"
        else:
            prev = "inside" if j > 0 else ("none" if nb == n_blk else "before")
        if prev == "none":
            parts = [cur]
        elif prev == "inside":
            before = slice((j - 1) * BLOCK, j * BLOCK)
            parts = [(kc_ref[before, :], vc_ref[before, :], 0, None), cur]
        else:
            has_prev = (step % (nb // n_blk)) != 0
            parts = [(kp_ref[...], vp_ref[...], 0, jnp.where(has_prev, 0.0, MASKED)), cur]
        blocks.append((rows, q_ref[rows, :], parts))
    attend(blocks)


def _qkv_proj_kernel(x_ref, mod_ref, w_ref, o0_ref, o1_ref, o2_ref, xs_ref, xs4_ref, *, tm):
    m = mod_ref[0]
    scale = 1.0 + m[1:2]
    shift = m[0:1]
    n_col = x_ref.shape[-1] // LANES
    gw = 3 * DIL_WIDTH
    step = DIL_PATTERNS[1][1]
    assert [dil for _, dil in DIL_PATTERNS] == [1, step, step * step]

    for c in range(n_col):
        xs_ref[c] = x_ref[0, :, c * LANES:(c + 1) * LANES]
    rows4 = tm // step
    x4 = [jnp.concatenate([xs_ref[c, pl.ds(a, rows4, stride=step), :] for a in range(step)], axis=0)
          for c in range(n_col)]
    for c in range(n_col):
        xs4_ref[c] = x4[c]
    rows16 = rows4 // step
    x16 = [jnp.concatenate([xs4_ref[c, pl.ds(a * rows4 + b, rows16, stride=step), :]
                            for a in range(step) for b in range(step)], axis=0) for c in range(n_col)]
    inputs = [x_ref[0], jnp.concatenate(x4, axis=1), jnp.concatenate(x16, axis=1)]
    residue_of_group = [[0], list(range(step)), [a + step * b for a in range(step) for b in range(step)]]

    for g, o_ref in enumerate((o0_ref, o1_ref, o2_ref)):
        u = (inputs[g] * scale + shift).astype(BF16)
        res = jnp.dot(u, w_ref[:, g * gw:(g + 1) * gw].astype(BF16), preferred_element_type=F32).astype(BF16)
        rows = tm // len(residue_of_group[g])
        for grp, r in enumerate(residue_of_group[g]):
            o_ref[0, r] = res[grp * rows:(grp + 1) * rows]


def _qkv_proj(x, mod, w, *, tm=1024):
    bsz, t, d = x.shape
    gw = 3 * DIL_WIDTH
    out_specs, out_shape = [], []
    for _, dil in DIL_PATTERNS:
        out_specs.append(pl.BlockSpec((1, dil, tm // dil, gw), lambda b, i: (b, 0, i, 0)))
        out_shape.append(jax.ShapeDtypeStruct((bsz, dil, t // dil, gw), BF16))
    return pl.pallas_call(
        functools.partial(_qkv_proj_kernel, tm=tm),
        grid=(bsz, t // tm),
        in_specs=[pl.BlockSpec((1, tm, d), lambda b, i: (b, i, 0)),
                  pl.BlockSpec((1, 6, d), lambda b, i: (b, 0, 0)),
                  pl.BlockSpec(w.shape, lambda b, i: (0, 0), pipeline_mode=pl.Buffered(1))],
        out_specs=out_specs,
        out_shape=out_shape,
        scratch_shapes=[pltpu.VMEM((d // LANES, tm, LANES), F32), pltpu.VMEM((d // LANES, tm, LANES), F32)],
        compiler_params=_cparams(2),
        name="qkv_proj",
    )(x, mod, w)


def _dilated_attention_group(qkv, bias, g, *, rb=2048):
    bsz, dil, length, width = qkv.shape
    t = dil * length
    nb = length // BLOCK
    n_blk = rb // BLOCK
    assert nb % n_blk == 0 or n_blk % nb == 0
    flat = qkv.reshape(bsz, t, width)

    def cur(j):
        return lambda b, s: (b, s, j)

    def prev(j):
        return lambda b, s: (b, jnp.maximum(s * n_blk - 1, 0), j)

    blk = (None, rb, DIL_WIDTH)
    pblk = (None, BLOCK, DIL_WIDTH)
    o, lse = pl.pallas_call(
        functools.partial(_attn_kernel, nb=nb, rb=rb),
        grid=(bsz, t // rb),
        in_specs=[pl.BlockSpec(blk, cur(0)),
                  pl.BlockSpec(pblk, prev(1)), pl.BlockSpec(blk, cur(1)),
                  pl.BlockSpec(pblk, prev(2)), pl.BlockSpec(blk, cur(2)),
                  pl.BlockSpec((DIL_HEADS, BLOCK, 2 * BLOCK), lambda b, s: (0, 0, 0))],
        out_specs=[pl.BlockSpec(blk, lambda b, s: (b, s, 0)),
                   pl.BlockSpec((None, rb, LANES), lambda b, s: (b, s, 0))],
        out_shape=[jax.ShapeDtypeStruct((bsz, t, DIL_WIDTH), BF16),
                   jax.ShapeDtypeStruct((bsz, t, LANES), F32)],
        compiler_params=_cparams(2),
        name=f"dilated_attn_g{g}",
    )(flat, flat, flat, flat, flat, bias)
    return o.reshape(bsz, dil, length, DIL_WIDTH), lse.reshape(bsz, dil, length, LANES)


def _t5_bucket(dist):
    exact = N_BUCKETS // 2
    logd = jnp.log(jnp.maximum(dist, 1).astype(F32) / exact) / math.log(MAX_DISTANCE / exact)
    large = jnp.minimum(exact + (logd * (N_BUCKETS - exact)).astype(jnp.int32), N_BUCKETS - 1)
    return jnp.where(dist < exact, dist, large)


def _bias_tables(rel_bias):
    rel = jnp.arange(BLOCK + 1)
    tables = []
    for g, (window, dil) in enumerate(DIL_PATTERNS):
        span = window // dil
        bucket = _t5_bucket(jnp.clip(rel, 0, span) * dil)
        vec = rel_bias[bucket][:, g * DIL_HEADS:(g + 1) * DIL_HEADS].T.astype(F32)
        ext = jnp.concatenate([vec[:, ::-1], jnp.zeros((DIL_HEADS, BLOCK), F32)], axis=1)
        flat = jnp.tile(ext, (1, BLOCK))[:, :BLOCK * 2 * BLOCK]
        tables.append(flat.reshape(DIL_HEADS, BLOCK, 2 * BLOCK))
    qi = np.arange(BLOCK)[:, None]
    ki = np.arange(2 * BLOCK)[None, :]
    band = (ki >= qi) & (ki <= qi + BLOCK)
    return jnp.where(band, jnp.stack(tables), MASKED)


def _tail(mix_fn, wout_ref, x_ref, mod_ref, lng_ref, lnb_ref, w1_ref, w2_ref, o_ref, *, tm, n_sub, ff_chunk):
    m = mod_ref[0]
    sub = tm // n_sub
    tiles = [slice(i * sub, (i + 1) * sub) for i in range(n_sub)]
    x1s, us = [], []
    for rows in tiles:
        y = None
        for piece, k0 in mix_fn(rows):
            part = jnp.dot(piece, wout_ref[k0:k0 + piece.shape[-1], :], preferred_element_type=F32)
            y = part if y is None else y + part
        x1 = _layer_norm_rows(DEEPNORM_ALPHA * x_ref[0, rows, :] + (1.0 + m[2:3]) * y, lng_ref[0:1], lnb_ref[0:1])
        x1s.append(x1)
        us.append((x1 * (1.0 + m[4:5]) + m[3:4]).astype(BF16))
    accs = [None] * n_sub
    for c in range(D_FF // ff_chunk):
        cols = slice(c * ff_chunk, (c + 1) * ff_chunk)
        w1c = w1_ref[:, cols].astype(BF16)
        w2c = w2_ref[cols, :].astype(BF16)
        for i in range(n_sub):
            h = jnp.maximum(jnp.dot(us[i], w1c, preferred_element_type=F32), 0.0)
            part = jnp.dot((h * h).astype(BF16), w2c, preferred_element_type=F32)
            accs[i] = part if accs[i] is None else accs[i] + part
    for i, rows in enumerate(tiles):
        z = DEEPNORM_ALPHA * x1s[i] + (1.0 + m[5:6]) * accs[i]
        o_ref[0, rows, :] = _layer_norm_rows(z, lng_ref[1:2], lnb_ref[1:2])


def _even_tail_kernel(ya_ref, yb_ref, wout_ref, x_ref, mod_ref, lng_ref, lnb_ref, w1_ref, w2_ref, o_ref, **kw):
    def mix(rows):
        return [(ya_ref[0, rows, :], 0), (yb_ref[0, rows, :], ya_ref.shape[-1])]

    _tail(mix, wout_ref, x_ref, mod_ref, lng_ref, lnb_ref, w1_ref, w2_ref, o_ref, **kw)


def _odd_tail_kernel(o0_ref, o1_ref, o2_ref, l0_ref, l1_ref, l2_ref, wout_ref, x_ref, mod_ref, lng_ref, lnb_ref,
                     w1_ref, w2_ref, o_ref, o_nat, l_nat, mix_s, *, tm, **kw):
    def natural(src_ref, scr, dil):
        if dil == 1:
            return src_ref[0, 0].astype(F32)
        n_col = src_ref.shape[-1] // LANES
        for c in range(n_col):
            for r in range(dil):
                scr[c, pl.ds(r, tm // dil, stride=dil), :] = src_ref[0, r, :, c * LANES:(c + 1) * LANES].astype(F32)
        return jnp.concatenate([scr[c] for c in range(n_col)], axis=1)

    dils = [dil for _, dil in DIL_PATTERNS]
    lses = [natural(ref, l_nat, dil) for ref, dil in zip((l0_ref, l1_ref, l2_ref), dils)]
    mx = jnp.maximum(jnp.maximum(lses[0], lses[1]), lses[2])
    es = [jnp.exp(l - mx) for l in lses]
    inv = 1.0 / (es[0] + es[1] + es[2])
    hr = lax.broadcasted_iota(jnp.int32, (LANES, DIL_WIDTH), 0)
    hc = lax.broadcasted_iota(jnp.int32, (LANES, DIL_WIDTH), 1) // DIL_HEAD_DIM
    expand = jnp.where(hr == hc, 1.0, 0.0).astype(BF16)

    def widen(wt):
        return jnp.dot(wt.astype(BF16), expand, preferred_element_type=F32)

    merged = None
    for e, ref, dil in zip(es, (o0_ref, o1_ref, o2_ref), dils):
        term = widen(e * inv) * natural(ref, o_nat, dil)
        merged = term if merged is None else merged + term
    mix_s[...] = merged.astype(BF16)
    _tail(lambda rows: [(mix_s[rows, :], 0)], wout_ref, x_ref, mod_ref, lng_ref, lnb_ref, w1_ref, w2_ref, o_ref,
          tm=tm, **kw)


def _layer_tail(mix, w_out, x, mod, ln_g, ln_b, w1, w2, layer, *, tm=512, n_sub=2, ff_chunk=1024):
    bsz, t, d = x.shape

    def const(shape):
        return pl.BlockSpec(shape, lambda b, i: (0,) * len(shape), pipeline_mode=pl.Buffered(1))

    def of_layer(shape):
        return pl.BlockSpec((None,) + shape, lambda b, i: (layer,) + (0,) * len(shape), pipeline_mode=pl.Buffered(1))

    tail_specs = [const(w_out.shape),
                  pl.BlockSpec((1, tm, d), lambda b, i: (b, i, 0)),
                  pl.BlockSpec((1, 6, d), lambda b, i: (b, 0, 0)),
                  of_layer((2, d)), of_layer((2, d)), of_layer((d, D_FF)), of_layer((D_FF, d))]
    tail_args = (w_out, x, mod, ln_g, ln_b, w1, w2)
    kw = dict(tm=tm, n_sub=n_sub, ff_chunk=ff_chunk)
    if isinstance(mix, tuple):
        outs, lses = mix
        mix_specs = ([pl.BlockSpec((1, dil, tm // dil, DIL_WIDTH), lambda b, i: (b, 0, i, 0)) for _, dil in DIL_PATTERNS]
                     + [pl.BlockSpec((1, dil, tm // dil, LANES), lambda b, i: (b, 0, i, 0)) for _, dil in DIL_PATTERNS])
        mix_args = (*outs, *lses)
        body = functools.partial(_odd_tail_kernel, **kw)
        scratch = [pltpu.VMEM((DIL_WIDTH // LANES, tm, LANES), F32), pltpu.VMEM((1, tm, LANES), F32),
                   pltpu.VMEM((tm, DIL_WIDTH), BF16)]
    else:
        mix_specs = [pl.BlockSpec((1, tm, piece.shape[-1]), lambda b, i: (b, i, 0)) for piece in mix]
        mix_args = tuple(mix)
        body = functools.partial(_even_tail_kernel, **kw)
        scratch = []
    return pl.pallas_call(
        body,
        grid=(bsz, t // tm),
        in_specs=mix_specs + tail_specs,
        out_specs=pl.BlockSpec((1, tm, d), lambda b, i: (b, i, 0)),
        out_shape=jax.ShapeDtypeStruct((bsz, t, d), F32),
        scratch_shapes=scratch,
        compiler_params=_cparams(2, TAIL_VMEM_LIMIT),
        name="layer_tail",
    )(*mix_args, *tail_args)


def _prep_even_weights(mu, w_up, a_up, g_up):
    pad = LR_PAD - (RW_PROJ - 3 * RW_WIDTH)
    mu_pad = jnp.concatenate([mu, jnp.zeros((pad,), mu.dtype)]).reshape(1, 3 * RW_WIDTH + LR_PAD)
    zeros = jnp.zeros((RW_DECAY_RANK, RW_WIDTH), w_up.dtype)
    wwa = jnp.concatenate([jnp.concatenate([w_up, zeros], axis=1),
                           jnp.concatenate([zeros, a_up], axis=1)], axis=0).astype(BF16)
    gup = jnp.concatenate([g_up, jnp.zeros((pad, RW_WIDTH), g_up.dtype)], axis=0).astype(BF16)
    return mu_pad, wwa, gup


def kernel(x, c, ada_w, ada_b, ln_g, ln_b, ab_w_in, rw_mu, rw_w0, rw_w_up, rw_a0, rw_a_up, rw_g_up,
           rw_k_k, rw_k_a, rw_r_k, rw_lnx_g, rw_lnx_b, sc_conv_w, ab_w_out, dil_w_qkv, dil_w_out,
           rel_bias, mlp_w1, mlp_w2):
    bsz, t, d = x.shape
    mods = _adaln(c, ada_w, ada_b).reshape(DEPTH, bsz, 6, d)
    bias = None
    row = lambda a: a.reshape(1, -1)
    w1_all, w2_all = mlp_w1, mlp_w2
    for i in range(DEPTH):
        mod = mods[i]
        j = i // 2
        if i % 2 == 0:
            mu_pad, wwa, gup = _prep_even_weights(rw_mu[j], rw_w_up[j], rw_a_up[j], rw_g_up[j])
            rkv, lr, yb = _ab_proj(x, mod, ab_w_in[j].T, mu_pad, sc_conv_w[j])
            ya = _rwkv(rkv, lr, row(rw_w0[j]), row(rw_a0[j]), wwa, gup,
                       row(rw_k_k[j]), row(rw_k_a[j]), row(rw_r_k[j]),
                       row(rw_lnx_g[j]), row(rw_lnx_b[j]))
            mix, w_out = [ya, yb], ab_w_out[j]
        else:
            if bias is None:
                bias = _bias_tables(rel_bias)
            qkvs = _qkv_proj(x, mod, dil_w_qkv[j])
            outs, lses = [], []
            for g in range(N_GROUPS):
                o, lse = _dilated_attention_group(qkvs[g], bias[g], g)
                outs.append(o)
                lses.append(lse)
            mix, w_out = (outs, lses), dil_w_out[j]
        x = _layer_tail(mix, w_out.astype(BF16), x, mod, ln_g, ln_b, w1_all, w2_all, i)
    return x
```

```python
import functools
import math

import jax
import jax.numpy as jnp
import numpy as np
from jax import lax
from jax.experimental import pallas as pl
from jax.experimental.pallas import tpu as pltpu

F32 = jnp.float32
BF16 = jnp.bfloat16

D_MODEL = 1024
DEPTH = 2
RW_HEADS = 8
RW_HEAD_DIM = 64
RW_WIDTH = 512
RW_DECAY_RANK = 64
RW_ICLR_RANK = 64
RW_GATE_RANK = 160
RW_GN_EPS = 64e-5
RW_PROJ = 3 * RW_WIDTH + RW_DECAY_RANK + RW_ICLR_RANK + RW_GATE_RANK
SC_WIDTH = 512
SC_CONV = 3
DIL_PATTERNS = ((128, 1), (512, 4), (2048, 16))
N_GROUPS = 3
DIL_HEADS = 8
DIL_HEAD_DIM = 64
DIL_WIDTH = 512
BLOCK = 128
N_BUCKETS = 32
MAX_DISTANCE = 2048
D_FF = 4 * D_MODEL
DEEPNORM_ALPHA = (2 * DEPTH) ** 0.25
LN_EPS = 1e-5

LANES = 128
SUBLANES = 8
MXU_DIM = 256
CHUNK = 64
INV_BLOCK = 16
LR_PAD = 384
VMEM_LIMIT = 56 * 1024 * 1024
TAIL_VMEM_LIMIT = 60 * 1024 * 1024
MASKED = -1e30


def _cparams(n_axes, vmem_limit=VMEM_LIMIT):
    return pltpu.CompilerParams(dimension_semantics=("arbitrary",) * n_axes,
                                vmem_limit_bytes=vmem_limit)


def _sigmoid(x):
    return 0.5 * jnp.tanh(0.5 * x) + 0.5


def _dot(a, b):
    return jnp.dot(a.astype(BF16), b.astype(BF16), preferred_element_type=F32)


def _layer_norm_rows(z, g, b):
    mu = jnp.mean(z, axis=-1, keepdims=True)
    zc = z - mu
    var = jnp.mean(zc * zc, axis=-1, keepdims=True)
    return zc * lax.rsqrt(var + LN_EPS) * g + b


def _adaln_kernel(c_ref, w_ref, b_ref, o_ref):
    c = c_ref[...]
    cond = c * _sigmoid(c)
    o_ref[0] = _dot(cond, w_ref[0]) + b_ref[0]


def _adaln(c, ada_w, ada_b):
    depth, d, n = ada_w.shape
    bsz = c.shape[0]
    tn = 1536
    return pl.pallas_call(
        _adaln_kernel,
        grid=(depth, n // tn),
        in_specs=[pl.BlockSpec((bsz, d), lambda i, j: (0, 0)),
                  pl.BlockSpec((1, d, tn), lambda i, j: (i, 0, j)),
                  pl.BlockSpec((1, 1, tn), lambda i, j: (i, 0, j))],
        out_specs=pl.BlockSpec((1, bsz, tn), lambda i, j: (i, 0, j)),
        out_shape=jax.ShapeDtypeStruct((depth, bsz, n), F32),
        compiler_params=_cparams(2),
        name="adaln",
    )(c, ada_w, ada_b.reshape(depth, 1, n))


def _shift_rows(x, carry, k):
    rolled = pltpu.roll(x, k, 0)
    head = rolled[:SUBLANES]
    row = lax.broadcasted_iota(jnp.int32, head.shape, 0)
    for i in range(k):
        head = jnp.where(row == i, carry[SUBLANES - k + i:SUBLANES - k + i + 1], head)
    return jnp.concatenate([head, rolled[SUBLANES:]], axis=0)


def _ab_proj_kernel(x_ref, mod_ref, wt_ref, mu_ref, convw_ref, rkv_ref, lr_ref, yb_ref, carry_p, carry_z, *, tm):
    @pl.when(pl.program_id(1) == 0)
    def _():
        carry_p[...] = jnp.zeros_like(carry_p)
        carry_z[...] = jnp.zeros_like(carry_z)

    m = mod_ref[0]
    u = (x_ref[0] * (1.0 + m[1:2]) + m[0:1]).astype(BF16)

    def proj(c0, width):
        return lax.dot_general(u, wt_ref[c0:c0 + width, :].astype(BF16), (((1,), (1,)), ((), ())),
                               preferred_element_type=F32)

    def mixed(c0, width):
        p = proj(c0, width)
        prev = _shift_rows(p, carry_p[:, c0:c0 + width], 1)
        carry_p[:, c0:c0 + width] = p[tm - SUBLANES:]
        return (p + mu_ref[:, c0:c0 + width] * (prev - p)).astype(BF16)

    w3 = 3 * RW_WIDTH
    for c in range(3):
        rkv_ref[0, :, c * RW_WIDTH:(c + 1) * RW_WIDTH] = mixed(c * RW_WIDTH, RW_WIDTH)
    lr_ref[0] = mixed(w3, LR_PAD)

    h = proj(RW_PROJ, SC_WIDTH)
    z = proj(RW_PROJ + 2 * SC_WIDTH, SC_WIDTH) * h
    zc = carry_z[...]
    cw = convw_ref[...]
    conv = cw[2:3] * z + cw[1:2] * _shift_rows(z, zc, 1) + cw[0:1] * _shift_rows(z, zc, 2)
    carry_z[...] = z[tm - SUBLANES:]
    yb_ref[0] = (proj(RW_PROJ + SC_WIDTH, SC_WIDTH) * conv).astype(BF16)


def _ab_proj(x, mod, w_t, mu, conv_w, *, tm=1024):
    bsz, t, d = x.shape
    w3 = 3 * RW_WIDTH

    def const(shape):
        return pl.BlockSpec(shape, lambda b, i: (0,) * len(shape), pipeline_mode=pl.Buffered(1))

    def rows(width):
        return pl.BlockSpec((1, tm, width), lambda b, i: (b, i, 0))

    return pl.pallas_call(
        functools.partial(_ab_proj_kernel, tm=tm),
        grid=(bsz, t // tm),
        in_specs=[rows(d), pl.BlockSpec((1, 6, d), lambda b, i: (b, 0, 0)),
                  const(w_t.shape), const((1, w3 + LR_PAD)), const((SC_CONV, SC_WIDTH))],
        out_specs=[rows(w3), rows(LR_PAD), rows(SC_WIDTH)],
        out_shape=[jax.ShapeDtypeStruct((bsz, t, w3), BF16), jax.ShapeDtypeStruct((bsz, t, LR_PAD), BF16),
                   jax.ShapeDtypeStruct((bsz, t, SC_WIDTH), BF16)],
        scratch_shapes=[pltpu.VMEM((SUBLANES, w3 + LR_PAD), F32), pltpu.VMEM((SUBLANES, SC_WIDTH), F32)],
        compiler_params=_cparams(2),
        name="ab_proj",
    )(x, mod, w_t, mu, conv_w)


def _rwkv_chunk_terms(a_t, r_t, b_t, k_t, b_e, k_e, v, p_end, hooks=()):
    n = len(a_t)
    idx = range(n)
    hooks = list(hooks)

    def hook():
        if hooks:
            hooks.pop(0)()
    row = lax.broadcasted_iota(jnp.int32, (CHUNK, LANES), 0)
    col = lax.broadcasted_iota(jnp.int32, (CHUNK, LANES), 1)
    s_idx = col % CHUNK
    strict = s_idx < row
    incl = s_idx <= row
    near = strict & ((s_idx // INV_BLOCK) == (row // INV_BLOCK))
    far = strict & ((s_idx // INV_BLOCK) != (row // INV_BLOCK))
    eye = jnp.where(s_idx == row, 1.0, 0.0)
    r2 = lax.broadcasted_iota(jnp.int32, (LANES, 2 * LANES), 0)
    c2 = lax.broadcasted_iota(jnp.int32, (LANES, 2 * LANES), 1)
    bdmask2 = (r2 // RW_HEAD_DIM) == ((c2 % LANES) // RW_HEAD_DIM)
    eye_bd = jnp.where(lax.broadcasted_iota(jnp.int32, (LANES, LANES), 0)
                       == lax.broadcasted_iota(jnp.int32, (LANES, LANES), 1), 1.0, 0.0)
    zero16 = jnp.zeros((CHUNK, LANES), BF16)
    lane1 = lax.broadcasted_iota(jnp.int32, (1, LANES), 1)
    m0_16 = jnp.where(lane1 < RW_HEAD_DIM, 1.0, 0.0).astype(BF16)
    m1_16 = jnp.where(lane1 < RW_HEAD_DIM, 0.0, 1.0).astype(BF16)

    def cast(xs):
        return [x.astype(BF16) for x in xs]

    def bd(q16):
        return jnp.concatenate([q16 * m0_16, q16 * m1_16], axis=0)

    def mm(p16, q16):
        return jnp.dot(p16, q16, preferred_element_type=F32)

    def mm_nt(p16, q16):
        return lax.dot_general(p16, q16, (((1,), (1,)), ((), ())), preferred_element_type=F32)

    a16, r16, v16 = cast(a_t), cast(r_t), cast(v)
    lhs = [jnp.concatenate([a16[i], r16[i]], axis=0) for i in idx]
    b16, k16 = cast(b_t), cast(k_t)
    bdv = [bd(x) for x in v16]
    g = [mm_nt(lhs[i], jnp.concatenate([bd(b16[i]), bd(k16[i])], axis=0)) for i in idx]
    x1 = [jnp.where(near, g[i][:CHUNK, :LANES], 0.0) for i in idx]
    x1_16 = cast(x1)
    l_e16 = [jnp.where(far, g[i][:CHUNK, :LANES], 0.0).astype(BF16) for i in idx]
    a_k16 = [jnp.concatenate([jnp.where(strict, g[i][:CHUNK, LANES:], 0.0),
                              jnp.where(incl, g[i][CHUNK:, LANES:], 0.0)], axis=0).astype(BF16) for i in idx]
    a_rb16 = [jnp.where(incl, g[i][CHUNK:, :LANES], 0.0).astype(BF16) for i in idx]

    def both(p16, q16):
        return jnp.concatenate([bd(p16), bd(q16)], axis=1)

    x2_16 = cast([mm(x1_16[i], bd(x1_16[i])) for i in idx])
    hook()
    acc = [eye + x1[i] for i in idx]
    sq = [mm(x2_16[i], both(x2_16[i], acc[i].astype(BF16))) for i in idx]
    x4_16 = [sq[i][:, :LANES].astype(BF16) for i in idx]
    acc = [acc[i] + sq[i][:, LANES:] for i in idx]
    sq = [mm(x4_16[i], both(x4_16[i], acc[i].astype(BF16))) for i in idx]
    hook()
    x8_16 = [sq[i][:, :LANES].astype(BF16) for i in idx]
    acc = [acc[i] + sq[i][:, LANES:] for i in idx]
    dinv = [acc[i] + mm(x8_16[i], bd(acc[i].astype(BF16))) for i in idx]
    dinv16 = cast(dinv)
    hook()
    f16 = cast([mm(dinv16[i], bd(l_e16[i])) for i in idx])
    sq = [mm(f16[i], both(f16[i], dinv16[i])) for i in idx]
    f2_16 = [sq[i][:, :LANES].astype(BF16) for i in idx]
    hook()
    hmat = [dinv[i] + sq[i][:, LANES:] for i in idx]
    tinv16 = cast([hmat[i] + mm(f2_16[i], bd(hmat[i].astype(BF16))) for i in idx])

    kv = [mm(a_k16[i], bdv[i]) for i in idx]
    wv16 = [kv[i][:CHUNK].astype(BF16) for i in idx]
    hook()
    aw = [mm(tinv16[i], both(a16[i], wv16[i])) for i in idx]
    aw16 = cast(aw)
    hook()
    qy = [mm(a_rb16[i], both(aw16[i][:, :LANES], aw16[i][:, LANES:]))
          + jnp.concatenate([r_t[i], kv[i][CHUNK:]], axis=1) for i in idx]
    be_ke_t = [jnp.concatenate([b_e[i], k_e[i]], axis=0).T.astype(BF16) for i in idx]
    rhs = [jnp.concatenate([aw16[i], jnp.concatenate([zero16, v16[i]], axis=1)], axis=0) for i in idx]
    mn = [jnp.where(bdmask2, mm(be_ke_t[i], rhs[i]), 0.0) for i in idx]
    qhat = [x[:, :LANES] for x in qy]
    yhat = [x[:, LANES:] for x in qy]
    mmat = [eye_bd * p_end[i] + mn[i][:, :LANES] for i in idx]
    nmat = [x[:, LANES:] for x in mn]
    while hooks:
        hook()
    return qhat, yhat, mmat, nmat


def _rwkv_kernel(rkv_ref, lr_ref, w0_ref, a0_ref, wwa_ref, gup_ref,
                 kk_ref, ka_ref, rk_ref, lnxg_ref, lnxb_ref, o_ref, state, yout_s, *, tb):
    n_chunks = tb // CHUNK
    n_pairs = RW_WIDTH // LANES

    @pl.when(pl.program_id(1) == 0)
    def _():
        state[...] = jnp.zeros_like(state)

    r = rkv_ref[0, :, :RW_WIDTH].astype(F32)
    k = rkv_ref[0, :, RW_WIDTH:2 * RW_WIDTH].astype(F32)
    v = rkv_ref[0, :, 2 * RW_WIDTH:].astype(F32)
    lr_m = lr_ref[0].astype(F32)

    wa = lr_m[:, :LANES]
    lane = lax.broadcasted_iota(jnp.int32, wa.shape, 1)
    wa = jnp.where(lane < RW_DECAY_RANK, jnp.tanh(wa), wa)
    wa_up = _dot(wa, wwa_ref[...])
    ld = -math.exp(-0.5) * _sigmoid(w0_ref[...] + wa_up[:, :RW_WIDTH])
    iclr = _sigmoid(a0_ref[...] + wa_up[:, RW_WIDTH:])
    gate = _dot(_sigmoid(lr_m[:, LANES:]), gup_ref[...])

    hr = lax.broadcasted_iota(jnp.int32, (MXU_DIM, MXU_DIM), 0) // RW_HEAD_DIM
    hc = lax.broadcasted_iota(jnp.int32, (MXU_DIM, MXU_DIM), 1) // RW_HEAD_DIM
    seg = jnp.where(hr == hc, 1.0 / RW_HEAD_DIM, 0.0).astype(BF16)

    def seg_mean(x):
        x16 = x.astype(BF16)
        return jnp.concatenate([jnp.dot(x16[:, c:c + MXU_DIM], seg, preferred_element_type=F32)
                                for c in range(0, RW_WIDTH, MXU_DIM)], axis=1)

    kk = k * kk_ref[...]
    ss = seg_mean(kk * kk) * RW_HEAD_DIM
    kk = kk * lax.rsqrt(jnp.maximum(ss, 1e-24))
    kh = k * (1.0 + (iclr - 1.0) * ka_ref[...])

    tr = lax.broadcasted_iota(jnp.int32, (MXU_DIM, MXU_DIM), 0)
    tc = lax.broadcasted_iota(jnp.int32, (MXU_DIM, MXU_DIM), 1)
    tri = jnp.where((tc <= tr) & (tc // CHUNK == tr // CHUNK), 1.0, 0.0).astype(BF16)
    ld_hi = ld.astype(BF16)
    ld_lo = (ld - ld_hi.astype(F32)).astype(BF16)
    lp = jnp.concatenate(
        [jnp.dot(tri, ld_hi[t0:t0 + MXU_DIM], preferred_element_type=F32)
         + jnp.dot(tri, ld_lo[t0:t0 + MXU_DIM], preferred_element_type=F32)
         for t0 in range(0, tb, MXU_DIM)], axis=0)

    a_t, r_t, b_t, k_t, b_e, k_e, vs, p_end = [], [], [], [], [], [], [], []
    b = kk * iclr
    for c in range(n_chunks):
        rs = slice(c * CHUNK, (c + 1) * CHUNK)
        lp_c = lp[rs]
        lp_last = lp_c[CHUNK - 1:CHUNK]
        e_in = jnp.exp(lp_c)
        e_inv = jnp.exp(-lp_c)
        pe_c = jnp.exp(lp_last)
        a_c = -kk[rs] * jnp.exp(lp_c - ld[rs])
        r_c = r[rs] * e_in
        b_c = b[rs] * e_inv
        k_c = kh[rs] * e_inv
        be_c = b_c * pe_c
        ke_c = k_c * pe_c
        for p in range(n_pairs):
            cs = slice(p * LANES, (p + 1) * LANES)
            a_t.append(a_c[:, cs])
            r_t.append(r_c[:, cs])
            b_t.append(b_c[:, cs])
            k_t.append(k_c[:, cs])
            b_e.append(be_c[:, cs])
            k_e.append(ke_c[:, cs])
            vs.append(v[rs, cs])
            p_end.append(pe_c[:, cs])

    bonus = seg_mean(r * kh * rk_ref[...]) * RW_HEAD_DIM * v
    s = [state[p] for p in range(n_pairs)]

    def serial_step(c, terms, c0):
        qhat, yhat, mmat, nmat = terms
        base = (c - c0) * n_pairs
        ym = [_dot(jnp.concatenate([qhat[base + p], mmat[base + p]], axis=0), s[p]) for p in range(n_pairs)]
        for p in range(n_pairs):
            yout_s[c * CHUNK:(c + 1) * CHUNK, p * LANES:(p + 1) * LANES] = ym[p][:CHUNK] + yhat[base + p]
            s[p] = ym[p][CHUNK:] + nmat[base + p]

    def finish(rows):
        y = yout_s[rows, :]
        yc = y - seg_mean(y)
        var = seg_mean(yc * yc)
        y = yc * lax.rsqrt(var + RW_GN_EPS) * lnxg_ref[...] + lnxb_ref[...]
        o_ref[0, rows, :] = ((y + bonus[rows]) * gate[rows]).astype(o_ref.dtype)

    half = n_chunks // 2
    cut = half * n_pairs
    slabs = (a_t, r_t, b_t, k_t, b_e, k_e, vs, p_end)
    first = _rwkv_chunk_terms(*[x[:cut] for x in slabs])
    hooks = [functools.partial(serial_step, c, first, 0) for c in range(half)]
    hooks.append(functools.partial(finish, slice(0, half * CHUNK)))
    second = _rwkv_chunk_terms(*[x[cut:] for x in slabs], hooks=hooks)
    for c in range(half, n_chunks):
        serial_step(c, second, half)
    for p in range(n_pairs):
        state[p] = s[p]
    finish(slice(half * CHUNK, tb))


def _rwkv(rkv, lr, w0, a0, wwa, gup, k_k, k_a, r_k, lnx_g, lnx_b, *, tb=512):
    bsz, t, w3 = rkv.shape
    n_pairs = RW_WIDTH // LANES

    def full(shape):
        return pl.BlockSpec(shape, lambda b, i: (0,) * len(shape))

    return pl.pallas_call(
        functools.partial(_rwkv_kernel, tb=tb),
        grid=(bsz, t // tb),
        in_specs=[pl.BlockSpec((1, tb, w3), lambda b, i: (b, i, 0)),
                  pl.BlockSpec((1, tb, LR_PAD), lambda b, i: (b, i, 0)),
                  full((1, RW_WIDTH)), full((1, RW_WIDTH)),
                  full((LANES, 2 * RW_WIDTH)), full((LR_PAD - LANES, RW_WIDTH)),
                  full((1, RW_WIDTH)), full((1, RW_WIDTH)), full((1, RW_WIDTH)),
                  full((1, RW_WIDTH)), full((1, RW_WIDTH))],
        out_specs=pl.BlockSpec((1, tb, RW_WIDTH), lambda b, i: (b, i, 0)),
        out_shape=jax.ShapeDtypeStruct((bsz, t, RW_WIDTH), BF16),
        scratch_shapes=[pltpu.VMEM((n_pairs, LANES, LANES), F32),
                        pltpu.VMEM((tb, RW_WIDTH), F32)],
        compiler_params=_cparams(2),
        name="rwkv",
    )(rkv, lr, w0, a0, wwa, gup, k_k, k_a, r_k, lnx_g, lnx_b)


def _attn_kernel(q_ref, kp_ref, kc_ref, vp_ref, vc_ref, bias_ref, o_ref, lse_ref, *, nb, rb):
    step = pl.program_id(1)
    n_blk = rb // BLOCK
    scale = DIL_HEAD_DIM ** -0.5
    n_pairs = DIL_WIDTH // LANES
    krow = lax.broadcasted_iota(jnp.int32, (2 * BLOCK, LANES), 0) // BLOCK
    klane = lax.broadcasted_iota(jnp.int32, (2 * BLOCK, LANES), 1) // DIL_HEAD_DIM
    own = krow == klane
    zeros = jnp.zeros((2 * BLOCK, LANES), BF16)
    ones_bd = jnp.where(own, 1.0, 0.0).astype(BF16)
    lane_o = lax.broadcasted_iota(jnp.int32, (BLOCK, LANES), 1)

    def stack_bd(x):
        return jnp.where(own, jnp.concatenate([x, x], axis=0), zeros)

    def attend(blocks):
        units = [(rows, q, parts, p) for rows, q, parts in blocks for p in range(n_pairs)]
        lanes = [slice(p * LANES, (p + 1) * LANES) for p in range(n_pairs)]
        scores = []
        for _, q, parts, p in units:
            qs = q[:, lanes[p]] * scale
            row = []
            for k_blk, _, col0, pen in parts:
                s = lax.dot_general(qs, stack_bd(k_blk[:, lanes[p]]), (((1,), (1,)), ((), ())),
                                    preferred_element_type=F32)
                bias = jnp.concatenate([bias_ref[2 * p, :, col0:col0 + BLOCK],
                                        bias_ref[2 * p + 1, :, col0:col0 + BLOCK]], axis=1)
                s = s + bias
                row.append(s if pen is None else s + pen)
            scores.append(row)
        ms = []
        for u in range(len(units)):
            halves = []
            for hh in range(2):
                cols = slice(hh * BLOCK, (hh + 1) * BLOCK)
                m = jnp.max(scores[u][0][:, cols], axis=-1, keepdims=True)
                for s in scores[u][1:]:
                    m = jnp.maximum(m, jnp.max(s[:, cols], axis=-1, keepdims=True))
                halves.append(m)
            ms.append(halves)
        es = [[jnp.exp((s - jnp.concatenate([jnp.broadcast_to(ms[u][0], (BLOCK, BLOCK)),
                                             jnp.broadcast_to(ms[u][1], (BLOCK, BLOCK))], axis=1)).astype(BF16))
               for s in scores[u]] for u in range(len(units))]
        res = []
        for u, (_, _, parts, p) in enumerate(units):
            res.append(sum(jnp.dot(e, jnp.concatenate([stack_bd(part[1][:, lanes[p]]), ones_bd], axis=1),
                                   preferred_element_type=F32) for e, part in zip(es[u], parts)))
        for b, (rows, _, _) in enumerate(blocks):
            m_all = jnp.zeros((BLOCK, LANES), F32)
            den_all = jnp.ones((BLOCK, LANES), F32)
            for p in range(n_pairs):
                u = b * n_pairs + p
                num, den = res[u][:, :LANES], res[u][:, LANES:]
                o_ref[rows, lanes[p]] = (num / den).astype(o_ref.dtype)
                m_all = jnp.where(lane_o == 2 * p, ms[u][0], jnp.where(lane_o == 2 * p + 1, ms[u][1], m_all))
                den_all = jnp.where((lane_o == 2 * p), den,
                                    jnp.where(lane_o == 2 * p + 1, pltpu.roll(den, DIL_HEAD_DIM, 1), den_all))
            lse_ref[rows, :] = m_all + jnp.log(den_all)

    blocks = []
    for j in range(n_blk):
        rows = slice(j * BLOCK, (j + 1) * BLOCK)
        cur = (kc_ref[rows, :], vc_ref[rows, :], BLOCK, None)
        if n_blk % nb == 0:
            prev = "none" if j % nb == 0 else "inside"
        else:
            prev = "inside" if j > 0 else ("none" if nb == n_blk else "before")
        if prev == "none":
            parts = [cur]
        elif prev == "inside":
            before = slice((j - 1) * BLOCK, j * BLOCK)
            parts = [(kc_ref[before, :], vc_ref[before, :], 0, None), cur]
        else:
            has_prev = (step % (nb // n_blk)) != 0
            parts = [(kp_ref[...], vp_ref[...], 0, jnp.where(has_prev, 0.0, MASKED)), cur]
        blocks.append((rows, q_ref[rows, :], parts))
    attend(blocks)


def _qkv_proj_kernel(x_ref, mod_ref, w_ref, o0_ref, o1_ref, o2_ref, xs_ref, xs4_ref, *, tm):
    m = mod_ref[0]
    scale = 1.0 + m[1:2]
    shift = m[0:1]
    n_col = x_ref.shape[-1] // LANES
    gw = 3 * DIL_WIDTH
    step = DIL_PATTERNS[1][1]
    assert [dil for _, dil in DIL_PATTERNS] == [1, step, step * step]

    for c in range(n_col):
        xs_ref[c] = x_ref[0, :, c * LANES:(c + 1) * LANES]
    rows4 = tm // step
    x4 = [jnp.concatenate([xs_ref[c, pl.ds(a, rows4, stride=step), :] for a in range(step)], axis=0)
          for c in range(n_col)]
    for c in range(n_col):
        xs4_ref[c] = x4[c]
    rows16 = rows4 // step
    x16 = [jnp.concatenate([xs4_ref[c, pl.ds(a * rows4 + b, rows16, stride=step), :]
                            for a in range(step) for b in range(step)], axis=0) for c in range(n_col)]
    inputs = [x_ref[0], jnp.concatenate(x4, axis=1), jnp.concatenate(x16, axis=1)]
    residue_of_group = [[0], list(range(step)), [a + step * b for a in range(step) for b in range(step)]]

    for g, o_ref in enumerate((o0_ref, o1_ref, o2_ref)):
        u = (inputs[g] * scale + shift).astype(BF16)
        res = jnp.dot(u, w_ref[:, g * gw:(g + 1) * gw].astype(BF16), preferred_element_type=F32).astype(BF16)
        rows = tm // len(residue_of_group[g])
        for grp, r in enumerate(residue_of_group[g]):
            o_ref[0, r] = res[grp * rows:(grp + 1) * rows]


def _qkv_proj(x, mod, w, *, tm=1024):
    bsz, t, d = x.shape
    gw = 3 * DIL_WIDTH
    out_specs, out_shape = [], []
    for _, dil in DIL_PATTERNS:
        out_specs.append(pl.BlockSpec((1, dil, tm // dil, gw), lambda b, i: (b, 0, i, 0)))
        out_shape.append(jax.ShapeDtypeStruct((bsz, dil, t // dil, gw), BF16))
    return pl.pallas_call(
        functools.partial(_qkv_proj_kernel, tm=tm),
        grid=(bsz, t // tm),
        in_specs=[pl.BlockSpec((1, tm, d), lambda b, i: (b, i, 0)),
                  pl.BlockSpec((1, 6, d), lambda b, i: (b, 0, 0)),
                  pl.BlockSpec(w.shape, lambda b, i: (0, 0), pipeline_mode=pl.Buffered(1))],
        out_specs=out_specs,
        out_shape=out_shape,
        scratch_shapes=[pltpu.VMEM((d // LANES, tm, LANES), F32), pltpu.VMEM((d // LANES, tm, LANES), F32)],
        compiler_params=_cparams(2),
        name="qkv_proj",
    )(x, mod, w)


def _dilated_attention_group(qkv, bias, g, *, rb=2048):
    bsz, dil, length, width = qkv.shape
    t = dil * length
    nb = length // BLOCK
    n_blk = rb // BLOCK
    assert nb % n_blk == 0 or n_blk % nb == 0
    flat = qkv.reshape(bsz, t, width)

    def cur(j):
        return lambda b, s: (b, s, j)

    def prev(j):
        return lambda b, s: (b, jnp.maximum(s * n_blk - 1, 0), j)

    blk = (None, rb, DIL_WIDTH)
    pblk = (None, BLOCK, DIL_WIDTH)
    o, lse = pl.pallas_call(
        functools.partial(_attn_kernel, nb=nb, rb=rb),
        grid=(bsz, t // rb),
        in_specs=[pl.BlockSpec(blk, cur(0)),
                  pl.BlockSpec(pblk, prev(1)), pl.BlockSpec(blk, cur(1)),
                  pl.BlockSpec(pblk, prev(2)), pl.BlockSpec(blk, cur(2)),
                  pl.BlockSpec((DIL_HEADS, BLOCK, 2 * BLOCK), lambda b, s: (0, 0, 0))],
        out_specs=[pl.BlockSpec(blk, lambda b, s: (b, s, 0)),
                   pl.BlockSpec((None, rb, LANES), lambda b, s: (b, s, 0))],
        out_shape=[jax.ShapeDtypeStruct((bsz, t, DIL_WIDTH), BF16),
                   jax.ShapeDtypeStruct((bsz, t, LANES), F32)],
        compiler_params=_cparams(2),
        name=f"dilated_attn_g{g}",
    )(flat, flat, flat, flat, flat, bias)
    return o.reshape(bsz, dil, length, DIL_WIDTH), lse.reshape(bsz, dil, length, LANES)


def _t5_bucket(dist):
    exact = N_BUCKETS // 2
    logd = jnp.log(jnp.maximum(dist, 1).astype(F32) / exact) / math.log(MAX_DISTANCE / exact)
    large = jnp.minimum(exact + (logd * (N_BUCKETS - exact)).astype(jnp.int32), N_BUCKETS - 1)
    return jnp.where(dist < exact, dist, large)


def _bias_tables(rel_bias):
    rel = jnp.arange(BLOCK + 1)
    tables = []
    for g, (window, dil) in enumerate(DIL_PATTERNS):
        span = window // dil
        bucket = _t5_bucket(jnp.clip(rel, 0, span) * dil)
        vec = rel_bias[bucket][:, g * DIL_HEADS:(g + 1) * DIL_HEADS].T.astype(F32)
        ext = jnp.concatenate([vec[:, ::-1], jnp.zeros((DIL_HEADS, BLOCK), F32)], axis=1)
        flat = jnp.tile(ext, (1, BLOCK))[:, :BLOCK * 2 * BLOCK]
        tables.append(flat.reshape(DIL_HEADS, BLOCK, 2 * BLOCK))
    qi = np.arange(BLOCK)[:, None]
    ki = np.arange(2 * BLOCK)[None, :]
    band = (ki >= qi) & (ki <= qi + BLOCK)
    return jnp.where(band, jnp.stack(tables), MASKED)


def _tail(mix_fn, wout_ref, x_ref, mod_ref, lng_ref, lnb_ref, w1_ref, w2_ref, o_ref, *, tm, n_sub, ff_chunk):
    m = mod_ref[0]
    sub = tm // n_sub
    tiles = [slice(i * sub, (i + 1) * sub) for i in range(n_sub)]
    x1s, us = [], []
    for rows in tiles:
        y = None
        for piece, k0 in mix_fn(rows):
            part = jnp.dot(piece, wout_ref[k0:k0 + piece.shape[-1], :], preferred_element_type=F32)
            y = part if y is None else y + part
        x1 = _layer_norm_rows(DEEPNORM_ALPHA * x_ref[0, rows, :] + (1.0 + m[2:3]) * y, lng_ref[0:1], lnb_ref[0:1])
        x1s.append(x1)
        us.append((x1 * (1.0 + m[4:5]) + m[3:4]).astype(BF16))
    accs = [None] * n_sub
    for c in range(D_FF // ff_chunk):
        cols = slice(c * ff_chunk, (c + 1) * ff_chunk)
        w1c = w1_ref[:, cols].astype(BF16)
        w2c = w2_ref[cols, :].astype(BF16)
        for i in range(n_sub):
            h = jnp.maximum(jnp.dot(us[i], w1c, preferred_element_type=F32), 0.0)
            part = jnp.dot((h * h).astype(BF16), w2c, preferred_element_type=F32)
            accs[i] = part if accs[i] is None else accs[i] + part
    for i, rows in enumerate(tiles):
        z = DEEPNORM_ALPHA * x1s[i] + (1.0 + m[5:6]) * accs[i]
        o_ref[0, rows, :] = _layer_norm_rows(z, lng_ref[1:2], lnb_ref[1:2])


def _even_tail_kernel(ya_ref, yb_ref, wout_ref, x_ref, mod_ref, lng_ref, lnb_ref, w1_ref, w2_ref, o_ref, **kw):
    def mix(rows):
        return [(ya_ref[0, rows, :], 0), (yb_ref[0, rows, :], ya_ref.shape[-1])]

    _tail(mix, wout_ref, x_ref, mod_ref, lng_ref, lnb_ref, w1_ref, w2_ref, o_ref, **kw)


def _odd_tail_kernel(o0_ref, o1_ref, o2_ref, l0_ref, l1_ref, l2_ref, wout_ref, x_ref, mod_ref, lng_ref, lnb_ref,
                     w1_ref, w2_ref, o_ref, o_nat, l_nat, mix_s, *, tm, **kw):
    def natural(src_ref, scr, dil):
        if dil == 1:
            return src_ref[0, 0].astype(F32)
        n_col = src_ref.shape[-1] // LANES
        for c in range(n_col):
            for r in range(dil):
                scr[c, pl.ds(r, tm // dil, stride=dil), :] = src_ref[0, r, :, c * LANES:(c + 1) * LANES].astype(F32)
        return jnp.concatenate([scr[c] for c in range(n_col)], axis=1)

    dils = [dil for _, dil in DIL_PATTERNS]
    lses = [natural(ref, l_nat, dil) for ref, dil in zip((l0_ref, l1_ref, l2_ref), dils)]
    mx = jnp.maximum(jnp.maximum(lses[0], lses[1]), lses[2])
    es = [jnp.exp(l - mx) for l in lses]
    inv = 1.0 / (es[0] + es[1] + es[2])
    hr = lax.broadcasted_iota(jnp.int32, (LANES, DIL_WIDTH), 0)
    hc = lax.broadcasted_iota(jnp.int32, (LANES, DIL_WIDTH), 1) // DIL_HEAD_DIM
    expand = jnp.where(hr == hc, 1.0, 0.0).astype(BF16)

    def widen(wt):
        return jnp.dot(wt.astype(BF16), expand, preferred_element_type=F32)

    merged = None
    for e, ref, dil in zip(es, (o0_ref, o1_ref, o2_ref), dils):
        term = widen(e * inv) * natural(ref, o_nat, dil)
        merged = term if merged is None else merged + term
    mix_s[...] = merged.astype(BF16)
    _tail(lambda rows: [(mix_s[rows, :], 0)], wout_ref, x_ref, mod_ref, lng_ref, lnb_ref, w1_ref, w2_ref, o_ref,
          tm=tm, **kw)


def _layer_tail(mix, w_out, x, mod, ln_g, ln_b, w1, w2, layer, *, tm=512, n_sub=2, ff_chunk=1024):
    bsz, t, d = x.shape

    def const(shape):
        return pl.BlockSpec(shape, lambda b, i: (0,) * len(shape), pipeline_mode=pl.Buffered(1))

    def of_layer(shape):
        return pl.BlockSpec((None,) + shape, lambda b, i: (layer,) + (0,) * len(shape), pipeline_mode=pl.Buffered(1))

    tail_specs = [const(w_out.shape),
                  pl.BlockSpec((1, tm, d), lambda b, i: (b, i, 0)),
                  pl.BlockSpec((1, 6, d), lambda b, i: (b, 0, 0)),
                  of_layer((2, d)), of_layer((2, d)), of_layer((d, D_FF)), of_layer((D_FF, d))]
    tail_args = (w_out, x, mod, ln_g, ln_b, w1, w2)
    kw = dict(tm=tm, n_sub=n_sub, ff_chunk=ff_chunk)
    if isinstance(mix, tuple):
        outs, lses = mix
        mix_specs = ([pl.BlockSpec((1, dil, tm // dil, DIL_WIDTH), lambda b, i: (b, 0, i, 0)) for _, dil in DIL_PATTERNS]
                     + [pl.BlockSpec((1, dil, tm // dil, LANES), lambda b, i: (b, 0, i, 0)) for _, dil in DIL_PATTERNS])
        mix_args = (*outs, *lses)
        body = functools.partial(_odd_tail_kernel, **kw)
        scratch = [pltpu.VMEM((DIL_WIDTH // LANES, tm, LANES), F32), pltpu.VMEM((1, tm, LANES), F32),
                   pltpu.VMEM((tm, DIL_WIDTH), BF16)]
    else:
        mix_specs = [pl.BlockSpec((1, tm, piece.shape[-1]), lambda b, i: (b, i, 0)) for piece in mix]
        mix_args = tuple(mix)
        body = functools.partial(_even_tail_kernel, **kw)
        scratch = []
    return pl.pallas_call(
        body,
        grid=(bsz, t // tm),
        in_specs=mix_specs + tail_specs,
        out_specs=pl.BlockSpec((1, tm, d), lambda b, i: (b, i, 0)),
        out_shape=jax.ShapeDtypeStruct((bsz, t, d), F32),
        scratch_shapes=scratch,
        compiler_params=_cparams(2, TAIL_VMEM_LIMIT),
        name="layer_tail",
    )(*mix_args, *tail_args)


def _prep_even_weights(mu, w_up, a_up, g_up):
    pad = LR_PAD - (RW_PROJ - 3 * RW_WIDTH)
    mu_pad = jnp.concatenate([mu, jnp.zeros((pad,), mu.dtype)]).reshape(1, 3 * RW_WIDTH + LR_PAD)
    zeros = jnp.zeros((RW_DECAY_RANK, RW_WIDTH), w_up.dtype)
    wwa = jnp.concatenate([jnp.concatenate([w_up, zeros], axis=1),
                           jnp.concatenate([zeros, a_up], axis=1)], axis=0).astype(BF16)
    gup = jnp.concatenate([g_up, jnp.zeros((pad, RW_WIDTH), g_up.dtype)], axis=0).astype(BF16)
    return mu_pad, wwa, gup


def kernel(x, c, ada_w, ada_b, ln_g, ln_b, ab_w_in, rw_mu, rw_w0, rw_w_up, rw_a0, rw_a_up, rw_g_up,
           rw_k_k, rw_k_a, rw_r_k, rw_lnx_g, rw_lnx_b, sc_conv_w, ab_w_out, dil_w_qkv, dil_w_out,
           rel_bias, mlp_w1, mlp_w2):
    bsz, t, d = x.shape
    mods = _adaln(c, ada_w, ada_b).reshape(DEPTH, bsz, 6, d)
    bias = None
    row = lambda a: a.reshape(1, -1)
    w1_all, w2_all = mlp_w1, mlp_w2
    for i in range(DEPTH):
        mod = mods[i]
        j = i // 2
        if i % 2 == 0:
            mu_pad, wwa, gup = _prep_even_weights(rw_mu[j], rw_w_up[j], rw_a_up[j], rw_g_up[j])
            rkv, lr, yb = _ab_proj(x, mod, ab_w_in[j].T, mu_pad, sc_conv_w[j])
            ya = _rwkv(rkv, lr, row(rw_w0[j]), row(rw_a0[j]), wwa, gup,
                       row(rw_k_k[j]), row(rw_k_a[j]), row(rw_r_k[j]),
                       row(rw_lnx_g[j]), row(rw_lnx_b[j]))
            mix, w_out = [ya, yb], ab_w_out[j]
        else:
            if bias is None:
                bias = _bias_tables(rel_bias)
            qkvs = _qkv_proj(x, mod, dil_w_qkv[j])
            outs, lses = [], []
            for g in range(N_GROUPS):
                o, lse = _dilated_attention_group(qkvs[g], bias[g], g)
                outs.append(o)
                lses.append(lse)
            mix, w_out = (outs, lses), dil_w_out[j]
        x = _layer_tail(mix, w_out.astype(BF16), x, mod, ln_g, ln_b, w1_all, w2_all, i)
    return x
```

```python
import functools
import math

import jax
import jax.numpy as jnp
import numpy as np
from jax import lax
from jax.experimental import pallas as pl
from jax.experimental.pallas import tpu as pltpu

F32 = jnp.float32
BF16 = jnp.bfloat16

D_MODEL = 1024
DEPTH = 2
RW_HEADS = 8
RW_HEAD_DIM = 64
RW_WIDTH = 512
RW_DECAY_RANK = 64
RW_ICLR_RANK = 64
RW_GATE_RANK = 160
RW_GN_EPS = 64e-5
RW_PROJ = 3 * RW_WIDTH + RW_DECAY_RANK + RW_ICLR_RANK + RW_GATE_RANK
SC_WIDTH = 512
SC_CONV = 3
DIL_PATTERNS = ((128, 1), (512, 4), (2048, 16))
N_GROUPS = 3
DIL_HEADS = 8
DIL_HEAD_DIM = 64
DIL_WIDTH = 512
BLOCK = 128
N_BUCKETS = 32
MAX_DISTANCE = 2048
D_FF = 4 * D_MODEL
DEEPNORM_ALPHA = (2 * DEPTH) ** 0.25
LN_EPS = 1e-5

LANES = 128
SUBLANES = 8
MXU_DIM = 256
CHUNK = 64
INV_BLOCK = 16
LR_PAD = 384
VMEM_LIMIT = 56 * 1024 * 1024
TAIL_VMEM_LIMIT = 60 * 1024 * 1024
MASKED = -1e30


def _cparams(n_axes, vmem_limit=VMEM_LIMIT):
    return pltpu.CompilerParams(dimension_semantics=("arbitrary",) * n_axes,
                                vmem_limit_bytes=vmem_limit)


def _sigmoid(x):
    return 0.5 * jnp.tanh(0.5 * x) + 0.5


def _dot(a, b):
    return jnp.dot(a.astype(BF16), b.astype(BF16), preferred_element_type=F32)


def _layer_norm_rows(z, g, b):
    mu = jnp.mean(z, axis=-1, keepdims=True)
    zc = z - mu
    var = jnp.mean(zc * zc, axis=-1, keepdims=True)
    return zc * lax.rsqrt(var + LN_EPS) * g + b


def _adaln_kernel(c_ref, w_ref, b_ref, o_ref):
    c = c_ref[...]
    cond = c * _sigmoid(c)
    o_ref[0] = _dot(cond, w_ref[0]) + b_ref[0]


def _adaln(c, ada_w, ada_b):
    depth, d, n = ada_w.shape
    bsz = c.shape[0]
    tn = 1536
    return pl.pallas_call(
        _adaln_kernel,
        grid=(depth, n // tn),
        in_specs=[pl.BlockSpec((bsz, d), lambda i, j: (0, 0)),
                  pl.BlockSpec((1, d, tn), lambda i, j: (i, 0, j)),
                  pl.BlockSpec((1, 1, tn), lambda i, j: (i, 0, j))],
        out_specs=pl.BlockSpec((1, bsz, tn), lambda i, j: (i, 0, j)),
        out_shape=jax.ShapeDtypeStruct((depth, bsz, n), F32),
        compiler_params=_cparams(2),
        name="adaln",
    )(c, ada_w, ada_b.reshape(depth, 1, n))


def _shift_rows(x, carry, k):
    rolled = pltpu.roll(x, k, 0)
    head = rolled[:SUBLANES]
    row = lax.broadcasted_iota(jnp.int32, head.shape, 0)
    for i in range(k):
        head = jnp.where(row == i, carry[SUBLANES - k + i:SUBLANES - k + i + 1], head)
    return jnp.concatenate([head, rolled[SUBLANES:]], axis=0)


def _ab_proj_kernel(x_ref, mod_ref, wt_ref, mu_ref, convw_ref, rkv_ref, lr_ref, yb_ref, carry_p, carry_z, *, tm):
    @pl.when(pl.program_id(1) == 0)
    def _():
        carry_p[...] = jnp.zeros_like(carry_p)
        carry_z[...] = jnp.zeros_like(carry_z)

    m = mod_ref[0]
    u = (x_ref[0] * (1.0 + m[1:2]) + m[0:1]).astype(BF16)

    def proj(c0, width):
        return lax.dot_general(u, wt_ref[c0:c0 + width, :].astype(BF16), (((1,), (1,)), ((), ())),
                               preferred_element_type=F32)

    def mixed(c0, width):
        p = proj(c0, width)
        prev = _shift_rows(p, carry_p[:, c0:c0 + width], 1)
        carry_p[:, c0:c0 + width] = p[tm - SUBLANES:]
        return (p + mu_ref[:, c0:c0 + width] * (prev - p)).astype(BF16)

    w3 = 3 * RW_WIDTH
    for c in range(3):
        rkv_ref[0, :, c * RW_WIDTH:(c + 1) * RW_WIDTH] = mixed(c * RW_WIDTH, RW_WIDTH)
    lr_ref[0] = mixed(w3, LR_PAD)

    h = proj(RW_PROJ, SC_WIDTH)
    z = proj(RW_PROJ + 2 * SC_WIDTH, SC_WIDTH) * h
    zc = carry_z[...]
    cw = convw_ref[...]
    conv = cw[2:3] * z + cw[1:2] * _shift_rows(z, zc, 1) + cw[0:1] * _shift_rows(z, zc, 2)
    carry_z[...] = z[tm - SUBLANES:]
    yb_ref[0] = (proj(RW_PROJ + SC_WIDTH, SC_WIDTH) * conv).astype(BF16)


def _ab_proj(x, mod, w_t, mu, conv_w, *, tm=1024):
    bsz, t, d = x.shape
    w3 = 3 * RW_WIDTH

    def const(shape):
        return pl.BlockSpec(shape, lambda b, i: (0,) * len(shape), pipeline_mode=pl.Buffered(1))

    def rows(width):
        return pl.BlockSpec((1, tm, width), lambda b, i: (b, i, 0))

    return pl.pallas_call(
        functools.partial(_ab_proj_kernel, tm=tm),
        grid=(bsz, t // tm),
        in_specs=[rows(d), pl.BlockSpec((1, 6, d), lambda b, i: (b, 0, 0)),
                  const(w_t.shape), const((1, w3 + LR_PAD)), const((SC_CONV, SC_WIDTH))],
        out_specs=[rows(w3), rows(LR_PAD), rows(SC_WIDTH)],
        out_shape=[jax.ShapeDtypeStruct((bsz, t, w3), BF16), jax.ShapeDtypeStruct((bsz, t, LR_PAD), BF16),
                   jax.ShapeDtypeStruct((bsz, t, SC_WIDTH), BF16)],
        scratch_shapes=[pltpu.VMEM((SUBLANES, w3 + LR_PAD), F32), pltpu.VMEM((SUBLANES, SC_WIDTH), F32)],
        compiler_params=_cparams(2),
        name="ab_proj",
    )(x, mod, w_t, mu, conv_w)


def _rwkv_chunk_terms(a_t, r_t, b_t, k_t, b_e, k_e, v, p_end, hooks=()):
    n = len(a_t)
    idx = range(n)
    hooks = list(hooks)

    def hook():
        if hooks:
            hooks.pop(0)()
    row = lax.broadcasted_iota(jnp.int32, (CHUNK, LANES), 0)
    col = lax.broadcasted_iota(jnp.int32, (CHUNK, LANES), 1)
    s_idx = col % CHUNK
    strict = s_idx < row
    incl = s_idx <= row
    near = strict & ((s_idx // INV_BLOCK) == (row // INV_BLOCK))
    far = strict & ((s_idx // INV_BLOCK) != (row // INV_BLOCK))
    eye = jnp.where(s_idx == row, 1.0, 0.0)
    r2 = lax.broadcasted_iota(jnp.int32, (LANES, 2 * LANES), 0)
    c2 = lax.broadcasted_iota(jnp.int32, (LANES, 2 * LANES), 1)
    bdmask2 = (r2 // RW_HEAD_DIM) == ((c2 % LANES) // RW_HEAD_DIM)
    eye_bd = jnp.where(lax.broadcasted_iota(jnp.int32, (LANES, LANES), 0)
                       == lax.broadcasted_iota(jnp.int32, (LANES, LANES), 1), 1.0, 0.0)
    zero16 = jnp.zeros((CHUNK, LANES), BF16)
    lane1 = lax.broadcasted_iota(jnp.int32, (1, LANES), 1)
    m0_16 = jnp.where(lane1 < RW_HEAD_DIM, 1.0, 0.0).astype(BF16)
    m1_16 = jnp.where(lane1 < RW_HEAD_DIM, 0.0, 1.0).astype(BF16)

    def cast(xs):
        return [x.astype(BF16) for x in xs]

    def bd(q16):
        return jnp.concatenate([q16 * m0_16, q16 * m1_16], axis=0)

    def mm(p16, q16):
        return jnp.dot(p16, q16, preferred_element_type=F32)

    def mm_nt(p16, q16):
        return lax.dot_general(p16, q16, (((1,), (1,)), ((), ())), preferred_element_type=F32)

    a16, r16, v16 = cast(a_t), cast(r_t), cast(v)
    lhs = [jnp.concatenate([a16[i], r16[i]], axis=0) for i in idx]
    b16, k16 = cast(b_t), cast(k_t)
    bdv = [bd(x) for x in v16]
    g = [mm_nt(lhs[i], jnp.concatenate([bd(b16[i]), bd(k16[i])], axis=0)) for i in idx]
    x1 = [jnp.where(near, g[i][:CHUNK, :LANES], 0.0) for i in idx]
    x1_16 = cast(x1)
    l_e16 = [jnp.where(far, g[i][:CHUNK, :LANES], 0.0).astype(BF16) for i in idx]
    a_k16 = [jnp.concatenate([jnp.where(strict, g[i][:CHUNK, LANES:], 0.0),
                              jnp.where(incl, g[i][CHUNK:, LANES:], 0.0)], axis=0).astype(BF16) for i in idx]
    a_rb16 = [jnp.where(incl, g[i][CHUNK:, :LANES], 0.0).astype(BF16) for i in idx]

    def both(p16, q16):
        return jnp.concatenate([bd(p16), bd(q16)], axis=1)

    x2_16 = cast([mm(x1_16[i], bd(x1_16[i])) for i in idx])
    hook()
    acc = [eye + x1[i] for i in idx]
    sq = [mm(x2_16[i], both(x2_16[i], acc[i].astype(BF16))) for i in idx]
    x4_16 = [sq[i][:, :LANES].astype(BF16) for i in idx]
    acc = [acc[i] + sq[i][:, LANES:] for i in idx]
    sq = [mm(x4_16[i], both(x4_16[i], acc[i].astype(BF16))) for i in idx]
    hook()
    x8_16 = [sq[i][:, :LANES].astype(BF16) for i in idx]
    acc = [acc[i] + sq[i][:, LANES:] for i in idx]
    dinv = [acc[i] + mm(x8_16[i], bd(acc[i].astype(BF16))) for i in idx]
    dinv16 = cast(dinv)
    hook()
    f16 = cast([mm(dinv16[i], bd(l_e16[i])) for i in idx])
    sq = [mm(f16[i], both(f16[i], dinv16[i])) for i in idx]
    f2_16 = [sq[i][:, :LANES].astype(BF16) for i in idx]
    hook()
    hmat = [dinv[i] + sq[i][:, LANES:] for i in idx]
    tinv16 = cast([hmat[i] + mm(f2_16[i], bd(hmat[i].astype(BF16))) for i in idx])

    kv = [mm(a_k16[i], bdv[i]) for i in idx]
    wv16 = [kv[i][:CHUNK].astype(BF16) for i in idx]
    hook()
    aw = [mm(tinv16[i], both(a16[i], wv16[i])) for i in idx]
    aw16 = cast(aw)
    hook()
    qy = [mm(a_rb16[i], both(aw16[i][:, :LANES], aw16[i][:, LANES:]))
          + jnp.concatenate([r_t[i], kv[i][CHUNK:]], axis=1) for i in idx]
    be_ke_t = [jnp.concatenate([b_e[i], k_e[i]], axis=0).T.astype(BF16) for i in idx]
    rhs = [jnp.concatenate([aw16[i], jnp.concatenate([zero16, v16[i]], axis=1)], axis=0) for i in idx]
    mn = [jnp.where(bdmask2, mm(be_ke_t[i], rhs[i]), 0.0) for i in idx]
    qhat = [x[:, :LANES] for x in qy]
    yhat = [x[:, LANES:] for x in qy]
    mmat = [eye_bd * p_end[i] + mn[i][:, :LANES] for i in idx]
    nmat = [x[:, LANES:] for x in mn]
    while hooks:
        hook()
    return qhat, yhat, mmat, nmat


def _rwkv_kernel(rkv_ref, lr_ref, w0_ref, a0_ref, wwa_ref, gup_ref,
                 kk_ref, ka_ref, rk_ref, lnxg_ref, lnxb_ref, o_ref, state, yout_s, *, tb):
    n_chunks = tb // CHUNK
    n_pairs = RW_WIDTH // LANES

    @pl.when(pl.program_id(1) == 0)
    def _():
        state[...] = jnp.zeros_like(state)

    r = rkv_ref[0, :, :RW_WIDTH].astype(F32)
    k = rkv_ref[0, :, RW_WIDTH:2 * RW_WIDTH].astype(F32)
    v = rkv_ref[0, :, 2 * RW_WIDTH:].astype(F32)
    lr_m = lr_ref[0].astype(F32)

    wa = lr_m[:, :LANES]
    lane = lax.broadcasted_iota(jnp.int32, wa.shape, 1)
    wa = jnp.where(lane < RW_DECAY_RANK, jnp.tanh(wa), wa)
    wa_up = _dot(wa, wwa_ref[...])
    ld = -math.exp(-0.5) * _sigmoid(w0_ref[...] + wa_up[:, :RW_WIDTH])
    iclr = _sigmoid(a0_ref[...] + wa_up[:, RW_WIDTH:])
    gate = _dot(_sigmoid(lr_m[:, LANES:]), gup_ref[...])

    hr = lax.broadcasted_iota(jnp.int32, (MXU_DIM, MXU_DIM), 0) // RW_HEAD_DIM
    hc = lax.broadcasted_iota(jnp.int32, (MXU_DIM, MXU_DIM), 1) // RW_HEAD_DIM
    seg = jnp.where(hr == hc, 1.0 / RW_HEAD_DIM, 0.0).astype(BF16)

    def seg_mean(x):
        x16 = x.astype(BF16)
        return jnp.concatenate([jnp.dot(x16[:, c:c + MXU_DIM], seg, preferred_element_type=F32)
                                for c in range(0, RW_WIDTH, MXU_DIM)], axis=1)

    kk = k * kk_ref[...]
    ss = seg_mean(kk * kk) * RW_HEAD_DIM
    kk = kk * lax.rsqrt(jnp.maximum(ss, 1e-24))
    kh = k * (1.0 + (iclr - 1.0) * ka_ref[...])

    tr = lax.broadcasted_iota(jnp.int32, (MXU_DIM, MXU_DIM), 0)
    tc = lax.broadcasted_iota(jnp.int32, (MXU_DIM, MXU_DIM), 1)
    tri = jnp.where((tc <= tr) & (tc // CHUNK == tr // CHUNK), 1.0, 0.0).astype(BF16)
    ld_hi = ld.astype(BF16)
    ld_lo = (ld - ld_hi.astype(F32)).astype(BF16)
    lp = jnp.concatenate(
        [jnp.dot(tri, ld_hi[t0:t0 + MXU_DIM], preferred_element_type=F32)
         + jnp.dot(tri, ld_lo[t0:t0 + MXU_DIM], preferred_element_type=F32)
         for t0 in range(0, tb, MXU_DIM)], axis=0)

    a_t, r_t, b_t, k_t, b_e, k_e, vs, p_end = [], [], [], [], [], [], [], []
    b = kk * iclr
    for c in range(n_chunks):
        rs = slice(c * CHUNK, (c + 1) * CHUNK)
        lp_c = lp[rs]
        lp_last = lp_c[CHUNK - 1:CHUNK]
        e_in = jnp.exp(lp_c)
        e_inv = jnp.exp(-lp_c)
        pe_c = jnp.exp(lp_last)
        a_c = -kk[rs] * jnp.exp(lp_c - ld[rs])
        r_c = r[rs] * e_in
        b_c = b[rs] * e_inv
        k_c = kh[rs] * e_inv
        be_c = b_c * pe_c
        ke_c = k_c * pe_c
        for p in range(n_pairs):
            cs = slice(p * LANES, (p + 1) * LANES)
            a_t.append(a_c[:, cs])
            r_t.append(r_c[:, cs])
            b_t.append(b_c[:, cs])
            k_t.append(k_c[:, cs])
            b_e.append(be_c[:, cs])
            k_e.append(ke_c[:, cs])
            vs.append(v[rs, cs])
            p_end.append(pe_c[:, cs])

    bonus = seg_mean(r * kh * rk_ref[...]) * RW_HEAD_DIM * v
    s = [state[p] for p in range(n_pairs)]

    def serial_step(c, terms, c0):
        qhat, yhat, mmat, nmat = terms
        base = (c - c0) * n_pairs
        ym = [_dot(jnp.concatenate([qhat[base + p], mmat[base + p]], axis=0), s[p]) for p in range(n_pairs)]
        for p in range(n_pairs):
            yout_s[c * CHUNK:(c + 1) * CHUNK, p * LANES:(p + 1) * LANES] = ym[p][:CHUNK] + yhat[base + p]
            s[p] = ym[p][CHUNK:] + nmat[base + p]

    def finish(rows):
        y = yout_s[rows, :]
        yc = y - seg_mean(y)
        var = seg_mean(yc * yc)
        y = yc * lax.rsqrt(var + RW_GN_EPS) * lnxg_ref[...] + lnxb_ref[...]
        o_ref[0, rows, :] = ((y + bonus[rows]) * gate[rows]).astype(o_ref.dtype)

    half = n_chunks // 2
    cut = half * n_pairs
    slabs = (a_t, r_t, b_t, k_t, b_e, k_e, vs, p_end)
    first = _rwkv_chunk_terms(*[x[:cut] for x in slabs])
    hooks = [functools.partial(serial_step, c, first, 0) for c in range(half)]
    hooks.append(functools.partial(finish, slice(0, half * CHUNK)))
    second = _rwkv_chunk_terms(*[x[cut:] for x in slabs], hooks=hooks)
    for c in range(half, n_chunks):
        serial_step(c, second, half)
    for p in range(n_pairs):
        state[p] = s[p]
    finish(slice(half * CHUNK, tb))


def _rwkv(rkv, lr, w0, a0, wwa, gup, k_k, k_a, r_k, lnx_g, lnx_b, *, tb=1024):
    bsz, t, w3 = rkv.shape
    n_pairs = RW_WIDTH // LANES

    def full(shape):
        return pl.BlockSpec(shape, lambda b, i: (0,) * len(shape))

    return pl.pallas_call(
        functools.partial(_rwkv_kernel, tb=tb),
        grid=(bsz, t // tb),
        in_specs=[pl.BlockSpec((1, tb, w3), lambda b, i: (b, i, 0)),
                  pl.BlockSpec((1, tb, LR_PAD), lambda b, i: (b, i, 0)),
                  full((1, RW_WIDTH)), full((1, RW_WIDTH)),
                  full((LANES, 2 * RW_WIDTH)), full((LR_PAD - LANES, RW_WIDTH)),
                  full((1, RW_WIDTH)), full((1, RW_WIDTH)), full((1, RW_WIDTH)),
                  full((1, RW_WIDTH)), full((1, RW_WIDTH))],
        out_specs=pl.BlockSpec((1, tb, RW_WIDTH), lambda b, i: (b, i, 0)),
        out_shape=jax.ShapeDtypeStruct((bsz, t, RW_WIDTH), BF16),
        scratch_shapes=[pltpu.VMEM((n_pairs, LANES, LANES), F32),
                        pltpu.VMEM((tb, RW_WIDTH), F32)],
        compiler_params=_cparams(2),
        name="rwkv",
    )(rkv, lr, w0, a0, wwa, gup, k_k, k_a, r_k, lnx_g, lnx_b)


def _attn_kernel(q_ref, kp_ref, kc_ref, vp_ref, vc_ref, bias_ref, o_ref, lse_ref, *, nb, rb):
    step = pl.program_id(1)
    n_blk = rb // BLOCK
    scale = DIL_HEAD_DIM ** -0.5
    n_pairs = DIL_WIDTH // LANES
    krow = lax.broadcasted_iota(jnp.int32, (2 * BLOCK, LANES), 0) // BLOCK
    klane = lax.broadcasted_iota(jnp.int32, (2 * BLOCK, LANES), 1) // DIL_HEAD_DIM
    own = krow == klane
    zeros = jnp.zeros((2 * BLOCK, LANES), BF16)
    ones_bd = jnp.where(own, 1.0, 0.0).astype(BF16)
    lane_o = lax.broadcasted_iota(jnp.int32, (BLOCK, LANES), 1)

    def stack_bd(x):
        return jnp.where(own, jnp.concatenate([x, x], axis=0), zeros)

    def attend(blocks):
        units = [(rows, q, parts, p) for rows, q, parts in blocks for p in range(n_pairs)]
        lanes = [slice(p * LANES, (p + 1) * LANES) for p in range(n_pairs)]
        scores = []
        for _, q, parts, p in units:
            qs = q[:, lanes[p]] * scale
            row = []
            for k_blk, _, col0, pen in parts:
                s = lax.dot_general(qs, stack_bd(k_blk[:, lanes[p]]), (((1,), (1,)), ((), ())),
                                    preferred_element_type=F32)
                bias = jnp.concatenate([bias_ref[2 * p, :, col0:col0 + BLOCK],
                                        bias_ref[2 * p + 1, :, col0:col0 + BLOCK]], axis=1)
                s = s + bias
                row.append(s if pen is None else s + pen)
            scores.append(row)
        ms = []
        for u in range(len(units)):
            halves = []
            for hh in range(2):
                cols = slice(hh * BLOCK, (hh + 1) * BLOCK)
                m = jnp.max(scores[u][0][:, cols], axis=-1, keepdims=True)
                for s in scores[u][1:]:
                    m = jnp.maximum(m, jnp.max(s[:, cols], axis=-1, keepdims=True))
                halves.append(m)
            ms.append(halves)
        es = [[jnp.exp((s - jnp.concatenate([jnp.broadcast_to(ms[u][0], (BLOCK, BLOCK)),
                                             jnp.broadcast_to(ms[u][1], (BLOCK, BLOCK))], axis=1)).astype(BF16))
               for s in scores[u]] for u in range(len(units))]
        res = []
        for u, (_, _, parts, p) in enumerate(units):
            res.append(sum(jnp.dot(e, jnp.concatenate([stack_bd(part[1][:, lanes[p]]), ones_bd], axis=1),
                                   preferred_element_type=F32) for e, part in zip(es[u], parts)))
        for b, (rows, _, _) in enumerate(blocks):
            m_all = jnp.zeros((BLOCK, LANES), F32)
            den_all = jnp.ones((BLOCK, LANES), F32)
            for p in range(n_pairs):
                u = b * n_pairs + p
                num, den = res[u][:, :LANES], res[u][:, LANES:]
                o_ref[rows, lanes[p]] = (num / den).astype(o_ref.dtype)
                m_all = jnp.where(lane_o == 2 * p, ms[u][0], jnp.where(lane_o == 2 * p + 1, ms[u][1], m_all))
                den_all = jnp.where((lane_o == 2 * p), den,
                                    jnp.where(lane_o == 2 * p + 1, pltpu.roll(den, DIL_HEAD_DIM, 1), den_all))
            lse_ref[rows, :] = m_all + jnp.log(den_all)

    blocks = []
    for j in range(n_blk):
        rows = slice(j * BLOCK, (j + 1) * BLOCK)
        cur = (kc_ref[rows, :], vc_ref[rows, :], BLOCK, None)
        if n_blk % nb == 0:
            prev = "none" if j % nb == 0 else "inside"
        else:
            prev = "inside" if j > 0 else ("none" if nb == n_blk else "before")
        if prev == "none":
            parts = [cur]
        elif prev == "inside":
            before = slice((j - 1) * BLOCK, j * BLOCK)
            parts = [(kc_ref[before, :], vc_ref[before, :], 0, None), cur]
        else:
            has_prev = (step % (nb // n_blk)) != 0
            parts = [(kp_ref[...], vp_ref[...], 0, jnp.where(has_prev, 0.0, MASKED)), cur]
        blocks.append((rows, q_ref[rows, :], parts))
    attend(blocks)


def _qkv_proj_kernel(x_ref, mod_ref, w_ref, o0_ref, o1_ref, o2_ref, xs_ref, xs4_ref, *, tm):
    m = mod_ref[0]
    scale = 1.0 + m[1:2]
    shift = m[0:1]
    n_col = x_ref.shape[-1] // LANES
    gw = 3 * DIL_WIDTH
    step = DIL_PATTERNS[1][1]
    assert [dil for _, dil in DIL_PATTERNS] == [1, step, step * step]

    for c in range(n_col):
        xs_ref[c] = x_ref[0, :, c * LANES:(c + 1) * LANES]
    rows4 = tm // step
    x4 = [jnp.concatenate([xs_ref[c, pl.ds(a, rows4, stride=step), :] for a in range(step)], axis=0)
          for c in range(n_col)]
    for c in range(n_col):
        xs4_ref[c] = x4[c]
    rows16 = rows4 // step
    x16 = [jnp.concatenate([xs4_ref[c, pl.ds(a * rows4 + b, rows16, stride=step), :]
                            for a in range(step) for b in range(step)], axis=0) for c in range(n_col)]
    inputs = [x_ref[0], jnp.concatenate(x4, axis=1), jnp.concatenate(x16, axis=1)]
    residue_of_group = [[0], list(range(step)), [a + step * b for a in range(step) for b in range(step)]]

    for g, o_ref in enumerate((o0_ref, o1_ref, o2_ref)):
        u = (inputs[g] * scale + shift).astype(BF16)
        res = jnp.dot(u, w_ref[:, g * gw:(g + 1) * gw].astype(BF16), preferred_element_type=F32).astype(BF16)
        rows = tm // len(residue_of_group[g])
        for grp, r in enumerate(residue_of_group[g]):
            o_ref[0, r] = res[grp * rows:(grp + 1) * rows]


def _qkv_proj(x, mod, w, *, tm=1024):
    bsz, t, d = x.shape
    gw = 3 * DIL_WIDTH
    out_specs, out_shape = [], []
    for _, dil in DIL_PATTERNS:
        out_specs.append(pl.BlockSpec((1, dil, tm // dil, gw), lambda b, i: (b, 0, i, 0)))
        out_shape.append(jax.ShapeDtypeStruct((bsz, dil, t // dil, gw), BF16))
    return pl.pallas_call(
        functools.partial(_qkv_proj_kernel, tm=tm),
        grid=(bsz, t // tm),
        in_specs=[pl.BlockSpec((1, tm, d), lambda b, i: (b, i, 0)),
                  pl.BlockSpec((1, 6, d), lambda b, i: (b, 0, 0)),
                  pl.BlockSpec(w.shape, lambda b, i: (0, 0), pipeline_mode=pl.Buffered(1))],
        out_specs=out_specs,
        out_shape=out_shape,
        scratch_shapes=[pltpu.VMEM((d // LANES, tm, LANES), F32), pltpu.VMEM((d // LANES, tm, LANES), F32)],
        compiler_params=_cparams(2),
        name="qkv_proj",
    )(x, mod, w)


def _dilated_attention_group(qkv, bias, g, *, rb=2048):
    bsz, dil, length, width = qkv.shape
    t = dil * length
    nb = length // BLOCK
    n_blk = rb // BLOCK
    assert nb % n_blk == 0 or n_blk % nb == 0
    flat = qkv.reshape(bsz, t, width)

    def cur(j):
        return lambda b, s: (b, s, j)

    def prev(j):
        return lambda b, s: (b, jnp.maximum(s * n_blk - 1, 0), j)

    blk = (None, rb, DIL_WIDTH)
    pblk = (None, BLOCK, DIL_WIDTH)
    o, lse = pl.pallas_call(
        functools.partial(_attn_kernel, nb=nb, rb=rb),
        grid=(bsz, t // rb),
        in_specs=[pl.BlockSpec(blk, cur(0)),
                  pl.BlockSpec(pblk, prev(1)), pl.BlockSpec(blk, cur(1)),
                  pl.BlockSpec(pblk, prev(2)), pl.BlockSpec(blk, cur(2)),
                  pl.BlockSpec((DIL_HEADS, BLOCK, 2 * BLOCK), lambda b, s: (0, 0, 0))],
        out_specs=[pl.BlockSpec(blk, lambda b, s: (b, s, 0)),
                   pl.BlockSpec((None, rb, LANES), lambda b, s: (b, s, 0))],
        out_shape=[jax.ShapeDtypeStruct((bsz, t, DIL_WIDTH), BF16),
                   jax.ShapeDtypeStruct((bsz, t, LANES), F32)],
        compiler_params=_cparams(2),
        name=f"dilated_attn_g{g}",
    )(flat, flat, flat, flat, flat, bias)
    return o.reshape(bsz, dil, length, DIL_WIDTH), lse.reshape(bsz, dil, length, LANES)


def _t5_bucket(dist):
    exact = N_BUCKETS // 2
    logd = jnp.log(jnp.maximum(dist, 1).astype(F32) / exact) / math.log(MAX_DISTANCE / exact)
    large = jnp.minimum(exact + (logd * (N_BUCKETS - exact)).astype(jnp.int32), N_BUCKETS - 1)
    return jnp.where(dist < exact, dist, large)


def _bias_tables(rel_bias):
    rel = jnp.arange(BLOCK + 1)
    tables = []
    for g, (window, dil) in enumerate(DIL_PATTERNS):
        span = window // dil
        bucket = _t5_bucket(jnp.clip(rel, 0, span) * dil)
        vec = rel_bias[bucket][:, g * DIL_HEADS:(g + 1) * DIL_HEADS].T.astype(F32)
        ext = jnp.concatenate([vec[:, ::-1], jnp.zeros((DIL_HEADS, BLOCK), F32)], axis=1)
        flat = jnp.tile(ext, (1, BLOCK))[:, :BLOCK * 2 * BLOCK]
        tables.append(flat.reshape(DIL_HEADS, BLOCK, 2 * BLOCK))
    qi = np.arange(BLOCK)[:, None]
    ki = np.arange(2 * BLOCK)[None, :]
    band = (ki >= qi) & (ki <= qi + BLOCK)
    return jnp.where(band, jnp.stack(tables), MASKED)


def _tail(mix_fn, wout_ref, x_ref, mod_ref, lng_ref, lnb_ref, w1_ref, w2_ref, o_ref, *, tm, n_sub, ff_chunk):
    m = mod_ref[0]
    sub = tm // n_sub
    tiles = [slice(i * sub, (i + 1) * sub) for i in range(n_sub)]
    x1s, us = [], []
    for rows in tiles:
        y = None
        for piece, k0 in mix_fn(rows):
            part = jnp.dot(piece, wout_ref[k0:k0 + piece.shape[-1], :], preferred_element_type=F32)
            y = part if y is None else y + part
        x1 = _layer_norm_rows(DEEPNORM_ALPHA * x_ref[0, rows, :] + (1.0 + m[2:3]) * y, lng_ref[0:1], lnb_ref[0:1])
        x1s.append(x1)
        us.append((x1 * (1.0 + m[4:5]) + m[3:4]).astype(BF16))
    accs = [None] * n_sub
    for c in range(D_FF // ff_chunk):
        cols = slice(c * ff_chunk, (c + 1) * ff_chunk)
        w1c = w1_ref[:, cols].astype(BF16)
        w2c = w2_ref[cols, :].astype(BF16)
        for i in range(n_sub):
            h = jnp.maximum(jnp.dot(us[i], w1c, preferred_element_type=F32), 0.0)
            part = jnp.dot((h * h).astype(BF16), w2c, preferred_element_type=F32)
            accs[i] = part if accs[i] is None else accs[i] + part
    for i, rows in enumerate(tiles):
        z = DEEPNORM_ALPHA * x1s[i] + (1.0 + m[5:6]) * accs[i]
        o_ref[0, rows, :] = _layer_norm_rows(z, lng_ref[1:2], lnb_ref[1:2])


def _even_tail_kernel(ya_ref, yb_ref, wout_ref, x_ref, mod_ref, lng_ref, lnb_ref, w1_ref, w2_ref, o_ref, **kw):
    def mix(rows):
        return [(ya_ref[0, rows, :], 0), (yb_ref[0, rows, :], ya_ref.shape[-1])]

    _tail(mix, wout_ref, x_ref, mod_ref, lng_ref, lnb_ref, w1_ref, w2_ref, o_ref, **kw)


def _odd_tail_kernel(o0_ref, o1_ref, o2_ref, l0_ref, l1_ref, l2_ref, wout_ref, x_ref, mod_ref, lng_ref, lnb_ref,
                     w1_ref, w2_ref, o_ref, o_nat, l_nat, mix_s, *, tm, **kw):
    def natural(src_ref, scr, dil):
        if dil == 1:
            return src_ref[0, 0].astype(F32)
        n_col = src_ref.shape[-1] // LANES
        for c in range(n_col):
            for r in range(dil):
                scr[c, pl.ds(r, tm // dil, stride=dil), :] = src_ref[0, r, :, c * LANES:(c + 1) * LANES].astype(F32)
        return jnp.concatenate([scr[c] for c in range(n_col)], axis=1)

    dils = [dil for _, dil in DIL_PATTERNS]
    lses = [natural(ref, l_nat, dil) for ref, dil in zip((l0_ref, l1_ref, l2_ref), dils)]
    mx = jnp.maximum(jnp.maximum(lses[0], lses[1]), lses[2])
    es = [jnp.exp(l - mx) for l in lses]
    inv = 1.0 / (es[0] + es[1] + es[2])
    hr = lax.broadcasted_iota(jnp.int32, (LANES, DIL_WIDTH), 0)
    hc = lax.broadcasted_iota(jnp.int32, (LANES, DIL_WIDTH), 1) // DIL_HEAD_DIM
    expand = jnp.where(hr == hc, 1.0, 0.0).astype(BF16)

    def widen(wt):
        return jnp.dot(wt.astype(BF16), expand, preferred_element_type=F32)

    merged = None
    for e, ref, dil in zip(es, (o0_ref, o1_ref, o2_ref), dils):
        term = widen(e * inv) * natural(ref, o_nat, dil)
        merged = term if merged is None else merged + term
    mix_s[...] = merged.astype(BF16)
    _tail(lambda rows: [(mix_s[rows, :], 0)], wout_ref, x_ref, mod_ref, lng_ref, lnb_ref, w1_ref, w2_ref, o_ref,
          tm=tm, **kw)


def _layer_tail(mix, w_out, x, mod, ln_g, ln_b, w1, w2, layer, *, tm=512, n_sub=2, ff_chunk=1024):
    bsz, t, d = x.shape

    def const(shape):
        return pl.BlockSpec(shape, lambda b, i: (0,) * len(shape), pipeline_mode=pl.Buffered(1))

    def of_layer(shape):
        return pl.BlockSpec((None,) + shape, lambda b, i: (layer,) + (0,) * len(shape), pipeline_mode=pl.Buffered(1))

    tail_specs = [const(w_out.shape),
                  pl.BlockSpec((1, tm, d), lambda b, i: (b, i, 0)),
                  pl.BlockSpec((1, 6, d), lambda b, i: (b, 0, 0)),
                  of_layer((2, d)), of_layer((2, d)), of_layer((d, D_FF)), of_layer((D_FF, d))]
    tail_args = (w_out, x, mod, ln_g, ln_b, w1, w2)
    kw = dict(tm=tm, n_sub=n_sub, ff_chunk=ff_chunk)
    if isinstance(mix, tuple):
        outs, lses = mix
        mix_specs = ([pl.BlockSpec((1, dil, tm // dil, DIL_WIDTH), lambda b, i: (b, 0, i, 0)) for _, dil in DIL_PATTERNS]
                     + [pl.BlockSpec((1, dil, tm // dil, LANES), lambda b, i: (b, 0, i, 0)) for _, dil in DIL_PATTERNS])
        mix_args = (*outs, *lses)
        body = functools.partial(_odd_tail_kernel, **kw)
        scratch = [pltpu.VMEM((DIL_WIDTH // LANES, tm, LANES), F32), pltpu.VMEM((1, tm, LANES), F32),
                   pltpu.VMEM((tm, DIL_WIDTH), BF16)]
    else:
        mix_specs = [pl.BlockSpec((1, tm, piece.shape[-1]), lambda b, i: (b, i, 0)) for piece in mix]
        mix_args = tuple(mix)
        body = functools.partial(_even_tail_kernel, **kw)
        scratch = []
    return pl.pallas_call(
        body,
        grid=(bsz, t // tm),
        in_specs=mix_specs + tail_specs,
        out_specs=pl.BlockSpec((1, tm, d), lambda b, i: (b, i, 0)),
        out_shape=jax.ShapeDtypeStruct((bsz, t, d), F32),
        scratch_shapes=scratch,
        compiler_params=_cparams(2, TAIL_VMEM_LIMIT),
        name="layer_tail",
    )(*mix_args, *tail_args)


def _prep_even_weights(mu, w_up, a_up, g_up):
    pad = LR_PAD - (RW_PROJ - 3 * RW_WIDTH)
    mu_pad = jnp.concatenate([mu, jnp.zeros((pad,), mu.dtype)]).reshape(1, 3 * RW_WIDTH + LR_PAD)
    zeros = jnp.zeros((RW_DECAY_RANK, RW_WIDTH), w_up.dtype)
    wwa = jnp.concatenate([jnp.concatenate([w_up, zeros], axis=1),
                           jnp.concatenate([zeros, a_up], axis=1)], axis=0).astype(BF16)
    gup = jnp.concatenate([g_up, jnp.zeros((pad, RW_WIDTH), g_up.dtype)], axis=0).astype(BF16)
    return mu_pad, wwa, gup


def kernel(x, c, ada_w, ada_b, ln_g, ln_b, ab_w_in, rw_mu, rw_w0, rw_w_up, rw_a0, rw_a_up, rw_g_up,
           rw_k_k, rw_k_a, rw_r_k, rw_lnx_g, rw_lnx_b, sc_conv_w, ab_w_out, dil_w_qkv, dil_w_out,
           rel_bias, mlp_w1, mlp_w2):
    bsz, t, d = x.shape
    mods = _adaln(c, ada_w, ada_b).reshape(DEPTH, bsz, 6, d)
    bias = None
    row = lambda a: a.reshape(1, -1)
    w1_all, w2_all = mlp_w1, mlp_w2
    for i in range(DEPTH):
        mod = mods[i]
        j = i // 2
        if i % 2 == 0:
            mu_pad, wwa, gup = _prep_even_weights(rw_mu[j], rw_w_up[j], rw_a_up[j], rw_g_up[j])
            rkv, lr, yb = _ab_proj(x, mod, ab_w_in[j].T, mu_pad, sc_conv_w[j])
            ya = _rwkv(rkv, lr, row(rw_w0[j]), row(rw_a0[j]), wwa, gup,
                       row(rw_k_k[j]), row(rw_k_a[j]), row(rw_r_k[j]),
                       row(rw_lnx_g[j]), row(rw_lnx_b[j]))
            mix, w_out = [ya, yb], ab_w_out[j]
        else:
            if bias is None:
                bias = _bias_tables(rel_bias)
            qkvs = _qkv_proj(x, mod, dil_w_qkv[j])
            outs, lses = [], []
            for g in range(N_GROUPS):
                o, lse = _dilated_attention_group(qkvs[g], bias[g], g)
                outs.append(o)
                lses.append(lse)
            mix, w_out = (outs, lses), dil_w_out[j]
        x = _layer_tail(mix, w_out.astype(BF16), x, mod, ln_g, ln_b, w1_all, w2_all, i)
    return x
```

```python
import functools
import math

import jax
import jax.numpy as jnp
import numpy as np
from jax import lax
from jax.experimental import pallas as pl
from jax.experimental.pallas import tpu as pltpu

F32 = jnp.float32
BF16 = jnp.bfloat16

D_MODEL = 1024
DEPTH = 2
RW_HEADS = 8
RW_HEAD_DIM = 64
RW_WIDTH = 512
RW_DECAY_RANK = 64
RW_ICLR_RANK = 64
RW_GATE_RANK = 160
RW_GN_EPS = 64e-5
RW_PROJ = 3 * RW_WIDTH + RW_DECAY_RANK + RW_ICLR_RANK + RW_GATE_RANK
SC_WIDTH = 512
SC_CONV = 3
DIL_PATTERNS = ((128, 1), (512, 4), (2048, 16))
N_GROUPS = 3
DIL_HEADS = 8
DIL_HEAD_DIM = 64
DIL_WIDTH = 512
BLOCK = 128
N_BUCKETS = 32
MAX_DISTANCE = 2048
D_FF = 4 * D_MODEL
DEEPNORM_ALPHA = (2 * DEPTH) ** 0.25
LN_EPS = 1e-5

LANES = 128
SUBLANES = 8
MXU_DIM = 256
CHUNK = 64
INV_BLOCK = 16
LR_PAD = 384
VMEM_LIMIT = 56 * 1024 * 1024
TAIL_VMEM_LIMIT = 60 * 1024 * 1024
MASKED = -1e30


def _cparams(n_axes, vmem_limit=VMEM_LIMIT):
    return pltpu.CompilerParams(dimension_semantics=("arbitrary",) * n_axes,
                                vmem_limit_bytes=vmem_limit)


def _sigmoid(x):
    return 0.5 * jnp.tanh(0.5 * x) + 0.5


def _dot(a, b):
    return jnp.dot(a.astype(BF16), b.astype(BF16), preferred_element_type=F32)


def _layer_norm_rows(z, g, b):
    mu = jnp.mean(z, axis=-1, keepdims=True)
    zc = z - mu
    var = jnp.mean(zc * zc, axis=-1, keepdims=True)
    return zc * lax.rsqrt(var + LN_EPS) * g + b


ADALN_BUFFERS = 3


def _adaln_kernel(c_ref, w_hbm, b_ref, o_ref, buf, sem, *, tn):
    depth, _, n = w_hbm.shape
    blocks = [(i, j * tn) for i in range(depth) for j in range(n // tn)]

    def copy(k):
        i, c0 = blocks[k]
        slot = k % ADALN_BUFFERS
        return pltpu.make_async_copy(w_hbm.at[i, :, pl.ds(c0, tn)], buf.at[slot], sem.at[slot])

    for k in range(min(ADALN_BUFFERS, len(blocks))):
        copy(k).start()
    c = c_ref[...]
    cond = (c * _sigmoid(c)).astype(BF16)
    for k, (i, c0) in enumerate(blocks):
        copy(k).wait()
        w = buf[k % ADALN_BUFFERS].astype(BF16)
        o_ref[i, :, c0:c0 + tn] = jnp.dot(cond, w, preferred_element_type=F32) + b_ref[i, :, c0:c0 + tn]
        if k + ADALN_BUFFERS < len(blocks):
            copy(k + ADALN_BUFFERS).start()


def _adaln(c, ada_w, ada_b, *, tn=1536):
    depth, d, n = ada_w.shape
    bsz = c.shape[0]
    vmem = pl.BlockSpec(memory_space=pltpu.VMEM)
    return pl.pallas_call(
        functools.partial(_adaln_kernel, tn=tn),
        in_specs=[vmem, pl.BlockSpec(memory_space=pl.ANY), vmem],
        out_specs=vmem,
        out_shape=jax.ShapeDtypeStruct((depth, bsz, n), F32),
        scratch_shapes=[pltpu.VMEM((ADALN_BUFFERS, d, tn), F32), pltpu.SemaphoreType.DMA((ADALN_BUFFERS,))],
        compiler_params=pltpu.CompilerParams(vmem_limit_bytes=VMEM_LIMIT),
        name="adaln",
    )(c, ada_w, ada_b.reshape(depth, 1, n))


def _shift_rows(x, carry, k):
    rolled = pltpu.roll(x, k, 0)
    head = rolled[:SUBLANES]
    row = lax.broadcasted_iota(jnp.int32, head.shape, 0)
    for i in range(k):
        head = jnp.where(row == i, carry[SUBLANES - k + i:SUBLANES - k + i + 1], head)
    return jnp.concatenate([head, rolled[SUBLANES:]], axis=0)


def _ab_proj_kernel(x_ref, mod_ref, wt_ref, mu_ref, convw_ref, rkv_ref, lr_ref, yb_ref, carry_p, carry_z, *, tm):
    @pl.when(pl.program_id(1) == 0)
    def _():
        carry_p[...] = jnp.zeros_like(carry_p)
        carry_z[...] = jnp.zeros_like(carry_z)

    m = mod_ref[0]
    u = (x_ref[0] * (1.0 + m[1:2]) + m[0:1]).astype(BF16)

    def proj(c0, width):
        return lax.dot_general(u, wt_ref[c0:c0 + width, :].astype(BF16), (((1,), (1,)), ((), ())),
                               preferred_element_type=F32)

    def mixed(c0, width):
        p = proj(c0, width)
        prev = _shift_rows(p, carry_p[:, c0:c0 + width], 1)
        carry_p[:, c0:c0 + width] = p[tm - SUBLANES:]
        return (p + mu_ref[:, c0:c0 + width] * (prev - p)).astype(BF16)

    w3 = 3 * RW_WIDTH
    for c in range(3):
        rkv_ref[0, :, c * RW_WIDTH:(c + 1) * RW_WIDTH] = mixed(c * RW_WIDTH, RW_WIDTH)
    lr_ref[0] = mixed(w3, LR_PAD)

    h = proj(RW_PROJ, SC_WIDTH)
    z = proj(RW_PROJ + 2 * SC_WIDTH, SC_WIDTH) * h
    zc = carry_z[...]
    cw = convw_ref[...]
    conv = cw[2:3] * z + cw[1:2] * _shift_rows(z, zc, 1) + cw[0:1] * _shift_rows(z, zc, 2)
    carry_z[...] = z[tm - SUBLANES:]
    yb_ref[0] = (proj(RW_PROJ + SC_WIDTH, SC_WIDTH) * conv).astype(BF16)


def _ab_proj(x, mod, w_t, mu, conv_w, *, tm=1024):
    bsz, t, d = x.shape
    w3 = 3 * RW_WIDTH

    def const(shape):
        return pl.BlockSpec(shape, lambda b, i: (0,) * len(shape), pipeline_mode=pl.Buffered(1))

    def rows(width):
        return pl.BlockSpec((1, tm, width), lambda b, i: (b, i, 0))

    return pl.pallas_call(
        functools.partial(_ab_proj_kernel, tm=tm),
        grid=(bsz, t // tm),
        in_specs=[rows(d), pl.BlockSpec((1, 6, d), lambda b, i: (b, 0, 0)),
                  const(w_t.shape), const((1, w3 + LR_PAD)), const((SC_CONV, SC_WIDTH))],
        out_specs=[rows(w3), rows(LR_PAD), rows(SC_WIDTH)],
        out_shape=[jax.ShapeDtypeStruct((bsz, t, w3), BF16), jax.ShapeDtypeStruct((bsz, t, LR_PAD), BF16),
                   jax.ShapeDtypeStruct((bsz, t, SC_WIDTH), BF16)],
        scratch_shapes=[pltpu.VMEM((SUBLANES, w3 + LR_PAD), F32), pltpu.VMEM((SUBLANES, SC_WIDTH), F32)],
        compiler_params=_cparams(2),
        name="ab_proj",
    )(x, mod, w_t, mu, conv_w)


def _rwkv_chunk_terms(a_t, r_t, b_t, k_t, b_e, k_e, v, p_end, hooks=()):
    n = len(a_t)
    idx = range(n)
    hooks = list(hooks)

    def hook():
        if hooks:
            hooks.pop(0)()
    row = lax.broadcasted_iota(jnp.int32, (CHUNK, LANES), 0)
    col = lax.broadcasted_iota(jnp.int32, (CHUNK, LANES), 1)
    s_idx = col % CHUNK
    strict = s_idx < row
    incl = s_idx <= row
    near = strict & ((s_idx // INV_BLOCK) == (row // INV_BLOCK))
    far = strict & ((s_idx // INV_BLOCK) != (row // INV_BLOCK))
    eye = jnp.where(s_idx == row, 1.0, 0.0)
    r2 = lax.broadcasted_iota(jnp.int32, (LANES, 2 * LANES), 0)
    c2 = lax.broadcasted_iota(jnp.int32, (LANES, 2 * LANES), 1)
    bdmask2 = (r2 // RW_HEAD_DIM) == ((c2 % LANES) // RW_HEAD_DIM)
    eye_bd = jnp.where(lax.broadcasted_iota(jnp.int32, (LANES, LANES), 0)
                       == lax.broadcasted_iota(jnp.int32, (LANES, LANES), 1), 1.0, 0.0)
    zero16 = jnp.zeros((CHUNK, LANES), BF16)
    lane1 = lax.broadcasted_iota(jnp.int32, (1, LANES), 1)
    m0_16 = jnp.where(lane1 < RW_HEAD_DIM, 1.0, 0.0).astype(BF16)
    m1_16 = jnp.where(lane1 < RW_HEAD_DIM, 0.0, 1.0).astype(BF16)

    def cast(xs):
        return [x.astype(BF16) for x in xs]

    def bd(q16):
        return jnp.concatenate([q16 * m0_16, q16 * m1_16], axis=0)

    def mm(p16, q16):
        return jnp.dot(p16, q16, preferred_element_type=F32)

    def mm_nt(p16, q16):
        return lax.dot_general(p16, q16, (((1,), (1,)), ((), ())), preferred_element_type=F32)

    a16, r16, v16 = cast(a_t), cast(r_t), cast(v)
    lhs = [jnp.concatenate([a16[i], r16[i]], axis=0) for i in idx]
    b16, k16 = cast(b_t), cast(k_t)
    bdv = [bd(x) for x in v16]
    g = [mm_nt(lhs[i], jnp.concatenate([bd(b16[i]), bd(k16[i])], axis=0)) for i in idx]
    x1 = [jnp.where(near, g[i][:CHUNK, :LANES], 0.0) for i in idx]
    x1_16 = cast(x1)
    l_e16 = [jnp.where(far, g[i][:CHUNK, :LANES], 0.0).astype(BF16) for i in idx]
    a_k16 = [jnp.concatenate([jnp.where(strict, g[i][:CHUNK, LANES:], 0.0),
                              jnp.where(incl, g[i][CHUNK:, LANES:], 0.0)], axis=0).astype(BF16) for i in idx]
    a_rb16 = [jnp.where(incl, g[i][CHUNK:, :LANES], 0.0).astype(BF16) for i in idx]

    def both(p16, q16):
        return jnp.concatenate([bd(p16), bd(q16)], axis=1)

    x2_16 = cast([mm(x1_16[i], bd(x1_16[i])) for i in idx])
    hook()
    acc = [eye + x1[i] for i in idx]
    sq = [mm(x2_16[i], both(x2_16[i], acc[i].astype(BF16))) for i in idx]
    x4_16 = [sq[i][:, :LANES].astype(BF16) for i in idx]
    acc = [acc[i] + sq[i][:, LANES:] for i in idx]
    sq = [mm(x4_16[i], both(x4_16[i], acc[i].astype(BF16))) for i in idx]
    hook()
    x8_16 = [sq[i][:, :LANES].astype(BF16) for i in idx]
    acc = [acc[i] + sq[i][:, LANES:] for i in idx]
    dinv = [acc[i] + mm(x8_16[i], bd(acc[i].astype(BF16))) for i in idx]
    dinv16 = cast(dinv)
    hook()
    f16 = cast([mm(dinv16[i], bd(l_e16[i])) for i in idx])
    sq = [mm(f16[i], both(f16[i], dinv16[i])) for i in idx]
    f2_16 = [sq[i][:, :LANES].astype(BF16) for i in idx]
    hook()
    hmat = [dinv[i] + sq[i][:, LANES:] for i in idx]
    tinv16 = cast([hmat[i] + mm(f2_16[i], bd(hmat[i].astype(BF16))) for i in idx])

    kv = [mm(a_k16[i], bdv[i]) for i in idx]
    wv16 = [kv[i][:CHUNK].astype(BF16) for i in idx]
    hook()
    aw = [mm(tinv16[i], both(a16[i], wv16[i])) for i in idx]
    aw16 = cast(aw)
    hook()
    qy = [mm(a_rb16[i], both(aw16[i][:, :LANES], aw16[i][:, LANES:]))
          + jnp.concatenate([r_t[i], kv[i][CHUNK:]], axis=1) for i in idx]
    be_ke_t = [jnp.concatenate([b_e[i], k_e[i]], axis=0).T.astype(BF16) for i in idx]
    rhs = [jnp.concatenate([aw16[i], jnp.concatenate([zero16, v16[i]], axis=1)], axis=0) for i in idx]
    mn = [jnp.where(bdmask2, mm(be_ke_t[i], rhs[i]), 0.0) for i in idx]
    qhat = [x[:, :LANES] for x in qy]
    yhat = [x[:, LANES:] for x in qy]
    mmat = [eye_bd * p_end[i] + mn[i][:, :LANES] for i in idx]
    nmat = [x[:, LANES:] for x in mn]
    while hooks:
        hook()
    return qhat, yhat, mmat, nmat


def _rwkv_kernel(rkv_ref, lr_ref, w0_ref, a0_ref, wwa_ref, gup_ref,
                 kk_ref, ka_ref, rk_ref, lnxg_ref, lnxb_ref, o_ref, state, yout_s, *, tb):
    n_chunks = tb // CHUNK
    n_pairs = RW_WIDTH // LANES

    @pl.when(pl.program_id(1) == 0)
    def _():
        state[...] = jnp.zeros_like(state)

    r = rkv_ref[0, :, :RW_WIDTH].astype(F32)
    k = rkv_ref[0, :, RW_WIDTH:2 * RW_WIDTH].astype(F32)
    v = rkv_ref[0, :, 2 * RW_WIDTH:].astype(F32)
    lr_m = lr_ref[0].astype(F32)

    wa = lr_m[:, :LANES]
    lane = lax.broadcasted_iota(jnp.int32, wa.shape, 1)
    wa = jnp.where(lane < RW_DECAY_RANK, jnp.tanh(wa), wa)
    wa_up = _dot(wa, wwa_ref[...])
    ld = -math.exp(-0.5) * _sigmoid(w0_ref[...] + wa_up[:, :RW_WIDTH])
    iclr = _sigmoid(a0_ref[...] + wa_up[:, RW_WIDTH:])
    gate = _dot(_sigmoid(lr_m[:, LANES:]), gup_ref[...])

    hr = lax.broadcasted_iota(jnp.int32, (MXU_DIM, MXU_DIM), 0) // RW_HEAD_DIM
    hc = lax.broadcasted_iota(jnp.int32, (MXU_DIM, MXU_DIM), 1) // RW_HEAD_DIM
    seg = jnp.where(hr == hc, 1.0 / RW_HEAD_DIM, 0.0).astype(BF16)

    def seg_mean(x):
        x16 = x.astype(BF16)
        return jnp.concatenate([jnp.dot(x16[:, c:c + MXU_DIM], seg, preferred_element_type=F32)
                                for c in range(0, RW_WIDTH, MXU_DIM)], axis=1)

    kk = k * kk_ref[...]
    ss = seg_mean(kk * kk) * RW_HEAD_DIM
    kk = kk * lax.rsqrt(jnp.maximum(ss, 1e-24))
    kh = k * (1.0 + (iclr - 1.0) * ka_ref[...])

    tr = lax.broadcasted_iota(jnp.int32, (MXU_DIM, MXU_DIM), 0)
    tc = lax.broadcasted_iota(jnp.int32, (MXU_DIM, MXU_DIM), 1)
    tri = jnp.where((tc <= tr) & (tc // CHUNK == tr // CHUNK), 1.0, 0.0).astype(BF16)
    ld_hi = ld.astype(BF16)
    ld_lo = (ld - ld_hi.astype(F32)).astype(BF16)
    lp = jnp.concatenate(
        [jnp.dot(tri, ld_hi[t0:t0 + MXU_DIM], preferred_element_type=F32)
         + jnp.dot(tri, ld_lo[t0:t0 + MXU_DIM], preferred_element_type=F32)
         for t0 in range(0, tb, MXU_DIM)], axis=0)

    a_t, r_t, b_t, k_t, b_e, k_e, vs, p_end = [], [], [], [], [], [], [], []
    b = kk * iclr
    for c in range(n_chunks):
        rs = slice(c * CHUNK, (c + 1) * CHUNK)
        lp_c = lp[rs]
        lp_last = lp_c[CHUNK - 1:CHUNK]
        e_in = jnp.exp(lp_c)
        e_inv = jnp.exp(-lp_c)
        pe_c = jnp.exp(lp_last)
        a_c = -kk[rs] * jnp.exp(lp_c - ld[rs])
        r_c = r[rs] * e_in
        b_c = b[rs] * e_inv
        k_c = kh[rs] * e_inv
        be_c = b_c * pe_c
        ke_c = k_c * pe_c
        for p in range(n_pairs):
            cs = slice(p * LANES, (p + 1) * LANES)
            a_t.append(a_c[:, cs])
            r_t.append(r_c[:, cs])
            b_t.append(b_c[:, cs])
            k_t.append(k_c[:, cs])
            b_e.append(be_c[:, cs])
            k_e.append(ke_c[:, cs])
            vs.append(v[rs, cs])
            p_end.append(pe_c[:, cs])

    bonus = seg_mean(r * kh * rk_ref[...]) * RW_HEAD_DIM * v
    s = [state[p] for p in range(n_pairs)]

    def serial_step(c, terms, c0):
        qhat, yhat, mmat, nmat = terms
        base = (c - c0) * n_pairs
        ym = [_dot(jnp.concatenate([qhat[base + p], mmat[base + p]], axis=0), s[p]) for p in range(n_pairs)]
        for p in range(n_pairs):
            yout_s[c * CHUNK:(c + 1) * CHUNK, p * LANES:(p + 1) * LANES] = ym[p][:CHUNK] + yhat[base + p]
            s[p] = ym[p][CHUNK:] + nmat[base + p]

    def finish(rows):
        y = yout_s[rows, :]
        yc = y - seg_mean(y)
        var = seg_mean(yc * yc)
        y = yc * lax.rsqrt(var + RW_GN_EPS) * lnxg_ref[...] + lnxb_ref[...]
        o_ref[0, rows, :] = ((y + bonus[rows]) * gate[rows]).astype(o_ref.dtype)

    half = n_chunks // 2
    cut = half * n_pairs
    slabs = (a_t, r_t, b_t, k_t, b_e, k_e, vs, p_end)
    first = _rwkv_chunk_terms(*[x[:cut] for x in slabs])
    hooks = [functools.partial(serial_step, c, first, 0) for c in range(half)]
    hooks.append(functools.partial(finish, slice(0, half * CHUNK)))
    second = _rwkv_chunk_terms(*[x[cut:] for x in slabs], hooks=hooks)
    for c in range(half, n_chunks):
        serial_step(c, second, half)
    for p in range(n_pairs):
        state[p] = s[p]
    finish(slice(half * CHUNK, tb))


def _rwkv(rkv, lr, w0, a0, wwa, gup, k_k, k_a, r_k, lnx_g, lnx_b, *, tb=1024):
    bsz, t, w3 = rkv.shape
    n_pairs = RW_WIDTH // LANES

    def full(shape):
        return pl.BlockSpec(shape, lambda b, i: (0,) * len(shape))

    return pl.pallas_call(
        functools.partial(_rwkv_kernel, tb=tb),
        grid=(bsz, t // tb),
        in_specs=[pl.BlockSpec((1, tb, w3), lambda b, i: (b, i, 0)),
                  pl.BlockSpec((1, tb, LR_PAD), lambda b, i: (b, i, 0)),
                  full((1, RW_WIDTH)), full((1, RW_WIDTH)),
                  full((LANES, 2 * RW_WIDTH)), full((LR_PAD - LANES, RW_WIDTH)),
                  full((1, RW_WIDTH)), full((1, RW_WIDTH)), full((1, RW_WIDTH)),
                  full((1, RW_WIDTH)), full((1, RW_WIDTH))],
        out_specs=pl.BlockSpec((1, tb, RW_WIDTH), lambda b, i: (b, i, 0)),
        out_shape=jax.ShapeDtypeStruct((bsz, t, RW_WIDTH), BF16),
        scratch_shapes=[pltpu.VMEM((n_pairs, LANES, LANES), F32),
                        pltpu.VMEM((tb, RW_WIDTH), F32)],
        compiler_params=_cparams(2),
        name="rwkv",
    )(rkv, lr, w0, a0, wwa, gup, k_k, k_a, r_k, lnx_g, lnx_b)


def _attn_kernel(q_ref, kp_ref, kc_ref, vp_ref, vc_ref, bias_ref, o_ref, lse_ref, *, nb, rb):
    step = pl.program_id(1)
    n_blk = rb // BLOCK
    scale = DIL_HEAD_DIM ** -0.5
    n_pairs = DIL_WIDTH // LANES
    krow = lax.broadcasted_iota(jnp.int32, (2 * BLOCK, LANES), 0) // BLOCK
    klane = lax.broadcasted_iota(jnp.int32, (2 * BLOCK, LANES), 1) // DIL_HEAD_DIM
    own = krow == klane
    zeros = jnp.zeros((2 * BLOCK, LANES), BF16)
    ones_bd = jnp.where(own, 1.0, 0.0).astype(BF16)
    lane_o = lax.broadcasted_iota(jnp.int32, (BLOCK, LANES), 1)

    def stack_bd(x):
        return jnp.where(own, jnp.concatenate([x, x], axis=0), zeros)

    def attend(blocks):
        units = [(rows, q, parts, p) for rows, q, parts in blocks for p in range(n_pairs)]
        lanes = [slice(p * LANES, (p + 1) * LANES) for p in range(n_pairs)]
        scores = []
        for _, q, parts, p in units:
            qs = q[:, lanes[p]] * scale
            row = []
            for k_blk, _, col0, pen in parts:
                s = lax.dot_general(qs, stack_bd(k_blk[:, lanes[p]]), (((1,), (1,)), ((), ())),
                                    preferred_element_type=F32)
                bias = jnp.concatenate([bias_ref[2 * p, :, col0:col0 + BLOCK],
                                        bias_ref[2 * p + 1, :, col0:col0 + BLOCK]], axis=1)
                s = s + bias
                row.append(s if pen is None else s + pen)
            scores.append(row)
        ms = []
        for u in range(len(units)):
            halves = []
            for hh in range(2):
                cols = slice(hh * BLOCK, (hh + 1) * BLOCK)
                m = jnp.max(scores[u][0][:, cols], axis=-1, keepdims=True)
                for s in scores[u][1:]:
                    m = jnp.maximum(m, jnp.max(s[:, cols], axis=-1, keepdims=True))
                halves.append(m)
            ms.append(halves)
        es = [[jnp.exp((s - jnp.concatenate([jnp.broadcast_to(ms[u][0], (BLOCK, BLOCK)),
                                             jnp.broadcast_to(ms[u][1], (BLOCK, BLOCK))], axis=1)).astype(BF16))
               for s in scores[u]] for u in range(len(units))]
        res = []
        for u, (_, _, parts, p) in enumerate(units):
            res.append(sum(jnp.dot(e, jnp.concatenate([stack_bd(part[1][:, lanes[p]]), ones_bd], axis=1),
                                   preferred_element_type=F32) for e, part in zip(es[u], parts)))
        for b, (rows, _, _) in enumerate(blocks):
            m_all = jnp.zeros((BLOCK, LANES), F32)
            den_all = jnp.ones((BLOCK, LANES), F32)
            for p in range(n_pairs):
                u = b * n_pairs + p
                num, den = res[u][:, :LANES], res[u][:, LANES:]
                o_ref[rows, lanes[p]] = (num / den).astype(o_ref.dtype)
                m_all = jnp.where(lane_o == 2 * p, ms[u][0], jnp.where(lane_o == 2 * p + 1, ms[u][1], m_all))
                den_all = jnp.where((lane_o == 2 * p), den,
                                    jnp.where(lane_o == 2 * p + 1, pltpu.roll(den, DIL_HEAD_DIM, 1), den_all))
            lse_ref[rows, :] = m_all + jnp.log(den_all)

    blocks = []
    for j in range(n_blk):
        rows = slice(j * BLOCK, (j + 1) * BLOCK)
        cur = (kc_ref[rows, :], vc_ref[rows, :], BLOCK, None)
        if n_blk % nb == 0:
            prev = "none" if j % nb == 0 else "inside"
        else:
            prev = "inside" if j > 0 else ("none" if nb == n_blk else "before")
        if prev == "none":
            parts = [cur]
        elif prev == "inside":
            before = slice((j - 1) * BLOCK, j * BLOCK)
            parts = [(kc_ref[before, :], vc_ref[before, :], 0, None), cur]
        else:
            has_prev = (step % (nb // n_blk)) != 0
            parts = [(kp_ref[...], vp_ref[...], 0, jnp.where(has_prev, 0.0, MASKED)), cur]
        blocks.append((rows, q_ref[rows, :], parts))
    attend(blocks)


def _qkv_proj_kernel(x_ref, mod_ref, w_ref, o0_ref, o1_ref, o2_ref, xs_ref, xs4_ref, *, tm):
    m = mod_ref[0]
    scale = 1.0 + m[1:2]
    shift = m[0:1]
    n_col = x_ref.shape[-1] // LANES
    gw = 3 * DIL_WIDTH
    step = DIL_PATTERNS[1][1]
    assert [dil for _, dil in DIL_PATTERNS] == [1, step, step * step]

    for c in range(n_col):
        xs_ref[c] = x_ref[0, :, c * LANES:(c + 1) * LANES]
    rows4 = tm // step
    x4 = [jnp.concatenate([xs_ref[c, pl.ds(a, rows4, stride=step), :] for a in range(step)], axis=0)
          for c in range(n_col)]
    for c in range(n_col):
        xs4_ref[c] = x4[c]
    rows16 = rows4 // step
    x16 = [jnp.concatenate([xs4_ref[c, pl.ds(a * rows4 + b, rows16, stride=step), :]
                            for a in range(step) for b in range(step)], axis=0) for c in range(n_col)]
    inputs = [x_ref[0], jnp.concatenate(x4, axis=1), jnp.concatenate(x16, axis=1)]
    residue_of_group = [[0], list(range(step)), [a + step * b for a in range(step) for b in range(step)]]

    for g, o_ref in enumerate((o0_ref, o1_ref, o2_ref)):
        u = (inputs[g] * scale + shift).astype(BF16)
        res = jnp.dot(u, w_ref[:, g * gw:(g + 1) * gw].astype(BF16), preferred_element_type=F32).astype(BF16)
        rows = tm // len(residue_of_group[g])
        for grp, r in enumerate(residue_of_group[g]):
            o_ref[0, r] = res[grp * rows:(grp + 1) * rows]


def _qkv_proj(x, mod, w, *, tm=1024):
    bsz, t, d = x.shape
    gw = 3 * DIL_WIDTH
    out_specs, out_shape = [], []
    for _, dil in DIL_PATTERNS:
        out_specs.append(pl.BlockSpec((1, dil, tm // dil, gw), lambda b, i: (b, 0, i, 0)))
        out_shape.append(jax.ShapeDtypeStruct((bsz, dil, t // dil, gw), BF16))
    return pl.pallas_call(
        functools.partial(_qkv_proj_kernel, tm=tm),
        grid=(bsz, t // tm),
        in_specs=[pl.BlockSpec((1, tm, d), lambda b, i: (b, i, 0)),
                  pl.BlockSpec((1, 6, d), lambda b, i: (b, 0, 0)),
                  pl.BlockSpec(w.shape, lambda b, i: (0, 0), pipeline_mode=pl.Buffered(1))],
        out_specs=out_specs,
        out_shape=out_shape,
        scratch_shapes=[pltpu.VMEM((d // LANES, tm, LANES), F32), pltpu.VMEM((d // LANES, tm, LANES), F32)],
        compiler_params=_cparams(2),
        name="qkv_proj",
    )(x, mod, w)


def _dilated_attention_group(qkv, bias, g, *, rb=2048):
    bsz, dil, length, width = qkv.shape
    t = dil * length
    nb = length // BLOCK
    n_blk = rb // BLOCK
    assert nb % n_blk == 0 or n_blk % nb == 0
    flat = qkv.reshape(bsz, t, width)

    def cur(j):
        return lambda b, s: (b, s, j)

    def prev(j):
        return lambda b, s: (b, jnp.maximum(s * n_blk - 1, 0), j)

    blk = (None, rb, DIL_WIDTH)
    pblk = (None, BLOCK, DIL_WIDTH)
    o, lse = pl.pallas_call(
        functools.partial(_attn_kernel, nb=nb, rb=rb),
        grid=(bsz, t // rb),
        in_specs=[pl.BlockSpec(blk, cur(0)),
                  pl.BlockSpec(pblk, prev(1)), pl.BlockSpec(blk, cur(1)),
                  pl.BlockSpec(pblk, prev(2)), pl.BlockSpec(blk, cur(2)),
                  pl.BlockSpec((DIL_HEADS, BLOCK, 2 * BLOCK), lambda b, s: (0, 0, 0))],
        out_specs=[pl.BlockSpec(blk, lambda b, s: (b, s, 0)),
                   pl.BlockSpec((None, rb, LANES), lambda b, s: (b, s, 0))],
        out_shape=[jax.ShapeDtypeStruct((bsz, t, DIL_WIDTH), BF16),
                   jax.ShapeDtypeStruct((bsz, t, LANES), F32)],
        compiler_params=_cparams(2),
        name=f"dilated_attn_g{g}",
    )(flat, flat, flat, flat, flat, bias)
    return o.reshape(bsz, dil, length, DIL_WIDTH), lse.reshape(bsz, dil, length, LANES)


def _t5_bucket(dist):
    exact = N_BUCKETS // 2
    logd = jnp.log(jnp.maximum(dist, 1).astype(F32) / exact) / math.log(MAX_DISTANCE / exact)
    large = jnp.minimum(exact + (logd * (N_BUCKETS - exact)).astype(jnp.int32), N_BUCKETS - 1)
    return jnp.where(dist < exact, dist, large)


def _bias_tables(rel_bias):
    rel = jnp.arange(BLOCK + 1)
    tables = []
    for g, (window, dil) in enumerate(DIL_PATTERNS):
        span = window // dil
        bucket = _t5_bucket(jnp.clip(rel, 0, span) * dil)
        vec = rel_bias[bucket][:, g * DIL_HEADS:(g + 1) * DIL_HEADS].T.astype(F32)
        ext = jnp.concatenate([vec[:, ::-1], jnp.zeros((DIL_HEADS, BLOCK), F32)], axis=1)
        flat = jnp.tile(ext, (1, BLOCK))[:, :BLOCK * 2 * BLOCK]
        tables.append(flat.reshape(DIL_HEADS, BLOCK, 2 * BLOCK))
    qi = np.arange(BLOCK)[:, None]
    ki = np.arange(2 * BLOCK)[None, :]
    band = (ki >= qi) & (ki <= qi + BLOCK)
    return jnp.where(band, jnp.stack(tables), MASKED)


def _tail(mix_fn, wout_ref, x_ref, mod_ref, lng_ref, lnb_ref, w1_ref, w2_ref, o_ref, *, tm, n_sub, ff_chunk):
    m = mod_ref[0]
    sub = tm // n_sub
    tiles = [slice(i * sub, (i + 1) * sub) for i in range(n_sub)]
    x1s, us = [], []
    for rows in tiles:
        y = None
        for piece, k0 in mix_fn(rows):
            part = jnp.dot(piece, wout_ref[k0:k0 + piece.shape[-1], :], preferred_element_type=F32)
            y = part if y is None else y + part
        x1 = _layer_norm_rows(DEEPNORM_ALPHA * x_ref[0, rows, :] + (1.0 + m[2:3]) * y, lng_ref[0:1], lnb_ref[0:1])
        x1s.append(x1)
        us.append((x1 * (1.0 + m[4:5]) + m[3:4]).astype(BF16))
    accs = [None] * n_sub
    for c in range(D_FF // ff_chunk):
        cols = slice(c * ff_chunk, (c + 1) * ff_chunk)
        w1c = w1_ref[:, cols].astype(BF16)
        w2c = w2_ref[cols, :].astype(BF16)
        for i in range(n_sub):
            h = jnp.maximum(jnp.dot(us[i], w1c, preferred_element_type=F32), 0.0)
            part = jnp.dot((h * h).astype(BF16), w2c, preferred_element_type=F32)
            accs[i] = part if accs[i] is None else accs[i] + part
    for i, rows in enumerate(tiles):
        z = DEEPNORM_ALPHA * x1s[i] + (1.0 + m[5:6]) * accs[i]
        o_ref[0, rows, :] = _layer_norm_rows(z, lng_ref[1:2], lnb_ref[1:2])


def _even_tail_kernel(ya_ref, yb_ref, wout_ref, x_ref, mod_ref, lng_ref, lnb_ref, w1_ref, w2_ref, o_ref, **kw):
    def mix(rows):
        return [(ya_ref[0, rows, :], 0), (yb_ref[0, rows, :], ya_ref.shape[-1])]

    _tail(mix, wout_ref, x_ref, mod_ref, lng_ref, lnb_ref, w1_ref, w2_ref, o_ref, **kw)


def _odd_tail_kernel(o0_ref, o1_ref, o2_ref, l0_ref, l1_ref, l2_ref, wout_ref, x_ref, mod_ref, lng_ref, lnb_ref,
                     w1_ref, w2_ref, o_ref, o_nat, l_nat, mix_s, *, tm, **kw):
    def natural(src_ref, scr, dil):
        if dil == 1:
            return src_ref[0, 0].astype(F32)
        n_col = src_ref.shape[-1] // LANES
        for c in range(n_col):
            for r in range(dil):
                scr[c, pl.ds(r, tm // dil, stride=dil), :] = src_ref[0, r, :, c * LANES:(c + 1) * LANES].astype(F32)
        return jnp.concatenate([scr[c] for c in range(n_col)], axis=1)

    dils = [dil for _, dil in DIL_PATTERNS]
    lses = [natural(ref, l_nat, dil) for ref, dil in zip((l0_ref, l1_ref, l2_ref), dils)]
    mx = jnp.maximum(jnp.maximum(lses[0], lses[1]), lses[2])
    es = [jnp.exp(l - mx) for l in lses]
    inv = 1.0 / (es[0] + es[1] + es[2])
    hr = lax.broadcasted_iota(jnp.int32, (LANES, DIL_WIDTH), 0)
    hc = lax.broadcasted_iota(jnp.int32, (LANES, DIL_WIDTH), 1) // DIL_HEAD_DIM
    expand = jnp.where(hr == hc, 1.0, 0.0).astype(BF16)

    def widen(wt):
        return jnp.dot(wt.astype(BF16), expand, preferred_element_type=F32)

    merged = None
    for e, ref, dil in zip(es, (o0_ref, o1_ref, o2_ref), dils):
        term = widen(e * inv) * natural(ref, o_nat, dil)
        merged = term if merged is None else merged + term
    mix_s[...] = merged.astype(BF16)
    _tail(lambda rows: [(mix_s[rows, :], 0)], wout_ref, x_ref, mod_ref, lng_ref, lnb_ref, w1_ref, w2_ref, o_ref,
          tm=tm, **kw)


def _layer_tail(mix, w_out, x, mod, ln_g, ln_b, w1, w2, layer, *, tm=512, n_sub=2, ff_chunk=1024):
    bsz, t, d = x.shape

    def const(shape):
        return pl.BlockSpec(shape, lambda b, i: (0,) * len(shape), pipeline_mode=pl.Buffered(1))

    def of_layer(shape):
        return pl.BlockSpec((None,) + shape, lambda b, i: (layer,) + (0,) * len(shape), pipeline_mode=pl.Buffered(1))

    tail_specs = [const(w_out.shape),
                  pl.BlockSpec((1, tm, d), lambda b, i: (b, i, 0)),
                  pl.BlockSpec((1, 6, d), lambda b, i: (b, 0, 0)),
                  of_layer((2, d)), of_layer((2, d)), of_layer((d, D_FF)), of_layer((D_FF, d))]
    tail_args = (w_out, x, mod, ln_g, ln_b, w1, w2)
    kw = dict(tm=tm, n_sub=n_sub, ff_chunk=ff_chunk)
    if isinstance(mix, tuple):
        outs, lses = mix
        mix_specs = ([pl.BlockSpec((1, dil, tm // dil, DIL_WIDTH), lambda b, i: (b, 0, i, 0)) for _, dil in DIL_PATTERNS]
                     + [pl.BlockSpec((1, dil, tm // dil, LANES), lambda b, i: (b, 0, i, 0)) for _, dil in DIL_PATTERNS])
        mix_args = (*outs, *lses)
        body = functools.partial(_odd_tail_kernel, **kw)
        scratch = [pltpu.VMEM((DIL_WIDTH // LANES, tm, LANES), F32), pltpu.VMEM((1, tm, LANES), F32),
                   pltpu.VMEM((tm, DIL_WIDTH), BF16)]
    else:
        mix_specs = [pl.BlockSpec((1, tm, piece.shape[-1]), lambda b, i: (b, i, 0)) for piece in mix]
        mix_args = tuple(mix)
        body = functools.partial(_even_tail_kernel, **kw)
        scratch = []
    return pl.pallas_call(
        body,
        grid=(bsz, t // tm),
        in_specs=mix_specs + tail_specs,
        out_specs=pl.BlockSpec((1, tm, d), lambda b, i: (b, i, 0)),
        out_shape=jax.ShapeDtypeStruct((bsz, t, d), F32),
        scratch_shapes=scratch,
        compiler_params=_cparams(2, TAIL_VMEM_LIMIT),
        name="layer_tail",
    )(*mix_args, *tail_args)


def _prep_even_weights(mu, w_up, a_up, g_up):
    pad = LR_PAD - (RW_PROJ - 3 * RW_WIDTH)
    mu_pad = jnp.concatenate([mu, jnp.zeros((pad,), mu.dtype)]).reshape(1, 3 * RW_WIDTH + LR_PAD)
    zeros = jnp.zeros((RW_DECAY_RANK, RW_WIDTH), w_up.dtype)
    wwa = jnp.concatenate([jnp.concatenate([w_up, zeros], axis=1),
                           jnp.concatenate([zeros, a_up], axis=1)], axis=0).astype(BF16)
    gup = jnp.concatenate([g_up, jnp.zeros((pad, RW_WIDTH), g_up.dtype)], axis=0).astype(BF16)
    return mu_pad, wwa, gup


def kernel(x, c, ada_w, ada_b, ln_g, ln_b, ab_w_in, rw_mu, rw_w0, rw_w_up, rw_a0, rw_a_up, rw_g_up,
           rw_k_k, rw_k_a, rw_r_k, rw_lnx_g, rw_lnx_b, sc_conv_w, ab_w_out, dil_w_qkv, dil_w_out,
           rel_bias, mlp_w1, mlp_w2):
    bsz, t, d = x.shape
    mods = _adaln(c, ada_w, ada_b).reshape(DEPTH, bsz, 6, d)
    bias = None
    row = lambda a: a.reshape(1, -1)
    w1_all, w2_all = mlp_w1, mlp_w2
    for i in range(DEPTH):
        mod = mods[i]
        j = i // 2
        if i % 2 == 0:
            mu_pad, wwa, gup = _prep_even_weights(rw_mu[j], rw_w_up[j], rw_a_up[j], rw_g_up[j])
            rkv, lr, yb = _ab_proj(x, mod, ab_w_in[j].T, mu_pad, sc_conv_w[j])
            ya = _rwkv(rkv, lr, row(rw_w0[j]), row(rw_a0[j]), wwa, gup,
                       row(rw_k_k[j]), row(rw_k_a[j]), row(rw_r_k[j]),
                       row(rw_lnx_g[j]), row(rw_lnx_b[j]))
            mix, w_out = [ya, yb], ab_w_out[j]
        else:
            if bias is None:
                bias = _bias_tables(rel_bias)
            qkvs = _qkv_proj(x, mod, dil_w_qkv[j])
            outs, lses = [], []
            for g in range(N_GROUPS):
                o, lse = _dilated_attention_group(qkvs[g], bias[g], g)
                outs.append(o)
                lses.append(lse)
            mix, w_out = (outs, lses), dil_w_out[j]
        x = _layer_tail(mix, w_out.astype(BF16), x, mod, ln_g, ln_b, w1_all, w2_all, i)
    return x
```

```python
import functools
import math

import jax
import jax.numpy as jnp
import numpy as np
from jax import lax
from jax.experimental import pallas as pl
from jax.experimental.pallas import tpu as pltpu

F32 = jnp.float32
BF16 = jnp.bfloat16

D_MODEL = 1024
DEPTH = 2
RW_HEADS = 8
RW_HEAD_DIM = 64
RW_WIDTH = 512
RW_DECAY_RANK = 64
RW_ICLR_RANK = 64
RW_GATE_RANK = 160
RW_GN_EPS = 64e-5
RW_PROJ = 3 * RW_WIDTH + RW_DECAY_RANK + RW_ICLR_RANK + RW_GATE_RANK
SC_WIDTH = 512
SC_CONV = 3
DIL_PATTERNS = ((128, 1), (512, 4), (2048, 16))
N_GROUPS = 3
DIL_HEADS = 8
DIL_HEAD_DIM = 64
DIL_WIDTH = 512
BLOCK = 128
N_BUCKETS = 32
MAX_DISTANCE = 2048
D_FF = 4 * D_MODEL
DEEPNORM_ALPHA = (2 * DEPTH) ** 0.25
LN_EPS = 1e-5

LANES = 128
SUBLANES = 8
MXU_DIM = 256
CHUNK = 64
INV_BLOCK = 16
LR_PAD = 384
VMEM_LIMIT = 56 * 1024 * 1024
TAIL_VMEM_LIMIT = 60 * 1024 * 1024
MASKED = -1e30


def _cparams(n_axes, vmem_limit=VMEM_LIMIT):
    return pltpu.CompilerParams(dimension_semantics=("arbitrary",) * n_axes,
                                vmem_limit_bytes=vmem_limit)


def _sigmoid(x):
    return 0.5 * jnp.tanh(0.5 * x) + 0.5


def _dot(a, b):
    return jnp.dot(a.astype(BF16), b.astype(BF16), preferred_element_type=F32)


def _layer_norm_rows(z, g, b):
    mu = jnp.mean(z, axis=-1, keepdims=True)
    zc = z - mu
    var = jnp.mean(zc * zc, axis=-1, keepdims=True)
    return zc * lax.rsqrt(var + LN_EPS) * g + b


def _adaln_kernel(c_ref, w_ref, b_ref, o_ref):
    c = c_ref[...]
    cond = c * _sigmoid(c)
    o_ref[0] = _dot(cond, w_ref[0]) + b_ref[0]


def _adaln(c, ada_w, ada_b):
    depth, d, n = ada_w.shape
    bsz = c.shape[0]
    tn = 1536
    return pl.pallas_call(
        _adaln_kernel,
        grid=(depth, n // tn),
        in_specs=[pl.BlockSpec((bsz, d), lambda i, j: (0, 0)),
                  pl.BlockSpec((1, d, tn), lambda i, j: (i, 0, j)),
                  pl.BlockSpec((1, 1, tn), lambda i, j: (i, 0, j))],
        out_specs=pl.BlockSpec((1, bsz, tn), lambda i, j: (i, 0, j)),
        out_shape=jax.ShapeDtypeStruct((depth, bsz, n), F32),
        compiler_params=_cparams(2),
        name="adaln",
    )(c, ada_w, ada_b.reshape(depth, 1, n))


def _shift_rows(x, carry, k):
    rolled = pltpu.roll(x, k, 0)
    head = rolled[:SUBLANES]
    row = lax.broadcasted_iota(jnp.int32, head.shape, 0)
    for i in range(k):
        head = jnp.where(row == i, carry[SUBLANES - k + i:SUBLANES - k + i + 1], head)
    return jnp.concatenate([head, rolled[SUBLANES:]], axis=0)


def _ab_proj_kernel(x_ref, mod_ref, wt_ref, mu_ref, convw_ref, rkv_ref, lr_ref, yb_ref, carry_p, carry_z, *, tm):
    @pl.when(pl.program_id(1) == 0)
    def _():
        carry_p[...] = jnp.zeros_like(carry_p)
        carry_z[...] = jnp.zeros_like(carry_z)

    m = mod_ref[0]
    u = (x_ref[0] * (1.0 + m[1:2]) + m[0:1]).astype(BF16)

    def proj(c0, width):
        return lax.dot_general(u, wt_ref[c0:c0 + width, :].astype(BF16), (((1,), (1,)), ((), ())),
                               preferred_element_type=F32)

    def mixed(c0, width):
        p = proj(c0, width)
        prev = _shift_rows(p, carry_p[:, c0:c0 + width], 1)
        carry_p[:, c0:c0 + width] = p[tm - SUBLANES:]
        return (p + mu_ref[:, c0:c0 + width] * (prev - p)).astype(BF16)

    w3 = 3 * RW_WIDTH
    for c in range(3):
        rkv_ref[0, :, c * RW_WIDTH:(c + 1) * RW_WIDTH] = mixed(c * RW_WIDTH, RW_WIDTH)
    lr_ref[0] = mixed(w3, LR_PAD)

    h = proj(RW_PROJ, SC_WIDTH)
    z = proj(RW_PROJ + 2 * SC_WIDTH, SC_WIDTH) * h
    zc = carry_z[...]
    cw = convw_ref[...]
    conv = cw[2:3] * z + cw[1:2] * _shift_rows(z, zc, 1) + cw[0:1] * _shift_rows(z, zc, 2)
    carry_z[...] = z[tm - SUBLANES:]
    yb_ref[0] = (proj(RW_PROJ + SC_WIDTH, SC_WIDTH) * conv).astype(BF16)


def _ab_proj(x, mod, w_t, mu, conv_w, *, tm=1024):
    bsz, t, d = x.shape
    w3 = 3 * RW_WIDTH

    def const(shape):
        return pl.BlockSpec(shape, lambda b, i: (0,) * len(shape), pipeline_mode=pl.Buffered(1))

    def rows(width):
        return pl.BlockSpec((1, tm, width), lambda b, i: (b, i, 0))

    return pl.pallas_call(
        functools.partial(_ab_proj_kernel, tm=tm),
        grid=(bsz, t // tm),
        in_specs=[rows(d), pl.BlockSpec((1, 6, d), lambda b, i: (b, 0, 0)),
                  const(w_t.shape), const((1, w3 + LR_PAD)), const((SC_CONV, SC_WIDTH))],
        out_specs=[rows(w3), rows(LR_PAD), rows(SC_WIDTH)],
        out_shape=[jax.ShapeDtypeStruct((bsz, t, w3), BF16), jax.ShapeDtypeStruct((bsz, t, LR_PAD), BF16),
                   jax.ShapeDtypeStruct((bsz, t, SC_WIDTH), BF16)],
        scratch_shapes=[pltpu.VMEM((SUBLANES, w3 + LR_PAD), F32), pltpu.VMEM((SUBLANES, SC_WIDTH), F32)],
        compiler_params=_cparams(2),
        name="ab_proj",
    )(x, mod, w_t, mu, conv_w)


def _rwkv_chunk_terms(a_t, r_t, b_t, k_t, b_e, k_e, v, p_end, hooks=()):
    n = len(a_t)
    idx = range(n)
    hooks = list(hooks)

    def hook():
        if hooks:
            hooks.pop(0)()
    row = lax.broadcasted_iota(jnp.int32, (CHUNK, LANES), 0)
    col = lax.broadcasted_iota(jnp.int32, (CHUNK, LANES), 1)
    s_idx = col % CHUNK
    strict = s_idx < row
    incl = s_idx <= row
    near = strict & ((s_idx // INV_BLOCK) == (row // INV_BLOCK))
    far = strict & ((s_idx // INV_BLOCK) != (row // INV_BLOCK))
    eye = jnp.where(s_idx == row, 1.0, 0.0)
    r2 = lax.broadcasted_iota(jnp.int32, (LANES, 2 * LANES), 0)
    c2 = lax.broadcasted_iota(jnp.int32, (LANES, 2 * LANES), 1)
    bdmask2 = (r2 // RW_HEAD_DIM) == ((c2 % LANES) // RW_HEAD_DIM)
    eye_bd = jnp.where(lax.broadcasted_iota(jnp.int32, (LANES, LANES), 0)
                       == lax.broadcasted_iota(jnp.int32, (LANES, LANES), 1), 1.0, 0.0)
    zero16 = jnp.zeros((CHUNK, LANES), BF16)
    lane1 = lax.broadcasted_iota(jnp.int32, (1, LANES), 1)
    m0_16 = jnp.where(lane1 < RW_HEAD_DIM, 1.0, 0.0).astype(BF16)
    m1_16 = jnp.where(lane1 < RW_HEAD_DIM, 0.0, 1.0).astype(BF16)

    def cast(xs):
        return [x.astype(BF16) for x in xs]

    def bd(q16):
        return jnp.concatenate([q16 * m0_16, q16 * m1_16], axis=0)

    def mm(p16, q16):
        return jnp.dot(p16, q16, preferred_element_type=F32)

    def mm_nt(p16, q16):
        return lax.dot_general(p16, q16, (((1,), (1,)), ((), ())), preferred_element_type=F32)

    a16, r16, v16 = cast(a_t), cast(r_t), cast(v)
    lhs = [jnp.concatenate([a16[i], r16[i]], axis=0) for i in idx]
    b16, k16 = cast(b_t), cast(k_t)
    bdv = [bd(x) for x in v16]
    g = [mm_nt(lhs[i], jnp.concatenate([bd(b16[i]), bd(k16[i])], axis=0)) for i in idx]
    x1 = [jnp.where(near, g[i][:CHUNK, :LANES], 0.0) for i in idx]
    x1_16 = cast(x1)
    l_e16 = [jnp.where(far, g[i][:CHUNK, :LANES], 0.0).astype(BF16) for i in idx]
    a_k16 = [jnp.concatenate([jnp.where(strict, g[i][:CHUNK, LANES:], 0.0),
                              jnp.where(incl, g[i][CHUNK:, LANES:], 0.0)], axis=0).astype(BF16) for i in idx]
    a_rb16 = [jnp.where(incl, g[i][CHUNK:, :LANES], 0.0).astype(BF16) for i in idx]

    def both(p16, q16):
        return jnp.concatenate([bd(p16), bd(q16)], axis=1)

    x2_16 = cast([mm(x1_16[i], bd(x1_16[i])) for i in idx])
    hook()
    acc = [eye + x1[i] for i in idx]
    sq = [mm(x2_16[i], both(x2_16[i], acc[i].astype(BF16))) for i in idx]
    x4_16 = [sq[i][:, :LANES].astype(BF16) for i in idx]
    acc = [acc[i] + sq[i][:, LANES:] for i in idx]
    sq = [mm(x4_16[i], both(x4_16[i], acc[i].astype(BF16))) for i in idx]
    hook()
    x8_16 = [sq[i][:, :LANES].astype(BF16) for i in idx]
    acc = [acc[i] + sq[i][:, LANES:] for i in idx]
    dinv = [acc[i] + mm(x8_16[i], bd(acc[i].astype(BF16))) for i in idx]
    dinv16 = cast(dinv)
    hook()
    f16 = cast([mm(dinv16[i], bd(l_e16[i])) for i in idx])
    sq = [mm(f16[i], both(f16[i], dinv16[i])) for i in idx]
    f2_16 = [sq[i][:, :LANES].astype(BF16) for i in idx]
    hook()
    hmat = [dinv[i] + sq[i][:, LANES:] for i in idx]
    tinv16 = cast([hmat[i] + mm(f2_16[i], bd(hmat[i].astype(BF16))) for i in idx])

    kv = [mm(a_k16[i], bdv[i]) for i in idx]
    wv16 = [kv[i][:CHUNK].astype(BF16) for i in idx]
    hook()
    aw = [mm(tinv16[i], both(a16[i], wv16[i])) for i in idx]
    aw16 = cast(aw)
    hook()
    qy = [mm(a_rb16[i], both(aw16[i][:, :LANES], aw16[i][:, LANES:]))
          + jnp.concatenate([r_t[i], kv[i][CHUNK:]], axis=1) for i in idx]
    be_ke_t = [jnp.concatenate([b_e[i], k_e[i]], axis=0).T.astype(BF16) for i in idx]
    rhs = [jnp.concatenate([aw16[i], jnp.concatenate([zero16, v16[i]], axis=1)], axis=0) for i in idx]
    mn = [jnp.where(bdmask2, mm(be_ke_t[i], rhs[i]), 0.0) for i in idx]
    qhat = [x[:, :LANES] for x in qy]
    yhat = [x[:, LANES:] for x in qy]
    mmat = [eye_bd * p_end[i] + mn[i][:, :LANES] for i in idx]
    nmat = [x[:, LANES:] for x in mn]
    while hooks:
        hook()
    return qhat, yhat, mmat, nmat


def _rwkv_kernel(rkv_ref, lr_ref, w0_ref, a0_ref, wwa_ref, gup_ref,
                 kk_ref, ka_ref, rk_ref, lnxg_ref, lnxb_ref, o_ref, state, yout_s, *, tb):
    n_chunks = tb // CHUNK
    n_pairs = RW_WIDTH // LANES

    @pl.when(pl.program_id(1) == 0)
    def _():
        state[...] = jnp.zeros_like(state)

    r = rkv_ref[0, :, :RW_WIDTH].astype(F32)
    k = rkv_ref[0, :, RW_WIDTH:2 * RW_WIDTH].astype(F32)
    v = rkv_ref[0, :, 2 * RW_WIDTH:].astype(F32)
    lr_m = lr_ref[0].astype(F32)

    wa = lr_m[:, :LANES]
    lane = lax.broadcasted_iota(jnp.int32, wa.shape, 1)
    wa = jnp.where(lane < RW_DECAY_RANK, jnp.tanh(wa), wa)
    wa_up = _dot(wa, wwa_ref[...])
    ld = -math.exp(-0.5) * _sigmoid(w0_ref[...] + wa_up[:, :RW_WIDTH])
    iclr = _sigmoid(a0_ref[...] + wa_up[:, RW_WIDTH:])
    gate = _dot(_sigmoid(lr_m[:, LANES:]), gup_ref[...])

    hr = lax.broadcasted_iota(jnp.int32, (MXU_DIM, MXU_DIM), 0) // RW_HEAD_DIM
    hc = lax.broadcasted_iota(jnp.int32, (MXU_DIM, MXU_DIM), 1) // RW_HEAD_DIM
    seg = jnp.where(hr == hc, 1.0 / RW_HEAD_DIM, 0.0).astype(BF16)
    seg_ones = jnp.where(hr == hc, 1.0, 0.0).astype(BF16)

    def seg_mean(x, mat=seg):
        x16 = x.astype(BF16)
        return jnp.concatenate([jnp.dot(x16[:, c:c + MXU_DIM], mat, preferred_element_type=F32)
                                for c in range(0, RW_WIDTH, MXU_DIM)], axis=1)

    kk = k * kk_ref[...]
    ss = seg_mean(kk * kk, seg_ones)
    kk = kk * lax.rsqrt(jnp.maximum(ss, 1e-24))
    kh = k * (1.0 + (iclr - 1.0) * ka_ref[...])

    tr = lax.broadcasted_iota(jnp.int32, (MXU_DIM, MXU_DIM), 0)
    tc = lax.broadcasted_iota(jnp.int32, (MXU_DIM, MXU_DIM), 1)
    tri = jnp.where((tc <= tr) & (tc // CHUNK == tr // CHUNK), 1.0, 0.0).astype(BF16)
    ld_hi = ld.astype(BF16)
    ld_lo = (ld - ld_hi.astype(F32)).astype(BF16)
    lp = jnp.concatenate(
        [jnp.dot(tri, ld_hi[t0:t0 + MXU_DIM], preferred_element_type=F32)
         + jnp.dot(tri, ld_lo[t0:t0 + MXU_DIM], preferred_element_type=F32)
         for t0 in range(0, tb, MXU_DIM)], axis=0)

    a_t, r_t, b_t, k_t, b_e, k_e, vs, p_end = [], [], [], [], [], [], [], []
    b = kk * iclr
    for c in range(n_chunks):
        rs = slice(c * CHUNK, (c + 1) * CHUNK)
        lp_c = lp[rs]
        lp_last = lp_c[CHUNK - 1:CHUNK]
        e_in = jnp.exp(lp_c)
        e_inv = jnp.exp(-lp_c)
        pe_c = jnp.exp(lp_last)
        a_c = -kk[rs] * jnp.exp(lp_c - ld[rs])
        r_c = r[rs] * e_in
        b_c = b[rs] * e_inv
        k_c = kh[rs] * e_inv
        be_c = b_c * pe_c
        ke_c = k_c * pe_c
        for p in range(n_pairs):
            cs = slice(p * LANES, (p + 1) * LANES)
            a_t.append(a_c[:, cs])
            r_t.append(r_c[:, cs])
            b_t.append(b_c[:, cs])
            k_t.append(k_c[:, cs])
            b_e.append(be_c[:, cs])
            k_e.append(ke_c[:, cs])
            vs.append(v[rs, cs])
            p_end.append(pe_c[:, cs])

    bonus = seg_mean(r * kh * rk_ref[...], seg_ones) * v
    s = [state[p] for p in range(n_pairs)]

    def serial_step(c, terms, c0):
        qhat, yhat, mmat, nmat = terms
        base = (c - c0) * n_pairs
        ym = [_dot(jnp.concatenate([qhat[base + p], mmat[base + p]], axis=0), s[p]) for p in range(n_pairs)]
        for p in range(n_pairs):
            yout_s[c * CHUNK:(c + 1) * CHUNK, p * LANES:(p + 1) * LANES] = ym[p][:CHUNK] + yhat[base + p]
            s[p] = ym[p][CHUNK:] + nmat[base + p]

    def finish(rows):
        y = yout_s[rows, :]
        yc = y - seg_mean(y)
        var = seg_mean(yc * yc)
        y = yc * lax.rsqrt(var + RW_GN_EPS) * lnxg_ref[...] + lnxb_ref[...]
        o_ref[0, rows, :] = ((y + bonus[rows]) * gate[rows]).astype(o_ref.dtype)

    half = n_chunks // 2
    cut = half * n_pairs
    slabs = (a_t, r_t, b_t, k_t, b_e, k_e, vs, p_end)
    first = _rwkv_chunk_terms(*[x[:cut] for x in slabs])
    hooks = [functools.partial(serial_step, c, first, 0) for c in range(half)]
    hooks.append(functools.partial(finish, slice(0, half * CHUNK)))
    second = _rwkv_chunk_terms(*[x[cut:] for x in slabs], hooks=hooks)
    for c in range(half, n_chunks):
        serial_step(c, second, half)
    for p in range(n_pairs):
        state[p] = s[p]
    finish(slice(half * CHUNK, tb))


def _rwkv(rkv, lr, w0, a0, wwa, gup, k_k, k_a, r_k, lnx_g, lnx_b, *, tb=1024):
    bsz, t, w3 = rkv.shape
    n_pairs = RW_WIDTH // LANES

    def full(shape):
        return pl.BlockSpec(shape, lambda b, i: (0,) * len(shape))

    return pl.pallas_call(
        functools.partial(_rwkv_kernel, tb=tb),
        grid=(bsz, t // tb),
        in_specs=[pl.BlockSpec((1, tb, w3), lambda b, i: (b, i, 0)),
                  pl.BlockSpec((1, tb, LR_PAD), lambda b, i: (b, i, 0)),
                  full((1, RW_WIDTH)), full((1, RW_WIDTH)),
                  full((LANES, 2 * RW_WIDTH)), full((LR_PAD - LANES, RW_WIDTH)),
                  full((1, RW_WIDTH)), full((1, RW_WIDTH)), full((1, RW_WIDTH)),
                  full((1, RW_WIDTH)), full((1, RW_WIDTH))],
        out_specs=pl.BlockSpec((1, tb, RW_WIDTH), lambda b, i: (b, i, 0)),
        out_shape=jax.ShapeDtypeStruct((bsz, t, RW_WIDTH), BF16),
        scratch_shapes=[pltpu.VMEM((n_pairs, LANES, LANES), F32),
                        pltpu.VMEM((tb, RW_WIDTH), F32)],
        compiler_params=_cparams(2),
        name="rwkv",
    )(rkv, lr, w0, a0, wwa, gup, k_k, k_a, r_k, lnx_g, lnx_b)


def _attn_kernel(q_ref, kp_ref, kc_ref, vp_ref, vc_ref, bias_ref, o_ref, lse_ref, *, nb, rb):
    step = pl.program_id(1)
    n_blk = rb // BLOCK
    scale = DIL_HEAD_DIM ** -0.5
    n_pairs = DIL_WIDTH // LANES
    krow = lax.broadcasted_iota(jnp.int32, (2 * BLOCK, LANES), 0) // BLOCK
    klane = lax.broadcasted_iota(jnp.int32, (2 * BLOCK, LANES), 1) // DIL_HEAD_DIM
    own = krow == klane
    zeros = jnp.zeros((2 * BLOCK, LANES), BF16)
    ones_bd = jnp.where(own, 1.0, 0.0).astype(BF16)
    lane_o = lax.broadcasted_iota(jnp.int32, (BLOCK, LANES), 1)

    def stack_bd(x):
        return jnp.where(own, jnp.concatenate([x, x], axis=0), zeros)

    def attend(blocks):
        units = [(rows, q, parts, p) for rows, q, parts in blocks for p in range(n_pairs)]
        lanes = [slice(p * LANES, (p + 1) * LANES) for p in range(n_pairs)]
        scores = []
        for _, q, parts, p in units:
            qs = q[:, lanes[p]] * scale
            row = []
            for k_blk, _, col0, pen in parts:
                s = lax.dot_general(qs, stack_bd(k_blk[:, lanes[p]]), (((1,), (1,)), ((), ())),
                                    preferred_element_type=F32)
                bias = jnp.concatenate([bias_ref[2 * p, :, col0:col0 + BLOCK],
                                        bias_ref[2 * p + 1, :, col0:col0 + BLOCK]], axis=1)
                s = s + bias
                row.append(s if pen is None else s + pen)
            scores.append(row)
        ms = []
        for u in range(len(units)):
            halves = []
            for hh in range(2):
                cols = slice(hh * BLOCK, (hh + 1) * BLOCK)
                m = jnp.max(scores[u][0][:, cols], axis=-1, keepdims=True)
                for s in scores[u][1:]:
                    m = jnp.maximum(m, jnp.max(s[:, cols], axis=-1, keepdims=True))
                halves.append(m)
            ms.append(halves)
        es = [[jnp.exp((s - jnp.concatenate([jnp.broadcast_to(ms[u][0], (BLOCK, BLOCK)),
                                             jnp.broadcast_to(ms[u][1], (BLOCK, BLOCK))], axis=1)).astype(BF16))
               for s in scores[u]] for u in range(len(units))]
        res = []
        for u, (_, _, parts, p) in enumerate(units):
            res.append(sum(jnp.dot(e, jnp.concatenate([stack_bd(part[1][:, lanes[p]]), ones_bd], axis=1),
                                   preferred_element_type=F32) for e, part in zip(es[u], parts)))
        for b, (rows, _, _) in enumerate(blocks):
            m_all = jnp.zeros((BLOCK, LANES), F32)
            den_all = jnp.ones((BLOCK, LANES), F32)
            for p in range(n_pairs):
                u = b * n_pairs + p
                num, den = res[u][:, :LANES], res[u][:, LANES:]
                o_ref[rows, lanes[p]] = (num / den).astype(o_ref.dtype)
                m_all = jnp.where(lane_o == 2 * p, ms[u][0], jnp.where(lane_o == 2 * p + 1, ms[u][1], m_all))
                den_all = jnp.where((lane_o == 2 * p), den,
                                    jnp.where(lane_o == 2 * p + 1, pltpu.roll(den, DIL_HEAD_DIM, 1), den_all))
            lse_ref[rows, :] = m_all + jnp.log(den_all)

    blocks = []
    for j in range(n_blk):
        rows = slice(j * BLOCK, (j + 1) * BLOCK)
        cur = (kc_ref[rows, :], vc_ref[rows, :], BLOCK, None)
        if n_blk % nb == 0:
            prev = "none" if j % nb == 0 else "inside"
        else:
            prev = "inside" if j > 0 else ("none" if nb == n_blk else "before")
        if prev == "none":
            parts = [cur]
        elif prev == "inside":
            before = slice((j - 1) * BLOCK, j * BLOCK)
            parts = [(kc_ref[before, :], vc_ref[before, :], 0, None), cur]
        else:
            has_prev = (step % (nb // n_blk)) != 0
            parts = [(kp_ref[...], vp_ref[...], 0, jnp.where(has_prev, 0.0, MASKED)), cur]
        blocks.append((rows, q_ref[rows, :], parts))
    attend(blocks)


def _qkv_proj_kernel(x_ref, mod_ref, w_ref, o0_ref, o1_ref, o2_ref, xs_ref, xs4_ref, *, tm):
    m = mod_ref[0]
    scale = 1.0 + m[1:2]
    shift = m[0:1]
    n_col = x_ref.shape[-1] // LANES
    gw = 3 * DIL_WIDTH
    step = DIL_PATTERNS[1][1]
    assert [dil for _, dil in DIL_PATTERNS] == [1, step, step * step]

    for c in range(n_col):
        xs_ref[c] = x_ref[0, :, c * LANES:(c + 1) * LANES]
    rows4 = tm // step
    x4 = [jnp.concatenate([xs_ref[c, pl.ds(a, rows4, stride=step), :] for a in range(step)], axis=0)
          for c in range(n_col)]
    for c in range(n_col):
        xs4_ref[c] = x4[c]
    rows16 = rows4 // step
    x16 = [jnp.concatenate([xs4_ref[c, pl.ds(a * rows4 + b, rows16, stride=step), :]
                            for a in range(step) for b in range(step)], axis=0) for c in range(n_col)]
    inputs = [x_ref[0], jnp.concatenate(x4, axis=1), jnp.concatenate(x16, axis=1)]
    residue_of_group = [[0], list(range(step)), [a + step * b for a in range(step) for b in range(step)]]

    for g, o_ref in enumerate((o0_ref, o1_ref, o2_ref)):
        u = (inputs[g] * scale + shift).astype(BF16)
        res = jnp.dot(u, w_ref[:, g * gw:(g + 1) * gw].astype(BF16), preferred_element_type=F32).astype(BF16)
        rows = tm // len(residue_of_group[g])
        for grp, r in enumerate(residue_of_group[g]):
            o_ref[0, r] = res[grp * rows:(grp + 1) * rows]


def _qkv_proj(x, mod, w, *, tm=1024):
    bsz, t, d = x.shape
    gw = 3 * DIL_WIDTH
    out_specs, out_shape = [], []
    for _, dil in DIL_PATTERNS:
        out_specs.append(pl.BlockSpec((1, dil, tm // dil, gw), lambda b, i: (b, 0, i, 0)))
        out_shape.append(jax.ShapeDtypeStruct((bsz, dil, t // dil, gw), BF16))
    return pl.pallas_call(
        functools.partial(_qkv_proj_kernel, tm=tm),
        grid=(bsz, t // tm),
        in_specs=[pl.BlockSpec((1, tm, d), lambda b, i: (b, i, 0)),
                  pl.BlockSpec((1, 6, d), lambda b, i: (b, 0, 0)),
                  pl.BlockSpec(w.shape, lambda b, i: (0, 0), pipeline_mode=pl.Buffered(1))],
        out_specs=out_specs,
        out_shape=out_shape,
        scratch_shapes=[pltpu.VMEM((d // LANES, tm, LANES), F32), pltpu.VMEM((d // LANES, tm, LANES), F32)],
        compiler_params=_cparams(2),
        name="qkv_proj",
    )(x, mod, w)


def _dilated_attention_group(qkv, bias, g, *, rb=2048):
    bsz, dil, length, width = qkv.shape
    t = dil * length
    nb = length // BLOCK
    n_blk = rb // BLOCK
    assert nb % n_blk == 0 or n_blk % nb == 0
    flat = qkv.reshape(bsz, t, width)

    def cur(j):
        return lambda b, s: (b, s, j)

    def prev(j):
        return lambda b, s: (b, jnp.maximum(s * n_blk - 1, 0), j)

    blk = (None, rb, DIL_WIDTH)
    pblk = (None, BLOCK, DIL_WIDTH)
    o, lse = pl.pallas_call(
        functools.partial(_attn_kernel, nb=nb, rb=rb),
        grid=(bsz, t // rb),
        in_specs=[pl.BlockSpec(blk, cur(0)),
                  pl.BlockSpec(pblk, prev(1)), pl.BlockSpec(blk, cur(1)),
                  pl.BlockSpec(pblk, prev(2)), pl.BlockSpec(blk, cur(2)),
                  pl.BlockSpec((DIL_HEADS, BLOCK, 2 * BLOCK), lambda b, s: (0, 0, 0))],
        out_specs=[pl.BlockSpec(blk, lambda b, s: (b, s, 0)),
                   pl.BlockSpec((None, rb, LANES), lambda b, s: (b, s, 0))],
        out_shape=[jax.ShapeDtypeStruct((bsz, t, DIL_WIDTH), BF16),
                   jax.ShapeDtypeStruct((bsz, t, LANES), F32)],
        compiler_params=_cparams(2),
        name=f"dilated_attn_g{g}",
    )(flat, flat, flat, flat, flat, bias)
    return o.reshape(bsz, dil, length, DIL_WIDTH), lse.reshape(bsz, dil, length, LANES)


def _t5_bucket(dist):
    exact = N_BUCKETS // 2
    logd = jnp.log(jnp.maximum(dist, 1).astype(F32) / exact) / math.log(MAX_DISTANCE / exact)
    large = jnp.minimum(exact + (logd * (N_BUCKETS - exact)).astype(jnp.int32), N_BUCKETS - 1)
    return jnp.where(dist < exact, dist, large)


def _bias_tables(rel_bias):
    rel = jnp.arange(BLOCK + 1)
    tables = []
    for g, (window, dil) in enumerate(DIL_PATTERNS):
        span = window // dil
        bucket = _t5_bucket(jnp.clip(rel, 0, span) * dil)
        vec = rel_bias[bucket][:, g * DIL_HEADS:(g + 1) * DIL_HEADS].T.astype(F32)
        ext = jnp.concatenate([vec[:, ::-1], jnp.zeros((DIL_HEADS, BLOCK), F32)], axis=1)
        flat = jnp.tile(ext, (1, BLOCK))[:, :BLOCK * 2 * BLOCK]
        tables.append(flat.reshape(DIL_HEADS, BLOCK, 2 * BLOCK))
    qi = np.arange(BLOCK)[:, None]
    ki = np.arange(2 * BLOCK)[None, :]
    band = (ki >= qi) & (ki <= qi + BLOCK)
    return jnp.where(band, jnp.stack(tables), MASKED)


def _tail(mix_fn, wout_ref, x_ref, mod_ref, lng_ref, lnb_ref, w1_ref, w2_ref, o_ref, *, tm, n_sub, ff_chunk):
    m = mod_ref[0]
    sub = tm // n_sub
    tiles = [slice(i * sub, (i + 1) * sub) for i in range(n_sub)]
    x1s, us = [], []
    for rows in tiles:
        y = None
        for piece, k0 in mix_fn(rows):
            part = jnp.dot(piece, wout_ref[k0:k0 + piece.shape[-1], :], preferred_element_type=F32)
            y = part if y is None else y + part
        x1 = _layer_norm_rows(DEEPNORM_ALPHA * x_ref[0, rows, :] + (1.0 + m[2:3]) * y, lng_ref[0:1], lnb_ref[0:1])
        x1s.append(x1)
        us.append((x1 * (1.0 + m[4:5]) + m[3:4]).astype(BF16))
    accs = [None] * n_sub
    for c in range(D_FF // ff_chunk):
        cols = slice(c * ff_chunk, (c + 1) * ff_chunk)
        w1c = w1_ref[:, cols].astype(BF16)
        w2c = w2_ref[cols, :].astype(BF16)
        for i in range(n_sub):
            h = jnp.maximum(jnp.dot(us[i], w1c, preferred_element_type=F32), 0.0)
            part = jnp.dot((h * h).astype(BF16), w2c, preferred_element_type=F32)
            accs[i] = part if accs[i] is None else accs[i] + part
    for i, rows in enumerate(tiles):
        z = DEEPNORM_ALPHA * x1s[i] + (1.0 + m[5:6]) * accs[i]
        o_ref[0, rows, :] = _layer_norm_rows(z, lng_ref[1:2], lnb_ref[1:2])


def _even_tail_kernel(ya_ref, yb_ref, wout_ref, x_ref, mod_ref, lng_ref, lnb_ref, w1_ref, w2_ref, o_ref, **kw):
    def mix(rows):
        return [(ya_ref[0, rows, :], 0), (yb_ref[0, rows, :], ya_ref.shape[-1])]

    _tail(mix, wout_ref, x_ref, mod_ref, lng_ref, lnb_ref, w1_ref, w2_ref, o_ref, **kw)


def _odd_tail_kernel(o0_ref, o1_ref, o2_ref, l0_ref, l1_ref, l2_ref, wout_ref, x_ref, mod_ref, lng_ref, lnb_ref,
                     w1_ref, w2_ref, o_ref, o_nat, l_nat, mix_s, *, tm, **kw):
    def natural(src_ref, scr, dil):
        if dil == 1:
            return src_ref[0, 0].astype(F32)
        n_col = src_ref.shape[-1] // LANES
        for c in range(n_col):
            for r in range(dil):
                scr[c, pl.ds(r, tm // dil, stride=dil), :] = src_ref[0, r, :, c * LANES:(c + 1) * LANES].astype(F32)
        return jnp.concatenate([scr[c] for c in range(n_col)], axis=1)

    dils = [dil for _, dil in DIL_PATTERNS]
    lses = [natural(ref, l_nat, dil) for ref, dil in zip((l0_ref, l1_ref, l2_ref), dils)]
    mx = jnp.maximum(jnp.maximum(lses[0], lses[1]), lses[2])
    es = [jnp.exp(l - mx) for l in lses]
    inv = 1.0 / (es[0] + es[1] + es[2])
    hr = lax.broadcasted_iota(jnp.int32, (LANES, DIL_WIDTH), 0)
    hc = lax.broadcasted_iota(jnp.int32, (LANES, DIL_WIDTH), 1) // DIL_HEAD_DIM
    expand = jnp.where(hr == hc, 1.0, 0.0).astype(BF16)

    def widen(wt):
        return jnp.dot(wt.astype(BF16), expand, preferred_element_type=F32)

    merged = None
    for e, ref, dil in zip(es, (o0_ref, o1_ref, o2_ref), dils):
        term = widen(e * inv) * natural(ref, o_nat, dil)
        merged = term if merged is None else merged + term
    mix_s[...] = merged.astype(BF16)
    _tail(lambda rows: [(mix_s[rows, :], 0)], wout_ref, x_ref, mod_ref, lng_ref, lnb_ref, w1_ref, w2_ref, o_ref,
          tm=tm, **kw)


def _layer_tail(mix, w_out, x, mod, ln_g, ln_b, w1, w2, layer, *, tm=512, n_sub=2, ff_chunk=1024):
    bsz, t, d = x.shape

    def const(shape):
        return pl.BlockSpec(shape, lambda b, i: (0,) * len(shape), pipeline_mode=pl.Buffered(1))

    def of_layer(shape):
        return pl.BlockSpec((None,) + shape, lambda b, i: (layer,) + (0,) * len(shape), pipeline_mode=pl.Buffered(1))

    tail_specs = [const(w_out.shape),
                  pl.BlockSpec((1, tm, d), lambda b, i: (b, i, 0)),
                  pl.BlockSpec((1, 6, d), lambda b, i: (b, 0, 0)),
                  of_layer((2, d)), of_layer((2, d)), of_layer((d, D_FF)), of_layer((D_FF, d))]
    tail_args = (w_out, x, mod, ln_g, ln_b, w1, w2)
    kw = dict(tm=tm, n_sub=n_sub, ff_chunk=ff_chunk)
    if isinstance(mix, tuple):
        outs, lses = mix
        mix_specs = ([pl.BlockSpec((1, dil, tm // dil, DIL_WIDTH), lambda b, i: (b, 0, i, 0)) for _, dil in DIL_PATTERNS]
                     + [pl.BlockSpec((1, dil, tm // dil, LANES), lambda b, i: (b, 0, i, 0)) for _, dil in DIL_PATTERNS])
        mix_args = (*outs, *lses)
        body = functools.partial(_odd_tail_kernel, **kw)
        scratch = [pltpu.VMEM((DIL_WIDTH // LANES, tm, LANES), F32), pltpu.VMEM((1, tm, LANES), F32),
                   pltpu.VMEM((tm, DIL_WIDTH), BF16)]
    else:
        mix_specs = [pl.BlockSpec((1, tm, piece.shape[-1]), lambda b, i: (b, i, 0)) for piece in mix]
        mix_args = tuple(mix)
        body = functools.partial(_even_tail_kernel, **kw)
        scratch = []
    return pl.pallas_call(
        body,
        grid=(bsz, t // tm),
        in_specs=mix_specs + tail_specs,
        out_specs=pl.BlockSpec((1, tm, d), lambda b, i: (b, i, 0)),
        out_shape=jax.ShapeDtypeStruct((bsz, t, d), F32),
        scratch_shapes=scratch,
        compiler_params=_cparams(2, TAIL_VMEM_LIMIT),
        name="layer_tail",
    )(*mix_args, *tail_args)


def _prep_even_weights(mu, w_up, a_up, g_up):
    pad = LR_PAD - (RW_PROJ - 3 * RW_WIDTH)
    mu_pad = jnp.concatenate([mu, jnp.zeros((pad,), mu.dtype)]).reshape(1, 3 * RW_WIDTH + LR_PAD)
    zeros = jnp.zeros((RW_DECAY_RANK, RW_WIDTH), w_up.dtype)
    wwa = jnp.concatenate([jnp.concatenate([w_up, zeros], axis=1),
                           jnp.concatenate([zeros, a_up], axis=1)], axis=0).astype(BF16)
    gup = jnp.concatenate([g_up, jnp.zeros((pad, RW_WIDTH), g_up.dtype)], axis=0).astype(BF16)
    return mu_pad, wwa, gup


def kernel(x, c, ada_w, ada_b, ln_g, ln_b, ab_w_in, rw_mu, rw_w0, rw_w_up, rw_a0, rw_a_up, rw_g_up,
           rw_k_k, rw_k_a, rw_r_k, rw_lnx_g, rw_lnx_b, sc_conv_w, ab_w_out, dil_w_qkv, dil_w_out,
           rel_bias, mlp_w1, mlp_w2):
    bsz, t, d = x.shape
    mods = _adaln(c, ada_w, ada_b).reshape(DEPTH, bsz, 6, d)
    bias = None
    row = lambda a: a.reshape(1, -1)
    w1_all, w2_all = mlp_w1, mlp_w2
    for i in range(DEPTH):
        mod = mods[i]
        j = i // 2
        if i % 2 == 0:
            mu_pad, wwa, gup = _prep_even_weights(rw_mu[j], rw_w_up[j], rw_a_up[j], rw_g_up[j])
            rkv, lr, yb = _ab_proj(x, mod, ab_w_in[j].T, mu_pad, sc_conv_w[j])
            ya = _rwkv(rkv, lr, row(rw_w0[j]), row(rw_a0[j]), wwa, gup,
                       row(rw_k_k[j]), row(rw_k_a[j]), row(rw_r_k[j]),
                       row(rw_lnx_g[j]), row(rw_lnx_b[j]))
            mix, w_out = [ya, yb], ab_w_out[j]
        else:
            if bias is None:
                bias = _bias_tables(rel_bias)
            qkvs = _qkv_proj(x, mod, dil_w_qkv[j])
            outs, lses = [], []
            for g in range(N_GROUPS):
                o, lse = _dilated_attention_group(qkvs[g], bias[g], g)
                outs.append(o)
                lses.append(lse)
            mix, w_out = (outs, lses), dil_w_out[j]
        x = _layer_tail(mix, w_out.astype(BF16), x, mod, ln_g, ln_b, w1_all, w2_all, i)
    return x
```
